```python
import math
import jax, jax.numpy as jnp
from jax import lax
import numpy as np

D_MODEL = 1024
BATCH = 16
SEQ = 256
DEPTH = 2
DEC_BATCH = 4
DEC_SEQ = 2048
PAST_LEN = 512

GRID_W = 64
MIX_W = D_MODEL
FOURIER_W = MIX_W // 4
CONV_W = MIX_W // 4
ATTN_W = MIX_W // 2
N_HEADS = 4
HEAD_DIM = ATTN_W // (2 * N_HEADS)
QK_W = 2 * HEAD_DIM
CONV_K = 3
ROPE_BASE = 10000.0
Q_BLOCK = 128
EPS = 1e-6
IN_DIM = 2 * FOURIER_W + 4 * CONV_W + 4 * ATTN_W

kernel_name = "hybrid_fourier_conv_diffattn_dit_step"


def _split_points():
    sizes = [FOURIER_W, FOURIER_W, CONV_W, CONV_W, CONV_W, CONV_W, ATTN_W, ATTN_W, ATTN_W]
    return [int(s) for s in np.cumsum(sizes)]


def _rmsnorm(x, g):
    xf = x.astype(jnp.float32)
    y = xf * lax.rsqrt(jnp.mean(xf * xf, axis=-1, keepdims=True) + EPS)
    return (y * g.astype(jnp.float32)).astype(x.dtype)


def _fourier(u):
    f = jnp.fft.fft2(u.astype(jnp.float32), axes=(1, 2), norm="ortho")
    return jnp.real(f).astype(u.dtype)


def _short_conv(b_gate, c_gate, h, w, bias):
    z = c_gate * h
    n = z.shape[1]
    zp = jnp.pad(z, ((0, 0), (1, 1), (0, 0)))
    y = zp[:, 0:n] * w[0] + zp[:, 1:n + 1] * w[1] + zp[:, 2:n + 2] * w[2] + bias
    return b_gate * y


def _rope_axis(t, pos):
    r = t.shape[-1]
    half = r // 2
    inv = 1.0 / (ROPE_BASE ** (jnp.arange(0, r, 2, dtype=jnp.float32) / r))
    ang = pos.astype(jnp.float32)[:, None] * inv[None, :]
    ang = jnp.concatenate([ang, ang], axis=-1)
    cos = jnp.cos(ang)[None, :, None, None, :]
    sin = jnp.sin(ang)[None, :, None, None, :]
    tf = t.astype(jnp.float32)
    rot = jnp.concatenate([-tf[..., half:], tf[..., :half]], axis=-1)
    return (tf * cos + rot * sin).astype(t.dtype)


def _rope_2d(t, row, col):
    b, n, h, _ = t.shape
    t = t.reshape(b, n, h, 2, HEAD_DIM)
    half = HEAD_DIM // 2
    tr = _rope_axis(t[..., :half], row)
    tc = _rope_axis(t[..., half:], col)
    return jnp.concatenate([tr, tc], axis=-1).reshape(b, n, h, QK_W)


def _diff_attention(q, k, v, lam):
    b, n, h, _ = q.shape
    nb = n // Q_BLOCK
    scale = HEAD_DIM ** -0.5
    k1 = k[..., :HEAD_DIM]
    k2 = k[..., HEAD_DIM:]
    qb = q.reshape(b, nb, Q_BLOCK, h, QK_W).transpose(1, 0, 2, 3, 4)

    def block(qblk):
        s1 = jnp.einsum('bqhd,bkhd->bhqk', qblk[..., :HEAD_DIM], k1,
                        preferred_element_type=jnp.float32) * scale
        s2 = jnp.einsum('bqhd,bkhd->bhqk', qblk[..., HEAD_DIM:], k2,
                        preferred_element_type=jnp.float32) * scale
        p = jax.nn.softmax(s1, axis=-1) - lam * jax.nn.softmax(s2, axis=-1)
        return jnp.einsum('bhqk,bkhd->bqhd', p.astype(v.dtype), v)

    o = lax.map(block, qb)
    return o.transpose(1, 0, 2, 3, 4).reshape(b, n, h, v.shape[-1])


def _layer(x, mod, norm_g, w_in, conv_w, conv_b, lam_vec, subln_g, w_out,
           layer_idx, pos, ctx_kv):
    shift, scale, gate = jnp.split(mod, 3, axis=-1)
    h = _rmsnorm(x, norm_g) * (1.0 + scale) + shift
    z = h @ w_in
    fa, ga, bg, cg, hc, gb, q, k, v, gc = jnp.split(z, _split_points(), axis=-1)
    b, n, _ = x.shape
    ya = _fourier(fa) * jax.nn.silu(ga)
    yb = _short_conv(bg, cg, hc, conv_w, conv_b) * jax.nn.silu(gb)
    q = q.reshape(b, n, N_HEADS, QK_W)
    k = k.reshape(b, n, N_HEADS, QK_W)
    v = v.reshape(b, n, N_HEADS, QK_W)
    if pos is not None:
        q = _rope_2d(q, pos[0], pos[1])
        k = _rope_2d(k, pos[0], pos[1])
    if ctx_kv is not None:
        keys = jnp.concatenate([ctx_kv[0].astype(k.dtype), k], axis=1)
        vals = jnp.concatenate([ctx_kv[1].astype(v.dtype), v], axis=1)
    else:
        keys, vals = k, v
    lam_init = 0.8 - 0.6 * math.exp(-0.3 * layer_idx)
    lv = lam_vec.astype(jnp.float32)
    lam = jnp.exp(jnp.sum(lv[0] * lv[1])) - jnp.exp(jnp.sum(lv[2] * lv[3])) + lam_init
    o = _diff_attention(q, keys, vals, lam)
    o = _rmsnorm(o, subln_g) * (1.0 - lam_init)
    yc = o.reshape(b, n, ATTN_W) * jax.nn.silu(gc)
    out = jnp.concatenate([ya, yb, yc], axis=-1) @ w_out
    return x + gate * out, k, v


def setup_inputs(seed: int = 0) -> dict:
    key = jax.random.key(seed)
    ks = jax.random.split(key, 16)
    f32 = jnp.float32
    nrm = lambda k, s: jax.random.normal(k, s, dtype=f32)
    return {
        "x_prompt": nrm(ks[0], (BATCH, SEQ, D_MODEL)),
        "x_sample": nrm(ks[1], (DEC_BATCH, DEC_SEQ, D_MODEL)),
        "cache_k": nrm(ks[2], (DEC_BATCH, DEPTH, PAST_LEN, N_HEADS, QK_W)),
        "cache_v": nrm(ks[3], (DEC_BATCH, DEPTH, PAST_LEN, N_HEADS, QK_W)),
        "c": nrm(ks[4], (DEC_BATCH, D_MODEL)),
        "c_ctx": nrm(ks[5], (D_MODEL,)),
        "norm_g": 1.0 + 0.02 * nrm(ks[6], (DEPTH, D_MODEL)),
        "w_mod": nrm(ks[7], (DEPTH, D_MODEL, 3 * D_MODEL)) * D_MODEL ** -0.5,
        "b_mod": 0.02 * nrm(ks[8], (DEPTH, 3 * D_MODEL)),
        "w_in": nrm(ks[9], (DEPTH, D_MODEL, IN_DIM)) * D_MODEL ** -0.5,
        "conv_w": nrm(ks[10], (DEPTH, CONV_K, CONV_W)) * CONV_K ** -0.5,
        "conv_b": 0.02 * nrm(ks[11], (DEPTH, CONV_W)),
        "lam_vec": 0.1 * nrm(ks[12], (DEPTH, 4, HEAD_DIM)),
        "subln_g": 1.0 + 0.02 * nrm(ks[13], (DEPTH, QK_W)),
        "w_out": nrm(ks[14], (DEPTH, MIX_W, D_MODEL)) * MIX_W ** -0.5,
        "final_g": 1.0 + 0.02 * nrm(ks[15], (D_MODEL,)),
    }


def reference(x_prompt, x_sample, cache_k, cache_v, c, c_ctx, norm_g, w_mod, b_mod,
              w_in, conv_w, conv_b, lam_vec, subln_g, w_out, final_g):
    xc = x_prompt
    sc = jax.nn.silu(c_ctx)
    ks, vs = [], []
    for l in range(DEPTH):
        mod = sc @ w_mod[l] + b_mod[l]
        xc, kl, vl = _layer(xc, mod, norm_g[l], w_in[l], conv_w[l], conv_b[l],
                            lam_vec[l], subln_g[l], w_out[l], l, None, None)
        ks.append(kl)
        vs.append(vl)
    y_prompt = _rmsnorm(xc, final_g)
    new_k = jnp.stack(ks, axis=1)
    new_v = jnp.stack(vs, axis=1)

    n = x_sample.shape[1]
    rows = n // GRID_W
    row = jnp.repeat(jnp.arange(rows, dtype=jnp.int32), GRID_W)
    col = jnp.tile(jnp.arange(GRID_W, dtype=jnp.int32), rows)
    sl = jax.nn.silu(c)
    xl = x_sample
    for l in range(DEPTH):
        mod = (sl @ w_mod[l] + b_mod[l])[:, None, :]
        xl, _, _ = _layer(xl, mod, norm_g[l], w_in[l], conv_w[l], conv_b[l],
                          lam_vec[l], subln_g[l], w_out[l], l, (row, col),
                          (cache_k[:, l], cache_v[:, l]))
    y_sample = _rmsnorm(xl, final_g)
    return (y_prompt, y_sample, new_k, new_v)
```

```python
import functools
import math

import numpy as np
import jax
import jax.numpy as jnp
from jax import lax
from jax.experimental import pallas as pl
from jax.experimental.pallas import tpu as pltpu

D_MODEL = 1024
DEPTH = 2
GRID_W = 64
FOURIER_W = 256
CONV_W = 256
ATTN_W = 512
N_HEADS = 4
HEAD_DIM = 64
QK_W = 128
ROPE_BASE = 10000.0
EPS = 1e-6
IN_DIM = 3584

F32 = jnp.float32
BF16 = jnp.bfloat16

VMEM_LIMIT_BYTES = 56 * 1024 * 1024
TOKEN_TILE = 512
Q_TILE = 256
HALO_ROWS = 16
MOD_ROWS = 8
LOG2E = 1.4426950408889634

_COLS = {}
_off = 0
for _name, _w in (("fa", 256), ("ga", 256), ("bg", 256), ("cg", 256), ("hc", 256), ("gb", 256),
                  ("q", 512), ("k", 512), ("v", 512), ("gc", 512)):
    _COLS[_name] = (_off, _off + _w)
    _off += _w


def _silu(x):
    return x * (1.0 / (1.0 + jnp.exp(-x)))


def _cparams(sem):
    return pltpu.CompilerParams(dimension_semantics=sem, vmem_limit_bytes=VMEM_LIMIT_BYTES)


def _rope_tables(n_tokens):
    n = np.arange(n_tokens)
    row = (n // GRID_W).astype(np.float64)
    col = (n % GRID_W).astype(np.float64)
    j = np.arange(QK_W)
    jj = j % HEAD_DIM
    idx = jj % 32
    inv = 1.0 / (ROPE_BASE ** (2.0 * (idx % 16) / 32.0))
    pos = np.where((jj < 32)[None, :], row[:, None], col[:, None])
    ang = pos * inv[None, :]
    cos = np.cos(ang)
    sin = np.sin(ang)
    first = (idx < 16)[None, :]
    sin_a = np.where(first, -sin, 0.0)
    sin_b = np.where(first, 0.0, sin)
    return (np.asarray(cos, np.float32), np.asarray(sin_a, np.float32), np.asarray(sin_b, np.float32))


def _dft_tables(n):
    k = np.arange(n)
    kn = (k[:, None] * k[None, :]) % n
    ang = 2.0 * np.pi * kn / n
    return np.asarray(np.concatenate([np.cos(ang), -np.sin(ang)], axis=1) / math.sqrt(n), np.float32)


def _chan_tables():
    k = np.arange(FOURIER_W)
    kn = (k[:, None] * k[None, :]) % FOURIER_W
    ang = 2.0 * np.pi * kn / FOURIER_W
    return np.asarray(np.concatenate([np.cos(ang), np.sin(ang)], axis=1) / math.sqrt(FOURIER_W), np.float32)


def _mod_kernel(c_ref, w_ref, b_ref, o_ref):
    s = _silu(c_ref[...]).astype(BF16)
    w = w_ref[0].astype(BF16)
    o_ref[0] = jnp.dot(s, w, preferred_element_type=F32) + b_ref[0]


def _modulation(cvecs, w_mod, b_mod):
    chunk = D_MODEL
    n_chunks = 3 * D_MODEL // chunk
    return pl.pallas_call(
        _mod_kernel,
        grid=(DEPTH, n_chunks),
        in_specs=[
            pl.BlockSpec((MOD_ROWS, D_MODEL), lambda l, j: (0, 0)),
            pl.BlockSpec((1, D_MODEL, chunk), lambda l, j: (l, 0, j)),
            pl.BlockSpec((1, 1, chunk), lambda l, j: (l, 0, j)),
        ],
        out_specs=pl.BlockSpec((1, MOD_ROWS, chunk), lambda l, j: (l, 0, j)),
        out_shape=jax.ShapeDtypeStruct((DEPTH, MOD_ROWS, 3 * D_MODEL), F32),
        compiler_params=_cparams(("arbitrary", "arbitrary")),
        name="modulation",
    )(cvecs, w_mod, b_mod.reshape(DEPTH, 1, 3 * D_MODEL))


def _inproj_kernel(*refs, rope, emit_kv):
    if rope:
        x_ref, mod_ref, g_ref, w_ref, cos_ref, sa_ref, sb_ref = refs[:7]
        outs = refs[7:]
    else:
        x_ref, mod_ref, g_ref, w_ref = refs[:4]
        outs = refs[4:]
    fa_ref, sga_ref, zc_ref, bgs_ref, q_ref, k_ref, v_ref, sgc_ref = outs[:8]

    x = x_ref[...]
    ms = jnp.mean(x * x, axis=-1, keepdims=True)
    y = x * lax.rsqrt(ms + EPS) * g_ref[...]
    m = mod_ref[0]
    h = (y * (1.0 + m[:, D_MODEL:2 * D_MODEL]) + m[:, :D_MODEL]).astype(BF16)

    def proj(name):
        lo, hi = _COLS[name]
        return jnp.dot(h, w_ref[:, lo:hi], preferred_element_type=F32)

    fa_ref[...] = proj("fa").astype(BF16)
    sga_ref[...] = _silu(proj("ga")).astype(BF16)
    zc_ref[...] = (proj("cg") * proj("hc")).astype(BF16)
    bgs_ref[...] = (proj("bg") * _silu(proj("gb"))).astype(BF16)
    sgc_ref[...] = _silu(proj("gc")).astype(BF16)

    q = proj("q")
    k = proj("k")
    v = proj("v")
    if emit_kv:
        outs[8][...] = k
        outs[9][...] = v
    v_ref[...] = v.astype(BF16)

    q_scale = HEAD_DIM ** -0.5 * LOG2E
    if rope:
        cos = cos_ref[...]
        sa = sa_ref[...]
        sb = sb_ref[...]
        for hd in range(N_HEADS):
            sl = slice(hd * QK_W, (hd + 1) * QK_W)
            for t, ref, scale in ((q, q_ref, q_scale), (k, k_ref, None)):
                th = t[:, sl]
                r = (th * cos + pltpu.roll(th, QK_W - 16, axis=1) * sa
                     + pltpu.roll(th, 16, axis=1) * sb)
                if scale is not None:
                    r = r * scale
                ref[:, sl] = r.astype(BF16)
    else:
        q_ref[...] = (q * q_scale).astype(BF16)
        k_ref[...] = k.astype(BF16)


def _in_projection(x, mod_rows, norm_g, w_in_bf16, rope_tabs, *, tiles_per_mod, emit_kv):
    n_tok = x.shape[0]
    tm = TOKEN_TILE
    n_tiles = n_tok // tm
    rope = rope_tabs is not None
    row = lambda w: pl.BlockSpec((tm, w), lambda i: (i, 0))
    in_specs = [
        row(D_MODEL),
        pl.BlockSpec((1, 1, 3 * D_MODEL), lambda i: (i // tiles_per_mod, 0, 0)),
        pl.BlockSpec((1, D_MODEL), lambda i: (0, 0)),
        pl.BlockSpec((D_MODEL, IN_DIM), lambda i: (0, 0)),
    ]
    args = [x, mod_rows, norm_g.reshape(1, D_MODEL), w_in_bf16]
    if rope:
        seq_tiles = rope_tabs[0].shape[0] // tm
        tab = pl.BlockSpec((tm, QK_W), lambda i: (i % seq_tiles, 0))
        in_specs += [tab, tab, tab]
        args += list(rope_tabs)
    widths = (256, 256, 256, 256, 512, 512, 512, 512)
    out_specs = [row(w) for w in widths]
    out_shape = [jax.ShapeDtypeStruct((n_tok, w), BF16) for w in widths]
    if emit_kv:
        out_specs += [row(ATTN_W), row(ATTN_W)]
        out_shape += [jax.ShapeDtypeStruct((n_tok, ATTN_W), F32)] * 2
    return pl.pallas_call(
        functools.partial(_inproj_kernel, rope=rope, emit_kv=emit_kv),
        grid=(n_tiles,),
        in_specs=in_specs,
        out_specs=out_specs,
        out_shape=out_shape,
        compiler_params=_cparams(("arbitrary",)),
        name="in_projection_rope" if rope else "in_projection",
    )(*args)


def _lambda(lam_ref, lam_init):
    lv = lam_ref[...]
    a = jnp.sum(lv[0:1] * lv[1:2], axis=-1, keepdims=True)
    b = jnp.sum(lv[2:3] * lv[3:4], axis=-1, keepdims=True)
    return jnp.exp(a) - jnp.exp(b) + lam_init


def _diff_attention_tile(q, keys, vals, lam):
    tq = q.shape[0]
    lane = lax.broadcasted_iota(jnp.int32, q.shape, 1)
    first = lane < HEAD_DIM
    zero = jnp.zeros_like(q)
    qq = jnp.concatenate([jnp.where(first, q, zero), jnp.where(first, zero, q)], axis=0)
    dims = (((1,), (1,)), ((), ()))
    s = [lax.dot_general(qq, kp, dims, preferred_element_type=F32) for kp in keys]
    m = s[0].max(axis=-1, keepdims=True)
    for sp in s[1:]:
        m = jnp.maximum(m, sp.max(axis=-1, keepdims=True))
    e = [jnp.exp2(sp - m) for sp in s]
    l = e[0].sum(axis=-1, keepdims=True)
    for ep in e[1:]:
        l = l + ep.sum(axis=-1, keepdims=True)
    r = 1.0 / l
    w1 = r[:tq]
    w2 = r[tq:] * lam
    o = None
    for ep, vp in zip(e, vals):
        p = (ep[:tq] * w1 - ep[tq:] * w2).astype(BF16)
        c = jnp.dot(p, vp, preferred_element_type=F32)
        o = c if o is None else o + c
    return o


def _subln_gate(o, sg, sgc, lam_init):
    ms = jnp.mean(o * o, axis=-1, keepdims=True)
    y = o * lax.rsqrt(ms + EPS) * sg * (1.0 - lam_init)
    return (y * sgc.astype(F32)).astype(BF16)


def _attn_ctx_kernel(lam_ref, sg_ref, q_ref, k_ref, v_ref, sgc_ref, o_ref, *, lam_init):
    lam = _lambda(lam_ref, lam_init)
    sg = sg_ref[...]
    for hd in range(N_HEADS):
        sl = slice(hd * QK_W, (hd + 1) * QK_W)
        o = _diff_attention_tile(q_ref[:, sl], [k_ref[:, sl]], [v_ref[:, sl]], lam)
        o_ref[:, sl] = _subln_gate(o, sg, sgc_ref[:, sl], lam_init)


def _attention_ctx(q, k, v, sgc, lam_vec, subln_g, *, seq, lam_init):
    n_tok = q.shape[0]
    blk = pl.BlockSpec((seq, ATTN_W), lambda b: (b, 0))
    return pl.pallas_call(
        functools.partial(_attn_ctx_kernel, lam_init=lam_init),
        grid=(n_tok // seq,),
        in_specs=[
            pl.BlockSpec((4, HEAD_DIM), lambda b: (0, 0)),
            pl.BlockSpec((1, QK_W), lambda b: (0, 0)),
            blk, blk, blk, blk,
        ],
        out_specs=blk,
        out_shape=jax.ShapeDtypeStruct((n_tok, ATTN_W), BF16),
        compiler_params=_cparams(("arbitrary",)),
        name="attention_ctx",
    )(lam_vec, subln_g.reshape(1, QK_W), q, k, v, sgc)


def _attn_den_kernel(lam_ref, sg_ref, q_ref, kn_ref, vn_ref, kc_ref, vc_ref, sgc_ref, o_ref,
                     *, lam_init, n_q):
    lam = _lambda(lam_ref, lam_init)
    sg = sg_ref[...]
    kc = kc_ref[0, 0].astype(BF16)
    vc = vc_ref[0, 0].astype(BF16)
    kn = kn_ref[...]
    vn = vn_ref[...]

    def body(t, carry):
        r = pl.multiple_of(t * Q_TILE, Q_TILE)
        rows = pl.ds(r, Q_TILE)
        o = _diff_attention_tile(q_ref[rows, :], [kc, kn], [vc, vn], lam)
        o_ref[rows, :] = _subln_gate(o, sg, sgc_ref[rows, :], lam_init)
        return carry

    lax.fori_loop(0, n_q, body, 0)


def _attention_den(q, k, v, sgc, cache_k, cache_v, lam_vec, subln_g, *, layer, seq, lam_init):
    n_tok = q.shape[0]
    n_b = n_tok // seq
    past = cache_k.shape[2]
    blk = pl.BlockSpec((seq, QK_W), lambda b, h: (b, h))
    cblk = pl.BlockSpec((1, 1, past, QK_W), lambda b, h: (b, layer, 0, h))
    return pl.pallas_call(
        functools.partial(_attn_den_kernel, lam_init=lam_init, n_q=seq // Q_TILE),
        grid=(n_b, N_HEADS),
        in_specs=[
            pl.BlockSpec((4, HEAD_DIM), lambda b, h: (0, 0)),
            pl.BlockSpec((1, QK_W), lambda b, h: (0, 0)),
            blk, blk, blk, cblk, cblk, blk,
        ],
        out_specs=blk,
        out_shape=jax.ShapeDtypeStruct((n_tok, ATTN_W), BF16),
        compiler_params=_cparams(("arbitrary", "arbitrary")),
        name="attention_den",
    )(lam_vec, subln_g.reshape(1, QK_W), q, k, v, cache_k, cache_v, sgc)


def _channel_dft(fa, cs_ref):
    ab = jnp.dot(fa, cs_ref[...], preferred_element_type=F32).astype(BF16)
    return jnp.concatenate([ab[:, :FOURIER_W], ab[:, FOURIER_W:]], axis=0)


def _fourier_ctx_kernel(fa_ref, sga_ref, cs_ref, dn_ref, o_ref):
    ab2 = _channel_dft(fa_ref[...], cs_ref)
    f = jnp.dot(dn_ref[...], ab2, preferred_element_type=F32)
    o_ref[...] = (f * sga_ref[...].astype(F32)).astype(BF16)


def _fourier_ctx(fa, sga, cs, dn, *, seq):
    n_tok = fa.shape[0]
    blk = pl.BlockSpec((seq, FOURIER_W), lambda b: (b, 0))
    return pl.pallas_call(
        _fourier_ctx_kernel,
        grid=(n_tok // seq,),
        in_specs=[blk, blk,
                  pl.BlockSpec((FOURIER_W, 2 * FOURIER_W), lambda b: (0, 0)),
                  pl.BlockSpec((seq, 2 * seq), lambda b: (0, 0))],
        out_specs=blk,
        out_shape=jax.ShapeDtypeStruct((n_tok, FOURIER_W), BF16),
        compiler_params=_cparams(("arbitrary",)),
        name="fourier_ctx",
    )(fa, sga, cs, dn)


def _fourier_den_kernel(fa_ref, sga_ref, cs_ref, dn_ref, o_ref, ab_ref):
    @pl.when(pl.program_id(1) == 0)
    def _():
        ab_ref[...] = _channel_dft(fa_ref[...], cs_ref)

    f = jnp.dot(dn_ref[...], ab_ref[...], preferred_element_type=F32)
    o_ref[...] = (f * sga_ref[...].astype(F32)).astype(BF16)


def _fourier_den(fa, sga, cs, dn, *, seq):
    n_tok = fa.shape[0]
    tm = TOKEN_TILE
    tiles = seq // tm
    return pl.pallas_call(
        _fourier_den_kernel,
        grid=(n_tok // seq, tiles),
        in_specs=[
            pl.BlockSpec((seq, FOURIER_W), lambda b, j: (b, 0)),
            pl.BlockSpec((tm, FOURIER_W), lambda b, j: (b * tiles + j, 0)),
            pl.BlockSpec((FOURIER_W, 2 * FOURIER_W), lambda b, j: (0, 0)),
            pl.BlockSpec((tm, 2 * seq), lambda b, j: (j, 0)),
        ],
        out_specs=pl.BlockSpec((tm, FOURIER_W), lambda b, j: (b * tiles + j, 0)),
        out_shape=jax.ShapeDtypeStruct((n_tok, FOURIER_W), BF16),
        scratch_shapes=[pltpu.VMEM((2 * seq, FOURIER_W), BF16)],
        compiler_params=_cparams(("arbitrary", "arbitrary")),
        name="fourier_den",
    )(fa, sga, cs, dn)


def _outproj_kernel(x_ref, mod_ref, ya_ref, zc_ref, zp_ref, zn_ref, bgs_ref, yc_ref,
                    cw_ref, cb_ref, w_ref, fg_ref, o_ref, *, seq, final_norm):
    tm = x_ref.shape[0]
    z = zc_ref[...].astype(F32)
    row = lax.broadcasted_iota(jnp.int32, z.shape, 0)
    pos = (pl.program_id(0) * tm + row) & (seq - 1)
    prev_row = zp_ref[HALO_ROWS - 1:HALO_ROWS, :].astype(F32)
    next_row = zn_ref[0:1, :].astype(F32)
    z_prev = jnp.where(row == 0, prev_row, pltpu.roll(z, 1, axis=0))
    z_prev = jnp.where(pos == 0, 0.0, z_prev)
    z_next = jnp.where(row == tm - 1, next_row, pltpu.roll(z, tm - 1, axis=0))
    z_next = jnp.where(pos == seq - 1, 0.0, z_next)
    cw = cw_ref[...]
    conv = z_prev * cw[0:1] + z * cw[1:2] + z_next * cw[2:3] + cb_ref[...]
    yb = (bgs_ref[...].astype(F32) * conv).astype(BF16)

    mixed = jnp.concatenate([ya_ref[...], yb, yc_ref[...]], axis=-1)
    out = jnp.dot(mixed, w_ref[...], preferred_element_type=F32)
    gate = mod_ref[0][:, 2 * D_MODEL:]
    xn = x_ref[...] + gate * out
    if final_norm:
        ms = jnp.mean(xn * xn, axis=-1, keepdims=True)
        xn = xn * lax.rsqrt(ms + EPS) * fg_ref[...]
    o_ref[...] = xn


def _out_projection(x, mod_rows, ya, zc, bgs, yc, conv_w, conv_b, w_out_bf16, final_g,
                    *, seq, tiles_per_mod, final_norm):
    n_tok = x.shape[0]
    tm = TOKEN_TILE
    n_tiles = n_tok // tm
    halo_per_tile = tm // HALO_ROWS
    n_halo = n_tok // HALO_ROWS
    row = lambda w: pl.BlockSpec((tm, w), lambda i: (i, 0))
    return pl.pallas_call(
        functools.partial(_outproj_kernel, seq=seq, final_norm=final_norm),
        grid=(n_tiles,),
        in_specs=[
            row(D_MODEL),
            pl.BlockSpec((1, 1, 3 * D_MODEL), lambda i: (i // tiles_per_mod, 0, 0)),
            row(FOURIER_W),
            row(CONV_W),
            pl.BlockSpec((HALO_ROWS, CONV_W), lambda i: (jnp.maximum(i * halo_per_tile - 1, 0), 0)),
            pl.BlockSpec((HALO_ROWS, CONV_W),
                         lambda i: (jnp.minimum((i + 1) * halo_per_tile, n_halo - 1), 0)),
            row(CONV_W),
            row(ATTN_W),
            pl.BlockSpec((3, CONV_W), lambda i: (0, 0)),
            pl.BlockSpec((1, CONV_W), lambda i: (0, 0)),
            pl.BlockSpec((D_MODEL, D_MODEL), lambda i: (0, 0)),
            pl.BlockSpec((1, D_MODEL), lambda i: (0, 0)),
        ],
        out_specs=row(D_MODEL),
        out_shape=jax.ShapeDtypeStruct((n_tok, D_MODEL), F32),
        compiler_params=_cparams(("arbitrary",)),
        name="out_projection",
    )(x, mod_rows, ya, zc, zc, zc, bgs, yc, conv_w, conv_b.reshape(1, CONV_W), w_out_bf16,
      final_g.reshape(1, D_MODEL))


def kernel(x_prompt, x_sample, cache_k, cache_v, c, c_ctx, norm_g, w_mod, b_mod, w_in, conv_w,
           conv_b, lam_vec, subln_g, w_out, final_g):
    batch, seq, _ = x_prompt.shape
    dec_batch, dec_seq, _ = x_sample.shape
    past = cache_k.shape[2]

    cvecs = jnp.concatenate(
        [c_ctx[None, :], c, jnp.zeros((MOD_ROWS - 1 - dec_batch, D_MODEL), F32)], axis=0)
    mod = _modulation(cvecs, w_mod, b_mod)

    w_in_b = w_in.astype(BF16)
    w_out_b = w_out.astype(BF16)
    rope_tabs = tuple(jnp.asarray(t) for t in _rope_tables(dec_seq))
    cs = jnp.asarray(_chan_tables()).astype(BF16)
    dn_ctx = jnp.asarray(_dft_tables(seq)).astype(BF16)
    dn_den = jnp.asarray(_dft_tables(dec_seq)).astype(BF16)
    ck = cache_k.reshape(dec_batch, DEPTH, past, ATTN_W)
    cv = cache_v.reshape(dec_batch, DEPTH, past, ATTN_W)

    xc = x_prompt.reshape(batch * seq, D_MODEL)
    xl = x_sample.reshape(dec_batch * dec_seq, D_MODEL)
    ks, vs = [], []
    for l in range(DEPTH):
        lam_init = 0.8 - 0.6 * math.exp(-0.3 * l)
        last = l == DEPTH - 1

        mod_c = mod[l, 0:1][:, None, :]
        fa, sga, zc, bgs, q, k, v, sgc, k32, v32 = _in_projection(
            xc, mod_c, norm_g[l], w_in_b[l], None, tiles_per_mod=xc.shape[0] // TOKEN_TILE,
            emit_kv=True)
        ks.append(k32.reshape(batch, seq, N_HEADS, QK_W))
        vs.append(v32.reshape(batch, seq, N_HEADS, QK_W))
        yc = _attention_ctx(q, k, v, sgc, lam_vec[l], subln_g[l], seq=seq, lam_init=lam_init)
        ya = _fourier_ctx(fa, sga, cs, dn_ctx, seq=seq)
        xc = _out_projection(xc, mod_c, ya, zc, bgs, yc, conv_w[l], conv_b[l], w_out_b[l], final_g,
                             seq=seq, tiles_per_mod=xc.shape[0] // TOKEN_TILE, final_norm=last)

        mod_d = mod[l, 1:1 + dec_batch][:, None, :]
        fa, sga, zc, bgs, q, k, v, sgc = _in_projection(
            xl, mod_d, norm_g[l], w_in_b[l], rope_tabs, tiles_per_mod=dec_seq // TOKEN_TILE,
            emit_kv=False)
        yc = _attention_den(q, k, v, sgc, ck, cv, lam_vec[l], subln_g[l], layer=l, seq=dec_seq,
                            lam_init=lam_init)
        ya = _fourier_den(fa, sga, cs, dn_den, seq=dec_seq)
        xl = _out_projection(xl, mod_d, ya, zc, bgs, yc, conv_w[l], conv_b[l], w_out_b[l], final_g,
                             seq=dec_seq, tiles_per_mod=dec_seq // TOKEN_TILE, final_norm=last)

    y_prompt = xc.reshape(batch, seq, D_MODEL)
    y_sample = xl.reshape(dec_batch, dec_seq, D_MODEL)
    new_k = jnp.stack(ks, axis=1)
    new_v = jnp.stack(vs, axis=1)
    return (y_prompt, y_sample, new_k, new_v)
```

```python
import functools
import math

import numpy as np
import jax
import jax.numpy as jnp
from jax import lax
from jax.experimental import pallas as pl
from jax.experimental.pallas import tpu as pltpu

D_MODEL = 1024
DEPTH = 2
GRID_W = 64
FOURIER_W = 256
CONV_W = 256
ATTN_W = 512
N_HEADS = 4
HEAD_DIM = 64
QK_W = 128
ROPE_BASE = 10000.0
EPS = 1e-6
IN_DIM = 3584

F32 = jnp.float32
BF16 = jnp.bfloat16

VMEM_LIMIT_BYTES = 56 * 1024 * 1024
TOKEN_TILE = 512
Q_TILE = 256
CTX_Q_TILE = 256
KEY_CHUNK = 256
HALO_ROWS = 16
MOD_ROWS = 8
LOG2E = 1.4426950408889634

_COLS = {}
_off = 0
for _name, _w in (("fa", 256), ("ga", 256), ("bg", 256), ("cg", 256), ("hc", 256), ("gb", 256),
                  ("q", 512), ("k", 512), ("v", 512), ("gc", 512)):
    _COLS[_name] = (_off, _off + _w)
    _off += _w


def _silu(x):
    return x * (1.0 / (1.0 + jnp.exp(-x)))


def _cparams(sem):
    return pltpu.CompilerParams(dimension_semantics=sem, vmem_limit_bytes=VMEM_LIMIT_BYTES)


def _rope_tables(n_tokens):
    n = np.arange(n_tokens)
    row = (n // GRID_W).astype(np.float64)
    col = (n % GRID_W).astype(np.float64)
    j = np.arange(QK_W)
    jj = j % HEAD_DIM
    idx = jj % 32
    inv = 1.0 / (ROPE_BASE ** (2.0 * (idx % 16) / 32.0))
    pos = np.where((jj < 32)[None, :], row[:, None], col[:, None])
    ang = pos * inv[None, :]
    cos = np.cos(ang)
    sin = np.sin(ang)
    first = (idx < 16)[None, :]
    sin_a = np.where(first, -sin, 0.0)
    sin_b = np.where(first, 0.0, sin)
    return (np.asarray(cos, np.float32), np.asarray(sin_a, np.float32), np.asarray(sin_b, np.float32))


def _dft_tables(n):
    k = np.arange(n)
    kn = (k[:, None] * k[None, :]) % n
    ang = 2.0 * np.pi * kn / n
    return np.asarray(np.concatenate([np.cos(ang), -np.sin(ang)], axis=1) / math.sqrt(n), np.float32)


def _chan_tables():
    k = np.arange(FOURIER_W)
    kn = (k[:, None] * k[None, :]) % FOURIER_W
    ang = 2.0 * np.pi * kn / FOURIER_W
    return np.asarray(np.concatenate([np.cos(ang), np.sin(ang)], axis=1) / math.sqrt(FOURIER_W), np.float32)


def _mod_kernel(c_ref, w_ref, b_ref, o_ref):
    s = _silu(c_ref[...]).astype(BF16)
    w = w_ref[0].astype(BF16)
    o_ref[0] = jnp.dot(s, w, preferred_element_type=F32) + b_ref[0]


def _modulation(cvecs, w_mod, b_mod):
    chunk = D_MODEL
    n_chunks = 3 * D_MODEL // chunk
    return pl.pallas_call(
        _mod_kernel,
        grid=(DEPTH, n_chunks),
        in_specs=[
            pl.BlockSpec((MOD_ROWS, D_MODEL), lambda l, j: (0, 0)),
            pl.BlockSpec((1, D_MODEL, chunk), lambda l, j: (l, 0, j)),
            pl.BlockSpec((1, 1, chunk), lambda l, j: (l, 0, j)),
        ],
        out_specs=pl.BlockSpec((1, MOD_ROWS, chunk), lambda l, j: (l, 0, j)),
        out_shape=jax.ShapeDtypeStruct((DEPTH, MOD_ROWS, 3 * D_MODEL), F32),
        compiler_params=_cparams(("arbitrary", "arbitrary")),
        name="modulation",
    )(cvecs, w_mod, b_mod.reshape(DEPTH, 1, 3 * D_MODEL))


def _store_heads(ref, t):
    seqs, _, seq, _, _ = ref.shape
    for s in range(seqs):
        for hd in range(N_HEADS):
            ref[s, 0, :, hd, :] = t[s * seq:(s + 1) * seq, hd * QK_W:(hd + 1) * QK_W]


def _inproj_kernel(*refs, rope, emit_kv, n_alias):
    n_in = 4 + (3 if rope else 0)
    x_ref, mod_ref, g_ref, w_ref = refs[:4]
    if rope:
        cos_ref, sa_ref, sb_ref = refs[4:7]
    outs = refs[n_in + n_alias:]
    fa_ref, sga_ref, zc_ref, bgs_ref, q_ref, k_ref, v_ref, sgc_ref = outs[:8]

    x = x_ref[...]
    ms = jnp.mean(x * x, axis=-1, keepdims=True)
    y = x * lax.rsqrt(ms + EPS) * g_ref[...]
    m = mod_ref[0]
    h = (y * (1.0 + m[:, D_MODEL:2 * D_MODEL]) + m[:, :D_MODEL]).astype(BF16)

    def proj(name):
        lo, hi = _COLS[name]
        return jnp.dot(h, w_ref[:, lo:hi], preferred_element_type=F32)

    fa_ref[...] = proj("fa").astype(BF16)
    sga_ref[...] = _silu(proj("ga")).astype(BF16)
    zc_ref[...] = (proj("cg") * proj("hc")).astype(BF16)
    bgs_ref[...] = (proj("bg") * _silu(proj("gb"))).astype(BF16)
    sgc_ref[...] = _silu(proj("gc")).astype(BF16)

    q = proj("q")
    k = proj("k")
    v = proj("v")
    if emit_kv:
        _store_heads(outs[8], k)
        _store_heads(outs[9], v)
    v_ref[...] = v.astype(BF16)

    q_scale = HEAD_DIM ** -0.5 * LOG2E
    if rope:
        cos = cos_ref[...]
        sa = sa_ref[...]
        sb = sb_ref[...]
        for hd in range(N_HEADS):
            sl = slice(hd * QK_W, (hd + 1) * QK_W)
            for t, ref, scale in ((q, q_ref, q_scale), (k, k_ref, None)):
                th = t[:, sl]
                r = (th * cos + pltpu.roll(th, QK_W - 16, axis=1) * sa
                     + pltpu.roll(th, 16, axis=1) * sb)
                if scale is not None:
                    r = r * scale
                ref[:, sl] = r.astype(BF16)
    else:
        q_ref[...] = (q * q_scale).astype(BF16)
        k_ref[...] = k.astype(BF16)


def _in_projection(x, mod_rows, norm_g, w_in_bf16, rope_tabs, *, tiles_per_mod, kv_out=None):
    n_tok = x.shape[0]
    tm = TOKEN_TILE
    n_tiles = n_tok // tm
    rope = rope_tabs is not None
    emit_kv = kv_out is not None
    row = lambda w: pl.BlockSpec((tm, w), lambda i: (i, 0))
    in_specs = [
        row(D_MODEL),
        pl.BlockSpec((1, 1, 3 * D_MODEL), lambda i: (i // tiles_per_mod, 0, 0)),
        pl.BlockSpec((1, D_MODEL), lambda i: (0, 0)),
        pl.BlockSpec((D_MODEL, IN_DIM), lambda i: (0, 0)),
    ]
    args = [x, mod_rows, norm_g.reshape(1, D_MODEL), w_in_bf16]
    if rope:
        seq_tiles = rope_tabs[0].shape[0] // tm
        tab = pl.BlockSpec((tm, QK_W), lambda i: (i % seq_tiles, 0))
        in_specs += [tab, tab, tab]
        args += list(rope_tabs)
    widths = (256, 256, 256, 256, 512, 512, 512, 512)
    out_specs = [row(w) for w in widths]
    out_shape = [jax.ShapeDtypeStruct((n_tok, w), BF16) for w in widths]
    aliases = {}
    n_alias = 0
    if emit_kv:
        layer, seq, new_k, new_v = kv_out
        seqs = tm // seq
        kv_blk = pl.BlockSpec((seqs, 1, seq, N_HEADS, QK_W), lambda i: (i, layer, 0, 0, 0))
        out_specs += [kv_blk, kv_blk]
        out_shape += [jax.ShapeDtypeStruct((n_tok // seq, DEPTH, seq, N_HEADS, QK_W), F32)] * 2
        if new_k is not None:
            aliases = {len(args): len(widths), len(args) + 1: len(widths) + 1}
            in_specs += [pl.BlockSpec(memory_space=pl.ANY)] * 2
            args += [new_k, new_v]
            n_alias = 2
    return pl.pallas_call(
        functools.partial(_inproj_kernel, rope=rope, emit_kv=emit_kv, n_alias=n_alias),
        grid=(n_tiles,),
        in_specs=in_specs,
        out_specs=out_specs,
        out_shape=out_shape,
        input_output_aliases=aliases,
        compiler_params=_cparams(("arbitrary",)),
        name="in_projection_rope" if rope else "in_projection",
    )(*args)


def _lambda(lam_ref, lam_init):
    lv = lam_ref[...]
    a = jnp.sum(lv[0:1] * lv[1:2], axis=-1, keepdims=True)
    b = jnp.sum(lv[2:3] * lv[3:4], axis=-1, keepdims=True)
    return jnp.exp(a) - jnp.exp(b) + lam_init


def _stack_masked(q):
    first = lax.broadcasted_iota(jnp.int32, q.shape, 1) < HEAD_DIM
    zero = jnp.zeros_like(q)
    return jnp.concatenate([jnp.where(first, q, zero), jnp.where(first, zero, q)], axis=0)


def _scores_chunk(k, qq, m):
    s = lax.dot_general(k, qq, (((1,), (1,)), ((), ())), preferred_element_type=F32)
    mc = jnp.max(s, axis=0, keepdims=True)
    return s, (mc if m is None else jnp.maximum(m, mc))


def _values_chunk(s, m, v_t, acc):
    e = jnp.exp2(s - m)
    lc = jnp.sum(e, axis=0, keepdims=True)
    oc = jnp.dot(v_t, e.astype(BF16), preferred_element_type=F32)
    return (lc, oc) if acc is None else (acc[0] + lc, acc[1] + oc)


def _combine(acc, lam):
    l, o_t = acc
    tq = l.shape[1] // 2
    r = 1.0 / l
    y_t = o_t[:, :tq] * r[:, :tq] - o_t[:, tq:] * (r[:, tq:] * lam)
    return y_t.T


def _subln_gate(o, sg, sgc, lam_init):
    ms = jnp.mean(o * o, axis=-1, keepdims=True)
    y = o * lax.rsqrt(ms + EPS) * sg * (1.0 - lam_init)
    return (y * sgc.astype(F32)).astype(BF16)


def _attn_ctx_kernel(lam_ref, sg_ref, q_ref, k_ref, v_ref, sgc_ref, o_ref, *, lam_init):
    lam = _lambda(lam_ref, lam_init)
    sg = sg_ref[...]
    seq = q_ref.shape[0]
    for hd in range(N_HEADS):
        sl = slice(hd * QK_W, (hd + 1) * QK_W)
        k = k_ref[:, sl]
        v_t = v_ref[:, sl].astype(F32).T.astype(BF16)
        for t in range(seq // CTX_Q_TILE):
            rows = slice(t * CTX_Q_TILE, (t + 1) * CTX_Q_TILE)
            s, m = _scores_chunk(k, _stack_masked(q_ref[rows, sl]), None)
            o = _combine(_values_chunk(s, m, v_t, None), lam)
            o_ref[rows, sl] = _subln_gate(o, sg, sgc_ref[rows, sl], lam_init)


def _attention_ctx(q, k, v, sgc, lam_vec, subln_g, *, seq, lam_init):
    n_tok = q.shape[0]
    blk = pl.BlockSpec((seq, ATTN_W), lambda b: (b, 0))
    return pl.pallas_call(
        functools.partial(_attn_ctx_kernel, lam_init=lam_init),
        grid=(n_tok // seq,),
        in_specs=[
            pl.BlockSpec((4, HEAD_DIM), lambda b: (0, 0)),
            pl.BlockSpec((1, QK_W), lambda b: (0, 0)),
            blk, blk, blk, blk,
        ],
        out_specs=blk,
        out_shape=jax.ShapeDtypeStruct((n_tok, ATTN_W), BF16),
        compiler_params=_cparams(("arbitrary",)),
        name="attention_ctx",
    )(lam_vec, subln_g.reshape(1, QK_W), q, k, v, sgc)


def _attn_den_kernel(lam_ref, sg_ref, q_ref, kn_ref, vn_ref, kc_ref, vc_ref, sgc_ref, o_ref,
                     kcs_ref, vt_ref, sa_ref, sb_ref, *, lam_init):
    seq = q_ref.shape[0]
    past = kcs_ref.shape[1]
    past_chunks = past // KEY_CHUNK
    n_chunks = (past + seq) // KEY_CHUNK
    lam = _lambda(lam_ref, lam_init)
    sg = sg_ref[...]
    for hd in range(N_HEADS):
        sl = slice(hd * QK_W, (hd + 1) * QK_W)
        kcs_ref[hd] = kc_ref[0, 0, :, hd, :].astype(BF16)
        vt_ref[hd, :, :past] = vc_ref[0, 0, :, hd, :].T.astype(BF16)
        vt_ref[hd, :, past:] = vn_ref[:, sl].astype(F32).T.astype(BF16)

        def get_k(c):
            if c < past_chunks:
                return kcs_ref[hd, c * KEY_CHUNK:(c + 1) * KEY_CHUNK, :]
            c -= past_chunks
            return kn_ref[c * KEY_CHUNK:(c + 1) * KEY_CHUNK, sl]

        def get_vt(c):
            return vt_ref[hd, :, c * KEY_CHUNK:(c + 1) * KEY_CHUNK]

        def tile_rows(t):
            if isinstance(t, int):
                return pl.ds(t * Q_TILE, Q_TILE)
            return pl.ds(pl.multiple_of(t * Q_TILE, Q_TILE), Q_TILE)

        def chunk(c):
            return pl.ds(c * KEY_CHUNK, KEY_CHUNK)

        def step(t, m_cur, s_cur, s_nxt, do_scores=True, do_values=True):
            qq = _stack_masked(q_ref[tile_rows(t + 1), sl]) if do_scores else None
            m_nxt = None
            acc = None
            for c in range(n_chunks):
                if do_scores:
                    s, m_nxt = _scores_chunk(get_k(c), qq, m_nxt)
                    s_nxt[chunk(c), :] = s
                if do_values:
                    acc = _values_chunk(s_cur[chunk(c), :], m_cur, get_vt(c), acc)
            if do_values:
                rows = tile_rows(t)
                o_ref[rows, sl] = _subln_gate(_combine(acc, lam), sg, sgc_ref[rows, sl], lam_init)
            return m_nxt

        def tile_pair(j, m):
            m = step(2 * j, m, sa_ref, sb_ref)
            return step(2 * j + 1, m, sb_ref, sa_ref)

        n_tiles = seq // Q_TILE
        m = step(-1, None, None, sa_ref, do_values=False)
        m = lax.fori_loop(0, n_tiles // 2 - 1, tile_pair, m)
        m = step(n_tiles - 2, m, sa_ref, sb_ref)
        step(n_tiles - 1, m, sb_ref, None, do_scores=False)


def _attention_den(q, k, v, sgc, cache_k, cache_v, lam_vec, subln_g, *, layer, seq, lam_init):
    n_tok = q.shape[0]
    past = cache_k.shape[2]
    blk = pl.BlockSpec((seq, ATTN_W), lambda b: (b, 0))
    cblk = pl.BlockSpec((1, 1, past, N_HEADS, QK_W), lambda b: (b, layer, 0, 0, 0))
    return pl.pallas_call(
        functools.partial(_attn_den_kernel, lam_init=lam_init),
        grid=(n_tok // seq,),
        in_specs=[
            pl.BlockSpec((4, HEAD_DIM), lambda b: (0, 0)),
            pl.BlockSpec((1, QK_W), lambda b: (0, 0)),
            blk, blk, blk, cblk, cblk, blk,
        ],
        out_specs=blk,
        out_shape=jax.ShapeDtypeStruct((n_tok, ATTN_W), BF16),
        scratch_shapes=[
            pltpu.VMEM((N_HEADS, past, QK_W), BF16),
            pltpu.VMEM((N_HEADS, QK_W, past + seq), BF16),
            pltpu.VMEM((past + seq, 2 * Q_TILE), F32),
            pltpu.VMEM((past + seq, 2 * Q_TILE), F32),
        ],
        compiler_params=_cparams(("arbitrary",)),
        name="attention_den",
    )(lam_vec, subln_g.reshape(1, QK_W), q, k, v, cache_k, cache_v, sgc)


def _channel_dft(fa, cs_ref):
    ab = jnp.dot(fa, cs_ref[...], preferred_element_type=F32).astype(BF16)
    return jnp.concatenate([ab[:, :FOURIER_W], ab[:, FOURIER_W:]], axis=0)


def _fourier_ctx_kernel(fa_ref, sga_ref, cs_ref, dn_ref, o_ref):
    ab2 = _channel_dft(fa_ref[...], cs_ref)
    f = jnp.dot(dn_ref[...], ab2, preferred_element_type=F32)
    o_ref[...] = (f * sga_ref[...].astype(F32)).astype(BF16)


def _fourier_ctx(fa, sga, cs, dn, *, seq):
    n_tok = fa.shape[0]
    blk = pl.BlockSpec((seq, FOURIER_W), lambda b: (b, 0))
    return pl.pallas_call(
        _fourier_ctx_kernel,
        grid=(n_tok // seq,),
        in_specs=[blk, blk,
                  pl.BlockSpec((FOURIER_W, 2 * FOURIER_W), lambda b: (0, 0)),
                  pl.BlockSpec((seq, 2 * seq), lambda b: (0, 0))],
        out_specs=blk,
        out_shape=jax.ShapeDtypeStruct((n_tok, FOURIER_W), BF16),
        compiler_params=_cparams(("arbitrary",)),
        name="fourier_ctx",
    )(fa, sga, cs, dn)


def _fourier_den_kernel(fa_ref, sga_ref, cs_ref, dn_ref, o_ref, ab_ref):
    @pl.when(pl.program_id(1) == 0)
    def _():
        ab_ref[...] = _channel_dft(fa_ref[...], cs_ref)

    f = jnp.dot(dn_ref[...], ab_ref[...], preferred_element_type=F32)
    o_ref[...] = (f * sga_ref[...].astype(F32)).astype(BF16)


def _fourier_den(fa, sga, cs, dn, *, seq):
    n_tok = fa.shape[0]
    tm = TOKEN_TILE
    tiles = seq // tm
    return pl.pallas_call(
        _fourier_den_kernel,
        grid=(n_tok // seq, tiles),
        in_specs=[
            pl.BlockSpec((seq, FOURIER_W), lambda b, j: (b, 0)),
            pl.BlockSpec((tm, FOURIER_W), lambda b, j: (b * tiles + j, 0)),
            pl.BlockSpec((FOURIER_W, 2 * FOURIER_W), lambda b, j: (0, 0)),
            pl.BlockSpec((tm, 2 * seq), lambda b, j: (j, 0)),
        ],
        out_specs=pl.BlockSpec((tm, FOURIER_W), lambda b, j: (b * tiles + j, 0)),
        out_shape=jax.ShapeDtypeStruct((n_tok, FOURIER_W), BF16),
        scratch_shapes=[pltpu.VMEM((2 * seq, FOURIER_W), BF16)],
        compiler_params=_cparams(("arbitrary", "arbitrary")),
        name="fourier_den",
    )(fa, sga, cs, dn)


def _outproj_kernel(x_ref, mod_ref, ya_ref, zc_ref, zp_ref, zn_ref, bgs_ref, yc_ref,
                    cw_ref, cb_ref, w_ref, fg_ref, o_ref, *, seq, final_norm):
    tm = x_ref.shape[0]
    z = zc_ref[...].astype(F32)
    row = lax.broadcasted_iota(jnp.int32, z.shape, 0)
    pos = (pl.program_id(0) * tm + row) & (seq - 1)
    prev_row = zp_ref[HALO_ROWS - 1:HALO_ROWS, :].astype(F32)
    next_row = zn_ref[0:1, :].astype(F32)
    z_prev = jnp.where(row == 0, prev_row, pltpu.roll(z, 1, axis=0))
    z_prev = jnp.where(pos == 0, 0.0, z_prev)
    z_next = jnp.where(row == tm - 1, next_row, pltpu.roll(z, tm - 1, axis=0))
    z_next = jnp.where(pos == seq - 1, 0.0, z_next)
    cw = cw_ref[...]
    conv = z_prev * cw[0:1] + z * cw[1:2] + z_next * cw[2:3] + cb_ref[...]
    yb = (bgs_ref[...].astype(F32) * conv).astype(BF16)

    mixed = jnp.concatenate([ya_ref[...], yb, yc_ref[...]], axis=-1)
    out = jnp.dot(mixed, w_ref[...], preferred_element_type=F32)
    gate = mod_ref[0][:, 2 * D_MODEL:]
    xn = x_ref[...] + gate * out
    if final_norm:
        ms = jnp.mean(xn * xn, axis=-1, keepdims=True)
        xn = xn * lax.rsqrt(ms + EPS) * fg_ref[...]
    o_ref[...] = xn


def _out_projection(x, mod_rows, ya, zc, bgs, yc, conv_w, conv_b, w_out_bf16, final_g,
                    *, seq, tiles_per_mod, final_norm):
    n_tok = x.shape[0]
    tm = TOKEN_TILE
    n_tiles = n_tok // tm
    halo_per_tile = tm // HALO_ROWS
    n_halo = n_tok // HALO_ROWS
    row = lambda w: pl.BlockSpec((tm, w), lambda i: (i, 0))
    return pl.pallas_call(
        functools.partial(_outproj_kernel, seq=seq, final_norm=final_norm),
        grid=(n_tiles,),
        in_specs=[
            row(D_MODEL),
            pl.BlockSpec((1, 1, 3 * D_MODEL), lambda i: (i // tiles_per_mod, 0, 0)),
            row(FOURIER_W),
            row(CONV_W),
            pl.BlockSpec((HALO_ROWS, CONV_W), lambda i: (jnp.maximum(i * halo_per_tile - 1, 0), 0)),
            pl.BlockSpec((HALO_ROWS, CONV_W),
                         lambda i: (jnp.minimum((i + 1) * halo_per_tile, n_halo - 1), 0)),
            row(CONV_W),
            row(ATTN_W),
            pl.BlockSpec((3, CONV_W), lambda i: (0, 0)),
            pl.BlockSpec((1, CONV_W), lambda i: (0, 0)),
            pl.BlockSpec((D_MODEL, D_MODEL), lambda i: (0, 0)),
            pl.BlockSpec((1, D_MODEL), lambda i: (0, 0)),
        ],
        out_specs=row(D_MODEL),
        out_shape=jax.ShapeDtypeStruct((n_tok, D_MODEL), F32),
        compiler_params=_cparams(("arbitrary",)),
        name="out_projection",
    )(x, mod_rows, ya, zc, zc, zc, bgs, yc, conv_w, conv_b.reshape(1, CONV_W), w_out_bf16,
      final_g.reshape(1, D_MODEL))


def kernel(x_prompt, x_sample, cache_k, cache_v, c, c_ctx, norm_g, w_mod, b_mod, w_in, conv_w,
           conv_b, lam_vec, subln_g, w_out, final_g):
    batch, seq, _ = x_prompt.shape
    dec_batch, dec_seq, _ = x_sample.shape

    cvecs = jnp.concatenate(
        [c_ctx[None, :], c, jnp.zeros((MOD_ROWS - 1 - dec_batch, D_MODEL), F32)], axis=0)
    mod = _modulation(cvecs, w_mod, b_mod)

    w_in_b = w_in.astype(BF16)
    w_out_b = w_out.astype(BF16)
    rope_tabs = tuple(jnp.asarray(t) for t in _rope_tables(dec_seq))
    cs = jnp.asarray(_chan_tables()).astype(BF16)
    dn_ctx = jnp.asarray(_dft_tables(seq)).astype(BF16)
    dn_den = jnp.asarray(_dft_tables(dec_seq)).astype(BF16)

    xc = x_prompt.reshape(batch * seq, D_MODEL)
    xl = x_sample.reshape(dec_batch * dec_seq, D_MODEL)
    new_k = new_v = None
    for l in range(DEPTH):
        lam_init = 0.8 - 0.6 * math.exp(-0.3 * l)
        last = l == DEPTH - 1

        mod_c = mod[l, 0:1][:, None, :]
        fa, sga, zc, bgs, q, k, v, sgc, new_k, new_v = _in_projection(
            xc, mod_c, norm_g[l], w_in_b[l], None, tiles_per_mod=xc.shape[0] // TOKEN_TILE,
            kv_out=(l, seq, new_k, new_v))
        yc = _attention_ctx(q, k, v, sgc, lam_vec[l], subln_g[l], seq=seq, lam_init=lam_init)
        ya = _fourier_ctx(fa, sga, cs, dn_ctx, seq=seq)
        xc = _out_projection(xc, mod_c, ya, zc, bgs, yc, conv_w[l], conv_b[l], w_out_b[l], final_g,
                             seq=seq, tiles_per_mod=xc.shape[0] // TOKEN_TILE, final_norm=last)

        mod_d = mod[l, 1:1 + dec_batch][:, None, :]
        fa, sga, zc, bgs, q, k, v, sgc = _in_projection(
            xl, mod_d, norm_g[l], w_in_b[l], rope_tabs, tiles_per_mod=dec_seq // TOKEN_TILE)
        yc = _attention_den(q, k, v, sgc, cache_k, cache_v, lam_vec[l], subln_g[l], layer=l,
                            seq=dec_seq, lam_init=lam_init)
        ya = _fourier_den(fa, sga, cs, dn_den, seq=dec_seq)
        xl = _out_projection(xl, mod_d, ya, zc, bgs, yc, conv_w[l], conv_b[l], w_out_b[l], final_g,
                             seq=dec_seq, tiles_per_mod=dec_seq // TOKEN_TILE, final_norm=last)

    y_prompt = xc.reshape(batch, seq, D_MODEL)
    y_sample = xl.reshape(dec_batch, dec_seq, D_MODEL)
    return (y_prompt, y_sample, new_k, new_v)
```

```python
import functools
import math

import numpy as np
import jax
import jax.numpy as jnp
from jax import lax
from jax.experimental import pallas as pl
from jax.experimental.pallas import tpu as pltpu

D_MODEL = 1024
DEPTH = 2
GRID_W = 64
FOURIER_W = 256
CONV_W = 256
ATTN_W = 512
N_HEADS = 4
HEAD_DIM = 64
QK_W = 128
ROPE_BASE = 10000.0
EPS = 1e-6
IN_DIM = 3584

F32 = jnp.float32
BF16 = jnp.bfloat16

VMEM_LIMIT_BYTES = 56 * 1024 * 1024
TOKEN_TILE = 512
Q_TILE = 256
CTX_Q_TILE = 256
KEY_CHUNK = 256
ONES_ROWS = 16
HALO_ROWS = 16
MOD_ROWS = 8
LOG2E = 1.4426950408889634

_COLS = {}
_off = 0
for _name, _w in (("fa", 256), ("ga", 256), ("bg", 256), ("cg", 256), ("hc", 256), ("gb", 256),
                  ("q", 512), ("k", 512), ("v", 512), ("gc", 512)):
    _COLS[_name] = (_off, _off + _w)
    _off += _w


def _silu(x):
    return x * (1.0 / (1.0 + jnp.exp(-x)))


def _cparams(sem):
    return pltpu.CompilerParams(dimension_semantics=sem, vmem_limit_bytes=VMEM_LIMIT_BYTES)


def _rope_tables(n_tokens):
    n = np.arange(n_tokens)
    row = (n // GRID_W).astype(np.float64)
    col = (n % GRID_W).astype(np.float64)
    j = np.arange(QK_W)
    jj = j % HEAD_DIM
    idx = jj % 32
    inv = 1.0 / (ROPE_BASE ** (2.0 * (idx % 16) / 32.0))
    pos = np.where((jj < 32)[None, :], row[:, None], col[:, None])
    ang = pos * inv[None, :]
    cos = np.cos(ang)
    sin = np.sin(ang)
    first = (idx < 16)[None, :]
    sin_a = np.where(first, -sin, 0.0)
    sin_b = np.where(first, 0.0, sin)
    return (np.asarray(cos, np.float32), np.asarray(sin_a, np.float32), np.asarray(sin_b, np.float32))


def _dft_tables(n):
    k = np.arange(n)
    kn = (k[:, None] * k[None, :]) % n
    ang = 2.0 * np.pi * kn / n
    return np.asarray(np.concatenate([np.cos(ang), -np.sin(ang)], axis=1) / math.sqrt(n), np.float32)


def _chan_tables():
    k = np.arange(FOURIER_W)
    kn = (k[:, None] * k[None, :]) % FOURIER_W
    ang = 2.0 * np.pi * kn / FOURIER_W
    return np.asarray(np.concatenate([np.cos(ang), np.sin(ang)], axis=1) / math.sqrt(FOURIER_W), np.float32)


def _mod_kernel(c_ref, w_ref, b_ref, o_ref):
    s = _silu(c_ref[...]).astype(BF16)
    w = w_ref[0].astype(BF16)
    o_ref[0] = jnp.dot(s, w, preferred_element_type=F32) + b_ref[0]


def _modulation(cvecs, w_mod, b_mod):
    chunk = D_MODEL
    n_chunks = 3 * D_MODEL // chunk
    return pl.pallas_call(
        _mod_kernel,
        grid=(DEPTH, n_chunks),
        in_specs=[
            pl.BlockSpec((MOD_ROWS, D_MODEL), lambda l, j: (0, 0)),
            pl.BlockSpec((1, D_MODEL, chunk), lambda l, j: (l, 0, j)),
            pl.BlockSpec((1, 1, chunk), lambda l, j: (l, 0, j)),
        ],
        out_specs=pl.BlockSpec((1, MOD_ROWS, chunk), lambda l, j: (l, 0, j)),
        out_shape=jax.ShapeDtypeStruct((DEPTH, MOD_ROWS, 3 * D_MODEL), F32),
        compiler_params=_cparams(("arbitrary", "arbitrary")),
        name="modulation",
    )(cvecs, w_mod, b_mod.reshape(DEPTH, 1, 3 * D_MODEL))


def _store_heads(ref, t):
    seqs, _, seq, _, _ = ref.shape
    for s in range(seqs):
        for hd in range(N_HEADS):
            ref[s, 0, :, hd, :] = t[s * seq:(s + 1) * seq, hd * QK_W:(hd + 1) * QK_W]


def _inproj_kernel(*refs, rope, emit_kv, n_alias):
    n_in = 4 + (3 if rope else 0)
    x_ref, mod_ref, g_ref, w_ref = refs[:4]
    if rope:
        cos_ref, sa_ref, sb_ref = refs[4:7]
    outs = refs[n_in + n_alias:]
    fa_ref, sga_ref, zc_ref, bgs_ref, q_ref, k_ref, v_ref, sgc_ref = outs[:8]

    x = x_ref[...]
    ms = jnp.mean(x * x, axis=-1, keepdims=True)
    y = x * lax.rsqrt(ms + EPS) * g_ref[...]
    m = mod_ref[0]
    h = (y * (1.0 + m[:, D_MODEL:2 * D_MODEL]) + m[:, :D_MODEL]).astype(BF16)

    def proj(name):
        lo, hi = _COLS[name]
        return jnp.dot(h, w_ref[:, lo:hi], preferred_element_type=F32)

    fa_ref[...] = proj("fa").astype(BF16)
    sga_ref[...] = _silu(proj("ga")).astype(BF16)
    zc_ref[...] = (proj("cg") * proj("hc")).astype(BF16)
    bgs_ref[...] = (proj("bg") * _silu(proj("gb"))).astype(BF16)
    sgc_ref[...] = _silu(proj("gc")).astype(BF16)

    q = proj("q")
    k = proj("k")
    v = proj("v")
    if emit_kv:
        _store_heads(outs[8], k)
        _store_heads(outs[9], v)
    v_ref[...] = v.astype(BF16)

    q_scale = HEAD_DIM ** -0.5 * LOG2E
    if rope:
        cos = cos_ref[...]
        sa = sa_ref[...]
        sb = sb_ref[...]
        for hd in range(N_HEADS):
            sl = slice(hd * QK_W, (hd + 1) * QK_W)
            for t, ref, scale in ((q, q_ref, q_scale), (k, k_ref, None)):
                th = t[:, sl]
                r = (th * cos + pltpu.roll(th, QK_W - 16, axis=1) * sa
                     + pltpu.roll(th, 16, axis=1) * sb)
                if scale is not None:
                    r = r * scale
                ref[:, sl] = r.astype(BF16)
    else:
        q_ref[...] = (q * q_scale).astype(BF16)
        k_ref[...] = k.astype(BF16)


def _in_projection(x, mod_rows, norm_g, w_in_bf16, rope_tabs, *, tiles_per_mod, kv_out=None):
    n_tok = x.shape[0]
    tm = TOKEN_TILE
    n_tiles = n_tok // tm
    rope = rope_tabs is not None
    emit_kv = kv_out is not None
    row = lambda w: pl.BlockSpec((tm, w), lambda i: (i, 0))
    in_specs = [
        row(D_MODEL),
        pl.BlockSpec((1, 1, 3 * D_MODEL), lambda i: (i // tiles_per_mod, 0, 0)),
        pl.BlockSpec((1, D_MODEL), lambda i: (0, 0)),
        pl.BlockSpec((D_MODEL, IN_DIM), lambda i: (0, 0)),
    ]
    args = [x, mod_rows, norm_g.reshape(1, D_MODEL), w_in_bf16]
    if rope:
        seq_tiles = rope_tabs[0].shape[0] // tm
        tab = pl.BlockSpec((tm, QK_W), lambda i: (i % seq_tiles, 0))
        in_specs += [tab, tab, tab]
        args += list(rope_tabs)
    widths = (256, 256, 256, 256, 512, 512, 512, 512)
    out_specs = [row(w) for w in widths]
    out_shape = [jax.ShapeDtypeStruct((n_tok, w), BF16) for w in widths]
    aliases = {}
    n_alias = 0
    if emit_kv:
        layer, seq, new_k, new_v = kv_out
        seqs = tm // seq
        kv_blk = pl.BlockSpec((seqs, 1, seq, N_HEADS, QK_W), lambda i: (i, layer, 0, 0, 0))
        out_specs += [kv_blk, kv_blk]
        out_shape += [jax.ShapeDtypeStruct((n_tok // seq, DEPTH, seq, N_HEADS, QK_W), F32)] * 2
        if new_k is not None:
            aliases = {len(args): len(widths), len(args) + 1: len(widths) + 1}
            in_specs += [pl.BlockSpec(memory_space=pl.ANY)] * 2
            args += [new_k, new_v]
            n_alias = 2
    return pl.pallas_call(
        functools.partial(_inproj_kernel, rope=rope, emit_kv=emit_kv, n_alias=n_alias),
        grid=(n_tiles,),
        in_specs=in_specs,
        out_specs=out_specs,
        out_shape=out_shape,
        input_output_aliases=aliases,
        compiler_params=_cparams(("arbitrary",)),
        name="in_projection_rope" if rope else "in_projection",
    )(*args)


def _lambda(lam_ref, lam_init):
    lv = lam_ref[...]
    a = jnp.sum(lv[0:1] * lv[1:2], axis=-1, keepdims=True)
    b = jnp.sum(lv[2:3] * lv[3:4], axis=-1, keepdims=True)
    return jnp.exp(a) - jnp.exp(b) + lam_init


def _stack_masked_t(q):
    q_t = q.astype(F32).T
    first = lax.broadcasted_iota(jnp.int32, q_t.shape, 0) < HEAD_DIM
    return jnp.concatenate([jnp.where(first, q_t, 0.0), jnp.where(first, 0.0, q_t)],
                           axis=1).astype(BF16)


def _scores_chunk(k, qq_t, m):
    s = jnp.dot(k, qq_t, preferred_element_type=F32)
    mc = jnp.max(s, axis=0, keepdims=True)
    return s, (mc if m is None else jnp.maximum(m, mc))


def _with_ones_rows(v_t):
    return jnp.concatenate([v_t, jnp.ones((ONES_ROWS, v_t.shape[1]), F32)], axis=0).astype(BF16)


def _values_chunk(s, m, v1_t, acc):
    e = jnp.exp2(s - m).astype(BF16)
    oc = jnp.dot(v1_t, e, preferred_element_type=F32)
    return oc if acc is None else acc + oc


def _combine(acc, lam):
    tq = acc.shape[1] // 2
    r = 1.0 / acc[QK_W:QK_W + 1, :]
    o_t = acc[:QK_W, :]
    y_t = o_t[:, :tq] * r[:, :tq] - o_t[:, tq:] * (r[:, tq:] * lam)
    return y_t.T


def _subln_gate(o, sg, sgc, lam_init):
    ms = jnp.mean(o * o, axis=-1, keepdims=True)
    y = o * lax.rsqrt(ms + EPS) * sg * (1.0 - lam_init)
    return (y * sgc.astype(F32)).astype(BF16)


def _attn_ctx_kernel(lam_ref, sg_ref, q_ref, k_ref, v_ref, sgc_ref, o_ref, *, lam_init):
    lam = _lambda(lam_ref, lam_init)
    sg = sg_ref[...]
    seq = q_ref.shape[0]
    for hd in range(N_HEADS):
        sl = slice(hd * QK_W, (hd + 1) * QK_W)
        k = k_ref[:, sl]
        v_t = _with_ones_rows(v_ref[:, sl].astype(F32).T)
        for t in range(seq // CTX_Q_TILE):
            rows = slice(t * CTX_Q_TILE, (t + 1) * CTX_Q_TILE)
            s, m = _scores_chunk(k, _stack_masked_t(q_ref[rows, sl]), None)
            o = _combine(_values_chunk(s, m, v_t, None), lam)
            o_ref[rows, sl] = _subln_gate(o, sg, sgc_ref[rows, sl], lam_init)


def _attention_ctx(q, k, v, sgc, lam_vec, subln_g, *, seq, lam_init):
    n_tok = q.shape[0]
    blk = pl.BlockSpec((seq, ATTN_W), lambda b: (b, 0))
    return pl.pallas_call(
        functools.partial(_attn_ctx_kernel, lam_init=lam_init),
        grid=(n_tok // seq,),
        in_specs=[
            pl.BlockSpec((4, HEAD_DIM), lambda b: (0, 0)),
            pl.BlockSpec((1, QK_W), lambda b: (0, 0)),
            blk, blk, blk, blk,
        ],
        out_specs=blk,
        out_shape=jax.ShapeDtypeStruct((n_tok, ATTN_W), BF16),
        compiler_params=_cparams(("arbitrary",)),
        name="attention_ctx",
    )(lam_vec, subln_g.reshape(1, QK_W), q, k, v, sgc)


def _attn_den_kernel(lam_ref, sg_ref, q_ref, kn_ref, vn_ref, kc_ref, vc_ref, sgc_ref, o_ref,
                     kcs_ref, vt_ref, sa_ref, sb_ref, *, lam_init):
    seq = q_ref.shape[0]
    past = kcs_ref.shape[1]
    past_chunks = past // KEY_CHUNK
    n_chunks = (past + seq) // KEY_CHUNK
    lam = _lambda(lam_ref, lam_init)
    sg = sg_ref[...]
    for hd in range(N_HEADS):
        sl = slice(hd * QK_W, (hd + 1) * QK_W)
        kcs_ref[hd] = kc_ref[0, 0, :, hd, :].astype(BF16)
        vt_ref[hd, :, :past] = _with_ones_rows(vc_ref[0, 0, :, hd, :].T)
        vt_ref[hd, :, past:] = _with_ones_rows(vn_ref[:, sl].astype(F32).T)

        def get_k(c):
            if c < past_chunks:
                return kcs_ref[hd, c * KEY_CHUNK:(c + 1) * KEY_CHUNK, :]
            c -= past_chunks
            return kn_ref[c * KEY_CHUNK:(c + 1) * KEY_CHUNK, sl]

        def get_vt(c):
            return vt_ref[hd, :, c * KEY_CHUNK:(c + 1) * KEY_CHUNK]

        def tile_rows(t):
            if isinstance(t, int):
                return pl.ds(t * Q_TILE, Q_TILE)
            return pl.ds(pl.multiple_of(t * Q_TILE, Q_TILE), Q_TILE)

        def chunk(c):
            return pl.ds(c * KEY_CHUNK, KEY_CHUNK)

        def step(t, m_cur, s_cur, s_nxt, do_scores=True, do_values=True):
            qq = _stack_masked_t(q_ref[tile_rows(t + 1), sl]) if do_scores else None
            m_nxt = None
            acc = None
            for c in range(n_chunks):
                if do_scores:
                    s, m_nxt = _scores_chunk(get_k(c), qq, m_nxt)
                    s_nxt[chunk(c), :] = s
                if do_values:
                    acc = _values_chunk(s_cur[chunk(c), :], m_cur, get_vt(c), acc)
            if do_values:
                rows = tile_rows(t)
                o_ref[rows, sl] = _subln_gate(_combine(acc, lam), sg, sgc_ref[rows, sl], lam_init)
            return m_nxt

        def tile_pair(j, m):
            m = step(2 * j, m, sa_ref, sb_ref)
            return step(2 * j + 1, m, sb_ref, sa_ref)

        n_tiles = seq // Q_TILE
        m = step(-1, None, None, sa_ref, do_values=False)
        m = lax.fori_loop(0, n_tiles // 2 - 1, tile_pair, m)
        m = step(n_tiles - 2, m, sa_ref, sb_ref)
        step(n_tiles - 1, m, sb_ref, None, do_scores=False)


def _attention_den(q, k, v, sgc, cache_k, cache_v, lam_vec, subln_g, *, layer, seq, lam_init):
    n_tok = q.shape[0]
    past = cache_k.shape[2]
    blk = pl.BlockSpec((seq, ATTN_W), lambda b: (b, 0))
    cblk = pl.BlockSpec((1, 1, past, N_HEADS, QK_W), lambda b: (b, layer, 0, 0, 0))
    return pl.pallas_call(
        functools.partial(_attn_den_kernel, lam_init=lam_init),
        grid=(n_tok // seq,),
        in_specs=[
            pl.BlockSpec((4, HEAD_DIM), lambda b: (0, 0)),
            pl.BlockSpec((1, QK_W), lambda b: (0, 0)),
            blk, blk, blk, cblk, cblk, blk,
        ],
        out_specs=blk,
        out_shape=jax.ShapeDtypeStruct((n_tok, ATTN_W), BF16),
        scratch_shapes=[
            pltpu.VMEM((N_HEADS, past, QK_W), BF16),
            pltpu.VMEM((N_HEADS, QK_W + ONES_ROWS, past + seq), BF16),
            pltpu.VMEM((past + seq, 2 * Q_TILE), F32),
            pltpu.VMEM((past + seq, 2 * Q_TILE), F32),
        ],
        compiler_params=_cparams(("arbitrary",)),
        name="attention_den",
    )(lam_vec, subln_g.reshape(1, QK_W), q, k, v, cache_k, cache_v, sgc)


def _channel_dft(fa, cs_ref):
    ab = jnp.dot(fa, cs_ref[...], preferred_element_type=F32).astype(BF16)
    return jnp.concatenate([ab[:, :FOURIER_W], ab[:, FOURIER_W:]], axis=0)


def _fourier_ctx_kernel(fa_ref, sga_ref, cs_ref, dn_ref, o_ref):
    ab2 = _channel_dft(fa_ref[...], cs_ref)
    f = jnp.dot(dn_ref[...], ab2, preferred_element_type=F32)
    o_ref[...] = (f * sga_ref[...].astype(F32)).astype(BF16)


def _fourier_ctx(fa, sga, cs, dn, *, seq):
    n_tok = fa.shape[0]
    blk = pl.BlockSpec((seq, FOURIER_W), lambda b: (b, 0))
    return pl.pallas_call(
        _fourier_ctx_kernel,
        grid=(n_tok // seq,),
        in_specs=[blk, blk,
                  pl.BlockSpec((FOURIER_W, 2 * FOURIER_W), lambda b: (0, 0)),
                  pl.BlockSpec((seq, 2 * seq), lambda b: (0, 0))],
        out_specs=blk,
        out_shape=jax.ShapeDtypeStruct((n_tok, FOURIER_W), BF16),
        compiler_params=_cparams(("arbitrary",)),
        name="fourier_ctx",
    )(fa, sga, cs, dn)


def _fourier_den_kernel(fa_ref, sga_ref, cs_ref, dn_ref, o_ref, ab_ref):
    @pl.when(pl.program_id(1) == 0)
    def _():
        ab_ref[...] = _channel_dft(fa_ref[...], cs_ref)

    f = jnp.dot(dn_ref[...], ab_ref[...], preferred_element_type=F32)
    o_ref[...] = (f * sga_ref[...].astype(F32)).astype(BF16)


def _fourier_den(fa, sga, cs, dn, *, seq):
    n_tok = fa.shape[0]
    tm = TOKEN_TILE
    tiles = seq // tm
    return pl.pallas_call(
        _fourier_den_kernel,
        grid=(n_tok // seq, tiles),
        in_specs=[
            pl.BlockSpec((seq, FOURIER_W), lambda b, j: (b, 0)),
            pl.BlockSpec((tm, FOURIER_W), lambda b, j: (b * tiles + j, 0)),
            pl.BlockSpec((FOURIER_W, 2 * FOURIER_W), lambda b, j: (0, 0)),
            pl.BlockSpec((tm, 2 * seq), lambda b, j: (j, 0)),
        ],
        out_specs=pl.BlockSpec((tm, FOURIER_W), lambda b, j: (b * tiles + j, 0)),
        out_shape=jax.ShapeDtypeStruct((n_tok, FOURIER_W), BF16),
        scratch_shapes=[pltpu.VMEM((2 * seq, FOURIER_W), BF16)],
        compiler_params=_cparams(("arbitrary", "arbitrary")),
        name="fourier_den",
    )(fa, sga, cs, dn)


def _outproj_kernel(x_ref, mod_ref, ya_ref, zc_ref, zp_ref, zn_ref, bgs_ref, yc_ref,
                    cw_ref, cb_ref, w_ref, fg_ref, o_ref, *, seq, final_norm):
    tm = x_ref.shape[0]
    z = zc_ref[...].astype(F32)
    row = lax.broadcasted_iota(jnp.int32, z.shape, 0)
    pos = (pl.program_id(0) * tm + row) & (seq - 1)
    prev_row = zp_ref[HALO_ROWS - 1:HALO_ROWS, :].astype(F32)
    next_row = zn_ref[0:1, :].astype(F32)
    z_prev = jnp.where(row == 0, prev_row, pltpu.roll(z, 1, axis=0))
    z_prev = jnp.where(pos == 0, 0.0, z_prev)
    z_next = jnp.where(row == tm - 1, next_row, pltpu.roll(z, tm - 1, axis=0))
    z_next = jnp.where(pos == seq - 1, 0.0, z_next)
    cw = cw_ref[...]
    conv = z_prev * cw[0:1] + z * cw[1:2] + z_next * cw[2:3] + cb_ref[...]
    yb = (bgs_ref[...].astype(F32) * conv).astype(BF16)

    mixed = jnp.concatenate([ya_ref[...], yb, yc_ref[...]], axis=-1)
    out = jnp.dot(mixed, w_ref[...], preferred_element_type=F32)
    gate = mod_ref[0][:, 2 * D_MODEL:]
    xn = x_ref[...] + gate * out
    if final_norm:
        ms = jnp.mean(xn * xn, axis=-1, keepdims=True)
        xn = xn * lax.rsqrt(ms + EPS) * fg_ref[...]
    o_ref[...] = xn


def _out_projection(x, mod_rows, ya, zc, bgs, yc, conv_w, conv_b, w_out_bf16, final_g,
                    *, seq, tiles_per_mod, final_norm):
    n_tok = x.shape[0]
    tm = TOKEN_TILE
    n_tiles = n_tok // tm
    halo_per_tile = tm // HALO_ROWS
    n_halo = n_tok // HALO_ROWS
    row = lambda w: pl.BlockSpec((tm, w), lambda i: (i, 0))
    return pl.pallas_call(
        functools.partial(_outproj_kernel, seq=seq, final_norm=final_norm),
        grid=(n_tiles,),
        in_specs=[
            row(D_MODEL),
            pl.BlockSpec((1, 1, 3 * D_MODEL), lambda i: (i // tiles_per_mod, 0, 0)),
            row(FOURIER_W),
            row(CONV_W),
            pl.BlockSpec((HALO_ROWS, CONV_W), lambda i: (jnp.maximum(i * halo_per_tile - 1, 0), 0)),
            pl.BlockSpec((HALO_ROWS, CONV_W),
                         lambda i: (jnp.minimum((i + 1) * halo_per_tile, n_halo - 1), 0)),
            row(CONV_W),
            row(ATTN_W),
            pl.BlockSpec((3, CONV_W), lambda i: (0, 0)),
            pl.BlockSpec((1, CONV_W), lambda i: (0, 0)),
            pl.BlockSpec((D_MODEL, D_MODEL), lambda i: (0, 0)),
            pl.BlockSpec((1, D_MODEL), lambda i: (0, 0)),
        ],
        out_specs=row(D_MODEL),
        out_shape=jax.ShapeDtypeStruct((n_tok, D_MODEL), F32),
        compiler_params=_cparams(("arbitrary",)),
        name="out_projection",
    )(x, mod_rows, ya, zc, zc, zc, bgs, yc, conv_w, conv_b.reshape(1, CONV_W), w_out_bf16,
      final_g.reshape(1, D_MODEL))


def kernel(x_prompt, x_sample, cache_k, cache_v, c, c_ctx, norm_g, w_mod, b_mod, w_in, conv_w,
           conv_b, lam_vec, subln_g, w_out, final_g):
    batch, seq, _ = x_prompt.shape
    dec_batch, dec_seq, _ = x_sample.shape

    cvecs = jnp.concatenate(
        [c_ctx[None, :], c, jnp.zeros((MOD_ROWS - 1 - dec_batch, D_MODEL), F32)], axis=0)
    mod = _modulation(cvecs, w_mod, b_mod)

    w_in_b = w_in.astype(BF16)
    w_out_b = w_out.astype(BF16)
    rope_tabs = tuple(jnp.asarray(t) for t in _rope_tables(dec_seq))
    cs = jnp.asarray(_chan_tables()).astype(BF16)
    dn_ctx = jnp.asarray(_dft_tables(seq)).astype(BF16)
    dn_den = jnp.asarray(_dft_tables(dec_seq)).astype(BF16)

    xc = x_prompt.reshape(batch * seq, D_MODEL)
    xl = x_sample.reshape(dec_batch * dec_seq, D_MODEL)
    new_k = new_v = None
    for l in range(DEPTH):
        lam_init = 0.8 - 0.6 * math.exp(-0.3 * l)
        last = l == DEPTH - 1

        mod_c = mod[l, 0:1][:, None, :]
        fa, sga, zc, bgs, q, k, v, sgc, new_k, new_v = _in_projection(
            xc, mod_c, norm_g[l], w_in_b[l], None, tiles_per_mod=xc.shape[0] // TOKEN_TILE,
            kv_out=(l, seq, new_k, new_v))
        yc = _attention_ctx(q, k, v, sgc, lam_vec[l], subln_g[l], seq=seq, lam_init=lam_init)
        ya = _fourier_ctx(fa, sga, cs, dn_ctx, seq=seq)
        xc = _out_projection(xc, mod_c, ya, zc, bgs, yc, conv_w[l], conv_b[l], w_out_b[l], final_g,
                             seq=seq, tiles_per_mod=xc.shape[0] // TOKEN_TILE, final_norm=last)

        mod_d = mod[l, 1:1 + dec_batch][:, None, :]
        fa, sga, zc, bgs, q, k, v, sgc = _in_projection(
            xl, mod_d, norm_g[l], w_in_b[l], rope_tabs, tiles_per_mod=dec_seq // TOKEN_TILE)
        yc = _attention_den(q, k, v, sgc, cache_k, cache_v, lam_vec[l], subln_g[l], layer=l,
                            seq=dec_seq, lam_init=lam_init)
        ya = _fourier_den(fa, sga, cs, dn_den, seq=dec_seq)
        xl = _out_projection(xl, mod_d, ya, zc, bgs, yc, conv_w[l], conv_b[l], w_out_b[l], final_g,
                             seq=dec_seq, tiles_per_mod=dec_seq // TOKEN_TILE, final_norm=last)

    y_prompt = xc.reshape(batch, seq, D_MODEL)
    y_sample = xl.reshape(dec_batch, dec_seq, D_MODEL)
    return (y_prompt, y_sample, new_k, new_v)
```

```python
import functools
import math

import numpy as np
import jax
import jax.numpy as jnp
from jax import lax
from jax.experimental import pallas as pl
from jax.experimental.pallas import tpu as pltpu

D_MODEL = 1024
DEPTH = 2
GRID_W = 64
FOURIER_W = 256
CONV_W = 256
ATTN_W = 512
N_HEADS = 4
HEAD_DIM = 64
QK_W = 128
ROPE_BASE = 10000.0
EPS = 1e-6
IN_DIM = 3584

F32 = jnp.float32
BF16 = jnp.bfloat16

VMEM_LIMIT_BYTES = 56 * 1024 * 1024
TOKEN_TILE = 512
Q_TILE = 256
CTX_Q_TILE = 256
KEY_CHUNK = 256
ONES_ROWS = 16
LANES = 128
FFT_RADIX = 8
FFT_ROWS = 16
SQRT_HALF = 0.7071067811865476
CTX_FOURIER_SEQS = 4
HALO_ROWS = 16
MOD_ROWS = 8
LOG2E = 1.4426950408889634

_COLS = {}
_off = 0
for _name, _w in (("fa", 256), ("ga", 256), ("bg", 256), ("cg", 256), ("hc", 256), ("gb", 256),
                  ("q", 512), ("k", 512), ("v", 512), ("gc", 512)):
    _COLS[_name] = (_off, _off + _w)
    _off += _w


def _silu(x):
    return x * (1.0 / (1.0 + jnp.exp(-x)))


def _cparams(sem):
    return pltpu.CompilerParams(dimension_semantics=sem, vmem_limit_bytes=VMEM_LIMIT_BYTES)


def _rope_tables(n_tokens):
    n = np.arange(n_tokens)
    row = (n // GRID_W).astype(np.float64)
    col = (n % GRID_W).astype(np.float64)
    j = np.arange(QK_W)
    jj = j % HEAD_DIM
    idx = jj % 32
    inv = 1.0 / (ROPE_BASE ** (2.0 * (idx % 16) / 32.0))
    pos = np.where((jj < 32)[None, :], row[:, None], col[:, None])
    ang = pos * inv[None, :]
    cos = np.cos(ang)
    sin = np.sin(ang)
    first = (idx < 16)[None, :]
    sin_a = np.where(first, -sin, 0.0)
    sin_b = np.where(first, 0.0, sin)
    return (np.asarray(cos, np.float32), np.asarray(sin_a, np.float32), np.asarray(sin_b, np.float32))


def _dft_tables(n):
    k = np.arange(n)
    kn = (k[:, None] * k[None, :]) % n
    ang = 2.0 * np.pi * kn / n
    return np.asarray(np.concatenate([np.cos(ang), -np.sin(ang)], axis=1) / math.sqrt(n), np.float32)


def _chan_tables():
    k = np.arange(FOURIER_W)
    kn = (k[:, None] * k[None, :]) % FOURIER_W
    ang = 2.0 * np.pi * kn / FOURIER_W
    return np.asarray(np.concatenate([np.cos(ang), np.sin(ang)], axis=1) / math.sqrt(FOURIER_W), np.float32)


def _ct_tables(n):
    n2 = n // FFT_RADIX
    k = np.arange(n2)
    ang = 2.0 * np.pi * ((k[:, None] * k[None, :]) % n2) / n2
    f1 = np.concatenate([np.cos(ang), -np.sin(ang)], axis=0) / math.sqrt(n * FOURIER_W)
    tw = 2.0 * np.pi * k[:, None] * np.arange(FFT_RADIX)[None, :] / n
    twc = np.repeat(np.cos(tw), FOURIER_W, axis=1)
    tws = np.repeat(np.sin(tw), FOURIER_W, axis=1)
    c = np.arange(FOURIER_W)
    angc = 2.0 * np.pi * ((c[:, None] * c[None, :]) % FOURIER_W) / FOURIER_W
    cs2 = np.concatenate([np.cos(angc), np.sin(angc)], axis=0)
    return tuple(np.asarray(t, np.float32) for t in (f1, twc, tws, cs2))


def _mod_kernel(c_ref, w_ref, b_ref, o_ref):
    s = _silu(c_ref[...]).astype(BF16)
    w = w_ref[0].astype(BF16)
    o_ref[0] = jnp.dot(s, w, preferred_element_type=F32) + b_ref[0]


def _modulation(cvecs, w_mod, b_mod):
    chunk = D_MODEL
    n_chunks = 3 * D_MODEL // chunk
    return pl.pallas_call(
        _mod_kernel,
        grid=(DEPTH, n_chunks),
        in_specs=[
            pl.BlockSpec((MOD_ROWS, D_MODEL), lambda l, j: (0, 0)),
            pl.BlockSpec((1, D_MODEL, chunk), lambda l, j: (l, 0, j)),
            pl.BlockSpec((1, 1, chunk), lambda l, j: (l, 0, j)),
        ],
        out_specs=pl.BlockSpec((1, MOD_ROWS, chunk), lambda l, j: (l, 0, j)),
        out_shape=jax.ShapeDtypeStruct((DEPTH, MOD_ROWS, 3 * D_MODEL), F32),
        compiler_params=_cparams(("arbitrary", "arbitrary")),
        name="modulation",
    )(cvecs, w_mod, b_mod.reshape(DEPTH, 1, 3 * D_MODEL))


def _store_heads(ref, t):
    seqs, _, seq, _, _ = ref.shape
    for s in range(seqs):
        for hd in range(N_HEADS):
            ref[s, 0, :, hd, :] = t[s * seq:(s + 1) * seq, hd * QK_W:(hd + 1) * QK_W]


def _inproj_kernel(*refs, rope, emit_kv, n_alias, decimate_fa):
    n_in = 4 + (3 if rope else 0)
    x_ref, mod_ref, g_ref, w3_ref = refs[:4]
    w_ref = w3_ref.at[0]
    if rope:
        cos_ref, sa_ref, sb_ref = refs[4:7]
    outs = refs[n_in + n_alias:]
    fa_ref, sga_ref, zc_ref, bgs_ref, q_ref, k_ref, v_ref, sgc_ref = outs[:8]
    n_out = 10 if emit_kv else 8

    x = x_ref[...]
    ms = jnp.mean(x * x, axis=-1, keepdims=True)
    y = x * lax.rsqrt(ms + EPS) * g_ref[...]
    m = mod_ref[0]
    h = (y * (1.0 + m[:, D_MODEL:2 * D_MODEL]) + m[:, :D_MODEL]).astype(BF16)

    def proj(name):
        lo, hi = _COLS[name]
        return jnp.dot(h, w_ref[:, lo:hi], preferred_element_type=F32)

    if decimate_fa:
        stage_ref = outs[n_out]
        fa = proj("fa")
        halves = stage_ref.shape[0]
        for hf in range(halves):
            stage_ref[hf] = fa[:, hf * LANES:(hf + 1) * LANES]
        rows = stage_ref.shape[1] // FFT_RADIX
        for n1 in range(FFT_RADIX):
            for hf in range(halves):
                lo = n1 * FOURIER_W + hf * LANES
                fa_ref[:, lo:lo + LANES] = (
                    stage_ref[hf, pl.ds(n1, rows, stride=FFT_RADIX), :].astype(BF16))
    else:
        fa_ref[...] = proj("fa").astype(BF16)
    sga_ref[...] = _silu(proj("ga")).astype(BF16)
    zc_ref[...] = (proj("cg") * proj("hc")).astype(BF16)
    bgs_ref[...] = (proj("bg") * _silu(proj("gb"))).astype(BF16)
    sgc_ref[...] = _silu(proj("gc")).astype(BF16)

    q = proj("q")
    k = proj("k")
    v = proj("v")
    if emit_kv:
        _store_heads(outs[8], k)
        _store_heads(outs[9], v)
    v_ref[...] = v.astype(BF16)

    q_scale = HEAD_DIM ** -0.5 * LOG2E
    if rope:
        cos = cos_ref[...]
        sa = sa_ref[...]
        sb = sb_ref[...]
        for hd in range(N_HEADS):
            sl = slice(hd * QK_W, (hd + 1) * QK_W)
            for t, ref, scale in ((q, q_ref, q_scale), (k, k_ref, None)):
                th = t[:, sl]
                r = (th * cos + pltpu.roll(th, QK_W - 16, axis=1) * sa
                     + pltpu.roll(th, 16, axis=1) * sb)
                if scale is not None:
                    r = r * scale
                ref[:, sl] = r.astype(BF16)
    else:
        q_ref[...] = (q * q_scale).astype(BF16)
        k_ref[...] = k.astype(BF16)


def _in_projection(x, mod_rows, norm_g, w_in_bf16, layer, rope_tabs, *, tiles_per_mod, kv_out=None,
                   decimate_fa=False):
    n_tok = x.shape[0]
    tm = TOKEN_TILE
    n_tiles = n_tok // tm
    rope = rope_tabs is not None
    emit_kv = kv_out is not None
    row = lambda w: pl.BlockSpec((tm, w), lambda i: (i, 0))
    in_specs = [
        row(D_MODEL),
        pl.BlockSpec((1, 1, 3 * D_MODEL), lambda i: (i // tiles_per_mod, 0, 0)),
        pl.BlockSpec((1, D_MODEL), lambda i: (0, 0)),
        pl.BlockSpec((1, D_MODEL, IN_DIM), lambda i: (layer, 0, 0)),
    ]
    args = [x, mod_rows, norm_g.reshape(1, D_MODEL), w_in_bf16]
    if rope:
        seq_tiles = rope_tabs[0].shape[0] // tm
        tab = pl.BlockSpec((tm, QK_W), lambda i: (i % seq_tiles, 0))
        in_specs += [tab, tab, tab]
        args += list(rope_tabs)
    widths = (256, 256, 256, 256, 512, 512, 512, 512)
    out_specs = [row(w) for w in widths]
    out_shape = [jax.ShapeDtypeStruct((n_tok, w), BF16) for w in widths]
    if decimate_fa:
        out_specs[0] = pl.BlockSpec((tm // FFT_RADIX, FFT_RADIX * FOURIER_W), lambda i: (i, 0))
        out_shape[0] = jax.ShapeDtypeStruct((n_tok // FFT_RADIX, FFT_RADIX * FOURIER_W), BF16)
    aliases = {}
    n_alias = 0
    if emit_kv:
        seq, new_k, new_v = kv_out
        seqs = tm // seq
        kv_blk = pl.BlockSpec((seqs, 1, seq, N_HEADS, QK_W), lambda i: (i, layer, 0, 0, 0))
        out_specs += [kv_blk, kv_blk]
        out_shape += [jax.ShapeDtypeStruct((n_tok // seq, DEPTH, seq, N_HEADS, QK_W), F32)] * 2
        if new_k is not None:
            aliases = {len(args): len(widths), len(args) + 1: len(widths) + 1}
            in_specs += [pl.BlockSpec(memory_space=pl.ANY)] * 2
            args += [new_k, new_v]
            n_alias = 2
    return pl.pallas_call(
        functools.partial(_inproj_kernel, rope=rope, emit_kv=emit_kv, n_alias=n_alias,
                          decimate_fa=decimate_fa),
        grid=(n_tiles,),
        in_specs=in_specs,
        out_specs=out_specs,
        out_shape=out_shape,
        scratch_shapes=[pltpu.VMEM((FOURIER_W // LANES, tm, LANES), F32)] if decimate_fa else [],
        input_output_aliases=aliases,
        compiler_params=_cparams(("arbitrary",)),
        name="in_projection_rope" if rope else "in_projection",
    )(*args)


def _lambda(lam_ref, lam_init):
    lv = lam_ref[...]
    a = jnp.sum(lv[0:1] * lv[1:2], axis=-1, keepdims=True)
    b = jnp.sum(lv[2:3] * lv[3:4], axis=-1, keepdims=True)
    return jnp.exp(a) - jnp.exp(b) + lam_init


def _stack_masked_t(q):
    q_t = q.astype(F32).T
    first = lax.broadcasted_iota(jnp.int32, q_t.shape, 0) < HEAD_DIM
    return jnp.concatenate([jnp.where(first, q_t, 0.0), jnp.where(first, 0.0, q_t)],
                           axis=1).astype(BF16)


def _scores_chunk(k, qq_t, m):
    s = jnp.dot(k, qq_t, preferred_element_type=F32)
    mc = jnp.max(s, axis=0, keepdims=True)
    return s, (mc if m is None else jnp.maximum(m, mc))


def _with_ones_rows(v_t):
    return jnp.concatenate([v_t, jnp.ones((ONES_ROWS, v_t.shape[1]), F32)], axis=0).astype(BF16)


def _values_chunk(s, m, v1_t, acc):
    e = jnp.exp2(s - m).astype(BF16)
    oc = jnp.dot(v1_t, e, preferred_element_type=F32)
    return oc if acc is None else acc + oc


def _combine(acc, lam):
    tq = acc.shape[1] // 2
    r = 1.0 / acc[QK_W:QK_W + 1, :]
    o_t = acc[:QK_W, :]
    y_t = o_t[:, :tq] * r[:, :tq] - o_t[:, tq:] * (r[:, tq:] * lam)
    return y_t.T


def _subln_gate(o, sg, sgc, lam_init):
    ms = jnp.mean(o * o, axis=-1, keepdims=True)
    y = o * lax.rsqrt(ms + EPS) * sg * (1.0 - lam_init)
    return (y * sgc.astype(F32)).astype(BF16)


def _attn_ctx_kernel(lam_ref, sg_ref, q_ref, k_ref, v_ref, sgc_ref, o_ref, *, lam_init):
    lam = _lambda(lam_ref, lam_init)
    sg = sg_ref[...]
    seq = q_ref.shape[0]
    for hd in range(N_HEADS):
        sl = slice(hd * QK_W, (hd + 1) * QK_W)
        k = k_ref[:, sl]
        v_t = _with_ones_rows(v_ref[:, sl].astype(F32).T)
        for t in range(seq // CTX_Q_TILE):
            rows = slice(t * CTX_Q_TILE, (t + 1) * CTX_Q_TILE)
            s, m = _scores_chunk(k, _stack_masked_t(q_ref[rows, sl]), None)
            o = _combine(_values_chunk(s, m, v_t, None), lam)
            o_ref[rows, sl] = _subln_gate(o, sg, sgc_ref[rows, sl], lam_init)


def _attention_ctx(q, k, v, sgc, lam_vec, subln_g, *, seq, lam_init):
    n_tok = q.shape[0]
    blk = pl.BlockSpec((seq, ATTN_W), lambda b: (b, 0))
    return pl.pallas_call(
        functools.partial(_attn_ctx_kernel, lam_init=lam_init),
        grid=(n_tok // seq,),
        in_specs=[
            pl.BlockSpec((4, HEAD_DIM), lambda b: (0, 0)),
            pl.BlockSpec((1, QK_W), lambda b: (0, 0)),
            blk, blk, blk, blk,
        ],
        out_specs=blk,
        out_shape=jax.ShapeDtypeStruct((n_tok, ATTN_W), BF16),
        compiler_params=_cparams(("arbitrary",)),
        name="attention_ctx",
    )(lam_vec, subln_g.reshape(1, QK_W), q, k, v, sgc)


def _attn_den_kernel(lam_ref, sg_ref, q_ref, kn_ref, vn_ref, kc_ref, vc_ref, sgc_ref, o_ref,
                     kcs_ref, vt_ref, sa_ref, sb_ref, *, lam_init):
    seq = q_ref.shape[0]
    past = kcs_ref.shape[1]
    past_chunks = past // KEY_CHUNK
    n_chunks = (past + seq) // KEY_CHUNK
    lam = _lambda(lam_ref, lam_init)
    sg = sg_ref[...]
    for hd in range(N_HEADS):
        sl = slice(hd * QK_W, (hd + 1) * QK_W)
        kcs_ref[hd] = kc_ref[0, 0, :, hd, :].astype(BF16)
        vt_ref[hd, :, :past] = _with_ones_rows(vc_ref[0, 0, :, hd, :].T)
        vt_ref[hd, :, past:] = _with_ones_rows(vn_ref[:, sl].astype(F32).T)

        def get_k(c):
            if c < past_chunks:
                return kcs_ref[hd, c * KEY_CHUNK:(c + 1) * KEY_CHUNK, :]
            c -= past_chunks
            return kn_ref[c * KEY_CHUNK:(c + 1) * KEY_CHUNK, sl]

        def get_vt(c):
            return vt_ref[hd, :, c * KEY_CHUNK:(c + 1) * KEY_CHUNK]

        def tile_rows(t):
            if isinstance(t, int):
                return pl.ds(t * Q_TILE, Q_TILE)
            return pl.ds(pl.multiple_of(t * Q_TILE, Q_TILE), Q_TILE)

        def chunk(c):
            return pl.ds(c * KEY_CHUNK, KEY_CHUNK)

        def step(t, m_cur, s_cur, s_nxt, do_scores=True, do_values=True):
            qq = _stack_masked_t(q_ref[tile_rows(t + 1), sl]) if do_scores else None
            m_nxt = None
            acc = None
            for c in range(n_chunks):
                if do_scores:
                    s, m_nxt = _scores_chunk(get_k(c), qq, m_nxt)
                    s_nxt[chunk(c), :] = s
                if do_values:
                    acc = _values_chunk(s_cur[chunk(c), :], m_cur, get_vt(c), acc)
            if do_values:
                rows = tile_rows(t)
                o_ref[rows, sl] = _subln_gate(_combine(acc, lam), sg, sgc_ref[rows, sl], lam_init)
            return m_nxt

        def tile_pair(j, m):
            m = step(2 * j, m, sa_ref, sb_ref)
            return step(2 * j + 1, m, sb_ref, sa_ref)

        n_tiles = seq // Q_TILE
        m = step(-1, None, None, sa_ref, do_values=False)
        m = lax.fori_loop(0, n_tiles // 2 - 1, tile_pair, m)
        m = step(n_tiles - 2, m, sa_ref, sb_ref)
        step(n_tiles - 1, m, sb_ref, None, do_scores=False)


def _attention_den(q, k, v, sgc, cache_k, cache_v, lam_vec, subln_g, *, layer, seq, lam_init):
    n_tok = q.shape[0]
    past = cache_k.shape[2]
    blk = pl.BlockSpec((seq, ATTN_W), lambda b: (b, 0))
    cblk = pl.BlockSpec((1, 1, past, N_HEADS, QK_W), lambda b: (b, layer, 0, 0, 0))
    return pl.pallas_call(
        functools.partial(_attn_den_kernel, lam_init=lam_init),
        grid=(n_tok // seq,),
        in_specs=[
            pl.BlockSpec((4, HEAD_DIM), lambda b: (0, 0)),
            pl.BlockSpec((1, QK_W), lambda b: (0, 0)),
            blk, blk, blk, cblk, cblk, blk,
        ],
        out_specs=blk,
        out_shape=jax.ShapeDtypeStruct((n_tok, ATTN_W), BF16),
        scratch_shapes=[
            pltpu.VMEM((N_HEADS, past, QK_W), BF16),
            pltpu.VMEM((N_HEADS, QK_W + ONES_ROWS, past + seq), BF16),
            pltpu.VMEM((past + seq, 2 * Q_TILE), F32),
            pltpu.VMEM((past + seq, 2 * Q_TILE), F32),
        ],
        compiler_params=_cparams(("arbitrary",)),
        name="attention_den",
    )(lam_vec, subln_g.reshape(1, QK_W), q, k, v, cache_k, cache_v, sgc)


def _fourier_ctx_kernel(fa_ref, sga_ref, cs_ref, dn_ref, o_ref, *, seq):
    ab = jnp.dot(fa_ref[...], cs_ref[...], preferred_element_type=F32).astype(BF16)
    for s in range(fa_ref.shape[0] // seq):
        rows = slice(s * seq, (s + 1) * seq)
        ab2 = jnp.concatenate([ab[rows, :FOURIER_W], ab[rows, FOURIER_W:]], axis=0)
        f = jnp.dot(dn_ref[...], ab2, preferred_element_type=F32)
        o_ref[rows, :] = (f * sga_ref[rows, :].astype(F32)).astype(BF16)


def _fourier_ctx(fa, sga, cs, dn, *, seq):
    n_tok = fa.shape[0]
    blk = pl.BlockSpec((CTX_FOURIER_SEQS * seq, FOURIER_W), lambda b: (b, 0))
    return pl.pallas_call(
        functools.partial(_fourier_ctx_kernel, seq=seq),
        grid=(n_tok // (CTX_FOURIER_SEQS * seq),),
        in_specs=[blk, blk,
                  pl.BlockSpec((FOURIER_W, 2 * FOURIER_W), lambda b: (0, 0)),
                  pl.BlockSpec((seq, 2 * seq), lambda b: (0, 0))],
        out_specs=blk,
        out_shape=jax.ShapeDtypeStruct((n_tok, FOURIER_W), BF16),
        compiler_params=_cparams(("arbitrary",)),
        name="fourier_ctx",
    )(fa, sga, cs, dn)


def _cadd(a, b):
    return (a[0] + b[0], a[1] + b[1])


def _csub(a, b):
    return (a[0] - b[0], a[1] - b[1])


def _mul_neg_i(a):
    return (a[1], -a[0])


def _mul_w8_1(a):
    return ((a[0] + a[1]) * SQRT_HALF, (a[1] - a[0]) * SQRT_HALF)


def _mul_w8_3(a):
    return ((a[1] - a[0]) * SQRT_HALF, (-a[0] - a[1]) * SQRT_HALF)


def _fft4(a0, a1, a2, a3):
    e0, e1 = _cadd(a0, a2), _csub(a0, a2)
    o0, o1 = _cadd(a1, a3), _mul_neg_i(_csub(a1, a3))
    return [_cadd(e0, o0), _cadd(e1, o1), _csub(e0, o0), _csub(e1, o1)]


def _fft8(x):
    e = _fft4(x[0], x[2], x[4], x[6])
    o = _fft4(x[1], x[3], x[5], x[7])
    t = [o[0], _mul_w8_1(o[1]), _mul_neg_i(o[2]), _mul_w8_3(o[3])]
    return [_cadd(e[k], t[k]) for k in range(4)] + [_csub(e[k], t[k]) for k in range(4)]


def _fourier_den_kernel(xw_ref, sga_ref, f1_ref, twc_ref, tws_ref, cs2_ref, o_ref, g_ref, x_ref):
    n2 = xw_ref.shape[0]
    g_ref[...] = jnp.dot(f1_ref[...], xw_ref[...], preferred_element_type=F32)

    def chunk(i, carry):
        r = pl.multiple_of(i * FFT_ROWS, FFT_ROWS)
        re_rows = pl.ds(r, FFT_ROWS)
        im_rows = pl.ds(n2 + r, FFT_ROWS)
        xs = []
        for n1 in range(FFT_RADIX):
            lanes = slice(n1 * FOURIER_W, (n1 + 1) * FOURIER_W)
            gr = g_ref[re_rows, lanes]
            gi = g_ref[im_rows, lanes]
            if n1 > 0:
                c = twc_ref[re_rows, lanes]
                s = tws_ref[re_rows, lanes]
                gr, gi = gr * c + gi * s, gi * c - gr * s
            xs.append((gr, gi))
        for k1, (xr, xi) in enumerate(_fft8(xs)):
            out_rows = pl.ds(k1 * n2 + r, FFT_ROWS)
            x_ref[out_rows, :FOURIER_W] = xr.astype(BF16)
            x_ref[out_rows, FOURIER_W:] = xi.astype(BF16)
        return carry

    lax.fori_loop(0, n2 // FFT_ROWS, chunk, 0)
    f = jnp.dot(x_ref[...], cs2_ref[...], preferred_element_type=F32)
    o_ref[...] = (f * sga_ref[...].astype(F32)).astype(BF16)


def _fourier_den(xw, sga, f1, twc, tws, cs2, *, seq):
    n_tok = sga.shape[0]
    n2 = seq // FFT_RADIX
    wide = FFT_RADIX * FOURIER_W
    const = lambda shape: pl.BlockSpec(shape, lambda b: (0, 0))
    return pl.pallas_call(
        _fourier_den_kernel,
        grid=(n_tok // seq,),
        in_specs=[
            pl.BlockSpec((n2, wide), lambda b: (b, 0)),
            pl.BlockSpec((seq, FOURIER_W), lambda b: (b, 0)),
            const((2 * n2, n2)), const((n2, wide)), const((n2, wide)),
            const((2 * FOURIER_W, FOURIER_W)),
        ],
        out_specs=pl.BlockSpec((seq, FOURIER_W), lambda b: (b, 0)),
        out_shape=jax.ShapeDtypeStruct((n_tok, FOURIER_W), BF16),
        scratch_shapes=[pltpu.VMEM((2 * n2, wide), F32),
                        pltpu.VMEM((seq, 2 * FOURIER_W), BF16)],
        compiler_params=_cparams(("arbitrary",)),
        name="fourier_den",
    )(xw, sga, f1, twc, tws, cs2)


def _outproj_kernel(x_ref, mod_ref, ya_ref, zc_ref, zp_ref, zn_ref, bgs_ref, yc_ref,
                    cw_ref, cb_ref, w_ref, fg_ref, o_ref, *, seq, final_norm):
    tm = x_ref.shape[0]
    z = zc_ref[...].astype(F32)
    row = lax.broadcasted_iota(jnp.int32, z.shape, 0)
    pos = (pl.program_id(0) * tm + row) & (seq - 1)
    prev_row = zp_ref[HALO_ROWS - 1:HALO_ROWS, :].astype(F32)
    next_row = zn_ref[0:1, :].astype(F32)
    z_prev = jnp.where(row == 0, prev_row, pltpu.roll(z, 1, axis=0))
    z_prev = jnp.where(pos == 0, 0.0, z_prev)
    z_next = jnp.where(row == tm - 1, next_row, pltpu.roll(z, tm - 1, axis=0))
    z_next = jnp.where(pos == seq - 1, 0.0, z_next)
    cw = cw_ref[...]
    conv = z_prev * cw[0:1] + z * cw[1:2] + z_next * cw[2:3] + cb_ref[...]
    yb = (bgs_ref[...].astype(F32) * conv).astype(BF16)

    mixed = jnp.concatenate([ya_ref[...], yb, yc_ref[...]], axis=-1)
    out = jnp.dot(mixed, w_ref[0], preferred_element_type=F32)
    gate = mod_ref[0][:, 2 * D_MODEL:]
    xn = x_ref[...] + gate * out
    if final_norm:
        ms = jnp.mean(xn * xn, axis=-1, keepdims=True)
        xn = xn * lax.rsqrt(ms + EPS) * fg_ref[...]
    o_ref[...] = xn


def _out_projection(x, mod_rows, ya, zc, bgs, yc, conv_w, conv_b, w_out_bf16, layer, final_g,
                    *, seq, tiles_per_mod, final_norm):
    n_tok = x.shape[0]
    tm = TOKEN_TILE
    n_tiles = n_tok // tm
    halo_per_tile = tm // HALO_ROWS
    n_halo = n_tok // HALO_ROWS
    row = lambda w: pl.BlockSpec((tm, w), lambda i: (i, 0))
    return pl.pallas_call(
        functools.partial(_outproj_kernel, seq=seq, final_norm=final_norm),
        grid=(n_tiles,),
        in_specs=[
            row(D_MODEL),
            pl.BlockSpec((1, 1, 3 * D_MODEL), lambda i: (i // tiles_per_mod, 0, 0)),
            row(FOURIER_W),
            row(CONV_W),
            pl.BlockSpec((HALO_ROWS, CONV_W), lambda i: (jnp.maximum(i * halo_per_tile - 1, 0), 0)),
            pl.BlockSpec((HALO_ROWS, CONV_W),
                         lambda i: (jnp.minimum((i + 1) * halo_per_tile, n_halo - 1), 0)),
            row(CONV_W),
            row(ATTN_W),
            pl.BlockSpec((3, CONV_W), lambda i: (0, 0)),
            pl.BlockSpec((1, CONV_W), lambda i: (0, 0)),
            pl.BlockSpec((1, D_MODEL, D_MODEL), lambda i: (layer, 0, 0)),
            pl.BlockSpec((1, D_MODEL), lambda i: (0, 0)),
        ],
        out_specs=row(D_MODEL),
        out_shape=jax.ShapeDtypeStruct((n_tok, D_MODEL), F32),
        compiler_params=_cparams(("arbitrary",)),
        name="out_projection",
    )(x, mod_rows, ya, zc, zc, zc, bgs, yc, conv_w, conv_b.reshape(1, CONV_W), w_out_bf16,
      final_g.reshape(1, D_MODEL))


def kernel(x_prompt, x_sample, cache_k, cache_v, c, c_ctx, norm_g, w_mod, b_mod, w_in, conv_w,
           conv_b, lam_vec, subln_g, w_out, final_g):
    batch, seq, _ = x_prompt.shape
    dec_batch, dec_seq, _ = x_sample.shape

    cvecs = jnp.concatenate(
        [c_ctx[None, :], c, jnp.zeros((MOD_ROWS - 1 - dec_batch, D_MODEL), F32)], axis=0)
    mod = _modulation(cvecs, w_mod, b_mod)

    w_in_b = w_in.astype(BF16)
    w_out_b = w_out.astype(BF16)
    rope_tabs = tuple(jnp.asarray(t) for t in _rope_tables(dec_seq))
    cs = jnp.asarray(_chan_tables()).astype(BF16)
    dn_ctx = jnp.asarray(_dft_tables(seq)).astype(BF16)
    f1, twc, tws, cs2 = (jnp.asarray(t) for t in _ct_tables(dec_seq))
    f1 = f1.astype(BF16)
    cs2 = cs2.astype(BF16)

    xc = x_prompt.reshape(batch * seq, D_MODEL)
    xl = x_sample.reshape(dec_batch * dec_seq, D_MODEL)
    new_k = new_v = None
    for l in range(DEPTH):
        lam_init = 0.8 - 0.6 * math.exp(-0.3 * l)
        last = l == DEPTH - 1

        mod_c = mod[l, 0:1][:, None, :]
        fa, sga, zc, bgs, q, k, v, sgc, new_k, new_v = _in_projection(
            xc, mod_c, norm_g[l], w_in_b, l, None, tiles_per_mod=xc.shape[0] // TOKEN_TILE,
            kv_out=(seq, new_k, new_v))
        yc = _attention_ctx(q, k, v, sgc, lam_vec[l], subln_g[l], seq=seq, lam_init=lam_init)
        ya = _fourier_ctx(fa, sga, cs, dn_ctx, seq=seq)
        xc = _out_projection(xc, mod_c, ya, zc, bgs, yc, conv_w[l], conv_b[l], w_out_b, l, final_g,
                             seq=seq, tiles_per_mod=xc.shape[0] // TOKEN_TILE, final_norm=last)

        mod_d = mod[l, 1:1 + dec_batch][:, None, :]
        fa, sga, zc, bgs, q, k, v, sgc = _in_projection(
            xl, mod_d, norm_g[l], w_in_b, l, rope_tabs, tiles_per_mod=dec_seq // TOKEN_TILE,
            decimate_fa=True)
        yc = _attention_den(q, k, v, sgc, cache_k, cache_v, lam_vec[l], subln_g[l], layer=l,
                            seq=dec_seq, lam_init=lam_init)
        ya = _fourier_den(fa, sga, f1, twc, tws, cs2, seq=dec_seq)
        xl = _out_projection(xl, mod_d, ya, zc, bgs, yc, conv_w[l], conv_b[l], w_out_b, l, final_g,
                             seq=dec_seq, tiles_per_mod=dec_seq // TOKEN_TILE, final_norm=last)

    y_prompt = xc.reshape(batch, seq, D_MODEL)
    y_sample = xl.reshape(dec_batch, dec_seq, D_MODEL)
    return (y_prompt, y_sample, new_k, new_v)
```

```python
import functools
import math

import numpy as np
import jax
import jax.numpy as jnp
from jax import lax
from jax.experimental import pallas as pl
from jax.experimental.pallas import tpu as pltpu

D_MODEL = 1024
DEPTH = 2
GRID_W = 64
FOURIER_W = 256
CONV_W = 256
ATTN_W = 512
N_HEADS = 4
HEAD_DIM = 64
QK_W = 128
ROPE_BASE = 10000.0
EPS = 1e-6
IN_DIM = 3584

F32 = jnp.float32
BF16 = jnp.bfloat16

VMEM_LIMIT_BYTES = 56 * 1024 * 1024
TOKEN_TILE = 512
OUT_TILE = 1024
Q_TILE = 256
KEY_CHUNK = 256
ONES_ROWS = 16
LANES = 128
FFT_RADIX = 8
FFT_ROWS = 16
SQRT_HALF = 0.7071067811865476
CTX_FOURIER_SEQS = 4
HALO_ROWS = 16
MOD_ROWS = 8
LOG2E = 1.4426950408889634

_COLS = {}
_off = 0
for _name, _w in (("fa", 256), ("ga", 256), ("bg", 256), ("cg", 256), ("hc", 256), ("gb", 256),
                  ("q", 512), ("k", 512), ("v", 512), ("gc", 512)):
    _COLS[_name] = (_off, _off + _w)
    _off += _w


def _silu(x):
    return x * (1.0 / (1.0 + jnp.exp(-x)))


def _cparams(sem):
    return pltpu.CompilerParams(dimension_semantics=sem, vmem_limit_bytes=VMEM_LIMIT_BYTES)


def _rope_tables(n_tokens):
    n = np.arange(n_tokens)
    row = (n // GRID_W).astype(np.float64)
    col = (n % GRID_W).astype(np.float64)
    j = np.arange(QK_W)
    jj = j % HEAD_DIM
    idx = jj % 32
    inv = 1.0 / (ROPE_BASE ** (2.0 * (idx % 16) / 32.0))
    pos = np.where((jj < 32)[None, :], row[:, None], col[:, None])
    ang = pos * inv[None, :]
    cos = np.cos(ang)
    sin = np.sin(ang)
    first = (idx < 16)[None, :]
    sin_a = np.where(first, -sin, 0.0)
    sin_b = np.where(first, 0.0, sin)
    return (np.asarray(cos, np.float32), np.asarray(sin_a, np.float32), np.asarray(sin_b, np.float32))


def _dft_tables(n):
    k = np.arange(n)
    kn = (k[:, None] * k[None, :]) % n
    ang = 2.0 * np.pi * kn / n
    return np.asarray(np.concatenate([np.cos(ang), -np.sin(ang)], axis=1) / math.sqrt(n), np.float32)


def _chan_tables():
    k = np.arange(FOURIER_W)
    kn = (k[:, None] * k[None, :]) % FOURIER_W
    ang = 2.0 * np.pi * kn / FOURIER_W
    return np.asarray(np.concatenate([np.cos(ang), np.sin(ang)], axis=1) / math.sqrt(FOURIER_W), np.float32)


def _ct_tables(n):
    n2 = n // FFT_RADIX
    k = np.arange(n2)
    ang = 2.0 * np.pi * ((k[:, None] * k[None, :]) % n2) / n2
    f1 = np.concatenate([np.cos(ang), -np.sin(ang)], axis=0) / math.sqrt(n * FOURIER_W)
    tw = 2.0 * np.pi * k[:, None] * np.arange(FFT_RADIX)[None, :] / n
    twc = np.repeat(np.cos(tw), FOURIER_W, axis=1)
    tws = np.repeat(np.sin(tw), FOURIER_W, axis=1)
    c = np.arange(FOURIER_W)
    angc = 2.0 * np.pi * ((c[:, None] * c[None, :]) % FOURIER_W) / FOURIER_W
    cs2 = np.concatenate([np.cos(angc), np.sin(angc)], axis=0)
    return tuple(np.asarray(t, np.float32) for t in (f1, twc, tws, cs2))


def _mod_kernel(c_ref, w_ref, b_ref, o_ref):
    s = _silu(c_ref[...]).astype(BF16)
    w = w_ref[0].astype(BF16)
    o_ref[0] = jnp.dot(s, w, preferred_element_type=F32) + b_ref[0]


def _modulation(cvecs, w_mod, b_mod):
    chunk = D_MODEL
    n_chunks = 3 * D_MODEL // chunk
    return pl.pallas_call(
        _mod_kernel,
        grid=(DEPTH, n_chunks),
        in_specs=[
            pl.BlockSpec((MOD_ROWS, D_MODEL), lambda l, j: (0, 0)),
            pl.BlockSpec((1, D_MODEL, chunk), lambda l, j: (l, 0, j)),
            pl.BlockSpec((1, 1, chunk), lambda l, j: (l, 0, j)),
        ],
        out_specs=pl.BlockSpec((1, MOD_ROWS, chunk), lambda l, j: (l, 0, j)),
        out_shape=jax.ShapeDtypeStruct((DEPTH, MOD_ROWS, 3 * D_MODEL), F32),
        compiler_params=_cparams(("arbitrary", "arbitrary")),
        name="modulation",
    )(cvecs, w_mod, b_mod.reshape(DEPTH, 1, 3 * D_MODEL))


def _store_heads(ref, t):
    seqs, _, seq, _, _ = ref.shape
    for s in range(seqs):
        for hd in range(N_HEADS):
            ref[s, 0, :, hd, :] = t[s * seq:(s + 1) * seq, hd * QK_W:(hd + 1) * QK_W]


def _inproj_kernel(*refs, rope, emit_kv, n_alias, decimate_fa):
    n_in = 4 + (3 if rope else 0)
    x_ref, mod_ref, g_ref, w3_ref = refs[:4]
    w_ref = w3_ref.at[0]
    if rope:
        cos_ref, sa_ref, sb_ref = refs[4:7]
    outs = refs[n_in + n_alias:]
    fa_ref, sga_ref, zc_ref, bgs_ref, q_ref, k_ref, v_ref, sgc_ref = outs[:8]
    n_out = 10 if emit_kv else 8

    x = x_ref[...]
    ms = jnp.mean(x * x, axis=-1, keepdims=True)
    y = x * lax.rsqrt(ms + EPS) * g_ref[...]
    m = mod_ref[0]
    h = (y * (1.0 + m[:, D_MODEL:2 * D_MODEL]) + m[:, :D_MODEL]).astype(BF16)

    def proj(name):
        lo, hi = _COLS[name]
        return jnp.dot(h, w_ref[:, lo:hi], preferred_element_type=F32)

    if decimate_fa:
        stage_ref = outs[n_out]
        fa = proj("fa")
        halves = stage_ref.shape[0]
        for hf in range(halves):
            stage_ref[hf] = fa[:, hf * LANES:(hf + 1) * LANES]
        rows = stage_ref.shape[1] // FFT_RADIX
        for n1 in range(FFT_RADIX):
            for hf in range(halves):
                lo = n1 * FOURIER_W + hf * LANES
                fa_ref[:, lo:lo + LANES] = (
                    stage_ref[hf, pl.ds(n1, rows, stride=FFT_RADIX), :].astype(BF16))
    else:
        fa_ref[...] = proj("fa").astype(BF16)
    sga_ref[...] = _silu(proj("ga")).astype(BF16)
    zc_ref[...] = (proj("cg") * proj("hc")).astype(BF16)
    bgs_ref[...] = (proj("bg") * _silu(proj("gb"))).astype(BF16)
    sgc_ref[...] = _silu(proj("gc")).astype(BF16)

    q = proj("q")
    k = proj("k")
    v = proj("v")
    if emit_kv:
        _store_heads(outs[8], k)
        _store_heads(outs[9], v)
    v_ref[...] = v.astype(BF16)

    q_scale = HEAD_DIM ** -0.5 * LOG2E
    if rope:
        cos = cos_ref[...]
        sa = sa_ref[...]
        sb = sb_ref[...]
        for hd in range(N_HEADS):
            sl = slice(hd * QK_W, (hd + 1) * QK_W)
            for t, ref, scale in ((q, q_ref, q_scale), (k, k_ref, None)):
                th = t[:, sl]
                r = (th * cos + pltpu.roll(th, QK_W - 16, axis=1) * sa
                     + pltpu.roll(th, 16, axis=1) * sb)
                if scale is not None:
                    r = r * scale
                ref[:, sl] = r.astype(BF16)
    else:
        q_ref[...] = (q * q_scale).astype(BF16)
        k_ref[...] = k.astype(BF16)


def _in_projection(x, mod_rows, norm_g, w_in_bf16, layer, rope_tabs, *, rows_per_mod, kv_out=None,
                   decimate_fa=False):
    n_tok = x.shape[0]
    tm = TOKEN_TILE
    n_tiles = n_tok // tm
    rope = rope_tabs is not None
    emit_kv = kv_out is not None
    row = lambda w: pl.BlockSpec((tm, w), lambda i: (i, 0))
    in_specs = [
        row(D_MODEL),
        pl.BlockSpec((1, 1, 3 * D_MODEL), lambda i: (i * tm // rows_per_mod, 0, 0)),
        pl.BlockSpec((1, D_MODEL), lambda i: (0, 0)),
        pl.BlockSpec((1, D_MODEL, IN_DIM), lambda i: (layer, 0, 0)),
    ]
    args = [x, mod_rows, norm_g.reshape(1, D_MODEL), w_in_bf16]
    if rope:
        seq_tiles = rope_tabs[0].shape[0] // tm
        tab = pl.BlockSpec((tm, QK_W), lambda i: (i % seq_tiles, 0))
        in_specs += [tab, tab, tab]
        args += list(rope_tabs)
    widths = (256, 256, 256, 256, 512, 512, 512, 512)
    out_specs = [row(w) for w in widths]
    out_shape = [jax.ShapeDtypeStruct((n_tok, w), BF16) for w in widths]
    if decimate_fa:
        out_specs[0] = pl.BlockSpec((tm // FFT_RADIX, FFT_RADIX * FOURIER_W), lambda i: (i, 0))
        out_shape[0] = jax.ShapeDtypeStruct((n_tok // FFT_RADIX, FFT_RADIX * FOURIER_W), BF16)
    aliases = {}
    n_alias = 0
    if emit_kv:
        seq, new_k, new_v = kv_out
        seqs = tm // seq
        kv_blk = pl.BlockSpec((seqs, 1, seq, N_HEADS, QK_W), lambda i: (i, layer, 0, 0, 0))
        out_specs += [kv_blk, kv_blk]
        out_shape += [jax.ShapeDtypeStruct((n_tok // seq, DEPTH, seq, N_HEADS, QK_W), F32)] * 2
        if new_k is not None:
            aliases = {len(args): len(widths), len(args) + 1: len(widths) + 1}
            in_specs += [pl.BlockSpec(memory_space=pl.ANY)] * 2
            args += [new_k, new_v]
            n_alias = 2
    return pl.pallas_call(
        functools.partial(_inproj_kernel, rope=rope, emit_kv=emit_kv, n_alias=n_alias,
                          decimate_fa=decimate_fa),
        grid=(n_tiles,),
        in_specs=in_specs,
        out_specs=out_specs,
        out_shape=out_shape,
        scratch_shapes=[pltpu.VMEM((FOURIER_W // LANES, tm, LANES), F32)] if decimate_fa else [],
        input_output_aliases=aliases,
        compiler_params=_cparams(("arbitrary",)),
        name="in_projection_rope" if rope else "in_projection",
    )(*args)


def _lambda(lam_ref, lam_init):
    lv = lam_ref[...]
    a = jnp.sum(lv[0:1] * lv[1:2], axis=-1, keepdims=True)
    b = jnp.sum(lv[2:3] * lv[3:4], axis=-1, keepdims=True)
    return jnp.exp(a) - jnp.exp(b) + lam_init


def _stack_masked(q_t):
    first = lax.broadcasted_iota(jnp.int32, q_t.shape, 0) < HEAD_DIM
    return jnp.concatenate([jnp.where(first, q_t, 0.0), jnp.where(first, 0.0, q_t)],
                           axis=1).astype(BF16)


def _scores_chunk(k, qq_t, m):
    s = jnp.dot(k, qq_t, preferred_element_type=F32)
    mc = jnp.max(s, axis=0, keepdims=True)
    return s, (mc if m is None else jnp.maximum(m, mc))


def _with_ones_rows(v_t):
    return jnp.concatenate([v_t, jnp.ones((ONES_ROWS, v_t.shape[1]), F32)], axis=0).astype(BF16)


def _values_chunk(s, m, v1_t, acc):
    e = jnp.exp2(s - m).astype(BF16)
    oc = jnp.dot(v1_t, e, preferred_element_type=F32)
    return oc if acc is None else acc + oc


def _combine_t(acc, lam):
    tq = acc.shape[1] // 2
    r = 1.0 / acc[QK_W:QK_W + 1, :]
    o_t = acc[:QK_W, :]
    return o_t[:, :tq] * r[:, :tq] - o_t[:, tq:] * (r[:, tq:] * lam)


def _subln_gate(o, sg, sgc, lam_init):
    ms = jnp.mean(o * o, axis=-1, keepdims=True)
    y = o * lax.rsqrt(ms + EPS) * sg * (1.0 - lam_init)
    return (y * sgc.astype(F32)).astype(BF16)


def _attn_ctx_kernel(lam_ref, sg_ref, q_ref, k_ref, v_ref, sgc_ref, o_ref, *, lam_init):
    lam = _lambda(lam_ref, lam_init)
    sg = sg_ref[...]
    q_t = q_ref[...].astype(F32).T
    v_t = v_ref[...].astype(F32).T
    y_t = []
    for hd in range(N_HEADS):
        sl = slice(hd * QK_W, (hd + 1) * QK_W)
        s, m = _scores_chunk(k_ref[:, sl], _stack_masked(q_t[sl]), None)
        y_t.append(_combine_t(_values_chunk(s, m, _with_ones_rows(v_t[sl]), None), lam))
    y = jnp.concatenate(y_t, axis=0).T
    for hd in range(N_HEADS):
        sl = slice(hd * QK_W, (hd + 1) * QK_W)
        o_ref[:, sl] = _subln_gate(y[:, sl], sg, sgc_ref[:, sl], lam_init)


def _attention_ctx(q, k, v, sgc, lam_vec, subln_g, *, seq, lam_init):
    n_tok = q.shape[0]
    blk = pl.BlockSpec((seq, ATTN_W), lambda b: (b, 0))
    return pl.pallas_call(
        functools.partial(_attn_ctx_kernel, lam_init=lam_init),
        grid=(n_tok // seq,),
        in_specs=[
            pl.BlockSpec((4, HEAD_DIM), lambda b: (0, 0)),
            pl.BlockSpec((1, QK_W), lambda b: (0, 0)),
            blk, blk, blk, blk,
        ],
        out_specs=blk,
        out_shape=jax.ShapeDtypeStruct((n_tok, ATTN_W), BF16),
        compiler_params=_cparams(("arbitrary",)),
        name="attention_ctx",
    )(lam_vec, subln_g.reshape(1, QK_W), q, k, v, sgc)


def _attn_den_kernel(lam_ref, sg_ref, q_ref, kn_ref, vn_ref, kc_ref, vc_ref, sgc_ref, o_ref,
                     kcs_ref, vt_ref, sa_ref, sb_ref, *, lam_init):
    seq = q_ref.shape[0]
    past = kcs_ref.shape[1]
    past_chunks = past // KEY_CHUNK
    n_chunks = (past + seq) // KEY_CHUNK
    lam = _lambda(lam_ref, lam_init)
    sg = sg_ref[...]
    for hd in range(N_HEADS):
        sl = slice(hd * QK_W, (hd + 1) * QK_W)
        kcs_ref[hd] = kc_ref[0, 0, :, hd, :].astype(BF16)
        vt_ref[hd, :, :past] = _with_ones_rows(vc_ref[0, 0, :, hd, :].T)
        vt_ref[hd, :, past:] = _with_ones_rows(vn_ref[:, sl].astype(F32).T)

        def get_k(c):
            if c < past_chunks:
                return kcs_ref[hd, c * KEY_CHUNK:(c + 1) * KEY_CHUNK, :]
            c -= past_chunks
            return kn_ref[c * KEY_CHUNK:(c + 1) * KEY_CHUNK, sl]

        def get_vt(c):
            return vt_ref[hd, :, c * KEY_CHUNK:(c + 1) * KEY_CHUNK]

        def tile_rows(t):
            if isinstance(t, int):
                return pl.ds(t * Q_TILE, Q_TILE)
            return pl.ds(pl.multiple_of(t * Q_TILE, Q_TILE), Q_TILE)

        def chunk(c):
            return pl.ds(c * KEY_CHUNK, KEY_CHUNK)

        def step(t, m_cur, s_cur, s_nxt, do_scores=True, do_values=True):
            qq = (_stack_masked(q_ref[tile_rows(t + 1), sl].astype(F32).T)
                  if do_scores else None)
            m_nxt = None
            acc = None
            for c in range(n_chunks):
                if do_scores:
                    s, m_nxt = _scores_chunk(get_k(c), qq, m_nxt)
                    s_nxt[chunk(c), :] = s
                if do_values:
                    acc = _values_chunk(s_cur[chunk(c), :], m_cur, get_vt(c), acc)
            if do_values:
                rows = tile_rows(t)
                o_ref[rows, sl] = _subln_gate(_combine_t(acc, lam).T, sg, sgc_ref[rows, sl],
                                              lam_init)
            return m_nxt

        def tile_pair(j, m):
            m = step(2 * j, m, sa_ref, sb_ref)
            return step(2 * j + 1, m, sb_ref, sa_ref)

        n_tiles = seq // Q_TILE
        m = step(-1, None, None, sa_ref, do_values=False)
        m = lax.fori_loop(0, n_tiles // 2 - 1, tile_pair, m)
        m = step(n_tiles - 2, m, sa_ref, sb_ref)
        step(n_tiles - 1, m, sb_ref, None, do_scores=False)


def _attention_den(q, k, v, sgc, cache_k, cache_v, lam_vec, subln_g, *, layer, seq, lam_init):
    n_tok = q.shape[0]
    past = cache_k.shape[2]
    blk = pl.BlockSpec((seq, ATTN_W), lambda b: (b, 0))
    cblk = pl.BlockSpec((1, 1, past, N_HEADS, QK_W), lambda b: (b, layer, 0, 0, 0))
    return pl.pallas_call(
        functools.partial(_attn_den_kernel, lam_init=lam_init),
        grid=(n_tok // seq,),
        in_specs=[
            pl.BlockSpec((4, HEAD_DIM), lambda b: (0, 0)),
            pl.BlockSpec((1, QK_W), lambda b: (0, 0)),
            blk, blk, blk, cblk, cblk, blk,
        ],
        out_specs=blk,
        out_shape=jax.ShapeDtypeStruct((n_tok, ATTN_W), BF16),
        scratch_shapes=[
            pltpu.VMEM((N_HEADS, past, QK_W), BF16),
            pltpu.VMEM((N_HEADS, QK_W + ONES_ROWS, past + seq), BF16),
            pltpu.VMEM((past + seq, 2 * Q_TILE), F32),
            pltpu.VMEM((past + seq, 2 * Q_TILE), F32),
        ],
        compiler_params=_cparams(("arbitrary",)),
        name="attention_den",
    )(lam_vec, subln_g.reshape(1, QK_W), q, k, v, cache_k, cache_v, sgc)


def _fourier_ctx_kernel(fa_ref, sga_ref, cs_ref, dn_ref, o_ref, *, seq):
    ab = jnp.dot(fa_ref[...], cs_ref[...], preferred_element_type=F32).astype(BF16)
    for s in range(fa_ref.shape[0] // seq):
        rows = slice(s * seq, (s + 1) * seq)
        ab2 = jnp.concatenate([ab[rows, :FOURIER_W], ab[rows, FOURIER_W:]], axis=0)
        f = jnp.dot(dn_ref[...], ab2, preferred_element_type=F32)
        o_ref[rows, :] = (f * sga_ref[rows, :].astype(F32)).astype(BF16)


def _fourier_ctx(fa, sga, cs, dn, *, seq):
    n_tok = fa.shape[0]
    blk = pl.BlockSpec((CTX_FOURIER_SEQS * seq, FOURIER_W), lambda b: (b, 0))
    return pl.pallas_call(
        functools.partial(_fourier_ctx_kernel, seq=seq),
        grid=(n_tok // (CTX_FOURIER_SEQS * seq),),
        in_specs=[blk, blk,
                  pl.BlockSpec((FOURIER_W, 2 * FOURIER_W), lambda b: (0, 0)),
                  pl.BlockSpec((seq, 2 * seq), lambda b: (0, 0))],
        out_specs=blk,
        out_shape=jax.ShapeDtypeStruct((n_tok, FOURIER_W), BF16),
        compiler_params=_cparams(("arbitrary",)),
        name="fourier_ctx",
    )(fa, sga, cs, dn)


def _cadd(a, b):
    return (a[0] + b[0], a[1] + b[1])


def _csub(a, b):
    return (a[0] - b[0], a[1] - b[1])


def _mul_neg_i(a):
    return (a[1], -a[0])


def _mul_w8_1(a):
    return ((a[0] + a[1]) * SQRT_HALF, (a[1] - a[0]) * SQRT_HALF)


def _mul_w8_3(a):
    return ((a[1] - a[0]) * SQRT_HALF, (-a[0] - a[1]) * SQRT_HALF)


def _fft4(a0, a1, a2, a3):
    e0, e1 = _cadd(a0, a2), _csub(a0, a2)
    o0, o1 = _cadd(a1, a3), _mul_neg_i(_csub(a1, a3))
    return [_cadd(e0, o0), _cadd(e1, o1), _csub(e0, o0), _csub(e1, o1)]


def _fft8(x):
    e = _fft4(x[0], x[2], x[4], x[6])
    o = _fft4(x[1], x[3], x[5], x[7])
    t = [o[0], _mul_w8_1(o[1]), _mul_neg_i(o[2]), _mul_w8_3(o[3])]
    return [_cadd(e[k], t[k]) for k in range(4)] + [_csub(e[k], t[k]) for k in range(4)]


def _fourier_den_kernel(xw_ref, sga_ref, f1_ref, twc_ref, tws_ref, cs2_ref, o_ref, g_ref, x_ref):
    n2 = xw_ref.shape[0]
    g_ref[...] = jnp.dot(f1_ref[...], xw_ref[...], preferred_element_type=F32)

    def chunk(i, carry):
        r = pl.multiple_of(i * FFT_ROWS, FFT_ROWS)
        re_rows = pl.ds(r, FFT_ROWS)
        im_rows = pl.ds(n2 + r, FFT_ROWS)
        xs = []
        for n1 in range(FFT_RADIX):
            lanes = slice(n1 * FOURIER_W, (n1 + 1) * FOURIER_W)
            gr = g_ref[re_rows, lanes]
            gi = g_ref[im_rows, lanes]
            if n1 > 0:
                c = twc_ref[re_rows, lanes]
                s = tws_ref[re_rows, lanes]
                gr, gi = gr * c + gi * s, gi * c - gr * s
            xs.append((gr, gi))
        for k1, (xr, xi) in enumerate(_fft8(xs)):
            out_rows = pl.ds(k1 * n2 + r, FFT_ROWS)
            x_ref[out_rows, :FOURIER_W] = xr.astype(BF16)
            x_ref[out_rows, FOURIER_W:] = xi.astype(BF16)
        return carry

    lax.fori_loop(0, n2 // FFT_ROWS, chunk, 0)
    f = jnp.dot(x_ref[...], cs2_ref[...], preferred_element_type=F32)
    o_ref[...] = (f * sga_ref[...].astype(F32)).astype(BF16)


def _fourier_den(xw, sga, f1, twc, tws, cs2, *, seq):
    n_tok = sga.shape[0]
    n2 = seq // FFT_RADIX
    wide = FFT_RADIX * FOURIER_W
    const = lambda shape: pl.BlockSpec(shape, lambda b: (0, 0))
    return pl.pallas_call(
        _fourier_den_kernel,
        grid=(n_tok // seq,),
        in_specs=[
            pl.BlockSpec((n2, wide), lambda b: (b, 0)),
            pl.BlockSpec((seq, FOURIER_W), lambda b: (b, 0)),
            const((2 * n2, n2)), const((n2, wide)), const((n2, wide)),
            const((2 * FOURIER_W, FOURIER_W)),
        ],
        out_specs=pl.BlockSpec((seq, FOURIER_W), lambda b: (b, 0)),
        out_shape=jax.ShapeDtypeStruct((n_tok, FOURIER_W), BF16),
        scratch_shapes=[pltpu.VMEM((2 * n2, wide), F32),
                        pltpu.VMEM((seq, 2 * FOURIER_W), BF16)],
        compiler_params=_cparams(("arbitrary",)),
        name="fourier_den",
    )(xw, sga, f1, twc, tws, cs2)


def _outproj_kernel(x_ref, mod_ref, ya_ref, zc_ref, zp_ref, zn_ref, bgs_ref, yc_ref,
                    cw_ref, cb_ref, w_ref, fg_ref, o_ref, *, seq, final_norm):
    tm = x_ref.shape[0]
    z = zc_ref[...].astype(F32)
    row = lax.broadcasted_iota(jnp.int32, z.shape, 0)
    pos = (pl.program_id(0) * tm + row) & (seq - 1)
    prev_row = zp_ref[HALO_ROWS - 1:HALO_ROWS, :].astype(F32)
    next_row = zn_ref[0:1, :].astype(F32)
    z_prev = jnp.where(row == 0, prev_row, pltpu.roll(z, 1, axis=0))
    z_prev = jnp.where(pos == 0, 0.0, z_prev)
    z_next = jnp.where(row == tm - 1, next_row, pltpu.roll(z, tm - 1, axis=0))
    z_next = jnp.where(pos == seq - 1, 0.0, z_next)
    cw = cw_ref[...]
    conv = z_prev * cw[0:1] + z * cw[1:2] + z_next * cw[2:3] + cb_ref[...]
    yb = (bgs_ref[...].astype(F32) * conv).astype(BF16)

    mixed = jnp.concatenate([ya_ref[...], yb, yc_ref[...]], axis=-1)
    out = jnp.dot(mixed, w_ref[0], preferred_element_type=F32)
    gate = mod_ref[0][:, 2 * D_MODEL:]
    xn = x_ref[...] + gate * out
    if final_norm:
        ms = jnp.mean(xn * xn, axis=-1, keepdims=True)
        xn = xn * lax.rsqrt(ms + EPS) * fg_ref[...]
    o_ref[...] = xn


def _out_projection(x, mod_rows, ya, zc, bgs, yc, conv_w, conv_b, w_out_bf16, layer, final_g,
                    *, seq, rows_per_mod, final_norm):
    n_tok = x.shape[0]
    tm = OUT_TILE
    n_tiles = n_tok // tm
    halo_per_tile = tm // HALO_ROWS
    n_halo = n_tok // HALO_ROWS
    row = lambda w: pl.BlockSpec((tm, w), lambda i: (i, 0))
    return pl.pallas_call(
        functools.partial(_outproj_kernel, seq=seq, final_norm=final_norm),
        grid=(n_tiles,),
        in_specs=[
            row(D_MODEL),
            pl.BlockSpec((1, 1, 3 * D_MODEL), lambda i: (i * tm // rows_per_mod, 0, 0)),
            row(FOURIER_W),
            row(CONV_W),
            pl.BlockSpec((HALO_ROWS, CONV_W), lambda i: (jnp.maximum(i * halo_per_tile - 1, 0), 0)),
            pl.BlockSpec((HALO_ROWS, CONV_W),
                         lambda i: (jnp.minimum((i + 1) * halo_per_tile, n_halo - 1), 0)),
            row(CONV_W),
            row(ATTN_W),
            pl.BlockSpec((3, CONV_W), lambda i: (0, 0)),
            pl.BlockSpec((1, CONV_W), lambda i: (0, 0)),
            pl.BlockSpec((1, D_MODEL, D_MODEL), lambda i: (layer, 0, 0)),
            pl.BlockSpec((1, D_MODEL), lambda i: (0, 0)),
        ],
        out_specs=row(D_MODEL),
        out_shape=jax.ShapeDtypeStruct((n_tok, D_MODEL), F32),
        compiler_params=_cparams(("arbitrary",)),
        name="out_projection",
    )(x, mod_rows, ya, zc, zc, zc, bgs, yc, conv_w, conv_b.reshape(1, CONV_W), w_out_bf16,
      final_g.reshape(1, D_MODEL))


def kernel(x_prompt, x_sample, cache_k, cache_v, c, c_ctx, norm_g, w_mod, b_mod, w_in, conv_w,
           conv_b, lam_vec, subln_g, w_out, final_g):
    batch, seq, _ = x_prompt.shape
    dec_batch, dec_seq, _ = x_sample.shape

    cvecs = jnp.concatenate(
        [c_ctx[None, :], c, jnp.zeros((MOD_ROWS - 1 - dec_batch, D_MODEL), F32)], axis=0)
    mod = _modulation(cvecs, w_mod, b_mod)

    w_in_b = w_in.astype(BF16)
    w_out_b = w_out.astype(BF16)
    rope_tabs = tuple(jnp.asarray(t) for t in _rope_tables(dec_seq))
    cs = jnp.asarray(_chan_tables()).astype(BF16)
    dn_ctx = jnp.asarray(_dft_tables(seq)).astype(BF16)
    f1, twc, tws, cs2 = (jnp.asarray(t) for t in _ct_tables(dec_seq))
    f1 = f1.astype(BF16)
    cs2 = cs2.astype(BF16)

    xc = x_prompt.reshape(batch * seq, D_MODEL)
    xl = x_sample.reshape(dec_batch * dec_seq, D_MODEL)
    new_k = new_v = None
    for l in range(DEPTH):
        lam_init = 0.8 - 0.6 * math.exp(-0.3 * l)
        last = l == DEPTH - 1

        mod_c = mod[l, 0:1][:, None, :]
        fa, sga, zc, bgs, q, k, v, sgc, new_k, new_v = _in_projection(
            xc, mod_c, norm_g[l], w_in_b, l, None, rows_per_mod=xc.shape[0],
            kv_out=(seq, new_k, new_v))
        yc = _attention_ctx(q, k, v, sgc, lam_vec[l], subln_g[l], seq=seq, lam_init=lam_init)
        ya = _fourier_ctx(fa, sga, cs, dn_ctx, seq=seq)
        xc = _out_projection(xc, mod_c, ya, zc, bgs, yc, conv_w[l], conv_b[l], w_out_b, l, final_g,
                             seq=seq, rows_per_mod=xc.shape[0], final_norm=last)

        mod_d = mod[l, 1:1 + dec_batch][:, None, :]
        fa, sga, zc, bgs, q, k, v, sgc = _in_projection(
            xl, mod_d, norm_g[l], w_in_b, l, rope_tabs, rows_per_mod=dec_seq,
            decimate_fa=True)
        yc = _attention_den(q, k, v, sgc, cache_k, cache_v, lam_vec[l], subln_g[l], layer=l,
                            seq=dec_seq, lam_init=lam_init)
        ya = _fourier_den(fa, sga, f1, twc, tws, cs2, seq=dec_seq)
        xl = _out_projection(xl, mod_d, ya, zc, bgs, yc, conv_w[l], conv_b[l], w_out_b, l, final_g,
                             seq=dec_seq, rows_per_mod=dec_seq, final_norm=last)

    y_prompt = xc.reshape(batch, seq, D_MODEL)
    y_sample = xl.reshape(dec_batch, dec_seq, D_MODEL)
    return (y_prompt, y_sample, new_k, new_v)
```

```python
import functools
import math

import numpy as np
import jax
import jax.numpy as jnp
from jax import lax
from jax.experimental import pallas as pl
from jax.experimental.pallas import tpu as pltpu

D_MODEL = 1024
DEPTH = 2
GRID_W = 64
FOURIER_W = 256
CONV_W = 256
ATTN_W = 512
N_HEADS = 4
HEAD_DIM = 64
QK_W = 128
ROPE_BASE = 10000.0
EPS = 1e-6
IN_DIM = 3584

F32 = jnp.float32
BF16 = jnp.bfloat16

VMEM_LIMIT_BYTES = 56 * 1024 * 1024
TOKEN_TILE = 512
OUT_TILE = 1024
Q_TILE = 256
KEY_CHUNK = 256
ONES_ROWS = 16
LANES = 128
FFT_RADIX = 8
FFT_ROWS = 16
SQRT_HALF = 0.7071067811865476
CTX_FOURIER_SEQS = 4
HALO_ROWS = 16
MOD_ROWS = 8
LOG2E = 1.4426950408889634

_COLS = {}
_off = 0
for _name, _w in (("fa", 256), ("ga", 256), ("bg", 256), ("cg", 256), ("hc", 256), ("gb", 256),
                  ("q", 512), ("k", 512), ("v", 512), ("gc", 512)):
    _COLS[_name] = (_off, _off + _w)
    _off += _w


def _silu(x):
    return x * (1.0 / (1.0 + jnp.exp(-x)))


def _cparams(sem):
    return pltpu.CompilerParams(dimension_semantics=sem, vmem_limit_bytes=VMEM_LIMIT_BYTES)


def _rope_tables(n_tokens):
    n = np.arange(n_tokens)
    row = (n // GRID_W).astype(np.float64)
    col = (n % GRID_W).astype(np.float64)
    j = np.arange(QK_W)
    jj = j % HEAD_DIM
    idx = jj % 32
    inv = 1.0 / (ROPE_BASE ** (2.0 * (idx % 16) / 32.0))
    pos = np.where((jj < 32)[None, :], row[:, None], col[:, None])
    ang = pos * inv[None, :]
    cos = np.cos(ang)
    sin = np.sin(ang)
    first = (idx < 16)[None, :]
    sin_a = np.where(first, -sin, 0.0)
    sin_b = np.where(first, 0.0, sin)
    return (np.asarray(cos, np.float32), np.asarray(sin_a, np.float32), np.asarray(sin_b, np.float32))


def _dft_tables(n):
    k = np.arange(n)
    kn = (k[:, None] * k[None, :]) % n
    ang = 2.0 * np.pi * kn / n
    return np.asarray(np.concatenate([np.cos(ang), -np.sin(ang)], axis=1) / math.sqrt(n), np.float32)


def _chan_tables():
    k = np.arange(FOURIER_W)
    kn = (k[:, None] * k[None, :]) % FOURIER_W
    ang = 2.0 * np.pi * kn / FOURIER_W
    return np.asarray(np.concatenate([np.cos(ang), np.sin(ang)], axis=1) / math.sqrt(FOURIER_W), np.float32)


def _ct_tables(n):
    n2 = n // FFT_RADIX
    k = np.arange(n2)
    ang = 2.0 * np.pi * ((k[:, None] * k[None, :]) % n2) / n2
    f1 = np.concatenate([np.cos(ang), -np.sin(ang)], axis=0) / math.sqrt(n * FOURIER_W)
    tw = 2.0 * np.pi * k[:, None] * np.arange(FFT_RADIX)[None, :] / n
    twc = np.repeat(np.cos(tw), FOURIER_W, axis=1)
    tws = np.repeat(np.sin(tw), FOURIER_W, axis=1)
    c = np.arange(FOURIER_W)
    angc = 2.0 * np.pi * ((c[:, None] * c[None, :]) % FOURIER_W) / FOURIER_W
    cs2 = np.concatenate([np.cos(angc), np.sin(angc)], axis=0)
    return tuple(np.asarray(t, np.float32) for t in (f1, twc, tws, cs2))


def _mod_kernel(c_ref, w_ref, b_ref, o_ref):
    s = _silu(c_ref[...]).astype(BF16)
    w = w_ref[0].astype(BF16)
    o_ref[0] = jnp.dot(s, w, preferred_element_type=F32) + b_ref[0]


def _modulation(cvecs, w_mod, b_mod):
    chunk = D_MODEL
    n_chunks = 3 * D_MODEL // chunk
    return pl.pallas_call(
        _mod_kernel,
        grid=(DEPTH, n_chunks),
        in_specs=[
            pl.BlockSpec((MOD_ROWS, D_MODEL), lambda l, j: (0, 0)),
            pl.BlockSpec((1, D_MODEL, chunk), lambda l, j: (l, 0, j)),
            pl.BlockSpec((1, 1, chunk), lambda l, j: (l, 0, j)),
        ],
        out_specs=pl.BlockSpec((1, MOD_ROWS, chunk), lambda l, j: (l, 0, j)),
        out_shape=jax.ShapeDtypeStruct((DEPTH, MOD_ROWS, 3 * D_MODEL), F32),
        compiler_params=_cparams(("arbitrary", "arbitrary")),
        name="modulation",
    )(cvecs, w_mod, b_mod.reshape(DEPTH, 1, 3 * D_MODEL))


def _store_heads(ref, t):
    seqs, _, seq, _, _ = ref.shape
    for s in range(seqs):
        for hd in range(N_HEADS):
            ref[s, 0, :, hd, :] = t[s * seq:(s + 1) * seq, hd * QK_W:(hd + 1) * QK_W]


def _inproj_kernel(*refs, rope, emit_kv, n_alias, decimate_fa):
    n_in = 4 + (3 if rope else 0)
    x_ref, mod_ref, g_ref, w3_ref = refs[:4]
    w_ref = w3_ref.at[0]
    if rope:
        cos_ref, sa_ref, sb_ref = refs[4:7]
    outs = refs[n_in + n_alias:]
    fa_ref, sga_ref, zc_ref, bgs_ref, q_ref, k_ref, v_ref, sgc_ref = outs[:8]
    n_out = 10 if emit_kv else 8

    x = x_ref[...]
    ms = jnp.mean(x * x, axis=-1, keepdims=True)
    y = x * lax.rsqrt(ms + EPS) * g_ref[...]
    m = mod_ref[0]
    h = (y * (1.0 + m[:, D_MODEL:2 * D_MODEL]) + m[:, :D_MODEL]).astype(BF16)

    def proj(name):
        lo, hi = _COLS[name]
        return jnp.dot(h, w_ref[:, lo:hi], preferred_element_type=F32)

    if decimate_fa:
        stage_ref = outs[n_out]
        fa = proj("fa")
        halves = stage_ref.shape[0]
        for hf in range(halves):
            stage_ref[hf] = fa[:, hf * LANES:(hf + 1) * LANES]
        rows = stage_ref.shape[1] // FFT_RADIX
        for n1 in range(FFT_RADIX):
            for hf in range(halves):
                lo = n1 * FOURIER_W + hf * LANES
                fa_ref[:, lo:lo + LANES] = (
                    stage_ref[hf, pl.ds(n1, rows, stride=FFT_RADIX), :].astype(BF16))
    else:
        fa_ref[...] = proj("fa").astype(BF16)
    sga_ref[...] = _silu(proj("ga")).astype(BF16)
    zc_ref[...] = (proj("cg") * proj("hc")).astype(BF16)
    bgs_ref[...] = (proj("bg") * _silu(proj("gb"))).astype(BF16)
    sgc_ref[...] = _silu(proj("gc")).astype(BF16)

    q = proj("q")
    k = proj("k")
    v = proj("v")
    if emit_kv:
        _store_heads(outs[8], k)
        _store_heads(outs[9], v)
    v_ref[...] = v.astype(BF16)

    q_scale = HEAD_DIM ** -0.5 * LOG2E
    if rope:
        cos = cos_ref[...]
        sa = sa_ref[...]
        sb = sb_ref[...]
        for hd in range(N_HEADS):
            sl = slice(hd * QK_W, (hd + 1) * QK_W)
            for t, ref, scale in ((q, q_ref, q_scale), (k, k_ref, None)):
                th = t[:, sl]
                r = (th * cos + pltpu.roll(th, QK_W - 16, axis=1) * sa
                     + pltpu.roll(th, 16, axis=1) * sb)
                if scale is not None:
                    r = r * scale
                ref[:, sl] = r.astype(BF16)
    else:
        q_ref[...] = (q * q_scale).astype(BF16)
        k_ref[...] = k.astype(BF16)


def _in_projection(x, mod_rows, norm_g, w_in_bf16, layer, rope_tabs, *, rows_per_mod, kv_out=None,
                   decimate_fa=False):
    n_tok = x.shape[0]
    tm = TOKEN_TILE
    n_tiles = n_tok // tm
    rope = rope_tabs is not None
    emit_kv = kv_out is not None
    row = lambda w: pl.BlockSpec((tm, w), lambda i: (i, 0))
    in_specs = [
        row(D_MODEL),
        pl.BlockSpec((1, 1, 3 * D_MODEL), lambda i: (i * tm // rows_per_mod, 0, 0)),
        pl.BlockSpec((1, D_MODEL), lambda i: (0, 0)),
        pl.BlockSpec((1, D_MODEL, IN_DIM), lambda i: (layer, 0, 0)),
    ]
    args = [x, mod_rows, norm_g.reshape(1, D_MODEL), w_in_bf16]
    if rope:
        seq_tiles = rope_tabs[0].shape[0] // tm
        tab = pl.BlockSpec((tm, QK_W), lambda i: (i % seq_tiles, 0))
        in_specs += [tab, tab, tab]
        args += list(rope_tabs)
    widths = (256, 256, 256, 256, 512, 512, 512, 512)
    out_specs = [row(w) for w in widths]
    out_shape = [jax.ShapeDtypeStruct((n_tok, w), BF16) for w in widths]
    if decimate_fa:
        out_specs[0] = pl.BlockSpec((tm // FFT_RADIX, FFT_RADIX * FOURIER_W), lambda i: (i, 0))
        out_shape[0] = jax.ShapeDtypeStruct((n_tok // FFT_RADIX, FFT_RADIX * FOURIER_W), BF16)
    aliases = {}
    n_alias = 0
    if emit_kv:
        seq, new_k, new_v = kv_out
        seqs = tm // seq
        kv_blk = pl.BlockSpec((seqs, 1, seq, N_HEADS, QK_W), lambda i: (i, layer, 0, 0, 0))
        out_specs += [kv_blk, kv_blk]
        out_shape += [jax.ShapeDtypeStruct((n_tok // seq, DEPTH, seq, N_HEADS, QK_W), F32)] * 2
        if new_k is not None:
            aliases = {len(args): len(widths), len(args) + 1: len(widths) + 1}
            in_specs += [pl.BlockSpec(memory_space=pl.ANY)] * 2
            args += [new_k, new_v]
            n_alias = 2
    return pl.pallas_call(
        functools.partial(_inproj_kernel, rope=rope, emit_kv=emit_kv, n_alias=n_alias,
                          decimate_fa=decimate_fa),
        grid=(n_tiles,),
        in_specs=in_specs,
        out_specs=out_specs,
        out_shape=out_shape,
        scratch_shapes=[pltpu.VMEM((FOURIER_W // LANES, tm, LANES), F32)] if decimate_fa else [],
        input_output_aliases=aliases,
        compiler_params=_cparams(("arbitrary",)),
        name="in_projection_rope" if rope else "in_projection",
    )(*args)


def _lambda(lam_ref, lam_init):
    lv = lam_ref[...]
    a = jnp.sum(lv[0:1] * lv[1:2], axis=-1, keepdims=True)
    b = jnp.sum(lv[2:3] * lv[3:4], axis=-1, keepdims=True)
    return jnp.exp(a) - jnp.exp(b) + lam_init


def _stack_masked(q_t):
    first = lax.broadcasted_iota(jnp.int32, q_t.shape, 0) < HEAD_DIM
    return jnp.concatenate([jnp.where(first, q_t, 0.0), jnp.where(first, 0.0, q_t)],
                           axis=1).astype(BF16)


def _scores_chunk(k, qq_t, m):
    s = jnp.dot(k, qq_t, preferred_element_type=F32)
    mc = jnp.max(s, axis=0, keepdims=True)
    return s, (mc if m is None else jnp.maximum(m, mc))


def _with_ones_rows(v_t):
    return jnp.concatenate([v_t, jnp.ones((ONES_ROWS, v_t.shape[1]), F32)], axis=0).astype(BF16)


def _values_chunk(s, m, v1_t, acc):
    e = jnp.exp2(s - m).astype(BF16)
    oc = jnp.dot(v1_t, e, preferred_element_type=F32)
    return oc if acc is None else acc + oc


def _combine_t(acc, lam):
    tq = acc.shape[1] // 2
    r = 1.0 / acc[QK_W:QK_W + 1, :]
    o_t = acc[:QK_W, :]
    return o_t[:, :tq] * r[:, :tq] - o_t[:, tq:] * (r[:, tq:] * lam)


def _subln_gate(o, sg, sgc, lam_init):
    ms = jnp.mean(o * o, axis=-1, keepdims=True)
    y = o * lax.rsqrt(ms + EPS) * sg * (1.0 - lam_init)
    return (y * sgc.astype(F32)).astype(BF16)


def _attn_ctx_kernel(lam_ref, sg_ref, q_ref, k_ref, v_ref, sgc_ref, o_ref, *, lam_init):
    lam = _lambda(lam_ref, lam_init)
    sg = sg_ref[...]
    q_t = q_ref[...].astype(F32).T
    v_t = v_ref[...].astype(F32).T
    y_t = []
    for hd in range(N_HEADS):
        sl = slice(hd * QK_W, (hd + 1) * QK_W)
        s, m = _scores_chunk(k_ref[:, sl], _stack_masked(q_t[sl]), None)
        y_t.append(_combine_t(_values_chunk(s, m, _with_ones_rows(v_t[sl]), None), lam))
    y = jnp.concatenate(y_t, axis=0).T
    for hd in range(N_HEADS):
        sl = slice(hd * QK_W, (hd + 1) * QK_W)
        o_ref[:, sl] = _subln_gate(y[:, sl], sg, sgc_ref[:, sl], lam_init)


def _attention_ctx(q, k, v, sgc, lam_vec, subln_g, *, seq, lam_init):
    n_tok = q.shape[0]
    blk = pl.BlockSpec((seq, ATTN_W), lambda b: (b, 0))
    return pl.pallas_call(
        functools.partial(_attn_ctx_kernel, lam_init=lam_init),
        grid=(n_tok // seq,),
        in_specs=[
            pl.BlockSpec((4, HEAD_DIM), lambda b: (0, 0)),
            pl.BlockSpec((1, QK_W), lambda b: (0, 0)),
            blk, blk, blk, blk,
        ],
        out_specs=blk,
        out_shape=jax.ShapeDtypeStruct((n_tok, ATTN_W), BF16),
        compiler_params=_cparams(("arbitrary",)),
        name="attention_ctx",
    )(lam_vec, subln_g.reshape(1, QK_W), q, k, v, sgc)


def _attn_den_kernel(lam_ref, sg_ref, q_ref, kn_ref, vn_ref, kc_ref, vc_ref, sgc_ref, o_ref,
                     kcs_ref, vt_ref, qq0_ref, qq1_ref, s0_ref, s1_ref, acc0_ref, acc1_ref,
                     *, lam_init):
    seq = q_ref.shape[0]
    past = kcs_ref.shape[1]
    past_chunks = past // KEY_CHUNK
    n_chunks = (past + seq) // KEY_CHUNK
    n_tiles = seq // Q_TILE
    n_elems = n_tiles * N_HEADS
    qq_refs = (qq0_ref, qq1_ref)
    s_refs = (s0_ref, s1_ref)
    acc_refs = (acc0_ref, acc1_ref)
    lam = _lambda(lam_ref, lam_init)
    sg = sg_ref[...]

    def lanes(hd):
        return slice(hd * QK_W, (hd + 1) * QK_W)

    def chunk(c):
        return slice(c * KEY_CHUNK, (c + 1) * KEY_CHUNK)

    def tile_rows(t):
        if isinstance(t, int):
            return pl.ds(t * Q_TILE, Q_TILE)
        return pl.ds(pl.multiple_of(t * Q_TILE, Q_TILE), Q_TILE)

    for hd in range(N_HEADS):
        kcs_ref[hd] = kc_ref[0, 0, :, hd, :].astype(BF16)
        vt_ref[hd, :, :past] = _with_ones_rows(vc_ref[0, 0, :, hd, :].T)
        vt_ref[hd, :, past:] = _with_ones_rows(vn_ref[:, lanes(hd)].astype(F32).T)

    def keys(hd, c):
        if c < past_chunks:
            return kcs_ref[hd, chunk(c), :]
        return kn_ref[chunk(c - past_chunks), lanes(hd)]

    def prep(t, hd, par):
        qq_refs[par][...] = _stack_masked(q_ref[tile_rows(t), lanes(hd)].astype(F32).T)

    def finish(t, hd, par):
        rows = tile_rows(t)
        y = _combine_t(acc_refs[par][...], lam).T
        o_ref[rows, lanes(hd)] = _subln_gate(y, sg, sgc_ref[rows, lanes(hd)], lam_init)

    def step(t, j, m_cur, first=False, last=False):
        def elem(off):
            return t + (j + off) // N_HEADS, (j + off) % N_HEADS

        par = j % 2
        e_static = N_HEADS * t + j if isinstance(t, int) else None
        do_finish = not (first and j == 0)
        do_scores = not (last and e_static + 1 >= n_elems)
        do_prep = not (last and e_static + 2 >= n_elems)
        if do_prep:
            prep(*elem(2), par)
        hd_nxt = elem(1)[1]
        qq = qq_refs[1 - par][...] if do_scores else None
        m_nxt = None
        acc = None
        for c in range(n_chunks):
            if do_scores:
                s, m_nxt = _scores_chunk(keys(hd_nxt, c), qq, m_nxt)
                s_refs[1 - par][chunk(c), :] = s
            acc = _values_chunk(s_refs[par][chunk(c), :], m_cur, vt_ref[j, :, chunk(c)], acc)
        acc_refs[par][...] = acc
        if do_finish:
            finish(*elem(-1), 1 - par)
        return m_nxt

    def tile_steps(t, m, **edge):
        for j in range(N_HEADS):
            m = step(t, j, m, **edge)
        return m

    prep(0, 0, 0)
    prep(0, 1, 1)
    qq = qq0_ref[...]
    m = None
    for c in range(n_chunks):
        s, m = _scores_chunk(keys(0, c), qq, m)
        s0_ref[chunk(c), :] = s
    m = tile_steps(0, m, first=True)
    m = lax.fori_loop(1, n_tiles - 1, tile_steps, m)
    tile_steps(n_tiles - 1, m, last=True)
    finish(n_tiles - 1, N_HEADS - 1, (n_elems - 1) % 2)


def _attention_den(q, k, v, sgc, cache_k, cache_v, lam_vec, subln_g, *, layer, seq, lam_init):
    n_tok = q.shape[0]
    past = cache_k.shape[2]
    blk = pl.BlockSpec((seq, ATTN_W), lambda b: (b, 0))
    cblk = pl.BlockSpec((1, 1, past, N_HEADS, QK_W), lambda b: (b, layer, 0, 0, 0))
    return pl.pallas_call(
        functools.partial(_attn_den_kernel, lam_init=lam_init),
        grid=(n_tok // seq,),
        in_specs=[
            pl.BlockSpec((4, HEAD_DIM), lambda b: (0, 0)),
            pl.BlockSpec((1, QK_W), lambda b: (0, 0)),
            blk, blk, blk, cblk, cblk, blk,
        ],
        out_specs=blk,
        out_shape=jax.ShapeDtypeStruct((n_tok, ATTN_W), BF16),
        scratch_shapes=[
            pltpu.VMEM((N_HEADS, past, QK_W), BF16),
            pltpu.VMEM((N_HEADS, QK_W + ONES_ROWS, past + seq), BF16),
            pltpu.VMEM((QK_W, 2 * Q_TILE), BF16),
            pltpu.VMEM((QK_W, 2 * Q_TILE), BF16),
            pltpu.VMEM((past + seq, 2 * Q_TILE), F32),
            pltpu.VMEM((past + seq, 2 * Q_TILE), F32),
            pltpu.VMEM((QK_W + ONES_ROWS, 2 * Q_TILE), F32),
            pltpu.VMEM((QK_W + ONES_ROWS, 2 * Q_TILE), F32),
        ],
        compiler_params=_cparams(("arbitrary",)),
        name="attention_den",
    )(lam_vec, subln_g.reshape(1, QK_W), q, k, v, cache_k, cache_v, sgc)


def _fourier_ctx_kernel(fa_ref, sga_ref, cs_ref, dn_ref, o_ref, *, seq):
    ab = jnp.dot(fa_ref[...], cs_ref[...], preferred_element_type=F32).astype(BF16)
    for s in range(fa_ref.shape[0] // seq):
        rows = slice(s * seq, (s + 1) * seq)
        ab2 = jnp.concatenate([ab[rows, :FOURIER_W], ab[rows, FOURIER_W:]], axis=0)
        f = jnp.dot(dn_ref[...], ab2, preferred_element_type=F32)
        o_ref[rows, :] = (f * sga_ref[rows, :].astype(F32)).astype(BF16)


def _fourier_ctx(fa, sga, cs, dn, *, seq):
    n_tok = fa.shape[0]
    blk = pl.BlockSpec((CTX_FOURIER_SEQS * seq, FOURIER_W), lambda b: (b, 0))
    return pl.pallas_call(
        functools.partial(_fourier_ctx_kernel, seq=seq),
        grid=(n_tok // (CTX_FOURIER_SEQS * seq),),
        in_specs=[blk, blk,
                  pl.BlockSpec((FOURIER_W, 2 * FOURIER_W), lambda b: (0, 0)),
                  pl.BlockSpec((seq, 2 * seq), lambda b: (0, 0))],
        out_specs=blk,
        out_shape=jax.ShapeDtypeStruct((n_tok, FOURIER_W), BF16),
        compiler_params=_cparams(("arbitrary",)),
        name="fourier_ctx",
    )(fa, sga, cs, dn)


def _cadd(a, b):
    return (a[0] + b[0], a[1] + b[1])


def _csub(a, b):
    return (a[0] - b[0], a[1] - b[1])


def _mul_neg_i(a):
    return (a[1], -a[0])


def _mul_w8_1(a):
    return ((a[0] + a[1]) * SQRT_HALF, (a[1] - a[0]) * SQRT_HALF)


def _mul_w8_3(a):
    return ((a[1] - a[0]) * SQRT_HALF, (-a[0] - a[1]) * SQRT_HALF)


def _fft4(a0, a1, a2, a3):
    e0, e1 = _cadd(a0, a2), _csub(a0, a2)
    o0, o1 = _cadd(a1, a3), _mul_neg_i(_csub(a1, a3))
    return [_cadd(e0, o0), _cadd(e1, o1), _csub(e0, o0), _csub(e1, o1)]


def _fft8(x):
    e = _fft4(x[0], x[2], x[4], x[6])
    o = _fft4(x[1], x[3], x[5], x[7])
    t = [o[0], _mul_w8_1(o[1]), _mul_neg_i(o[2]), _mul_w8_3(o[3])]
    return [_cadd(e[k], t[k]) for k in range(4)] + [_csub(e[k], t[k]) for k in range(4)]


def _fourier_den_kernel(xw_ref, sga_ref, f1_ref, twc_ref, tws_ref, cs2_ref, o_ref, g_ref, x_ref):
    n2 = xw_ref.shape[0]
    g_ref[...] = jnp.dot(f1_ref[...], xw_ref[...], preferred_element_type=F32)

    def chunk(i, carry):
        r = pl.multiple_of(i * FFT_ROWS, FFT_ROWS)
        re_rows = pl.ds(r, FFT_ROWS)
        im_rows = pl.ds(n2 + r, FFT_ROWS)
        xs = []
        for n1 in range(FFT_RADIX):
            lanes = slice(n1 * FOURIER_W, (n1 + 1) * FOURIER_W)
            gr = g_ref[re_rows, lanes]
            gi = g_ref[im_rows, lanes]
            if n1 > 0:
                c = twc_ref[re_rows, lanes]
                s = tws_ref[re_rows, lanes]
                gr, gi = gr * c + gi * s, gi * c - gr * s
            xs.append((gr, gi))
        for k1, (xr, xi) in enumerate(_fft8(xs)):
            out_rows = pl.ds(k1 * n2 + r, FFT_ROWS)
            x_ref[out_rows, :FOURIER_W] = xr.astype(BF16)
            x_ref[out_rows, FOURIER_W:] = xi.astype(BF16)
        return carry

    lax.fori_loop(0, n2 // FFT_ROWS, chunk, 0)
    f = jnp.dot(x_ref[...], cs2_ref[...], preferred_element_type=F32)
    o_ref[...] = (f * sga_ref[...].astype(F32)).astype(BF16)


def _fourier_den(xw, sga, f1, twc, tws, cs2, *, seq):
    n_tok = sga.shape[0]
    n2 = seq // FFT_RADIX
    wide = FFT_RADIX * FOURIER_W
    const = lambda shape: pl.BlockSpec(shape, lambda b: (0, 0))
    return pl.pallas_call(
        _fourier_den_kernel,
        grid=(n_tok // seq,),
        in_specs=[
            pl.BlockSpec((n2, wide), lambda b: (b, 0)),
            pl.BlockSpec((seq, FOURIER_W), lambda b: (b, 0)),
            const((2 * n2, n2)), const((n2, wide)), const((n2, wide)),
            const((2 * FOURIER_W, FOURIER_W)),
        ],
        out_specs=pl.BlockSpec((seq, FOURIER_W), lambda b: (b, 0)),
        out_shape=jax.ShapeDtypeStruct((n_tok, FOURIER_W), BF16),
        scratch_shapes=[pltpu.VMEM((2 * n2, wide), F32),
                        pltpu.VMEM((seq, 2 * FOURIER_W), BF16)],
        compiler_params=_cparams(("arbitrary",)),
        name="fourier_den",
    )(xw, sga, f1, twc, tws, cs2)


def _outproj_kernel(x_ref, mod_ref, ya_ref, zc_ref, zp_ref, zn_ref, bgs_ref, yc_ref,
                    cw_ref, cb_ref, w_ref, fg_ref, o_ref, *, seq, final_norm):
    tm = x_ref.shape[0]
    z = zc_ref[...].astype(F32)
    row = lax.broadcasted_iota(jnp.int32, z.shape, 0)
    pos = (pl.program_id(0) * tm + row) & (seq - 1)
    prev_row = zp_ref[HALO_ROWS - 1:HALO_ROWS, :].astype(F32)
    next_row = zn_ref[0:1, :].astype(F32)
    z_prev = jnp.where(row == 0, prev_row, pltpu.roll(z, 1, axis=0))
    z_prev = jnp.where(pos == 0, 0.0, z_prev)
    z_next = jnp.where(row == tm - 1, next_row, pltpu.roll(z, tm - 1, axis=0))
    z_next = jnp.where(pos == seq - 1, 0.0, z_next)
    cw = cw_ref[...]
    conv = z_prev * cw[0:1] + z * cw[1:2] + z_next * cw[2:3] + cb_ref[...]
    yb = (bgs_ref[...].astype(F32) * conv).astype(BF16)

    mixed = jnp.concatenate([ya_ref[...], yb, yc_ref[...]], axis=-1)
    out = jnp.dot(mixed, w_ref[0], preferred_element_type=F32)
    gate = mod_ref[0][:, 2 * D_MODEL:]
    xn = x_ref[...] + gate * out
    if final_norm:
        ms = jnp.mean(xn * xn, axis=-1, keepdims=True)
        xn = xn * lax.rsqrt(ms + EPS) * fg_ref[...]
    o_ref[...] = xn


def _out_projection(x, mod_rows, ya, zc, bgs, yc, conv_w, conv_b, w_out_bf16, layer, final_g,
                    *, seq, rows_per_mod, final_norm):
    n_tok = x.shape[0]
    tm = OUT_TILE
    n_tiles = n_tok // tm
    halo_per_tile = tm // HALO_ROWS
    n_halo = n_tok // HALO_ROWS
    row = lambda w: pl.BlockSpec((tm, w), lambda i: (i, 0))
    return pl.pallas_call(
        functools.partial(_outproj_kernel, seq=seq, final_norm=final_norm),
        grid=(n_tiles,),
        in_specs=[
            row(D_MODEL),
            pl.BlockSpec((1, 1, 3 * D_MODEL), lambda i: (i * tm // rows_per_mod, 0, 0)),
            row(FOURIER_W),
            row(CONV_W),
            pl.BlockSpec((HALO_ROWS, CONV_W), lambda i: (jnp.maximum(i * halo_per_tile - 1, 0), 0)),
            pl.BlockSpec((HALO_ROWS, CONV_W),
                         lambda i: (jnp.minimum((i + 1) * halo_per_tile, n_halo - 1), 0)),
            row(CONV_W),
            row(ATTN_W),
            pl.BlockSpec((3, CONV_W), lambda i: (0, 0)),
            pl.BlockSpec((1, CONV_W), lambda i: (0, 0)),
            pl.BlockSpec((1, D_MODEL, D_MODEL), lambda i: (layer, 0, 0)),
            pl.BlockSpec((1, D_MODEL), lambda i: (0, 0)),
        ],
        out_specs=row(D_MODEL),
        out_shape=jax.ShapeDtypeStruct((n_tok, D_MODEL), F32),
        compiler_params=_cparams(("arbitrary",)),
        name="out_projection",
    )(x, mod_rows, ya, zc, zc, zc, bgs, yc, conv_w, conv_b.reshape(1, CONV_W), w_out_bf16,
      final_g.reshape(1, D_MODEL))


def kernel(x_prompt, x_sample, cache_k, cache_v, c, c_ctx, norm_g, w_mod, b_mod, w_in, conv_w,
           conv_b, lam_vec, subln_g, w_out, final_g):
    batch, seq, _ = x_prompt.shape
    dec_batch, dec_seq, _ = x_sample.shape

    cvecs = jnp.concatenate(
        [c_ctx[None, :], c, jnp.zeros((MOD_ROWS - 1 - dec_batch, D_MODEL), F32)], axis=0)
    mod = _modulation(cvecs, w_mod, b_mod)

    w_in_b = w_in.astype(BF16)
    w_out_b = w_out.astype(BF16)
    rope_tabs = tuple(jnp.asarray(t) for t in _rope_tables(dec_seq))
    cs = jnp.asarray(_chan_tables()).astype(BF16)
    dn_ctx = jnp.asarray(_dft_tables(seq)).astype(BF16)
    f1, twc, tws, cs2 = (jnp.asarray(t) for t in _ct_tables(dec_seq))
    f1 = f1.astype(BF16)
    cs2 = cs2.astype(BF16)

    xc = x_prompt.reshape(batch * seq, D_MODEL)
    xl = x_sample.reshape(dec_batch * dec_seq, D_MODEL)
    new_k = new_v = None
    for l in range(DEPTH):
        lam_init = 0.8 - 0.6 * math.exp(-0.3 * l)
        last = l == DEPTH - 1

        mod_c = mod[l, 0:1][:, None, :]
        fa, sga, zc, bgs, q, k, v, sgc, new_k, new_v = _in_projection(
            xc, mod_c, norm_g[l], w_in_b, l, None, rows_per_mod=xc.shape[0],
            kv_out=(seq, new_k, new_v))
        yc = _attention_ctx(q, k, v, sgc, lam_vec[l], subln_g[l], seq=seq, lam_init=lam_init)
        ya = _fourier_ctx(fa, sga, cs, dn_ctx, seq=seq)
        xc = _out_projection(xc, mod_c, ya, zc, bgs, yc, conv_w[l], conv_b[l], w_out_b, l, final_g,
                             seq=seq, rows_per_mod=xc.shape[0], final_norm=last)

        mod_d = mod[l, 1:1 + dec_batch][:, None, :]
        fa, sga, zc, bgs, q, k, v, sgc = _in_projection(
            xl, mod_d, norm_g[l], w_in_b, l, rope_tabs, rows_per_mod=dec_seq,
            decimate_fa=True)
        yc = _attention_den(q, k, v, sgc, cache_k, cache_v, lam_vec[l], subln_g[l], layer=l,
                            seq=dec_seq, lam_init=lam_init)
        ya = _fourier_den(fa, sga, f1, twc, tws, cs2, seq=dec_seq)
        xl = _out_projection(xl, mod_d, ya, zc, bgs, yc, conv_w[l], conv_b[l], w_out_b, l, final_g,
                             seq=dec_seq, rows_per_mod=dec_seq, final_norm=last)

    y_prompt = xc.reshape(batch, seq, D_MODEL)
    y_sample = xl.reshape(dec_batch, dec_seq, D_MODEL)
    return (y_prompt, y_sample, new_k, new_v)
```

```python
import functools
import math

import numpy as np
import jax
import jax.numpy as jnp
from jax import lax
from jax.experimental import pallas as pl
from jax.experimental.pallas import tpu as pltpu

D_MODEL = 1024
DEPTH = 2
GRID_W = 64
FOURIER_W = 256
CONV_W = 256
ATTN_W = 512
N_HEADS = 4
HEAD_DIM = 64
QK_W = 128
ROPE_BASE = 10000.0
EPS = 1e-6
IN_DIM = 3584

F32 = jnp.float32
BF16 = jnp.bfloat16

VMEM_LIMIT_BYTES = 56 * 1024 * 1024
TOKEN_TILE = 1024
KV_TOKEN_TILE = 512
OUT_TILE = 1024
Q_TILE = 256
KEY_CHUNK = 256
ONES_ROWS = 16
LANES = 128
FFT_RADIX = 8
FFT_ROWS = 16
SQRT_HALF = 0.7071067811865476
CTX_ATTN_SEQS = 2
CTX_FOURIER_SEQS = 4
HALO_ROWS = 16
MOD_ROWS = 8
LOG2E = 1.4426950408889634

_COLS = {}
_off = 0
for _name, _w in (("fa", 256), ("ga", 256), ("bg", 256), ("cg", 256), ("hc", 256), ("gb", 256),
                  ("q", 512), ("k", 512), ("v", 512), ("gc", 512)):
    _COLS[_name] = (_off, _off + _w)
    _off += _w


def _silu(x):
    return x * (1.0 / (1.0 + jnp.exp(-x)))


def _cparams(sem):
    return pltpu.CompilerParams(dimension_semantics=sem, vmem_limit_bytes=VMEM_LIMIT_BYTES)


def _rope_tables(n_tokens):
    n = np.arange(n_tokens)
    row = (n // GRID_W).astype(np.float64)
    col = (n % GRID_W).astype(np.float64)
    j = np.arange(QK_W)
    jj = j % HEAD_DIM
    idx = jj % 32
    inv = 1.0 / (ROPE_BASE ** (2.0 * (idx % 16) / 32.0))
    pos = np.where((jj < 32)[None, :], row[:, None], col[:, None])
    ang = pos * inv[None, :]
    cos = np.cos(ang)
    sin = np.sin(ang)
    first = (idx < 16)[None, :]
    sin_a = np.where(first, -sin, 0.0)
    sin_b = np.where(first, 0.0, sin)
    return (np.asarray(cos, np.float32), np.asarray(sin_a, np.float32), np.asarray(sin_b, np.float32))


def _dft_tables(n):
    k = np.arange(n)
    kn = (k[:, None] * k[None, :]) % n
    ang = 2.0 * np.pi * kn / n
    return np.asarray(np.concatenate([np.cos(ang), -np.sin(ang)], axis=1) / math.sqrt(n), np.float32)


def _chan_tables():
    k = np.arange(FOURIER_W)
    kn = (k[:, None] * k[None, :]) % FOURIER_W
    ang = 2.0 * np.pi * kn / FOURIER_W
    return np.asarray(np.concatenate([np.cos(ang), np.sin(ang)], axis=1) / math.sqrt(FOURIER_W), np.float32)


def _ct_tables(n):
    n2 = n // FFT_RADIX
    k = np.arange(n2)
    ang = 2.0 * np.pi * ((k[:, None] * k[None, :]) % n2) / n2
    f1 = np.concatenate([np.cos(ang), -np.sin(ang)], axis=0) / math.sqrt(n * FOURIER_W)
    tw = 2.0 * np.pi * k[:, None] * np.arange(FFT_RADIX)[None, :] / n
    twc = np.repeat(np.cos(tw), FOURIER_W, axis=1)
    tws = np.repeat(np.sin(tw), FOURIER_W, axis=1)
    c = np.arange(FOURIER_W)
    angc = 2.0 * np.pi * ((c[:, None] * c[None, :]) % FOURIER_W) / FOURIER_W
    cs2 = np.concatenate([np.cos(angc), np.sin(angc)], axis=0)
    return tuple(np.asarray(t, np.float32) for t in (f1, twc, tws, cs2))


def _mod_kernel(c_ref, w_ref, b_ref, o_ref):
    s = _silu(c_ref[...]).astype(BF16)
    w = w_ref[0].astype(BF16)
    o_ref[0] = jnp.dot(s, w, preferred_element_type=F32) + b_ref[0]


def _modulation(cvecs, w_mod, b_mod):
    chunk = D_MODEL
    n_chunks = 3 * D_MODEL // chunk
    return pl.pallas_call(
        _mod_kernel,
        grid=(DEPTH, n_chunks),
        in_specs=[
            pl.BlockSpec((MOD_ROWS, D_MODEL), lambda l, j: (0, 0)),
            pl.BlockSpec((1, D_MODEL, chunk), lambda l, j: (l, 0, j)),
            pl.BlockSpec((1, 1, chunk), lambda l, j: (l, 0, j)),
        ],
        out_specs=pl.BlockSpec((1, MOD_ROWS, chunk), lambda l, j: (l, 0, j)),
        out_shape=jax.ShapeDtypeStruct((DEPTH, MOD_ROWS, 3 * D_MODEL), F32),
        compiler_params=_cparams(("arbitrary", "arbitrary")),
        name="modulation",
    )(cvecs, w_mod, b_mod.reshape(DEPTH, 1, 3 * D_MODEL))


def _store_heads(ref, t):
    seqs, _, seq, _, _ = ref.shape
    for s in range(seqs):
        for hd in range(N_HEADS):
            ref[s, 0, :, hd, :] = t[s * seq:(s + 1) * seq, hd * QK_W:(hd + 1) * QK_W]


def _inproj_kernel(*refs, rope, emit_kv, n_alias, decimate_fa):
    n_in = 4 + (3 if rope else 0)
    x_ref, mod_ref, g_ref, w3_ref = refs[:4]
    w_ref = w3_ref.at[0]
    if rope:
        cos_ref, sa_ref, sb_ref = refs[4:7]
    outs = refs[n_in + n_alias:]
    fa_ref, sga_ref, zc_ref, bgs_ref, q_ref, k_ref, v_ref, sgc_ref = outs[:8]
    n_out = 10 if emit_kv else 8

    x = x_ref[...]
    ms = jnp.mean(x * x, axis=-1, keepdims=True)
    y = x * lax.rsqrt(ms + EPS) * g_ref[...]
    m = mod_ref[0]
    h = (y * (1.0 + m[:, D_MODEL:2 * D_MODEL]) + m[:, :D_MODEL]).astype(BF16)

    def proj(name):
        lo, hi = _COLS[name]
        return jnp.dot(h, w_ref[:, lo:hi], preferred_element_type=F32)

    if decimate_fa:
        stage_ref = outs[n_out]
        fa = proj("fa")
        halves = stage_ref.shape[0]
        for hf in range(halves):
            stage_ref[hf] = fa[:, hf * LANES:(hf + 1) * LANES]
        rows = stage_ref.shape[1] // FFT_RADIX
        for n1 in range(FFT_RADIX):
            for hf in range(halves):
                lo = n1 * FOURIER_W + hf * LANES
                fa_ref[:, lo:lo + LANES] = (
                    stage_ref[hf, pl.ds(n1, rows, stride=FFT_RADIX), :].astype(BF16))
    else:
        fa_ref[...] = proj("fa").astype(BF16)
    sga_ref[...] = _silu(proj("ga")).astype(BF16)
    zc_ref[...] = (proj("cg") * proj("hc")).astype(BF16)
    bgs_ref[...] = (proj("bg") * _silu(proj("gb"))).astype(BF16)
    sgc_ref[...] = _silu(proj("gc")).astype(BF16)

    q = proj("q")
    k = proj("k")
    v = proj("v")
    if emit_kv:
        _store_heads(outs[8], k)
        _store_heads(outs[9], v)
    v_ref[...] = v.astype(BF16)

    q_scale = HEAD_DIM ** -0.5 * LOG2E
    if rope:
        cos = cos_ref[...]
        sa = sa_ref[...]
        sb = sb_ref[...]
        for hd in range(N_HEADS):
            sl = slice(hd * QK_W, (hd + 1) * QK_W)
            for t, ref, scale in ((q, q_ref, q_scale), (k, k_ref, None)):
                th = t[:, sl]
                r = (th * cos + pltpu.roll(th, QK_W - 16, axis=1) * sa
                     + pltpu.roll(th, 16, axis=1) * sb)
                if scale is not None:
                    r = r * scale
                ref[:, sl] = r.astype(BF16)
    else:
        q_ref[...] = (q * q_scale).astype(BF16)
        k_ref[...] = k.astype(BF16)


def _in_projection(x, mod_rows, norm_g, w_in_bf16, layer, rope_tabs, *, rows_per_mod, kv_out=None,
                   decimate_fa=False):
    n_tok = x.shape[0]
    rope = rope_tabs is not None
    emit_kv = kv_out is not None
    tm = KV_TOKEN_TILE if emit_kv else TOKEN_TILE
    n_tiles = n_tok // tm
    row = lambda w: pl.BlockSpec((tm, w), lambda i: (i, 0))
    in_specs = [
        row(D_MODEL),
        pl.BlockSpec((1, 1, 3 * D_MODEL), lambda i: (i * tm // rows_per_mod, 0, 0)),
        pl.BlockSpec((1, D_MODEL), lambda i: (0, 0)),
        pl.BlockSpec((1, D_MODEL, IN_DIM), lambda i: (layer, 0, 0)),
    ]
    args = [x, mod_rows, norm_g.reshape(1, D_MODEL), w_in_bf16]
    if rope:
        seq_tiles = rope_tabs[0].shape[0] // tm
        tab = pl.BlockSpec((tm, QK_W), lambda i: (i % seq_tiles, 0))
        in_specs += [tab, tab, tab]
        args += list(rope_tabs)
    widths = (256, 256, 256, 256, 512, 512, 512, 512)
    out_specs = [row(w) for w in widths]
    out_shape = [jax.ShapeDtypeStruct((n_tok, w), BF16) for w in widths]
    if decimate_fa:
        out_specs[0] = pl.BlockSpec((tm // FFT_RADIX, FFT_RADIX * FOURIER_W), lambda i: (i, 0))
        out_shape[0] = jax.ShapeDtypeStruct((n_tok // FFT_RADIX, FFT_RADIX * FOURIER_W), BF16)
    aliases = {}
    n_alias = 0
    if emit_kv:
        seq, new_k, new_v = kv_out
        seqs = tm // seq
        kv_blk = pl.BlockSpec((seqs, 1, seq, N_HEADS, QK_W), lambda i: (i, layer, 0, 0, 0))
        out_specs += [kv_blk, kv_blk]
        out_shape += [jax.ShapeDtypeStruct((n_tok // seq, DEPTH, seq, N_HEADS, QK_W), F32)] * 2
        if new_k is not None:
            aliases = {len(args): len(widths), len(args) + 1: len(widths) + 1}
            in_specs += [pl.BlockSpec(memory_space=pl.ANY)] * 2
            args += [new_k, new_v]
            n_alias = 2
    return pl.pallas_call(
        functools.partial(_inproj_kernel, rope=rope, emit_kv=emit_kv, n_alias=n_alias,
                          decimate_fa=decimate_fa),
        grid=(n_tiles,),
        in_specs=in_specs,
        out_specs=out_specs,
        out_shape=out_shape,
        scratch_shapes=[pltpu.VMEM((FOURIER_W // LANES, tm, LANES), F32)] if decimate_fa else [],
        input_output_aliases=aliases,
        compiler_params=_cparams(("arbitrary",)),
        name="in_projection_rope" if rope else "in_projection",
    )(*args)


def _lambda(lam_ref, lam_init):
    lv = lam_ref[...]
    a = jnp.sum(lv[0:1] * lv[1:2], axis=-1, keepdims=True)
    b = jnp.sum(lv[2:3] * lv[3:4], axis=-1, keepdims=True)
    return jnp.exp(a) - jnp.exp(b) + lam_init


def _stack_masked(q_t):
    first = lax.broadcasted_iota(jnp.int32, q_t.shape, 0) < HEAD_DIM
    return jnp.concatenate([jnp.where(first, q_t, 0.0), jnp.where(first, 0.0, q_t)],
                           axis=1).astype(BF16)


def _scores_chunk(k, qq_t, m):
    s = jnp.dot(k, qq_t, preferred_element_type=F32)
    mc = jnp.max(s, axis=0, keepdims=True)
    return s, (mc if m is None else jnp.maximum(m, mc))


def _with_ones_rows(v_t):
    return jnp.concatenate([v_t, jnp.ones((ONES_ROWS, v_t.shape[1]), F32)], axis=0).astype(BF16)


def _values_chunk(s, m, v1_t, acc):
    e = jnp.exp2(s - m).astype(BF16)
    oc = jnp.dot(v1_t, e, preferred_element_type=F32)
    return oc if acc is None else acc + oc


def _combine_t(acc, lam):
    tq = acc.shape[1] // 2
    r = 1.0 / acc[QK_W:QK_W + 1, :]
    o_t = acc[:QK_W, :]
    return o_t[:, :tq] * r[:, :tq] - o_t[:, tq:] * (r[:, tq:] * lam)


def _subln_gate(o, sg, sgc, lam_init):
    ms = jnp.mean(o * o, axis=-1, keepdims=True)
    y = o * lax.rsqrt(ms + EPS) * sg * (1.0 - lam_init)
    return (y * sgc.astype(F32)).astype(BF16)


def _attn_ctx_kernel(lam_ref, sg_ref, q_ref, k_ref, v_ref, sgc_ref, o_ref, *, lam_init, seq):
    lam = _lambda(lam_ref, lam_init)
    sg = sg_ref[...]
    n_seq = q_ref.shape[0] // seq
    q_t = q_ref[...].astype(F32).T
    v_t = v_ref[...].astype(F32).T
    pairs = [(slice(s * seq, (s + 1) * seq), slice(hd * QK_W, (hd + 1) * QK_W))
             for s in range(n_seq) for hd in range(N_HEADS)]
    qqs = [_stack_masked(q_t[hl, rows]) for rows, hl in pairs]
    v1s = [_with_ones_rows(v_t[hl, rows]) for rows, hl in pairs]
    sm = [_scores_chunk(k_ref[rows, hl], qq, None) for (rows, hl), qq in zip(pairs, qqs)]
    accs = [_values_chunk(s, m, v1, None) for (s, m), v1 in zip(sm, v1s)]
    y_t = [_combine_t(a, lam) for a in accs]
    for s in range(n_seq):
        rows = slice(s * seq, (s + 1) * seq)
        y = jnp.concatenate(y_t[s * N_HEADS:(s + 1) * N_HEADS], axis=0).T
        for hd in range(N_HEADS):
            hl = slice(hd * QK_W, (hd + 1) * QK_W)
            o_ref[rows, hl] = _subln_gate(y[:, hl], sg, sgc_ref[rows, hl], lam_init)


def _attention_ctx(q, k, v, sgc, lam_vec, subln_g, *, seq, lam_init):
    n_tok = q.shape[0]
    blk = pl.BlockSpec((CTX_ATTN_SEQS * seq, ATTN_W), lambda b: (b, 0))
    return pl.pallas_call(
        functools.partial(_attn_ctx_kernel, lam_init=lam_init, seq=seq),
        grid=(n_tok // (CTX_ATTN_SEQS * seq),),
        in_specs=[
            pl.BlockSpec((4, HEAD_DIM), lambda b: (0, 0)),
            pl.BlockSpec((1, QK_W), lambda b: (0, 0)),
            blk, blk, blk, blk,
        ],
        out_specs=blk,
        out_shape=jax.ShapeDtypeStruct((n_tok, ATTN_W), BF16),
        compiler_params=_cparams(("arbitrary",)),
        name="attention_ctx",
    )(lam_vec, subln_g.reshape(1, QK_W), q, k, v, sgc)


def _attn_den_kernel(lam_ref, sg_ref, q_ref, kn_ref, vn_ref, kc_ref, vc_ref, sgc_ref, o_ref,
                     kcs_ref, vt_ref, qq0_ref, qq1_ref, s0_ref, s1_ref, acc0_ref, acc1_ref,
                     *, lam_init):
    seq = q_ref.shape[0]
    past = kcs_ref.shape[1]
    past_chunks = past // KEY_CHUNK
    n_chunks = (past + seq) // KEY_CHUNK
    n_tiles = seq // Q_TILE
    n_elems = n_tiles * N_HEADS
    qq_refs = (qq0_ref, qq1_ref)
    s_refs = (s0_ref, s1_ref)
    acc_refs = (acc0_ref, acc1_ref)
    lam = _lambda(lam_ref, lam_init)
    sg = sg_ref[...]

    def lanes(hd):
        return slice(hd * QK_W, (hd + 1) * QK_W)

    def chunk(c):
        return slice(c * KEY_CHUNK, (c + 1) * KEY_CHUNK)

    def tile_rows(t):
        if isinstance(t, int):
            return pl.ds(t * Q_TILE, Q_TILE)
        return pl.ds(pl.multiple_of(t * Q_TILE, Q_TILE), Q_TILE)

    for hd in range(N_HEADS):
        kcs_ref[hd] = kc_ref[0, 0, :, hd, :].astype(BF16)
        vt_ref[hd, :, :past] = _with_ones_rows(vc_ref[0, 0, :, hd, :].T)
        vt_ref[hd, :, past:] = _with_ones_rows(vn_ref[:, lanes(hd)].astype(F32).T)

    def keys(hd, c):
        if c < past_chunks:
            return kcs_ref[hd, chunk(c), :]
        return kn_ref[chunk(c - past_chunks), lanes(hd)]

    def prep(t, hd, par):
        qq_refs[par][...] = _stack_masked(q_ref[tile_rows(t), lanes(hd)].astype(F32).T)

    def finish(t, hd, par):
        rows = tile_rows(t)
        y = _combine_t(acc_refs[par][...], lam).T
        o_ref[rows, lanes(hd)] = _subln_gate(y, sg, sgc_ref[rows, lanes(hd)], lam_init)

    def step(t, j, m_cur, first=False, last=False):
        def elem(off):
            return t + (j + off) // N_HEADS, (j + off) % N_HEADS

        par = j % 2
        e_static = N_HEADS * t + j if isinstance(t, int) else None
        do_finish = not (first and j == 0)
        do_scores = not (last and e_static + 1 >= n_elems)
        do_prep = not (last and e_static + 2 >= n_elems)
        if do_prep:
            prep(*elem(2), par)
        hd_nxt = elem(1)[1]
        qq = qq_refs[1 - par][...] if do_scores else None
        m_nxt = None
        acc = None
        for c in range(n_chunks):
            if do_scores:
                s, m_nxt = _scores_chunk(keys(hd_nxt, c), qq, m_nxt)
                s_refs[1 - par][chunk(c), :] = s
            acc = _values_chunk(s_refs[par][chunk(c), :], m_cur, vt_ref[j, :, chunk(c)], acc)
        acc_refs[par][...] = acc
        if do_finish:
            finish(*elem(-1), 1 - par)
        return m_nxt

    def tile_steps(t, m, **edge):
        for j in range(N_HEADS):
            m = step(t, j, m, **edge)
        return m

    prep(0, 0, 0)
    prep(0, 1, 1)
    qq = qq0_ref[...]
    m = None
    for c in range(n_chunks):
        s, m = _scores_chunk(keys(0, c), qq, m)
        s0_ref[chunk(c), :] = s
    m = tile_steps(0, m, first=True)
    m = lax.fori_loop(1, n_tiles - 1, tile_steps, m)
    tile_steps(n_tiles - 1, m, last=True)
    finish(n_tiles - 1, N_HEADS - 1, (n_elems - 1) % 2)


def _attention_den(q, k, v, sgc, cache_k, cache_v, lam_vec, subln_g, *, layer, seq, lam_init):
    n_tok = q.shape[0]
    past = cache_k.shape[2]
    blk = pl.BlockSpec((seq, ATTN_W), lambda b: (b, 0))
    cblk = pl.BlockSpec((1, 1, past, N_HEADS, QK_W), lambda b: (b, layer, 0, 0, 0))
    return pl.pallas_call(
        functools.partial(_attn_den_kernel, lam_init=lam_init),
        grid=(n_tok // seq,),
        in_specs=[
            pl.BlockSpec((4, HEAD_DIM), lambda b: (0, 0)),
            pl.BlockSpec((1, QK_W), lambda b: (0, 0)),
            blk, blk, blk, cblk, cblk, blk,
        ],
        out_specs=blk,
        out_shape=jax.ShapeDtypeStruct((n_tok, ATTN_W), BF16),
        scratch_shapes=[
            pltpu.VMEM((N_HEADS, past, QK_W), BF16),
            pltpu.VMEM((N_HEADS, QK_W + ONES_ROWS, past + seq), BF16),
            pltpu.VMEM((QK_W, 2 * Q_TILE), BF16),
            pltpu.VMEM((QK_W, 2 * Q_TILE), BF16),
            pltpu.VMEM((past + seq, 2 * Q_TILE), F32),
            pltpu.VMEM((past + seq, 2 * Q_TILE), F32),
            pltpu.VMEM((QK_W + ONES_ROWS, 2 * Q_TILE), F32),
            pltpu.VMEM((QK_W + ONES_ROWS, 2 * Q_TILE), F32),
        ],
        compiler_params=_cparams(("arbitrary",)),
        name="attention_den",
    )(lam_vec, subln_g.reshape(1, QK_W), q, k, v, cache_k, cache_v, sgc)


def _fourier_ctx_kernel(fa_ref, sga_ref, cs_ref, dn_ref, o_ref, *, seq):
    ab = jnp.dot(fa_ref[...], cs_ref[...], preferred_element_type=F32).astype(BF16)
    for s in range(fa_ref.shape[0] // seq):
        rows = slice(s * seq, (s + 1) * seq)
        ab2 = jnp.concatenate([ab[rows, :FOURIER_W], ab[rows, FOURIER_W:]], axis=0)
        f = jnp.dot(dn_ref[...], ab2, preferred_element_type=F32)
        o_ref[rows, :] = (f * sga_ref[rows, :].astype(F32)).astype(BF16)


def _fourier_ctx(fa, sga, cs, dn, *, seq):
    n_tok = fa.shape[0]
    blk = pl.BlockSpec((CTX_FOURIER_SEQS * seq, FOURIER_W), lambda b: (b, 0))
    return pl.pallas_call(
        functools.partial(_fourier_ctx_kernel, seq=seq),
        grid=(n_tok // (CTX_FOURIER_SEQS * seq),),
        in_specs=[blk, blk,
                  pl.BlockSpec((FOURIER_W, 2 * FOURIER_W), lambda b: (0, 0)),
                  pl.BlockSpec((seq, 2 * seq), lambda b: (0, 0))],
        out_specs=blk,
        out_shape=jax.ShapeDtypeStruct((n_tok, FOURIER_W), BF16),
        compiler_params=_cparams(("arbitrary",)),
        name="fourier_ctx",
    )(fa, sga, cs, dn)


def _cadd(a, b):
    return (a[0] + b[0], a[1] + b[1])


def _csub(a, b):
    return (a[0] - b[0], a[1] - b[1])


def _mul_neg_i(a):
    return (a[1], -a[0])


def _mul_w8_1(a):
    return ((a[0] + a[1]) * SQRT_HALF, (a[1] - a[0]) * SQRT_HALF)


def _mul_w8_3(a):
    return ((a[1] - a[0]) * SQRT_HALF, (-a[0] - a[1]) * SQRT_HALF)


def _fft4(a0, a1, a2, a3):
    e0, e1 = _cadd(a0, a2), _csub(a0, a2)
    o0, o1 = _cadd(a1, a3), _mul_neg_i(_csub(a1, a3))
    return [_cadd(e0, o0), _cadd(e1, o1), _csub(e0, o0), _csub(e1, o1)]


def _fft8(x):
    e = _fft4(x[0], x[2], x[4], x[6])
    o = _fft4(x[1], x[3], x[5], x[7])
    t = [o[0], _mul_w8_1(o[1]), _mul_neg_i(o[2]), _mul_w8_3(o[3])]
    return [_cadd(e[k], t[k]) for k in range(4)] + [_csub(e[k], t[k]) for k in range(4)]


def _fourier_den_kernel(xw_ref, sga_ref, f1_ref, twc_ref, tws_ref, cs2_ref, o_ref, g_ref, x_ref):
    n2 = xw_ref.shape[0]
    g_ref[...] = jnp.dot(f1_ref[...], xw_ref[...], preferred_element_type=F32)

    def chunk(i, carry):
        r = pl.multiple_of(i * FFT_ROWS, FFT_ROWS)
        re_rows = pl.ds(r, FFT_ROWS)
        im_rows = pl.ds(n2 + r, FFT_ROWS)
        xs = []
        for n1 in range(FFT_RADIX):
            lanes = slice(n1 * FOURIER_W, (n1 + 1) * FOURIER_W)
            gr = g_ref[re_rows, lanes]
            gi = g_ref[im_rows, lanes]
            if n1 > 0:
                c = twc_ref[re_rows, lanes]
                s = tws_ref[re_rows, lanes]
                gr, gi = gr * c + gi * s, gi * c - gr * s
            xs.append((gr, gi))
        for k1, (xr, xi) in enumerate(_fft8(xs)):
            out_rows = pl.ds(k1 * n2 + r, FFT_ROWS)
            x_ref[out_rows, :FOURIER_W] = xr.astype(BF16)
            x_ref[out_rows, FOURIER_W:] = xi.astype(BF16)
        return carry

    lax.fori_loop(0, n2 // FFT_ROWS, chunk, 0)
    f = jnp.dot(x_ref[...], cs2_ref[...], preferred_element_type=F32)
    o_ref[...] = (f * sga_ref[...].astype(F32)).astype(BF16)


def _fourier_den(xw, sga, f1, twc, tws, cs2, *, seq):
    n_tok = sga.shape[0]
    n2 = seq // FFT_RADIX
    wide = FFT_RADIX * FOURIER_W
    const = lambda shape: pl.BlockSpec(shape, lambda b: (0, 0))
    return pl.pallas_call(
        _fourier_den_kernel,
        grid=(n_tok // seq,),
        in_specs=[
            pl.BlockSpec((n2, wide), lambda b: (b, 0)),
            pl.BlockSpec((seq, FOURIER_W), lambda b: (b, 0)),
            const((2 * n2, n2)), const((n2, wide)), const((n2, wide)),
            const((2 * FOURIER_W, FOURIER_W)),
        ],
        out_specs=pl.BlockSpec((seq, FOURIER_W), lambda b: (b, 0)),
        out_shape=jax.ShapeDtypeStruct((n_tok, FOURIER_W), BF16),
        scratch_shapes=[pltpu.VMEM((2 * n2, wide), F32),
                        pltpu.VMEM((seq, 2 * FOURIER_W), BF16)],
        compiler_params=_cparams(("arbitrary",)),
        name="fourier_den",
    )(xw, sga, f1, twc, tws, cs2)


def _outproj_kernel(x_ref, mod_ref, ya_ref, zc_ref, zp_ref, zn_ref, bgs_ref, yc_ref,
                    cw_ref, cb_ref, w_ref, fg_ref, o_ref, *, seq, final_norm):
    tm = x_ref.shape[0]
    z = zc_ref[...].astype(F32)
    row = lax.broadcasted_iota(jnp.int32, z.shape, 0)
    pos = (pl.program_id(0) * tm + row) & (seq - 1)
    prev_row = zp_ref[HALO_ROWS - 1:HALO_ROWS, :].astype(F32)
    next_row = zn_ref[0:1, :].astype(F32)
    z_prev = jnp.where(row == 0, prev_row, pltpu.roll(z, 1, axis=0))
    z_prev = jnp.where(pos == 0, 0.0, z_prev)
    z_next = jnp.where(row == tm - 1, next_row, pltpu.roll(z, tm - 1, axis=0))
    z_next = jnp.where(pos == seq - 1, 0.0, z_next)
    cw = cw_ref[...]
    conv = z_prev * cw[0:1] + z * cw[1:2] + z_next * cw[2:3] + cb_ref[...]
    yb = (bgs_ref[...].astype(F32) * conv).astype(BF16)

    mixed = jnp.concatenate([ya_ref[...], yb, yc_ref[...]], axis=-1)
    out = jnp.dot(mixed, w_ref[0], preferred_element_type=F32)
    gate = mod_ref[0][:, 2 * D_MODEL:]
    xn = x_ref[...] + gate * out
    if final_norm:
        ms = jnp.mean(xn * xn, axis=-1, keepdims=True)
        xn = xn * lax.rsqrt(ms + EPS) * fg_ref[...]
    o_ref[...] = xn


def _out_projection(x, mod_rows, ya, zc, bgs, yc, conv_w, conv_b, w_out_bf16, layer, final_g,
                    *, seq, rows_per_mod, final_norm):
    n_tok = x.shape[0]
    tm = OUT_TILE
    n_tiles = n_tok // tm
    halo_per_tile = tm // HALO_ROWS
    n_halo = n_tok // HALO_ROWS
    row = lambda w: pl.BlockSpec((tm, w), lambda i: (i, 0))
    return pl.pallas_call(
        functools.partial(_outproj_kernel, seq=seq, final_norm=final_norm),
        grid=(n_tiles,),
        in_specs=[
            row(D_MODEL),
            pl.BlockSpec((1, 1, 3 * D_MODEL), lambda i: (i * tm // rows_per_mod, 0, 0)),
            row(FOURIER_W),
            row(CONV_W),
            pl.BlockSpec((HALO_ROWS, CONV_W), lambda i: (jnp.maximum(i * halo_per_tile - 1, 0), 0)),
            pl.BlockSpec((HALO_ROWS, CONV_W),
                         lambda i: (jnp.minimum((i + 1) * halo_per_tile, n_halo - 1), 0)),
            row(CONV_W),
            row(ATTN_W),
            pl.BlockSpec((3, CONV_W), lambda i: (0, 0)),
            pl.BlockSpec((1, CONV_W), lambda i: (0, 0)),
            pl.BlockSpec((1, D_MODEL, D_MODEL), lambda i: (layer, 0, 0)),
            pl.BlockSpec((1, D_MODEL), lambda i: (0, 0)),
        ],
        out_specs=row(D_MODEL),
        out_shape=jax.ShapeDtypeStruct((n_tok, D_MODEL), F32),
        compiler_params=_cparams(("arbitrary",)),
        name="out_projection",
    )(x, mod_rows, ya, zc, zc, zc, bgs, yc, conv_w, conv_b.reshape(1, CONV_W), w_out_bf16,
      final_g.reshape(1, D_MODEL))


def kernel(x_prompt, x_sample, cache_k, cache_v, c, c_ctx, norm_g, w_mod, b_mod, w_in, conv_w,
           conv_b, lam_vec, subln_g, w_out, final_g):
    batch, seq, _ = x_prompt.shape
    dec_batch, dec_seq, _ = x_sample.shape

    cvecs = jnp.concatenate(
        [c_ctx[None, :], c, jnp.zeros((MOD_ROWS - 1 - dec_batch, D_MODEL), F32)], axis=0)
    mod = _modulation(cvecs, w_mod, b_mod)

    w_in_b = w_in.astype(BF16)
    w_out_b = w_out.astype(BF16)
    rope_tabs = tuple(jnp.asarray(t) for t in _rope_tables(dec_seq))
    cs = jnp.asarray(_chan_tables()).astype(BF16)
    dn_ctx = jnp.asarray(_dft_tables(seq)).astype(BF16)
    f1, twc, tws, cs2 = (jnp.asarray(t) for t in _ct_tables(dec_seq))
    f1 = f1.astype(BF16)
    cs2 = cs2.astype(BF16)

    xc = x_prompt.reshape(batch * seq, D_MODEL)
    xl = x_sample.reshape(dec_batch * dec_seq, D_MODEL)
    new_k = new_v = None
    for l in range(DEPTH):
        lam_init = 0.8 - 0.6 * math.exp(-0.3 * l)
        last = l == DEPTH - 1

        mod_c = mod[l, 0:1][:, None, :]
        fa, sga, zc, bgs, q, k, v, sgc, new_k, new_v = _in_projection(
            xc, mod_c, norm_g[l], w_in_b, l, None, rows_per_mod=xc.shape[0],
            kv_out=(seq, new_k, new_v))
        yc = _attention_ctx(q, k, v, sgc, lam_vec[l], subln_g[l], seq=seq, lam_init=lam_init)
        ya = _fourier_ctx(fa, sga, cs, dn_ctx, seq=seq)
        xc = _out_projection(xc, mod_c, ya, zc, bgs, yc, conv_w[l], conv_b[l], w_out_b, l, final_g,
                             seq=seq, rows_per_mod=xc.shape[0], final_norm=last)

        mod_d = mod[l, 1:1 + dec_batch][:, None, :]
        fa, sga, zc, bgs, q, k, v, sgc = _in_projection(
            xl, mod_d, norm_g[l], w_in_b, l, rope_tabs, rows_per_mod=dec_seq,
            decimate_fa=True)
        yc = _attention_den(q, k, v, sgc, cache_k, cache_v, lam_vec[l], subln_g[l], layer=l,
                            seq=dec_seq, lam_init=lam_init)
        ya = _fourier_den(fa, sga, f1, twc, tws, cs2, seq=dec_seq)
        xl = _out_projection(xl, mod_d, ya, zc, bgs, yc, conv_w[l], conv_b[l], w_out_b, l, final_g,
                             seq=dec_seq, rows_per_mod=dec_seq, final_norm=last)

    y_prompt = xc.reshape(batch, seq, D_MODEL)
    y_sample = xl.reshape(dec_batch, dec_seq, D_MODEL)
    return (y_prompt, y_sample, new_k, new_v)
```

```python
import functools
import math

import numpy as np
import jax
import jax.numpy as jnp
from jax import lax
from jax.experimental import pallas as pl
from jax.experimental.pallas import tpu as pltpu

D_MODEL = 1024
DEPTH = 2
GRID_W = 64
FOURIER_W = 256
CONV_W = 256
ATTN_W = 512
N_HEADS = 4
HEAD_DIM = 64
QK_W = 128
ROPE_BASE = 10000.0
EPS = 1e-6
IN_DIM = 3584

F32 = jnp.float32
BF16 = jnp.bfloat16

VMEM_LIMIT_BYTES = 56 * 1024 * 1024
TOKEN_TILE = 512
OUT_TILE = 1024
Q_TILE = 256
KEY_CHUNK = 256
ONES_ROWS = 16
LANES = 128
FFT_RADIX = 8
FFT_ROWS = 16
SQRT_HALF = 0.7071067811865476
CTX_ATTN_SEQS = 2
CTX_FOURIER_SEQS = 4
HALO_ROWS = 16
MOD_ROWS = 8
LOG2E = 1.4426950408889634

_COLS = {}
_off = 0
for _name, _w in (("fa", 256), ("ga", 256), ("bg", 256), ("cg", 256), ("hc", 256), ("gb", 256),
                  ("q", 512), ("k", 512), ("v", 512), ("gc", 512)):
    _COLS[_name] = (_off, _off + _w)
    _off += _w


def _silu(x):
    return x * (1.0 / (1.0 + jnp.exp(-x)))


def _cparams(sem):
    return pltpu.CompilerParams(dimension_semantics=sem, vmem_limit_bytes=VMEM_LIMIT_BYTES)


def _rope_tables(n_tokens):
    n = np.arange(n_tokens)
    row = (n // GRID_W).astype(np.float64)
    col = (n % GRID_W).astype(np.float64)
    j = np.arange(QK_W)
    jj = j % HEAD_DIM
    idx = jj % 32
    inv = 1.0 / (ROPE_BASE ** (2.0 * (idx % 16) / 32.0))
    pos = np.where((jj < 32)[None, :], row[:, None], col[:, None])
    ang = pos * inv[None, :]
    cos = np.cos(ang)
    sin = np.sin(ang)
    first = (idx < 16)[None, :]
    sin_a = np.where(first, -sin, 0.0)
    sin_b = np.where(first, 0.0, sin)
    return (np.asarray(cos, np.float32), np.asarray(sin_a, np.float32), np.asarray(sin_b, np.float32))


def _dft_tables(n):
    k = np.arange(n)
    kn = (k[:, None] * k[None, :]) % n
    ang = 2.0 * np.pi * kn / n
    return np.asarray(np.concatenate([np.cos(ang), -np.sin(ang)], axis=1) / math.sqrt(n), np.float32)


def _chan_tables():
    k = np.arange(FOURIER_W)
    kn = (k[:, None] * k[None, :]) % FOURIER_W
    ang = 2.0 * np.pi * kn / FOURIER_W
    return np.asarray(np.concatenate([np.cos(ang), np.sin(ang)], axis=1) / math.sqrt(FOURIER_W), np.float32)


def _ct_tables(n):
    n2 = n // FFT_RADIX
    k = np.arange(n2)
    ang = 2.0 * np.pi * ((k[:, None] * k[None, :]) % n2) / n2
    f1 = np.concatenate([np.cos(ang), -np.sin(ang)], axis=0) / math.sqrt(n * FOURIER_W)
    tw = 2.0 * np.pi * k[:, None] * np.arange(FFT_RADIX)[None, :] / n
    twc = np.repeat(np.cos(tw), FOURIER_W, axis=1)
    tws = np.repeat(np.sin(tw), FOURIER_W, axis=1)
    c = np.arange(FOURIER_W)
    angc = 2.0 * np.pi * ((c[:, None] * c[None, :]) % FOURIER_W) / FOURIER_W
    cs2 = np.concatenate([np.cos(angc), np.sin(angc)], axis=0)
    return tuple(np.asarray(t, np.float32) for t in (f1, twc, tws, cs2))


def _mod_kernel(c_ref, w_ref, b_ref, o_ref):
    s = _silu(c_ref[...]).astype(BF16)
    w = w_ref[0].astype(BF16)
    o_ref[0] = jnp.dot(s, w, preferred_element_type=F32) + b_ref[0]


def _modulation(cvecs, w_mod, b_mod):
    chunk = D_MODEL
    n_chunks = 3 * D_MODEL // chunk
    return pl.pallas_call(
        _mod_kernel,
        grid=(DEPTH, n_chunks),
        in_specs=[
            pl.BlockSpec((MOD_ROWS, D_MODEL), lambda l, j: (0, 0)),
            pl.BlockSpec((1, D_MODEL, chunk), lambda l, j: (l, 0, j)),
            pl.BlockSpec((1, 1, chunk), lambda l, j: (l, 0, j)),
        ],
        out_specs=pl.BlockSpec((1, MOD_ROWS, chunk), lambda l, j: (l, 0, j)),
        out_shape=jax.ShapeDtypeStruct((DEPTH, MOD_ROWS, 3 * D_MODEL), F32),
        compiler_params=_cparams(("arbitrary", "arbitrary")),
        name="modulation",
    )(cvecs, w_mod, b_mod.reshape(DEPTH, 1, 3 * D_MODEL))


def _store_heads(ref, t):
    seqs, _, seq, _, _ = ref.shape
    for s in range(seqs):
        for hd in range(N_HEADS):
            ref[s, 0, :, hd, :] = t[s * seq:(s + 1) * seq, hd * QK_W:(hd + 1) * QK_W]


def _inproj_tile(x_ref, mod_ref, g_ref, w_ref, rope_refs, outs, kv_refs, stage_ref):
    fa_ref, sga_ref, zc_ref, bgs_ref, q_ref, k_ref, v_ref, sgc_ref = outs

    x = x_ref[...]
    ms = jnp.mean(x * x, axis=-1, keepdims=True)
    y = x * lax.rsqrt(ms + EPS) * g_ref[...]
    m = mod_ref[0]
    h = (y * (1.0 + m[:, D_MODEL:2 * D_MODEL]) + m[:, :D_MODEL]).astype(BF16)

    def proj(name):
        lo, hi = _COLS[name]
        return jnp.dot(h, w_ref[:, lo:hi], preferred_element_type=F32)

    if stage_ref is not None:
        fa = proj("fa")
        halves = stage_ref.shape[0]
        for hf in range(halves):
            stage_ref[hf] = fa[:, hf * LANES:(hf + 1) * LANES]
        rows = stage_ref.shape[1] // FFT_RADIX
        for n1 in range(FFT_RADIX):
            for hf in range(halves):
                lo = n1 * FOURIER_W + hf * LANES
                fa_ref[:, lo:lo + LANES] = (
                    stage_ref[hf, pl.ds(n1, rows, stride=FFT_RADIX), :].astype(BF16))
    else:
        fa_ref[...] = proj("fa").astype(BF16)
    sga_ref[...] = _silu(proj("ga")).astype(BF16)
    zc_ref[...] = (proj("cg") * proj("hc")).astype(BF16)
    bgs_ref[...] = (proj("bg") * _silu(proj("gb"))).astype(BF16)
    sgc_ref[...] = _silu(proj("gc")).astype(BF16)

    q = proj("q")
    k = proj("k")
    v = proj("v")
    if kv_refs is not None:
        _store_heads(kv_refs[0], k)
        _store_heads(kv_refs[1], v)
    v_ref[...] = v.astype(BF16)

    q_scale = HEAD_DIM ** -0.5 * LOG2E
    if rope_refs is not None:
        cos = rope_refs[0][...]
        sa = rope_refs[1][...]
        sb = rope_refs[2][...]
        for hd in range(N_HEADS):
            sl = slice(hd * QK_W, (hd + 1) * QK_W)
            for t, ref, scale in ((q, q_ref, q_scale), (k, k_ref, None)):
                th = t[:, sl]
                r = (th * cos + pltpu.roll(th, QK_W - 16, axis=1) * sa
                     + pltpu.roll(th, 16, axis=1) * sb)
                if scale is not None:
                    r = r * scale
                ref[:, sl] = r.astype(BF16)
    else:
        q_ref[...] = (q * q_scale).astype(BF16)
        k_ref[...] = k.astype(BF16)


N_PROJ_OUTS = 8


def _inproj_kernel(xc_ref, xd_ref, modc_ref, modd_ref, g_ref, w_ref, cos_ref, sa_ref, sb_ref,
                   *rest, n_ctx_tiles, n_alias):
    rest = rest[n_alias:]
    ctx_outs = rest[:N_PROJ_OUTS]
    kv_refs = rest[N_PROJ_OUTS:N_PROJ_OUTS + 2]
    den_outs = rest[N_PROJ_OUTS + 2:2 * N_PROJ_OUTS + 2]
    wb_ref, stage_ref = rest[2 * N_PROJ_OUTS + 2:]
    i = pl.program_id(0)

    @pl.when(i == 0)
    def _():
        for lo, hi in _COLS.values():
            wb_ref[:, lo:hi] = w_ref[0, :, lo:hi].astype(BF16)

    @pl.when(i < n_ctx_tiles)
    def _():
        _inproj_tile(xc_ref, modc_ref, g_ref, wb_ref, None, ctx_outs, kv_refs, None)

    @pl.when(i >= n_ctx_tiles)
    def _():
        _inproj_tile(xd_ref, modd_ref, g_ref, wb_ref, (cos_ref, sa_ref, sb_ref), den_outs, None,
                     stage_ref)


def _in_projection(xc, xd, mod_c, mod_d, norm_g, w_in, layer, rope_tabs, *, seq, dec_seq,
                   new_k, new_v):
    tm = TOKEN_TILE
    n_c, n_d = xc.shape[0], xd.shape[0]
    nct, ndt = n_c // tm, n_d // tm
    seq_tiles = dec_seq // tm
    ctx_i = lambda i: jnp.minimum(i, nct - 1)
    den_i = lambda i: jnp.maximum(i - nct, 0)
    row_c = lambda w: pl.BlockSpec((tm, w), lambda i: (ctx_i(i), 0))
    row_d = lambda w: pl.BlockSpec((tm, w), lambda i: (den_i(i), 0))
    tab = pl.BlockSpec((tm, QK_W), lambda i: (den_i(i) % seq_tiles, 0))
    in_specs = [
        row_c(D_MODEL), row_d(D_MODEL),
        pl.BlockSpec((1, 1, 3 * D_MODEL), lambda i: (0, 0, 0)),
        pl.BlockSpec((1, 1, 3 * D_MODEL), lambda i: (den_i(i) // seq_tiles, 0, 0)),
        pl.BlockSpec((1, D_MODEL), lambda i: (0, 0)),
        pl.BlockSpec((1, D_MODEL, IN_DIM), lambda i: (layer, 0, 0)),
        tab, tab, tab,
    ]
    args = [xc, xd, mod_c, mod_d, norm_g.reshape(1, D_MODEL), w_in, *rope_tabs]
    widths = (256, 256, 256, 256, 512, 512, 512, 512)
    seqs = tm // seq
    kv_blk = pl.BlockSpec((seqs, 1, seq, N_HEADS, QK_W), lambda i: (ctx_i(i), layer, 0, 0, 0))
    kv_shape = jax.ShapeDtypeStruct((n_c // seq, DEPTH, seq, N_HEADS, QK_W), F32)
    out_specs = [row_c(w) for w in widths] + [kv_blk, kv_blk] + [row_d(w) for w in widths]
    out_shape = ([jax.ShapeDtypeStruct((n_c, w), BF16) for w in widths] + [kv_shape, kv_shape]
                 + [jax.ShapeDtypeStruct((n_d, w), BF16) for w in widths])
    fa_d = N_PROJ_OUTS + 2
    out_specs[fa_d] = pl.BlockSpec((tm // FFT_RADIX, FFT_RADIX * FOURIER_W), lambda i: (den_i(i), 0))
    out_shape[fa_d] = jax.ShapeDtypeStruct((n_d // FFT_RADIX, FFT_RADIX * FOURIER_W), BF16)
    aliases = {}
    n_alias = 0
    if new_k is not None:
        aliases = {len(args): N_PROJ_OUTS, len(args) + 1: N_PROJ_OUTS + 1}
        in_specs += [pl.BlockSpec(memory_space=pl.ANY)] * 2
        args += [new_k, new_v]
        n_alias = 2
    outs = pl.pallas_call(
        functools.partial(_inproj_kernel, n_ctx_tiles=nct, n_alias=n_alias),
        grid=(nct + ndt,),
        in_specs=in_specs,
        out_specs=out_specs,
        out_shape=out_shape,
        scratch_shapes=[pltpu.VMEM((D_MODEL, IN_DIM), BF16),
                        pltpu.VMEM((FOURIER_W // LANES, tm, LANES), F32)],
        input_output_aliases=aliases,
        compiler_params=_cparams(("arbitrary",)),
        name="in_projection",
    )(*args)
    return outs[:N_PROJ_OUTS], outs[N_PROJ_OUTS], outs[N_PROJ_OUTS + 1], outs[fa_d:]


def _lambda(lam_ref, lam_init):
    lv = lam_ref[...]
    a = jnp.sum(lv[0:1] * lv[1:2], axis=-1, keepdims=True)
    b = jnp.sum(lv[2:3] * lv[3:4], axis=-1, keepdims=True)
    return jnp.exp(a) - jnp.exp(b) + lam_init


def _stack_masked(q_t):
    first = lax.broadcasted_iota(jnp.int32, q_t.shape, 0) < HEAD_DIM
    return jnp.concatenate([jnp.where(first, q_t, 0.0), jnp.where(first, 0.0, q_t)],
                           axis=1).astype(BF16)


def _scores_chunk(k, qq_t, m):
    s = jnp.dot(k, qq_t, preferred_element_type=F32)
    mc = jnp.max(s, axis=0, keepdims=True)
    return s, (mc if m is None else jnp.maximum(m, mc))


def _with_ones_rows(v_t):
    return jnp.concatenate([v_t, jnp.ones((ONES_ROWS, v_t.shape[1]), F32)], axis=0).astype(BF16)


def _values_chunk(s, m, v1_t, acc):
    e = jnp.exp2(s - m).astype(BF16)
    oc = jnp.dot(v1_t, e, preferred_element_type=F32)
    return oc if acc is None else acc + oc


def _combine_t(acc, lam):
    tq = acc.shape[1] // 2
    r = 1.0 / acc[QK_W:QK_W + 1, :]
    o_t = acc[:QK_W, :]
    return o_t[:, :tq] * r[:, :tq] - o_t[:, tq:] * (r[:, tq:] * lam)


def _subln_gate(o, sg, sgc, lam_init):
    ms = jnp.mean(o * o, axis=-1, keepdims=True)
    y = o * lax.rsqrt(ms + EPS) * sg * (1.0 - lam_init)
    return (y * sgc.astype(F32)).astype(BF16)


def _attn_ctx_kernel(lam_ref, sg_ref, q_ref, k_ref, v_ref, sgc_ref, o_ref, *, lam_init, seq):
    lam = _lambda(lam_ref, lam_init)
    sg = sg_ref[...]
    n_seq = q_ref.shape[0] // seq
    q_t = q_ref[...].astype(F32).T
    v_t = v_ref[...].astype(F32).T
    pairs = [(slice(s * seq, (s + 1) * seq), slice(hd * QK_W, (hd + 1) * QK_W))
             for s in range(n_seq) for hd in range(N_HEADS)]
    qqs = [_stack_masked(q_t[hl, rows]) for rows, hl in pairs]
    v1s = [_with_ones_rows(v_t[hl, rows]) for rows, hl in pairs]
    sm = [_scores_chunk(k_ref[rows, hl], qq, None) for (rows, hl), qq in zip(pairs, qqs)]
    accs = [_values_chunk(s, m, v1, None) for (s, m), v1 in zip(sm, v1s)]
    y_t = [_combine_t(a, lam) for a in accs]
    for s in range(n_seq):
        rows = slice(s * seq, (s + 1) * seq)
        y = jnp.concatenate(y_t[s * N_HEADS:(s + 1) * N_HEADS], axis=0).T
        for hd in range(N_HEADS):
            hl = slice(hd * QK_W, (hd + 1) * QK_W)
            o_ref[rows, hl] = _subln_gate(y[:, hl], sg, sgc_ref[rows, hl], lam_init)


def _attention_ctx(q, k, v, sgc, lam_vec, subln_g, *, seq, lam_init):
    n_tok = q.shape[0]
    blk = pl.BlockSpec((CTX_ATTN_SEQS * seq, ATTN_W), lambda b: (b, 0))
    return pl.pallas_call(
        functools.partial(_attn_ctx_kernel, lam_init=lam_init, seq=seq),
        grid=(n_tok // (CTX_ATTN_SEQS * seq),),
        in_specs=[
            pl.BlockSpec((4, HEAD_DIM), lambda b: (0, 0)),
            pl.BlockSpec((1, QK_W), lambda b: (0, 0)),
            blk, blk, blk, blk,
        ],
        out_specs=blk,
        out_shape=jax.ShapeDtypeStruct((n_tok, ATTN_W), BF16),
        compiler_params=_cparams(("arbitrary",)),
        name="attention_ctx",
    )(lam_vec, subln_g.reshape(1, QK_W), q, k, v, sgc)


def _attn_den_kernel(lam_ref, sg_ref, q_ref, kn_ref, vn_ref, kc_ref, vc_ref, sgc_ref, o_ref,
                     kcs_ref, vt_ref, qq0_ref, qq1_ref, s0_ref, s1_ref, acc0_ref, acc1_ref,
                     *, lam_init):
    seq = q_ref.shape[0]
    past = kcs_ref.shape[1]
    past_chunks = past // KEY_CHUNK
    n_chunks = (past + seq) // KEY_CHUNK
    n_tiles = seq // Q_TILE
    n_elems = n_tiles * N_HEADS
    qq_refs = (qq0_ref, qq1_ref)
    s_refs = (s0_ref, s1_ref)
    acc_refs = (acc0_ref, acc1_ref)
    lam = _lambda(lam_ref, lam_init)
    sg = sg_ref[...]

    def lanes(hd):
        return slice(hd * QK_W, (hd + 1) * QK_W)

    def chunk(c):
        return slice(c * KEY_CHUNK, (c + 1) * KEY_CHUNK)

    def tile_rows(t):
        if isinstance(t, int):
            return pl.ds(t * Q_TILE, Q_TILE)
        return pl.ds(pl.multiple_of(t * Q_TILE, Q_TILE), Q_TILE)

    for hd in range(N_HEADS):
        kcs_ref[hd] = kc_ref[0, 0, :, hd, :].astype(BF16)
        vt_ref[hd, :, :past] = _with_ones_rows(vc_ref[0, 0, :, hd, :].T)
        vt_ref[hd, :, past:] = _with_ones_rows(vn_ref[:, lanes(hd)].astype(F32).T)

    def keys(hd, c):
        if c < past_chunks:
            return kcs_ref[hd, chunk(c), :]
        return kn_ref[chunk(c - past_chunks), lanes(hd)]

    def prep(t, hd, par):
        qq_refs[par][...] = _stack_masked(q_ref[tile_rows(t), lanes(hd)].astype(F32).T)

    def finish(t, hd, par):
        rows = tile_rows(t)
        y = _combine_t(acc_refs[par][...], lam).T
        o_ref[rows, lanes(hd)] = _subln_gate(y, sg, sgc_ref[rows, lanes(hd)], lam_init)

    def step(t, j, m_cur, first=False, last=False):
        def elem(off):
            return t + (j + off) // N_HEADS, (j + off) % N_HEADS

        par = j % 2
        e_static = N_HEADS * t + j if isinstance(t, int) else None
        do_finish = not (first and j == 0)
        do_scores = not (last and e_static + 1 >= n_elems)
        do_prep = not (last and e_static + 2 >= n_elems)
        if do_prep:
            prep(*elem(2), par)
        hd_nxt = elem(1)[1]
        qq = qq_refs[1 - par][...] if do_scores else None
        m_nxt = None
        acc = None
        for c in range(n_chunks):
            if do_scores:
                s, m_nxt = _scores_chunk(keys(hd_nxt, c), qq, m_nxt)
                s_refs[1 - par][chunk(c), :] = s
            acc = _values_chunk(s_refs[par][chunk(c), :], m_cur, vt_ref[j, :, chunk(c)], acc)
        acc_refs[par][...] = acc
        if do_finish:
            finish(*elem(-1), 1 - par)
        return m_nxt

    def tile_steps(t, m, **edge):
        for j in range(N_HEADS):
            m = step(t, j, m, **edge)
        return m

    prep(0, 0, 0)
    prep(0, 1, 1)
    qq = qq0_ref[...]
    m = None
    for c in range(n_chunks):
        s, m = _scores_chunk(keys(0, c), qq, m)
        s0_ref[chunk(c), :] = s
    m = tile_steps(0, m, first=True)
    m = lax.fori_loop(1, n_tiles - 1, tile_steps, m)
    tile_steps(n_tiles - 1, m, last=True)
    finish(n_tiles - 1, N_HEADS - 1, (n_elems - 1) % 2)


def _attention_den(q, k, v, sgc, cache_k, cache_v, lam_vec, subln_g, *, layer, seq, lam_init):
    n_tok = q.shape[0]
    past = cache_k.shape[2]
    blk = pl.BlockSpec((seq, ATTN_W), lambda b: (b, 0))
    cblk = pl.BlockSpec((1, 1, past, N_HEADS, QK_W), lambda b: (b, layer, 0, 0, 0))
    return pl.pallas_call(
        functools.partial(_attn_den_kernel, lam_init=lam_init),
        grid=(n_tok // seq,),
        in_specs=[
            pl.BlockSpec((4, HEAD_DIM), lambda b: (0, 0)),
            pl.BlockSpec((1, QK_W), lambda b: (0, 0)),
            blk, blk, blk, cblk, cblk, blk,
        ],
        out_specs=blk,
        out_shape=jax.ShapeDtypeStruct((n_tok, ATTN_W), BF16),
        scratch_shapes=[
            pltpu.VMEM((N_HEADS, past, QK_W), BF16),
            pltpu.VMEM((N_HEADS, QK_W + ONES_ROWS, past + seq), BF16),
            pltpu.VMEM((QK_W, 2 * Q_TILE), BF16),
            pltpu.VMEM((QK_W, 2 * Q_TILE), BF16),
            pltpu.VMEM((past + seq, 2 * Q_TILE), F32),
            pltpu.VMEM((past + seq, 2 * Q_TILE), F32),
            pltpu.VMEM((QK_W + ONES_ROWS, 2 * Q_TILE), F32),
            pltpu.VMEM((QK_W + ONES_ROWS, 2 * Q_TILE), F32),
        ],
        compiler_params=_cparams(("arbitrary",)),
        name="attention_den",
    )(lam_vec, subln_g.reshape(1, QK_W), q, k, v, cache_k, cache_v, sgc)


def _fourier_ctx_kernel(fa_ref, sga_ref, cs_ref, dn_ref, o_ref, *, seq):
    ab = jnp.dot(fa_ref[...], cs_ref[...], preferred_element_type=F32).astype(BF16)
    for s in range(fa_ref.shape[0] // seq):
        rows = slice(s * seq, (s + 1) * seq)
        ab2 = jnp.concatenate([ab[rows, :FOURIER_W], ab[rows, FOURIER_W:]], axis=0)
        f = jnp.dot(dn_ref[...], ab2, preferred_element_type=F32)
        o_ref[rows, :] = (f * sga_ref[rows, :].astype(F32)).astype(BF16)


def _fourier_ctx(fa, sga, cs, dn, *, seq):
    n_tok = fa.shape[0]
    blk = pl.BlockSpec((CTX_FOURIER_SEQS * seq, FOURIER_W), lambda b: (b, 0))
    return pl.pallas_call(
        functools.partial(_fourier_ctx_kernel, seq=seq),
        grid=(n_tok // (CTX_FOURIER_SEQS * seq),),
        in_specs=[blk, blk,
                  pl.BlockSpec((FOURIER_W, 2 * FOURIER_W), lambda b: (0, 0)),
                  pl.BlockSpec((seq, 2 * seq), lambda b: (0, 0))],
        out_specs=blk,
        out_shape=jax.ShapeDtypeStruct((n_tok, FOURIER_W), BF16),
        compiler_params=_cparams(("arbitrary",)),
        name="fourier_ctx",
    )(fa, sga, cs, dn)


def _cadd(a, b):
    return (a[0] + b[0], a[1] + b[1])


def _csub(a, b):
    return (a[0] - b[0], a[1] - b[1])


def _mul_neg_i(a):
    return (a[1], -a[0])


def _mul_w8_1(a):
    return ((a[0] + a[1]) * SQRT_HALF, (a[1] - a[0]) * SQRT_HALF)


def _mul_w8_3(a):
    return ((a[1] - a[0]) * SQRT_HALF, (-a[0] - a[1]) * SQRT_HALF)


def _fft4(a0, a1, a2, a3):
    e0, e1 = _cadd(a0, a2), _csub(a0, a2)
    o0, o1 = _cadd(a1, a3), _mul_neg_i(_csub(a1, a3))
    return [_cadd(e0, o0), _cadd(e1, o1), _csub(e0, o0), _csub(e1, o1)]


def _fft8(x):
    e = _fft4(x[0], x[2], x[4], x[6])
    o = _fft4(x[1], x[3], x[5], x[7])
    t = [o[0], _mul_w8_1(o[1]), _mul_neg_i(o[2]), _mul_w8_3(o[3])]
    return [_cadd(e[k], t[k]) for k in range(4)] + [_csub(e[k], t[k]) for k in range(4)]


def _fourier_den_kernel(xw_ref, sga_ref, f1_ref, twc_ref, tws_ref, cs2_ref, o_ref, g_ref, x_ref):
    n2 = xw_ref.shape[0]
    g_ref[...] = jnp.dot(f1_ref[...], xw_ref[...], preferred_element_type=F32)

    def chunk(i, carry):
        r = pl.multiple_of(i * FFT_ROWS, FFT_ROWS)
        re_rows = pl.ds(r, FFT_ROWS)
        im_rows = pl.ds(n2 + r, FFT_ROWS)
        xs = []
        for n1 in range(FFT_RADIX):
            lanes = slice(n1 * FOURIER_W, (n1 + 1) * FOURIER_W)
            gr = g_ref[re_rows, lanes]
            gi = g_ref[im_rows, lanes]
            if n1 > 0:
                c = twc_ref[re_rows, lanes]
                s = tws_ref[re_rows, lanes]
                gr, gi = gr * c + gi * s, gi * c - gr * s
            xs.append((gr, gi))
        for k1, (xr, xi) in enumerate(_fft8(xs)):
            out_rows = pl.ds(k1 * n2 + r, FFT_ROWS)
            x_ref[out_rows, :FOURIER_W] = xr.astype(BF16)
            x_ref[out_rows, FOURIER_W:] = xi.astype(BF16)
        return carry

    lax.fori_loop(0, n2 // FFT_ROWS, chunk, 0)
    f = jnp.dot(x_ref[...], cs2_ref[...], preferred_element_type=F32)
    o_ref[...] = (f * sga_ref[...].astype(F32)).astype(BF16)


def _fourier_den(xw, sga, f1, twc, tws, cs2, *, seq):
    n_tok = sga.shape[0]
    n2 = seq // FFT_RADIX
    wide = FFT_RADIX * FOURIER_W
    const = lambda shape: pl.BlockSpec(shape, lambda b: (0, 0))
    return pl.pallas_call(
        _fourier_den_kernel,
        grid=(n_tok // seq,),
        in_specs=[
            pl.BlockSpec((n2, wide), lambda b: (b, 0)),
            pl.BlockSpec((seq, FOURIER_W), lambda b: (b, 0)),
            const((2 * n2, n2)), const((n2, wide)), const((n2, wide)),
            const((2 * FOURIER_W, FOURIER_W)),
        ],
        out_specs=pl.BlockSpec((seq, FOURIER_W), lambda b: (b, 0)),
        out_shape=jax.ShapeDtypeStruct((n_tok, FOURIER_W), BF16),
        scratch_shapes=[pltpu.VMEM((2 * n2, wide), F32),
                        pltpu.VMEM((seq, 2 * FOURIER_W), BF16)],
        compiler_params=_cparams(("arbitrary",)),
        name="fourier_den",
    )(xw, sga, f1, twc, tws, cs2)


def _outproj_tile(x_ref, mod_ref, ya_ref, zc_ref, zp_ref, zn_ref, bgs_ref, yc_ref,
                  cw_ref, cb_ref, w_ref, fg_ref, o_ref, tile, *, seq, final_norm):
    tm = x_ref.shape[0]
    z = zc_ref[...].astype(F32)
    row = lax.broadcasted_iota(jnp.int32, z.shape, 0)
    pos = (tile * tm + row) & (seq - 1)
    prev_row = zp_ref[HALO_ROWS - 1:HALO_ROWS, :].astype(F32)
    next_row = zn_ref[0:1, :].astype(F32)
    z_prev = jnp.where(row == 0, prev_row, pltpu.roll(z, 1, axis=0))
    z_prev = jnp.where(pos == 0, 0.0, z_prev)
    z_next = jnp.where(row == tm - 1, next_row, pltpu.roll(z, tm - 1, axis=0))
    z_next = jnp.where(pos == seq - 1, 0.0, z_next)
    cw = cw_ref[...]
    conv = z_prev * cw[0:1] + z * cw[1:2] + z_next * cw[2:3] + cb_ref[...]
    yb = (bgs_ref[...].astype(F32) * conv).astype(BF16)

    mixed = jnp.concatenate([ya_ref[...], yb, yc_ref[...]], axis=-1)
    out = jnp.dot(mixed, w_ref[...], preferred_element_type=F32)
    gate = mod_ref[0][:, 2 * D_MODEL:]
    xn = x_ref[...] + gate * out
    if final_norm:
        ms = jnp.mean(xn * xn, axis=-1, keepdims=True)
        xn = xn * lax.rsqrt(ms + EPS) * fg_ref[...]
    o_ref[...] = xn


N_GROUP_INS = 8


def _outproj_kernel(*refs, n_ctx_tiles, seq, dec_seq, final_norm):
    ctx_ins = refs[:N_GROUP_INS]
    den_ins = refs[N_GROUP_INS:2 * N_GROUP_INS]
    cw_ref, cb_ref, w_ref, fg_ref, oc_ref, od_ref, wb_ref = refs[2 * N_GROUP_INS:]
    i = pl.program_id(0)

    @pl.when(i == 0)
    def _():
        wb_ref[...] = w_ref[0].astype(BF16)

    @pl.when(i < n_ctx_tiles)
    def _():
        _outproj_tile(*ctx_ins, cw_ref, cb_ref, wb_ref, fg_ref, oc_ref, i, seq=seq,
                      final_norm=final_norm)

    @pl.when(i >= n_ctx_tiles)
    def _():
        _outproj_tile(*den_ins, cw_ref, cb_ref, wb_ref, fg_ref, od_ref, i - n_ctx_tiles,
                      seq=dec_seq, final_norm=final_norm)


def _out_projection(ctx, den, conv_w, conv_b, w_out, layer, final_g, *, seq, dec_seq, final_norm):
    tm = OUT_TILE
    nct = ctx[0].shape[0] // tm
    ndt = den[0].shape[0] // tm
    halo_per_tile = tm // HALO_ROWS
    seq_tiles = dec_seq // tm

    def group_specs(tile_of, n_tok, mod_of):
        n_halo = n_tok // HALO_ROWS
        row = lambda w: pl.BlockSpec((tm, w), lambda i: (tile_of(i), 0))
        return [
            row(D_MODEL),
            pl.BlockSpec((1, 1, 3 * D_MODEL), lambda i: (mod_of(tile_of(i)), 0, 0)),
            row(FOURIER_W),
            row(CONV_W),
            pl.BlockSpec((HALO_ROWS, CONV_W),
                         lambda i: (jnp.maximum(tile_of(i) * halo_per_tile - 1, 0), 0)),
            pl.BlockSpec((HALO_ROWS, CONV_W),
                         lambda i: (jnp.minimum((tile_of(i) + 1) * halo_per_tile, n_halo - 1), 0)),
            row(CONV_W),
            row(ATTN_W),
        ], row(D_MODEL)

    ctx_specs, ctx_out = group_specs(lambda i: jnp.minimum(i, nct - 1), ctx[0].shape[0],
                                     lambda t: 0)
    den_specs, den_out = group_specs(lambda i: jnp.maximum(i - nct, 0), den[0].shape[0],
                                     lambda t: t // seq_tiles)

    def group_args(g):
        x, mod_rows, ya, zc, bgs, yc = g
        return [x, mod_rows, ya, zc, zc, zc, bgs, yc]

    return pl.pallas_call(
        functools.partial(_outproj_kernel, n_ctx_tiles=nct, seq=seq, dec_seq=dec_seq,
                          final_norm=final_norm),
        grid=(nct + ndt,),
        in_specs=ctx_specs + den_specs + [
            pl.BlockSpec((3, CONV_W), lambda i: (0, 0)),
            pl.BlockSpec((1, CONV_W), lambda i: (0, 0)),
            pl.BlockSpec((1, D_MODEL, D_MODEL), lambda i: (layer, 0, 0)),
            pl.BlockSpec((1, D_MODEL), lambda i: (0, 0)),
        ],
        out_specs=[ctx_out, den_out],
        out_shape=[jax.ShapeDtypeStruct(ctx[0].shape, F32), jax.ShapeDtypeStruct(den[0].shape, F32)],
        scratch_shapes=[pltpu.VMEM((D_MODEL, D_MODEL), BF16)],
        compiler_params=_cparams(("arbitrary",)),
        name="out_projection",
    )(*group_args(ctx), *group_args(den), conv_w, conv_b.reshape(1, CONV_W), w_out,
      final_g.reshape(1, D_MODEL))


def kernel(x_prompt, x_sample, cache_k, cache_v, c, c_ctx, norm_g, w_mod, b_mod, w_in, conv_w,
           conv_b, lam_vec, subln_g, w_out, final_g):
    batch, seq, _ = x_prompt.shape
    dec_batch, dec_seq, _ = x_sample.shape

    cvecs = jnp.concatenate(
        [c_ctx[None, :], c, jnp.zeros((MOD_ROWS - 1 - dec_batch, D_MODEL), F32)], axis=0)
    mod = _modulation(cvecs, w_mod, b_mod)

    rope_tabs = tuple(jnp.asarray(t) for t in _rope_tables(dec_seq))
    cs = jnp.asarray(_chan_tables()).astype(BF16)
    dn_ctx = jnp.asarray(_dft_tables(seq)).astype(BF16)
    f1, twc, tws, cs2 = (jnp.asarray(t) for t in _ct_tables(dec_seq))
    f1 = f1.astype(BF16)
    cs2 = cs2.astype(BF16)

    xc = x_prompt.reshape(batch * seq, D_MODEL)
    xl = x_sample.reshape(dec_batch * dec_seq, D_MODEL)
    new_k = new_v = None
    for l in range(DEPTH):
        lam_init = 0.8 - 0.6 * math.exp(-0.3 * l)
        mod_c = mod[l, 0:1][:, None, :]
        mod_d = mod[l, 1:1 + dec_batch][:, None, :]
        ctx, new_k, new_v, den = _in_projection(
            xc, xl, mod_c, mod_d, norm_g[l], w_in, l, rope_tabs, seq=seq, dec_seq=dec_seq,
            new_k=new_k, new_v=new_v)

        fa, sga, zc_c, bgs_c, q, k, v, sgc = ctx
        yc_c = _attention_ctx(q, k, v, sgc, lam_vec[l], subln_g[l], seq=seq, lam_init=lam_init)
        ya_c = _fourier_ctx(fa, sga, cs, dn_ctx, seq=seq)

        fa, sga, zc_d, bgs_d, q, k, v, sgc = den
        yc_d = _attention_den(q, k, v, sgc, cache_k, cache_v, lam_vec[l], subln_g[l], layer=l,
                              seq=dec_seq, lam_init=lam_init)
        ya_d = _fourier_den(fa, sga, f1, twc, tws, cs2, seq=dec_seq)

        xc, xl = _out_projection(
            (xc, mod_c, ya_c, zc_c, bgs_c, yc_c), (xl, mod_d, ya_d, zc_d, bgs_d, yc_d),
            conv_w[l], conv_b[l], w_out, l, final_g, seq=seq, dec_seq=dec_seq,
            final_norm=l == DEPTH - 1)

    y_prompt = xc.reshape(batch, seq, D_MODEL)
    y_sample = xl.reshape(dec_batch, dec_seq, D_MODEL)
    return (y_prompt, y_sample, new_k, new_v)
```

```python
import functools
import math

import numpy as np
import jax
import jax.numpy as jnp
from jax import lax
from jax.experimental import pallas as pl
from jax.experimental.pallas import tpu as pltpu

D_MODEL = 1024
DEPTH = 2
GRID_W = 64
FOURIER_W = 256
CONV_W = 256
ATTN_W = 512
N_HEADS = 4
HEAD_DIM = 64
QK_W = 128
ROPE_BASE = 10000.0
EPS = 1e-6
IN_DIM = 3584

F32 = jnp.float32
BF16 = jnp.bfloat16

VMEM_LIMIT_BYTES = 56 * 1024 * 1024
TOKEN_TILE = 512
OUT_TILE = 1024
Q_TILE = 256
KEY_CHUNK = 256
ONES_ROWS = 16
LANES = 128
FFT_RADIX = 8
FFT_ROWS = 16
SQRT_HALF = 0.7071067811865476
CTX_ATTN_SEQS = 2
CTX_FOURIER_SEQS = 4
HALO_ROWS = 16
MOD_ROWS = 8
LOG2E = 1.4426950408889634

_COLS = {}
_off = 0
for _name, _w in (("fa", 256), ("ga", 256), ("bg", 256), ("cg", 256), ("hc", 256), ("gb", 256),
                  ("q", 512), ("k", 512), ("v", 512), ("gc", 512)):
    _COLS[_name] = (_off, _off + _w)
    _off += _w


def _silu(x):
    return x * (1.0 / (1.0 + jnp.exp(-x)))


def _cparams(sem):
    return pltpu.CompilerParams(dimension_semantics=sem, vmem_limit_bytes=VMEM_LIMIT_BYTES)


def _rope_tables(n_tokens):
    n = np.arange(n_tokens)
    row = (n // GRID_W).astype(np.float64)
    col = (n % GRID_W).astype(np.float64)
    j = np.arange(QK_W)
    jj = j % HEAD_DIM
    idx = jj % 32
    inv = 1.0 / (ROPE_BASE ** (2.0 * (idx % 16) / 32.0))
    pos = np.where((jj < 32)[None, :], row[:, None], col[:, None])
    ang = pos * inv[None, :]
    cos = np.cos(ang)
    sin = np.sin(ang)
    first = (idx < 16)[None, :]
    sin_a = np.where(first, -sin, 0.0)
    sin_b = np.where(first, 0.0, sin)
    return (np.asarray(cos, np.float32), np.asarray(sin_a, np.float32), np.asarray(sin_b, np.float32))


def _dft_tables(n):
    k = np.arange(n)
    kn = (k[:, None] * k[None, :]) % n
    ang = 2.0 * np.pi * kn / n
    return np.asarray(np.concatenate([np.cos(ang), -np.sin(ang)], axis=1) / math.sqrt(n), np.float32)


def _chan_tables():
    k = np.arange(FOURIER_W)
    kn = (k[:, None] * k[None, :]) % FOURIER_W
    ang = 2.0 * np.pi * kn / FOURIER_W
    return np.asarray(np.concatenate([np.cos(ang), np.sin(ang)], axis=1) / math.sqrt(FOURIER_W), np.float32)


def _ct_tables(n):
    n2 = n // FFT_RADIX
    k = np.arange(n2)
    ang = 2.0 * np.pi * ((k[:, None] * k[None, :]) % n2) / n2
    f1 = np.concatenate([np.cos(ang), -np.sin(ang)], axis=0) / math.sqrt(n * FOURIER_W)
    tw = 2.0 * np.pi * k[:, None] * np.arange(FFT_RADIX)[None, :] / n
    twc = np.repeat(np.cos(tw), FOURIER_W, axis=1)
    tws = np.repeat(np.sin(tw), FOURIER_W, axis=1)
    c = np.arange(FOURIER_W)
    angc = 2.0 * np.pi * ((c[:, None] * c[None, :]) % FOURIER_W) / FOURIER_W
    cs2 = np.concatenate([np.cos(angc), np.sin(angc)], axis=0)
    return tuple(np.asarray(t, np.float32) for t in (f1, twc, tws, cs2))


def _mod_kernel(cctx_ref, c_ref, w_ref, b_ref, o_ref, s_ref):
    n_c = c_ref.shape[0]
    s_ref[...] = jnp.zeros_like(s_ref)
    s_ref[0:1, :] = _silu(cctx_ref[...])
    s_ref[1:1 + n_c, :] = _silu(c_ref[...])
    w = w_ref[0].astype(BF16)
    res = jnp.dot(s_ref[...].astype(BF16), w, preferred_element_type=F32) + b_ref[0]
    for r in range(MOD_ROWS):
        o_ref[0, r] = res[r:r + 1]


def _modulation(c_ctx, c, w_mod, b_mod):
    chunk = D_MODEL
    n_chunks = 3 * D_MODEL // chunk
    return pl.pallas_call(
        _mod_kernel,
        grid=(DEPTH, n_chunks),
        in_specs=[
            pl.BlockSpec((1, D_MODEL), lambda l, j: (0, 0)),
            pl.BlockSpec(c.shape, lambda l, j: (0, 0)),
            pl.BlockSpec((1, D_MODEL, chunk), lambda l, j: (l, 0, j)),
            pl.BlockSpec((1, 1, chunk), lambda l, j: (l, 0, j)),
        ],
        out_specs=pl.BlockSpec((1, MOD_ROWS, 1, chunk), lambda l, j: (l, 0, 0, j)),
        out_shape=jax.ShapeDtypeStruct((DEPTH, MOD_ROWS, 1, 3 * D_MODEL), F32),
        scratch_shapes=[pltpu.VMEM((MOD_ROWS, D_MODEL), F32)],
        compiler_params=_cparams(("arbitrary", "arbitrary")),
        name="modulation",
    )(c_ctx.reshape(1, D_MODEL), c, w_mod, b_mod.reshape(DEPTH, 1, 3 * D_MODEL))


def _store_heads(ref, t):
    seqs, _, seq, _, _ = ref.shape
    for s in range(seqs):
        for hd in range(N_HEADS):
            ref[s, 0, :, hd, :] = t[s * seq:(s + 1) * seq, hd * QK_W:(hd + 1) * QK_W]


def _inproj_tile(x_ref, mod_ref, g_ref, w_ref, rope_refs, outs, kv_refs, stage_ref):
    fa_ref, sga_ref, zc_ref, bgs_ref, q_ref, k_ref, v_ref, sgc_ref = outs

    x = x_ref[...]
    ms = jnp.mean(x * x, axis=-1, keepdims=True)
    y = x * lax.rsqrt(ms + EPS) * g_ref[...]
    m = mod_ref[0, 0]
    h = (y * (1.0 + m[:, D_MODEL:2 * D_MODEL]) + m[:, :D_MODEL]).astype(BF16)

    def proj(name):
        lo, hi = _COLS[name]
        return jnp.dot(h, w_ref[:, lo:hi], preferred_element_type=F32)

    if stage_ref is not None:
        fa = proj("fa")
        halves = stage_ref.shape[0]
        for hf in range(halves):
            stage_ref[hf] = fa[:, hf * LANES:(hf + 1) * LANES]
        rows = stage_ref.shape[1] // FFT_RADIX
        for n1 in range(FFT_RADIX):
            for hf in range(halves):
                lo = n1 * FOURIER_W + hf * LANES
                fa_ref[:, lo:lo + LANES] = (
                    stage_ref[hf, pl.ds(n1, rows, stride=FFT_RADIX), :].astype(BF16))
    else:
        fa_ref[...] = proj("fa").astype(BF16)
    sga_ref[...] = _silu(proj("ga")).astype(BF16)
    zc_ref[...] = (proj("cg") * proj("hc")).astype(BF16)
    bgs_ref[...] = (proj("bg") * _silu(proj("gb"))).astype(BF16)
    sgc_ref[...] = _silu(proj("gc")).astype(BF16)

    q = proj("q")
    k = proj("k")
    v = proj("v")
    if kv_refs is not None:
        _store_heads(kv_refs[0], k)
        _store_heads(kv_refs[1], v)
    v_ref[...] = v.astype(BF16)

    q_scale = HEAD_DIM ** -0.5 * LOG2E
    if rope_refs is not None:
        cos = rope_refs[0][...]
        sa = rope_refs[1][...]
        sb = rope_refs[2][...]
        for hd in range(N_HEADS):
            sl = slice(hd * QK_W, (hd + 1) * QK_W)
            for t, ref, scale in ((q, q_ref, q_scale), (k, k_ref, None)):
                th = t[:, sl]
                r = (th * cos + pltpu.roll(th, QK_W - 16, axis=1) * sa
                     + pltpu.roll(th, 16, axis=1) * sb)
                if scale is not None:
                    r = r * scale
                ref[:, sl] = r.astype(BF16)
    else:
        q_ref[...] = (q * q_scale).astype(BF16)
        k_ref[...] = k.astype(BF16)


N_PROJ_OUTS = 8


def _inproj_kernel(xc_ref, xd_ref, modc_ref, modd_ref, g_ref, w_ref, cos_ref, sa_ref, sb_ref,
                   *rest, n_ctx_tiles, n_alias):
    rest = rest[n_alias:]
    ctx_outs = rest[:N_PROJ_OUTS]
    kv_refs = rest[N_PROJ_OUTS:N_PROJ_OUTS + 2]
    den_outs = rest[N_PROJ_OUTS + 2:2 * N_PROJ_OUTS + 2]
    wb_ref, stage_ref = rest[2 * N_PROJ_OUTS + 2:]
    i = pl.program_id(0)

    @pl.when(i == 0)
    def _():
        for lo, hi in _COLS.values():
            wb_ref[:, lo:hi] = w_ref[0, :, lo:hi].astype(BF16)

    @pl.when(i < n_ctx_tiles)
    def _():
        _inproj_tile(xc_ref, modc_ref, g_ref, wb_ref, None, ctx_outs, kv_refs, None)

    @pl.when(i >= n_ctx_tiles)
    def _():
        _inproj_tile(xd_ref, modd_ref, g_ref, wb_ref, (cos_ref, sa_ref, sb_ref), den_outs, None,
                     stage_ref)


def _in_projection(xc, xd, mod, norm_g, w_in, layer, rope_tabs, *, seq, dec_seq,
                   new_k, new_v):
    tm = TOKEN_TILE
    n_c, n_d = xc.shape[0], xd.shape[0]
    nct, ndt = n_c // tm, n_d // tm
    seq_tiles = dec_seq // tm
    ctx_i = lambda i: jnp.minimum(i, nct - 1)
    den_i = lambda i: jnp.maximum(i - nct, 0)
    row_c = lambda w: pl.BlockSpec((tm, w), lambda i: (ctx_i(i), 0))
    row_d = lambda w: pl.BlockSpec((tm, w), lambda i: (den_i(i), 0))
    tab = pl.BlockSpec((tm, QK_W), lambda i: (den_i(i) % seq_tiles, 0))
    in_specs = [
        row_c(D_MODEL), row_d(D_MODEL),
        pl.BlockSpec((1, 1, 1, 3 * D_MODEL), lambda i: (layer, 0, 0, 0)),
        pl.BlockSpec((1, 1, 1, 3 * D_MODEL), lambda i: (layer, 1 + den_i(i) // seq_tiles, 0, 0)),
        pl.BlockSpec((1, D_MODEL), lambda i: (0, 0)),
        pl.BlockSpec((1, D_MODEL, IN_DIM), lambda i: (layer, 0, 0)),
        tab, tab, tab,
    ]
    args = [xc, xd, mod, mod, norm_g.reshape(1, D_MODEL), w_in, *rope_tabs]
    widths = (256, 256, 256, 256, 512, 512, 512, 512)
    seqs = tm // seq
    kv_blk = pl.BlockSpec((seqs, 1, seq, N_HEADS, QK_W), lambda i: (ctx_i(i), layer, 0, 0, 0))
    kv_shape = jax.ShapeDtypeStruct((n_c // seq, DEPTH, seq, N_HEADS, QK_W), F32)
    out_specs = [row_c(w) for w in widths] + [kv_blk, kv_blk] + [row_d(w) for w in widths]
    out_shape = ([jax.ShapeDtypeStruct((n_c, w), BF16) for w in widths] + [kv_shape, kv_shape]
                 + [jax.ShapeDtypeStruct((n_d, w), BF16) for w in widths])
    fa_d = N_PROJ_OUTS + 2
    out_specs[fa_d] = pl.BlockSpec((tm // FFT_RADIX, FFT_RADIX * FOURIER_W), lambda i: (den_i(i), 0))
    out_shape[fa_d] = jax.ShapeDtypeStruct((n_d // FFT_RADIX, FFT_RADIX * FOURIER_W), BF16)
    aliases = {}
    n_alias = 0
    if new_k is not None:
        aliases = {len(args): N_PROJ_OUTS, len(args) + 1: N_PROJ_OUTS + 1}
        in_specs += [pl.BlockSpec(memory_space=pl.ANY)] * 2
        args += [new_k, new_v]
        n_alias = 2
    outs = pl.pallas_call(
        functools.partial(_inproj_kernel, n_ctx_tiles=nct, n_alias=n_alias),
        grid=(nct + ndt,),
        in_specs=in_specs,
        out_specs=out_specs,
        out_shape=out_shape,
        scratch_shapes=[pltpu.VMEM((D_MODEL, IN_DIM), BF16),
                        pltpu.VMEM((FOURIER_W // LANES, tm, LANES), F32)],
        input_output_aliases=aliases,
        compiler_params=_cparams(("arbitrary",)),
        name="in_projection",
    )(*args)
    return outs[:N_PROJ_OUTS], outs[N_PROJ_OUTS], outs[N_PROJ_OUTS + 1], outs[fa_d:]


def _lambda(lam_ref, lam_init):
    lv = lam_ref[...]
    a = jnp.sum(lv[0:1] * lv[1:2], axis=-1, keepdims=True)
    b = jnp.sum(lv[2:3] * lv[3:4], axis=-1, keepdims=True)
    return jnp.exp(a) - jnp.exp(b) + lam_init


def _stack_masked(q_t):
    first = lax.broadcasted_iota(jnp.int32, q_t.shape, 0) < HEAD_DIM
    return jnp.concatenate([jnp.where(first, q_t, 0.0), jnp.where(first, 0.0, q_t)],
                           axis=1).astype(BF16)


def _scores_chunk(k, qq_t, m):
    s = jnp.dot(k, qq_t, preferred_element_type=F32)
    mc = jnp.max(s, axis=0, keepdims=True)
    return s, (mc if m is None else jnp.maximum(m, mc))


def _with_ones_rows(v_t):
    return jnp.concatenate([v_t, jnp.ones((ONES_ROWS, v_t.shape[1]), F32)], axis=0).astype(BF16)


def _values_chunk(s, m, v1_t, acc):
    e = jnp.exp2(s - m).astype(BF16)
    oc = jnp.dot(v1_t, e, preferred_element_type=F32)
    return oc if acc is None else acc + oc


def _combine_t(acc, lam):
    tq = acc.shape[1] // 2
    r = 1.0 / acc[QK_W:QK_W + 1, :]
    o_t = acc[:QK_W, :]
    return o_t[:, :tq] * r[:, :tq] - o_t[:, tq:] * (r[:, tq:] * lam)


def _subln_gate(o, sg, sgc, lam_init):
    ms = jnp.mean(o * o, axis=-1, keepdims=True)
    y = o * lax.rsqrt(ms + EPS) * sg * (1.0 - lam_init)
    return (y * sgc.astype(F32)).astype(BF16)


def _attn_ctx_kernel(lam_ref, sg_ref, q_ref, k_ref, v_ref, sgc_ref, o_ref, *, lam_init, seq):
    lam = _lambda(lam_ref, lam_init)
    sg = sg_ref[...]
    n_seq = q_ref.shape[0] // seq
    q_t = q_ref[...].astype(F32).T
    v_t = v_ref[...].astype(F32).T
    pairs = [(slice(s * seq, (s + 1) * seq), slice(hd * QK_W, (hd + 1) * QK_W))
             for s in range(n_seq) for hd in range(N_HEADS)]
    qqs = [_stack_masked(q_t[hl, rows]) for rows, hl in pairs]
    v1s = [_with_ones_rows(v_t[hl, rows]) for rows, hl in pairs]
    sm = [_scores_chunk(k_ref[rows, hl], qq, None) for (rows, hl), qq in zip(pairs, qqs)]
    accs = [_values_chunk(s, m, v1, None) for (s, m), v1 in zip(sm, v1s)]
    y_t = [_combine_t(a, lam) for a in accs]
    for s in range(n_seq):
        rows = slice(s * seq, (s + 1) * seq)
        y = jnp.concatenate(y_t[s * N_HEADS:(s + 1) * N_HEADS], axis=0).T
        for hd in range(N_HEADS):
            hl = slice(hd * QK_W, (hd + 1) * QK_W)
            o_ref[rows, hl] = _subln_gate(y[:, hl], sg, sgc_ref[rows, hl], lam_init)


def _attention_ctx(q, k, v, sgc, lam_vec, subln_g, *, seq, lam_init):
    n_tok = q.shape[0]
    blk = pl.BlockSpec((CTX_ATTN_SEQS * seq, ATTN_W), lambda b: (b, 0))
    return pl.pallas_call(
        functools.partial(_attn_ctx_kernel, lam_init=lam_init, seq=seq),
        grid=(n_tok // (CTX_ATTN_SEQS * seq),),
        in_specs=[
            pl.BlockSpec((4, HEAD_DIM), lambda b: (0, 0)),
            pl.BlockSpec((1, QK_W), lambda b: (0, 0)),
            blk, blk, blk, blk,
        ],
        out_specs=blk,
        out_shape=jax.ShapeDtypeStruct((n_tok, ATTN_W), BF16),
        compiler_params=_cparams(("arbitrary",)),
        name="attention_ctx",
    )(lam_vec, subln_g.reshape(1, QK_W), q, k, v, sgc)


def _attn_den_kernel(lam_ref, sg_ref, q_ref, kn_ref, vn_ref, kc_ref, vc_ref, sgc_ref, o_ref,
                     kcs_ref, vt_ref, qq0_ref, qq1_ref, s0_ref, s1_ref, acc0_ref, acc1_ref,
                     *, lam_init):
    seq = q_ref.shape[0]
    past = kcs_ref.shape[1]
    past_chunks = past // KEY_CHUNK
    n_chunks = (past + seq) // KEY_CHUNK
    n_tiles = seq // Q_TILE
    n_elems = n_tiles * N_HEADS
    qq_refs = (qq0_ref, qq1_ref)
    s_refs = (s0_ref, s1_ref)
    acc_refs = (acc0_ref, acc1_ref)
    lam = _lambda(lam_ref, lam_init)
    sg = sg_ref[...]

    def lanes(hd):
        return slice(hd * QK_W, (hd + 1) * QK_W)

    def chunk(c):
        return slice(c * KEY_CHUNK, (c + 1) * KEY_CHUNK)

    def tile_rows(t):
        if isinstance(t, int):
            return pl.ds(t * Q_TILE, Q_TILE)
        return pl.ds(pl.multiple_of(t * Q_TILE, Q_TILE), Q_TILE)

    for hd in range(N_HEADS):
        kcs_ref[hd] = kc_ref[0, 0, :, hd, :].astype(BF16)
        vt_ref[hd, :, :past] = _with_ones_rows(vc_ref[0, 0, :, hd, :].T)
        vt_ref[hd, :, past:] = _with_ones_rows(vn_ref[:, lanes(hd)].astype(F32).T)

    def keys(hd, c):
        if c < past_chunks:
            return kcs_ref[hd, chunk(c), :]
        return kn_ref[chunk(c - past_chunks), lanes(hd)]

    def prep(t, hd, par):
        qq_refs[par][...] = _stack_masked(q_ref[tile_rows(t), lanes(hd)].astype(F32).T)

    def finish(t, hd, par):
        rows = tile_rows(t)
        y = _combine_t(acc_refs[par][...], lam).T
        o_ref[rows, lanes(hd)] = _subln_gate(y, sg, sgc_ref[rows, lanes(hd)], lam_init)

    def step(t, j, m_cur, first=False, last=False):
        def elem(off):
            return t + (j + off) // N_HEADS, (j + off) % N_HEADS

        par = j % 2
        e_static = N_HEADS * t + j if isinstance(t, int) else None
        do_finish = not (first and j == 0)
        do_scores = not (last and e_static + 1 >= n_elems)
        do_prep = not (last and e_static + 2 >= n_elems)
        if do_prep:
            prep(*elem(2), par)
        hd_nxt = elem(1)[1]
        qq = qq_refs[1 - par][...] if do_scores else None
        m_nxt = None
        acc = None
        for c in range(n_chunks):
            if do_scores:
                s, m_nxt = _scores_chunk(keys(hd_nxt, c), qq, m_nxt)
                s_refs[1 - par][chunk(c), :] = s
            acc = _values_chunk(s_refs[par][chunk(c), :], m_cur, vt_ref[j, :, chunk(c)], acc)
        acc_refs[par][...] = acc
        if do_finish:
            finish(*elem(-1), 1 - par)
        return m_nxt

    def tile_steps(t, m, **edge):
        for j in range(N_HEADS):
            m = step(t, j, m, **edge)
        return m

    prep(0, 0, 0)
    prep(0, 1, 1)
    qq = qq0_ref[...]
    m = None
    for c in range(n_chunks):
        s, m = _scores_chunk(keys(0, c), qq, m)
        s0_ref[chunk(c), :] = s
    m = tile_steps(0, m, first=True)
    m = lax.fori_loop(1, n_tiles - 1, tile_steps, m)
    tile_steps(n_tiles - 1, m, last=True)
    finish(n_tiles - 1, N_HEADS - 1, (n_elems - 1) % 2)


def _attention_den(q, k, v, sgc, cache_k, cache_v, lam_vec, subln_g, *, layer, seq, lam_init):
    n_tok = q.shape[0]
    past = cache_k.shape[2]
    blk = pl.BlockSpec((seq, ATTN_W), lambda b: (b, 0))
    cblk = pl.BlockSpec((1, 1, past, N_HEADS, QK_W), lambda b: (b, layer, 0, 0, 0))
    return pl.pallas_call(
        functools.partial(_attn_den_kernel, lam_init=lam_init),
        grid=(n_tok // seq,),
        in_specs=[
            pl.BlockSpec((4, HEAD_DIM), lambda b: (0, 0)),
            pl.BlockSpec((1, QK_W), lambda b: (0, 0)),
            blk, blk, blk, cblk, cblk, blk,
        ],
        out_specs=blk,
        out_shape=jax.ShapeDtypeStruct((n_tok, ATTN_W), BF16),
        scratch_shapes=[
            pltpu.VMEM((N_HEADS, past, QK_W), BF16),
            pltpu.VMEM((N_HEADS, QK_W + ONES_ROWS, past + seq), BF16),
            pltpu.VMEM((QK_W, 2 * Q_TILE), BF16),
            pltpu.VMEM((QK_W, 2 * Q_TILE), BF16),
            pltpu.VMEM((past + seq, 2 * Q_TILE), F32),
            pltpu.VMEM((past + seq, 2 * Q_TILE), F32),
            pltpu.VMEM((QK_W + ONES_ROWS, 2 * Q_TILE), F32),
            pltpu.VMEM((QK_W + ONES_ROWS, 2 * Q_TILE), F32),
        ],
        compiler_params=_cparams(("arbitrary",)),
        name="attention_den",
    )(lam_vec, subln_g.reshape(1, QK_W), q, k, v, cache_k, cache_v, sgc)


def _fourier_ctx_kernel(fa_ref, sga_ref, cs_ref, dn_ref, o_ref, *, seq):
    ab = jnp.dot(fa_ref[...], cs_ref[...], preferred_element_type=F32).astype(BF16)
    for s in range(fa_ref.shape[0] // seq):
        rows = slice(s * seq, (s + 1) * seq)
        ab2 = jnp.concatenate([ab[rows, :FOURIER_W], ab[rows, FOURIER_W:]], axis=0)
        f = jnp.dot(dn_ref[...], ab2, preferred_element_type=F32)
        o_ref[rows, :] = (f * sga_ref[rows, :].astype(F32)).astype(BF16)


def _fourier_ctx(fa, sga, cs, dn, *, seq):
    n_tok = fa.shape[0]
    blk = pl.BlockSpec((CTX_FOURIER_SEQS * seq, FOURIER_W), lambda b: (b, 0))
    return pl.pallas_call(
        functools.partial(_fourier_ctx_kernel, seq=seq),
        grid=(n_tok // (CTX_FOURIER_SEQS * seq),),
        in_specs=[blk, blk,
                  pl.BlockSpec((FOURIER_W, 2 * FOURIER_W), lambda b: (0, 0)),
                  pl.BlockSpec((seq, 2 * seq), lambda b: (0, 0))],
        out_specs=blk,
        out_shape=jax.ShapeDtypeStruct((n_tok, FOURIER_W), BF16),
        compiler_params=_cparams(("arbitrary",)),
        name="fourier_ctx",
    )(fa, sga, cs, dn)


def _cadd(a, b):
    return (a[0] + b[0], a[1] + b[1])


def _csub(a, b):
    return (a[0] - b[0], a[1] - b[1])


def _mul_neg_i(a):
    return (a[1], -a[0])


def _mul_w8_1(a):
    return ((a[0] + a[1]) * SQRT_HALF, (a[1] - a[0]) * SQRT_HALF)


def _mul_w8_3(a):
    return ((a[1] - a[0]) * SQRT_HALF, (-a[0] - a[1]) * SQRT_HALF)


def _fft4(a0, a1, a2, a3):
    e0, e1 = _cadd(a0, a2), _csub(a0, a2)
    o0, o1 = _cadd(a1, a3), _mul_neg_i(_csub(a1, a3))
    return [_cadd(e0, o0), _cadd(e1, o1), _csub(e0, o0), _csub(e1, o1)]


def _fft8(x):
    e = _fft4(x[0], x[2], x[4], x[6])
    o = _fft4(x[1], x[3], x[5], x[7])
    t = [o[0], _mul_w8_1(o[1]), _mul_neg_i(o[2]), _mul_w8_3(o[3])]
    return [_cadd(e[k], t[k]) for k in range(4)] + [_csub(e[k], t[k]) for k in range(4)]


def _fourier_den_kernel(xw_ref, sga_ref, f1_ref, twc_ref, tws_ref, cs2_ref, o_ref, g_ref, x_ref):
    n2 = xw_ref.shape[0]
    g_ref[...] = jnp.dot(f1_ref[...], xw_ref[...], preferred_element_type=F32)

    def chunk(i, carry):
        r = pl.multiple_of(i * FFT_ROWS, FFT_ROWS)
        re_rows = pl.ds(r, FFT_ROWS)
        im_rows = pl.ds(n2 + r, FFT_ROWS)
        xs = []
        for n1 in range(FFT_RADIX):
            lanes = slice(n1 * FOURIER_W, (n1 + 1) * FOURIER_W)
            gr = g_ref[re_rows, lanes]
            gi = g_ref[im_rows, lanes]
            if n1 > 0:
                c = twc_ref[re_rows, lanes]
                s = tws_ref[re_rows, lanes]
                gr, gi = gr * c + gi * s, gi * c - gr * s
            xs.append((gr, gi))
        for k1, (xr, xi) in enumerate(_fft8(xs)):
            out_rows = pl.ds(k1 * n2 + r, FFT_ROWS)
            x_ref[out_rows, :FOURIER_W] = xr.astype(BF16)
            x_ref[out_rows, FOURIER_W:] = xi.astype(BF16)
        return carry

    lax.fori_loop(0, n2 // FFT_ROWS, chunk, 0)
    f = jnp.dot(x_ref[...], cs2_ref[...], preferred_element_type=F32)
    o_ref[...] = (f * sga_ref[...].astype(F32)).astype(BF16)


def _fourier_den(xw, sga, f1, twc, tws, cs2, *, seq):
    n_tok = sga.shape[0]
    n2 = seq // FFT_RADIX
    wide = FFT_RADIX * FOURIER_W
    const = lambda shape: pl.BlockSpec(shape, lambda b: (0, 0))
    return pl.pallas_call(
        _fourier_den_kernel,
        grid=(n_tok // seq,),
        in_specs=[
            pl.BlockSpec((n2, wide), lambda b: (b, 0)),
            pl.BlockSpec((seq, FOURIER_W), lambda b: (b, 0)),
            const((2 * n2, n2)), const((n2, wide)), const((n2, wide)),
            const((2 * FOURIER_W, FOURIER_W)),
        ],
        out_specs=pl.BlockSpec((seq, FOURIER_W), lambda b: (b, 0)),
        out_shape=jax.ShapeDtypeStruct((n_tok, FOURIER_W), BF16),
        scratch_shapes=[pltpu.VMEM((2 * n2, wide), F32),
                        pltpu.VMEM((seq, 2 * FOURIER_W), BF16)],
        compiler_params=_cparams(("arbitrary",)),
        name="fourier_den",
    )(xw, sga, f1, twc, tws, cs2)


def _outproj_tile(x_ref, mod_ref, ya_ref, zc_ref, zp_ref, zn_ref, bgs_ref, yc_ref,
                  cw_ref, cb_ref, w_ref, fg_ref, o_ref, tile, *, seq, final_norm):
    tm = x_ref.shape[0]
    z = zc_ref[...].astype(F32)
    row = lax.broadcasted_iota(jnp.int32, z.shape, 0)
    pos = (tile * tm + row) & (seq - 1)
    prev_row = zp_ref[HALO_ROWS - 1:HALO_ROWS, :].astype(F32)
    next_row = zn_ref[0:1, :].astype(F32)
    z_prev = jnp.where(row == 0, prev_row, pltpu.roll(z, 1, axis=0))
    z_prev = jnp.where(pos == 0, 0.0, z_prev)
    z_next = jnp.where(row == tm - 1, next_row, pltpu.roll(z, tm - 1, axis=0))
    z_next = jnp.where(pos == seq - 1, 0.0, z_next)
    cw = cw_ref[...]
    conv = z_prev * cw[0:1] + z * cw[1:2] + z_next * cw[2:3] + cb_ref[...]
    yb = (bgs_ref[...].astype(F32) * conv).astype(BF16)

    mixed = jnp.concatenate([ya_ref[...], yb, yc_ref[...]], axis=-1)
    out = jnp.dot(mixed, w_ref[...], preferred_element_type=F32)
    gate = mod_ref[0, 0][:, 2 * D_MODEL:]
    xn = x_ref[...] + gate * out
    if final_norm:
        ms = jnp.mean(xn * xn, axis=-1, keepdims=True)
        xn = xn * lax.rsqrt(ms + EPS) * fg_ref[...]
    o_ref[...] = xn


N_GROUP_INS = 8


def _outproj_kernel(*refs, n_ctx_tiles, seq, dec_seq, final_norm):
    ctx_ins = refs[:N_GROUP_INS]
    den_ins = refs[N_GROUP_INS:2 * N_GROUP_INS]
    cw_ref, cb_ref, w_ref, fg_ref, oc_ref, od_ref, wb_ref = refs[2 * N_GROUP_INS:]
    i = pl.program_id(0)

    @pl.when(i == 0)
    def _():
        wb_ref[...] = w_ref[0].astype(BF16)

    @pl.when(i < n_ctx_tiles)
    def _():
        _outproj_tile(*ctx_ins, cw_ref, cb_ref, wb_ref, fg_ref, oc_ref, i, seq=seq,
                      final_norm=final_norm)

    @pl.when(i >= n_ctx_tiles)
    def _():
        _outproj_tile(*den_ins, cw_ref, cb_ref, wb_ref, fg_ref, od_ref, i - n_ctx_tiles,
                      seq=dec_seq, final_norm=final_norm)


def _out_projection(ctx, den, mod, conv_w, conv_b, w_out, layer, final_g, *, seq, dec_seq,
                    final_norm):
    tm = OUT_TILE
    nct = ctx[0].shape[0] // tm
    ndt = den[0].shape[0] // tm
    halo_per_tile = tm // HALO_ROWS
    seq_tiles = dec_seq // tm

    def group_specs(tile_of, n_tok, mod_of):
        n_halo = n_tok // HALO_ROWS
        row = lambda w: pl.BlockSpec((tm, w), lambda i: (tile_of(i), 0))
        return [
            row(D_MODEL),
            pl.BlockSpec((1, 1, 1, 3 * D_MODEL), lambda i: (layer, mod_of(tile_of(i)), 0, 0)),
            row(FOURIER_W),
            row(CONV_W),
            pl.BlockSpec((HALO_ROWS, CONV_W),
                         lambda i: (jnp.maximum(tile_of(i) * halo_per_tile - 1, 0), 0)),
            pl.BlockSpec((HALO_ROWS, CONV_W),
                         lambda i: (jnp.minimum((tile_of(i) + 1) * halo_per_tile, n_halo - 1), 0)),
            row(CONV_W),
            row(ATTN_W),
        ], row(D_MODEL)

    ctx_specs, ctx_out = group_specs(lambda i: jnp.minimum(i, nct - 1), ctx[0].shape[0],
                                     lambda t: 0)
    den_specs, den_out = group_specs(lambda i: jnp.maximum(i - nct, 0), den[0].shape[0],
                                     lambda t: 1 + t // seq_tiles)

    def group_args(g):
        x, ya, zc, bgs, yc = g
        return [x, mod, ya, zc, zc, zc, bgs, yc]

    return pl.pallas_call(
        functools.partial(_outproj_kernel, n_ctx_tiles=nct, seq=seq, dec_seq=dec_seq,
                          final_norm=final_norm),
        grid=(nct + ndt,),
        in_specs=ctx_specs + den_specs + [
            pl.BlockSpec((3, CONV_W), lambda i: (0, 0)),
            pl.BlockSpec((1, CONV_W), lambda i: (0, 0)),
            pl.BlockSpec((1, D_MODEL, D_MODEL), lambda i: (layer, 0, 0)),
            pl.BlockSpec((1, D_MODEL), lambda i: (0, 0)),
        ],
        out_specs=[ctx_out, den_out],
        out_shape=[jax.ShapeDtypeStruct(ctx[0].shape, F32), jax.ShapeDtypeStruct(den[0].shape, F32)],
        scratch_shapes=[pltpu.VMEM((D_MODEL, D_MODEL), BF16)],
        compiler_params=_cparams(("arbitrary",)),
        name="out_projection",
    )(*group_args(ctx), *group_args(den), conv_w, conv_b.reshape(1, CONV_W), w_out,
      final_g.reshape(1, D_MODEL))


def kernel(x_prompt, x_sample, cache_k, cache_v, c, c_ctx, norm_g, w_mod, b_mod, w_in, conv_w,
           conv_b, lam_vec, subln_g, w_out, final_g):
    batch, seq, _ = x_prompt.shape
    dec_batch, dec_seq, _ = x_sample.shape

    mod = _modulation(c_ctx, c, w_mod, b_mod)

    rope_tabs = tuple(jnp.asarray(t) for t in _rope_tables(dec_seq))
    cs = jnp.asarray(_chan_tables()).astype(BF16)
    dn_ctx = jnp.asarray(_dft_tables(seq)).astype(BF16)
    f1, twc, tws, cs2 = (jnp.asarray(t) for t in _ct_tables(dec_seq))
    f1 = f1.astype(BF16)
    cs2 = cs2.astype(BF16)

    xc = x_prompt.reshape(batch * seq, D_MODEL)
    xl = x_sample.reshape(dec_batch * dec_seq, D_MODEL)
    new_k = new_v = None
    for l in range(DEPTH):
        lam_init = 0.8 - 0.6 * math.exp(-0.3 * l)
        ctx, new_k, new_v, den = _in_projection(
            xc, xl, mod, norm_g[l], w_in, l, rope_tabs, seq=seq, dec_seq=dec_seq,
            new_k=new_k, new_v=new_v)

        fa, sga, zc_c, bgs_c, q, k, v, sgc = ctx
        yc_c = _attention_ctx(q, k, v, sgc, lam_vec[l], subln_g[l], seq=seq, lam_init=lam_init)
        ya_c = _fourier_ctx(fa, sga, cs, dn_ctx, seq=seq)

        fa, sga, zc_d, bgs_d, q, k, v, sgc = den
        yc_d = _attention_den(q, k, v, sgc, cache_k, cache_v, lam_vec[l], subln_g[l], layer=l,
                              seq=dec_seq, lam_init=lam_init)
        ya_d = _fourier_den(fa, sga, f1, twc, tws, cs2, seq=dec_seq)

        xc, xl = _out_projection(
            (xc, ya_c, zc_c, bgs_c, yc_c), (xl, ya_d, zc_d, bgs_d, yc_d), mod,
            conv_w[l], conv_b[l], w_out, l, final_g, seq=seq, dec_seq=dec_seq,
            final_norm=l == DEPTH - 1)

    y_prompt = xc.reshape(batch, seq, D_MODEL)
    y_sample = xl.reshape(dec_batch, dec_seq, D_MODEL)
    return (y_prompt, y_sample, new_k, new_v)
```

```python
import functools
import math

import numpy as np
import jax
import jax.numpy as jnp
from jax import lax
from jax.experimental import pallas as pl
from jax.experimental.pallas import tpu as pltpu

D_MODEL = 1024
DEPTH = 2
GRID_W = 64
FOURIER_W = 256
CONV_W = 256
ATTN_W = 512
N_HEADS = 4
HEAD_DIM = 64
QK_W = 128
ROPE_BASE = 10000.0
ROPE_W = HEAD_DIM // 2
ROPE_ROT = ROPE_W // 2
EPS = 1e-6
IN_DIM = 3584

F32 = jnp.float32
BF16 = jnp.bfloat16

VMEM_LIMIT_BYTES = 56 * 1024 * 1024
TOKEN_TILE = 512
OUT_TILE = 1024
Q_TILE = 256
KEY_CHUNK = 256
ONES_ROWS = 16
LANES = 128
FFT_RADIX = 8
FFT_ROWS = 16
SQRT_HALF = 0.7071067811865476
CTX_ATTN_SEQS = 4
CTX_FOURIER_SEQS = 4
HALO_ROWS = 16
MOD_ROWS = 8
MOD_CHUNK = 3 * D_MODEL
LOG2E = 1.4426950408889634

_COLS = {}
_off = 0
for _name, _w in (("fa", 256), ("ga", 256), ("bg", 256), ("cg", 256), ("hc", 256), ("gb", 256),
                  ("q", 512), ("k", 512), ("v", 512), ("gc", 512)):
    _COLS[_name] = (_off, _off + _w)
    _off += _w


def _silu(x):
    return x * (1.0 / (1.0 + jnp.exp(-x)))


def _cparams(sem):
    return pltpu.CompilerParams(dimension_semantics=sem, vmem_limit_bytes=VMEM_LIMIT_BYTES)


def _rope_tables(n_tokens):
    n = np.arange(n_tokens)
    row = (n // GRID_W).astype(np.float64)
    col = (n % GRID_W).astype(np.float64)
    j = np.arange(QK_W)
    jj = j % HEAD_DIM
    idx = jj % ROPE_W
    inv = 1.0 / (ROPE_BASE ** (2.0 * (idx % ROPE_ROT) / ROPE_W))
    pos = np.where((jj < ROPE_W)[None, :], row[:, None], col[:, None])
    ang = pos * inv[None, :]
    cos = np.cos(ang)
    sin = np.sin(ang)
    first = (idx < ROPE_ROT)[None, :]
    sin_a = np.where(first, -sin, 0.0)
    sin_b = np.where(first, 0.0, sin)
    return (np.asarray(cos, np.float32), np.asarray(sin_a, np.float32), np.asarray(sin_b, np.float32))


def _dft_tables(n):
    k = np.arange(n)
    kn = (k[:, None] * k[None, :]) % n
    ang = 2.0 * np.pi * kn / n
    return np.asarray(np.concatenate([np.cos(ang), -np.sin(ang)], axis=1) / math.sqrt(n), np.float32)


def _chan_tables():
    k = np.arange(FOURIER_W)
    kn = (k[:, None] * k[None, :]) % FOURIER_W
    ang = 2.0 * np.pi * kn / FOURIER_W
    return np.asarray(np.concatenate([np.cos(ang), np.sin(ang)], axis=1) / math.sqrt(FOURIER_W), np.float32)


def _ct_tables(n):
    n2 = n // FFT_RADIX
    k = np.arange(n2)
    ang = 2.0 * np.pi * ((k[:, None] * k[None, :]) % n2) / n2
    f1 = np.concatenate([np.cos(ang), -np.sin(ang)], axis=0) / math.sqrt(n * FOURIER_W)
    tw = 2.0 * np.pi * k[:, None] * np.arange(FFT_RADIX)[None, :] / n
    twc = np.repeat(np.cos(tw), FOURIER_W, axis=1)
    tws = np.repeat(np.sin(tw), FOURIER_W, axis=1)
    c = np.arange(FOURIER_W)
    angc = 2.0 * np.pi * ((c[:, None] * c[None, :]) % FOURIER_W) / FOURIER_W
    cs2 = np.concatenate([np.cos(angc), np.sin(angc)], axis=0)
    return tuple(np.asarray(t, np.float32) for t in (f1, twc, tws, cs2))


def _mod_kernel(cctx_ref, c_ref, w_ref, b_ref, o_ref, s_ref):
    n_c = c_ref.shape[0]
    s_ref[...] = jnp.zeros_like(s_ref)
    s_ref[0:1, :] = _silu(cctx_ref[...])
    s_ref[1:1 + n_c, :] = _silu(c_ref[...])
    w = w_ref[0].astype(BF16)
    res = jnp.dot(s_ref[...].astype(BF16), w, preferred_element_type=F32) + b_ref[0]
    for r in range(MOD_ROWS):
        o_ref[0, r] = res[r:r + 1]


def _modulation(c_ctx, c, w_mod, b_mod):
    chunk = MOD_CHUNK
    n_chunks = 3 * D_MODEL // chunk
    return pl.pallas_call(
        _mod_kernel,
        grid=(DEPTH, n_chunks),
        in_specs=[
            pl.BlockSpec((1, D_MODEL), lambda l, j: (0, 0)),
            pl.BlockSpec(c.shape, lambda l, j: (0, 0)),
            pl.BlockSpec((1, D_MODEL, chunk), lambda l, j: (l, 0, j)),
            pl.BlockSpec((1, 1, chunk), lambda l, j: (l, 0, j)),
        ],
        out_specs=pl.BlockSpec((1, MOD_ROWS, 1, chunk), lambda l, j: (l, 0, 0, j)),
        out_shape=jax.ShapeDtypeStruct((DEPTH, MOD_ROWS, 1, 3 * D_MODEL), F32),
        scratch_shapes=[pltpu.VMEM((MOD_ROWS, D_MODEL), F32)],
        compiler_params=_cparams(("arbitrary", "arbitrary")),
        name="modulation",
    )(c_ctx.reshape(1, D_MODEL), c, w_mod, b_mod.reshape(DEPTH, 1, 3 * D_MODEL))


def _store_heads(ref, t):
    seqs, _, seq, _, _ = ref.shape
    for s in range(seqs):
        for hd in range(N_HEADS):
            ref[s, 0, :, hd, :] = t[s * seq:(s + 1) * seq, hd * QK_W:(hd + 1) * QK_W]


def _inproj_tile(x_ref, mod_ref, g_ref, w_ref, rope_refs, outs, kv_refs, stage_ref):
    fa_ref, sga_ref, zc_ref, bgs_ref, q_ref, k_ref, v_ref, sgc_ref = outs

    x = x_ref[...]
    ms = jnp.mean(x * x, axis=-1, keepdims=True)
    y = x * lax.rsqrt(ms + EPS) * g_ref[...]
    m = mod_ref[0, 0]
    h = (y * (1.0 + m[:, D_MODEL:2 * D_MODEL]) + m[:, :D_MODEL]).astype(BF16)

    def proj(name):
        lo, hi = _COLS[name]
        return jnp.dot(h, w_ref[:, lo:hi], preferred_element_type=F32)

    if stage_ref is not None:
        fa = proj("fa")
        halves = stage_ref.shape[0]
        for hf in range(halves):
            stage_ref[hf] = fa[:, hf * LANES:(hf + 1) * LANES]
        rows = stage_ref.shape[1] // FFT_RADIX
        for n1 in range(FFT_RADIX):
            for hf in range(halves):
                lo = n1 * FOURIER_W + hf * LANES
                fa_ref[:, lo:lo + LANES] = (
                    stage_ref[hf, pl.ds(n1, rows, stride=FFT_RADIX), :].astype(BF16))
    else:
        fa_ref[...] = proj("fa").astype(BF16)
    sga_ref[...] = _silu(proj("ga")).astype(BF16)
    zc_ref[...] = (proj("cg") * proj("hc")).astype(BF16)
    bgs_ref[...] = (proj("bg") * _silu(proj("gb"))).astype(BF16)
    sgc_ref[...] = _silu(proj("gc")).astype(BF16)

    q = proj("q")
    k = proj("k")
    v = proj("v")
    if kv_refs is not None:
        _store_heads(kv_refs[0], k)
        _store_heads(kv_refs[1], v)
    v_ref[...] = v.astype(BF16)

    q_scale = HEAD_DIM ** -0.5 * LOG2E
    if rope_refs is not None:
        cos = rope_refs[0][...]
        sa = rope_refs[1][...]
        sb = rope_refs[2][...]
        for hd in range(N_HEADS):
            sl = slice(hd * QK_W, (hd + 1) * QK_W)
            for t, ref, scale in ((q, q_ref, q_scale), (k, k_ref, None)):
                th = t[:, sl]
                r = (th * cos + pltpu.roll(th, QK_W - ROPE_ROT, axis=1) * sa
                     + pltpu.roll(th, ROPE_ROT, axis=1) * sb)
                if scale is not None:
                    r = r * scale
                ref[:, sl] = r.astype(BF16)
    else:
        q_ref[...] = (q * q_scale).astype(BF16)
        k_ref[...] = k.astype(BF16)


N_PROJ_OUTS = 8


def _inproj_kernel(xc_ref, xd_ref, modc_ref, modd_ref, g_ref, w_ref, cos_ref, sa_ref, sb_ref,
                   *rest, n_ctx_tiles, n_alias):
    rest = rest[n_alias:]
    ctx_outs = rest[:N_PROJ_OUTS]
    kv_refs = rest[N_PROJ_OUTS:N_PROJ_OUTS + 2]
    den_outs = rest[N_PROJ_OUTS + 2:2 * N_PROJ_OUTS + 2]
    wb_ref, stage_ref = rest[2 * N_PROJ_OUTS + 2:]
    i = pl.program_id(0)

    @pl.when(i == 0)
    def _():
        for lo, hi in _COLS.values():
            wb_ref[:, lo:hi] = w_ref[0, :, lo:hi].astype(BF16)

    @pl.when(i < n_ctx_tiles)
    def _():
        _inproj_tile(xc_ref, modc_ref, g_ref, wb_ref, None, ctx_outs, kv_refs, None)

    @pl.when(i >= n_ctx_tiles)
    def _():
        _inproj_tile(xd_ref, modd_ref, g_ref, wb_ref, (cos_ref, sa_ref, sb_ref), den_outs, None,
                     stage_ref)


def _in_projection(xc, xd, mod, norm_g, w_in, layer, rope_tabs, *, seq, dec_seq,
                   new_k, new_v):
    tm = TOKEN_TILE
    n_c, n_d = xc.shape[0], xd.shape[0]
    nct, ndt = n_c // tm, n_d // tm
    seq_tiles = dec_seq // tm
    ctx_i = lambda i: jnp.minimum(i, nct - 1)
    den_i = lambda i: jnp.maximum(i - nct, 0)
    row_c = lambda w: pl.BlockSpec((tm, w), lambda i: (ctx_i(i), 0))
    row_d = lambda w: pl.BlockSpec((tm, w), lambda i: (den_i(i), 0))
    tab = pl.BlockSpec((tm, QK_W), lambda i: (den_i(i) % seq_tiles, 0))
    in_specs = [
        row_c(D_MODEL), row_d(D_MODEL),
        pl.BlockSpec((1, 1, 1, 3 * D_MODEL), lambda i: (layer, 0, 0, 0)),
        pl.BlockSpec((1, 1, 1, 3 * D_MODEL), lambda i: (layer, 1 + den_i(i) // seq_tiles, 0, 0)),
        pl.BlockSpec((1, D_MODEL), lambda i: (0, 0)),
        pl.BlockSpec((1, D_MODEL, IN_DIM), lambda i: (layer, 0, 0)),
        tab, tab, tab,
    ]
    args = [xc, xd, mod, mod, norm_g.reshape(1, D_MODEL), w_in, *rope_tabs]
    widths = (256, 256, 256, 256, 512, 512, 512, 512)
    seqs = tm // seq
    kv_blk = pl.BlockSpec((seqs, 1, seq, N_HEADS, QK_W), lambda i: (ctx_i(i), layer, 0, 0, 0))
    kv_shape = jax.ShapeDtypeStruct((n_c // seq, DEPTH, seq, N_HEADS, QK_W), F32)
    out_specs = [row_c(w) for w in widths] + [kv_blk, kv_blk] + [row_d(w) for w in widths]
    out_shape = ([jax.ShapeDtypeStruct((n_c, w), BF16) for w in widths] + [kv_shape, kv_shape]
                 + [jax.ShapeDtypeStruct((n_d, w), BF16) for w in widths])
    fa_d = N_PROJ_OUTS + 2
    out_specs[fa_d] = pl.BlockSpec((tm // FFT_RADIX, FFT_RADIX * FOURIER_W), lambda i: (den_i(i), 0))
    out_shape[fa_d] = jax.ShapeDtypeStruct((n_d // FFT_RADIX, FFT_RADIX * FOURIER_W), BF16)
    aliases = {}
    n_alias = 0
    if new_k is not None:
        aliases = {len(args): N_PROJ_OUTS, len(args) + 1: N_PROJ_OUTS + 1}
        in_specs += [pl.BlockSpec(memory_space=pl.ANY)] * 2
        args += [new_k, new_v]
        n_alias = 2
    outs = pl.pallas_call(
        functools.partial(_inproj_kernel, n_ctx_tiles=nct, n_alias=n_alias),
        grid=(nct + ndt,),
        in_specs=in_specs,
        out_specs=out_specs,
        out_shape=out_shape,
        scratch_shapes=[pltpu.VMEM((D_MODEL, IN_DIM), BF16),
                        pltpu.VMEM((FOURIER_W // LANES, tm, LANES), F32)],
        input_output_aliases=aliases,
        compiler_params=_cparams(("arbitrary",)),
        name="in_projection",
    )(*args)
    return outs[:N_PROJ_OUTS], outs[N_PROJ_OUTS], outs[N_PROJ_OUTS + 1], outs[fa_d:]


def _lambda(lam_ref, lam_init):
    lv = lam_ref[...]
    a = jnp.sum(lv[0:1] * lv[1:2], axis=-1, keepdims=True)
    b = jnp.sum(lv[2:3] * lv[3:4], axis=-1, keepdims=True)
    return jnp.exp(a) - jnp.exp(b) + lam_init


def _stack_masked(q_t):
    first = lax.broadcasted_iota(jnp.int32, q_t.shape, 0) < HEAD_DIM
    return jnp.concatenate([jnp.where(first, q_t, 0.0), jnp.where(first, 0.0, q_t)],
                           axis=1).astype(BF16)


def _scores_chunk(k, qq_t, m):
    s = jnp.dot(k, qq_t, preferred_element_type=F32)
    mc = jnp.max(s, axis=0, keepdims=True)
    return s, (mc if m is None else jnp.maximum(m, mc))


def _with_ones_rows(v_t):
    return jnp.concatenate([v_t, jnp.ones((ONES_ROWS, v_t.shape[1]), F32)], axis=0).astype(BF16)


def _values_chunk(s, m, v1_t, acc):
    e = jnp.exp2(s - m).astype(BF16)
    oc = jnp.dot(v1_t, e, preferred_element_type=F32)
    return oc if acc is None else acc + oc


def _combine_t(acc, lam):
    tq = acc.shape[1] // 2
    r = 1.0 / acc[QK_W:QK_W + 1, :]
    o_t = acc[:QK_W, :]
    return o_t[:, :tq] * r[:, :tq] - o_t[:, tq:] * (r[:, tq:] * lam)


def _subln_gate(o, sg, sgc, lam_init):
    ms = jnp.mean(o * o, axis=-1, keepdims=True)
    y = o * lax.rsqrt(ms + EPS) * sg * (1.0 - lam_init)
    return (y * sgc.astype(F32)).astype(BF16)


def _attn_ctx_kernel(lam_ref, sg_ref, q_ref, k_ref, v_ref, sgc_ref, o_ref, *, lam_init, seq):
    lam = _lambda(lam_ref, lam_init)
    sg = sg_ref[...]
    n_seq = q_ref.shape[0] // seq
    q_t = q_ref[...].astype(F32).T
    v_t = v_ref[...].astype(F32).T
    pairs = [(slice(s * seq, (s + 1) * seq), slice(hd * QK_W, (hd + 1) * QK_W))
             for s in range(n_seq) for hd in range(N_HEADS)]
    qqs = [_stack_masked(q_t[hl, rows]) for rows, hl in pairs]
    v1s = [_with_ones_rows(v_t[hl, rows]) for rows, hl in pairs]
    sm = [_scores_chunk(k_ref[rows, hl], qq, None) for (rows, hl), qq in zip(pairs, qqs)]
    accs = [_values_chunk(s, m, v1, None) for (s, m), v1 in zip(sm, v1s)]
    y_t = [_combine_t(a, lam) for a in accs]
    for s in range(n_seq):
        rows = slice(s * seq, (s + 1) * seq)
        y = jnp.concatenate(y_t[s * N_HEADS:(s + 1) * N_HEADS], axis=0).T
        for hd in range(N_HEADS):
            hl = slice(hd * QK_W, (hd + 1) * QK_W)
            o_ref[rows, hl] = _subln_gate(y[:, hl], sg, sgc_ref[rows, hl], lam_init)


def _attention_ctx(q, k, v, sgc, lam_vec, subln_g, *, seq, lam_init):
    n_tok = q.shape[0]
    blk = pl.BlockSpec((CTX_ATTN_SEQS * seq, ATTN_W), lambda b: (b, 0))
    return pl.pallas_call(
        functools.partial(_attn_ctx_kernel, lam_init=lam_init, seq=seq),
        grid=(n_tok // (CTX_ATTN_SEQS * seq),),
        in_specs=[
            pl.BlockSpec((4, HEAD_DIM), lambda b: (0, 0)),
            pl.BlockSpec((1, QK_W), lambda b: (0, 0)),
            blk, blk, blk, blk,
        ],
        out_specs=blk,
        out_shape=jax.ShapeDtypeStruct((n_tok, ATTN_W), BF16),
        compiler_params=_cparams(("arbitrary",)),
        name="attention_ctx",
    )(lam_vec, subln_g.reshape(1, QK_W), q, k, v, sgc)


def _attn_den_kernel(lam_ref, sg_ref, q_ref, kn_ref, vn_ref, kc_ref, vc_ref, sgc_ref, o_ref,
                     kcs_ref, vt_ref, qq0_ref, qq1_ref, s0_ref, s1_ref, acc0_ref, acc1_ref,
                     *, lam_init):
    seq = q_ref.shape[0]
    past = kcs_ref.shape[1]
    past_chunks = past // KEY_CHUNK
    n_chunks = (past + seq) // KEY_CHUNK
    n_tiles = seq // Q_TILE
    n_elems = n_tiles * N_HEADS
    qq_refs = (qq0_ref, qq1_ref)
    s_refs = (s0_ref, s1_ref)
    acc_refs = (acc0_ref, acc1_ref)
    lam = _lambda(lam_ref, lam_init)
    sg = sg_ref[...]

    def lanes(hd):
        return slice(hd * QK_W, (hd + 1) * QK_W)

    def chunk(c):
        return slice(c * KEY_CHUNK, (c + 1) * KEY_CHUNK)

    def tile_rows(t):
        if isinstance(t, int):
            return pl.ds(t * Q_TILE, Q_TILE)
        return pl.ds(pl.multiple_of(t * Q_TILE, Q_TILE), Q_TILE)

    for hd in range(N_HEADS):
        kcs_ref[hd] = kc_ref[0, 0, :, hd, :].astype(BF16)
        vt_ref[hd, :, :past] = _with_ones_rows(vc_ref[0, 0, :, hd, :].T)
        vt_ref[hd, :, past:] = _with_ones_rows(vn_ref[:, lanes(hd)].astype(F32).T)

    def keys(hd, c):
        if c < past_chunks:
            return kcs_ref[hd, chunk(c), :]
        return kn_ref[chunk(c - past_chunks), lanes(hd)]

    def prep(t, hd, par):
        qq_refs[par][...] = _stack_masked(q_ref[tile_rows(t), lanes(hd)].astype(F32).T)

    def finish(t, hd, par):
        rows = tile_rows(t)
        y = _combine_t(acc_refs[par][...], lam).T
        o_ref[rows, lanes(hd)] = _subln_gate(y, sg, sgc_ref[rows, lanes(hd)], lam_init)

    def step(t, j, m_cur, first=False, last=False):
        def elem(off):
            return t + (j + off) // N_HEADS, (j + off) % N_HEADS

        par = j % 2
        e_static = N_HEADS * t + j if isinstance(t, int) else None
        do_finish = not (first and j == 0)
        do_scores = not (last and e_static + 1 >= n_elems)
        do_prep = not (last and e_static + 2 >= n_elems)
        if do_prep:
            prep(*elem(2), par)
        hd_nxt = elem(1)[1]
        qq = qq_refs[1 - par][...] if do_scores else None
        m_nxt = None
        acc = None
        for c in range(n_chunks):
            if do_scores:
                s, m_nxt = _scores_chunk(keys(hd_nxt, c), qq, m_nxt)
                s_refs[1 - par][chunk(c), :] = s
            acc = _values_chunk(s_refs[par][chunk(c), :], m_cur, vt_ref[j, :, chunk(c)], acc)
        acc_refs[par][...] = acc
        if do_finish:
            finish(*elem(-1), 1 - par)
        return m_nxt

    def tile_steps(t, m, **edge):
        for j in range(N_HEADS):
            m = step(t, j, m, **edge)
        return m

    prep(0, 0, 0)
    prep(0, 1, 1)
    qq = qq0_ref[...]
    m = None
    for c in range(n_chunks):
        s, m = _scores_chunk(keys(0, c), qq, m)
        s0_ref[chunk(c), :] = s
    m = tile_steps(0, m, first=True)
    m = lax.fori_loop(1, n_tiles - 1, tile_steps, m)
    tile_steps(n_tiles - 1, m, last=True)
    finish(n_tiles - 1, N_HEADS - 1, (n_elems - 1) % 2)


def _attention_den(q, k, v, sgc, cache_k, cache_v, lam_vec, subln_g, *, layer, seq, lam_init):
    n_tok = q.shape[0]
    past = cache_k.shape[2]
    blk = pl.BlockSpec((seq, ATTN_W), lambda b: (b, 0))
    cblk = pl.BlockSpec((1, 1, past, N_HEADS, QK_W), lambda b: (b, layer, 0, 0, 0))
    return pl.pallas_call(
        functools.partial(_attn_den_kernel, lam_init=lam_init),
        grid=(n_tok // seq,),
        in_specs=[
            pl.BlockSpec((4, HEAD_DIM), lambda b: (0, 0)),
            pl.BlockSpec((1, QK_W), lambda b: (0, 0)),
            blk, blk, blk, cblk, cblk, blk,
        ],
        out_specs=blk,
        out_shape=jax.ShapeDtypeStruct((n_tok, ATTN_W), BF16),
        scratch_shapes=[
            pltpu.VMEM((N_HEADS, past, QK_W), BF16),
            pltpu.VMEM((N_HEADS, QK_W + ONES_ROWS, past + seq), BF16),
            pltpu.VMEM((QK_W, 2 * Q_TILE), BF16),
            pltpu.VMEM((QK_W, 2 * Q_TILE), BF16),
            pltpu.VMEM((past + seq, 2 * Q_TILE), F32),
            pltpu.VMEM((past + seq, 2 * Q_TILE), F32),
            pltpu.VMEM((QK_W + ONES_ROWS, 2 * Q_TILE), F32),
            pltpu.VMEM((QK_W + ONES_ROWS, 2 * Q_TILE), F32),
        ],
        compiler_params=_cparams(("arbitrary",)),
        name="attention_den",
    )(lam_vec, subln_g.reshape(1, QK_W), q, k, v, cache_k, cache_v, sgc)


def _fourier_ctx_kernel(fa_ref, sga_ref, cs_ref, dn_ref, o_ref, *, seq):
    ab = jnp.dot(fa_ref[...], cs_ref[...], preferred_element_type=F32).astype(BF16)
    for s in range(fa_ref.shape[0] // seq):
        rows = slice(s * seq, (s + 1) * seq)
        ab2 = jnp.concatenate([ab[rows, :FOURIER_W], ab[rows, FOURIER_W:]], axis=0)
        f = jnp.dot(dn_ref[...], ab2, preferred_element_type=F32)
        o_ref[rows, :] = (f * sga_ref[rows, :].astype(F32)).astype(BF16)


def _fourier_ctx(fa, sga, cs, dn, *, seq):
    n_tok = fa.shape[0]
    blk = pl.BlockSpec((CTX_FOURIER_SEQS * seq, FOURIER_W), lambda b: (b, 0))
    return pl.pallas_call(
        functools.partial(_fourier_ctx_kernel, seq=seq),
        grid=(n_tok // (CTX_FOURIER_SEQS * seq),),
        in_specs=[blk, blk,
                  pl.BlockSpec((FOURIER_W, 2 * FOURIER_W), lambda b: (0, 0)),
                  pl.BlockSpec((seq, 2 * seq), lambda b: (0, 0))],
        out_specs=blk,
        out_shape=jax.ShapeDtypeStruct((n_tok, FOURIER_W), BF16),
        compiler_params=_cparams(("arbitrary",)),
        name="fourier_ctx",
    )(fa, sga, cs, dn)


def _cadd(a, b):
    return (a[0] + b[0], a[1] + b[1])


def _csub(a, b):
    return (a[0] - b[0], a[1] - b[1])


def _mul_neg_i(a):
    return (a[1], -a[0])


def _mul_w8_1(a):
    return ((a[0] + a[1]) * SQRT_HALF, (a[1] - a[0]) * SQRT_HALF)


def _mul_w8_3(a):
    return ((a[1] - a[0]) * SQRT_HALF, (-a[0] - a[1]) * SQRT_HALF)


def _fft4(a0, a1, a2, a3):
    e0, e1 = _cadd(a0, a2), _csub(a0, a2)
    o0, o1 = _cadd(a1, a3), _mul_neg_i(_csub(a1, a3))
    return [_cadd(e0, o0), _cadd(e1, o1), _csub(e0, o0), _csub(e1, o1)]


def _fft8(x):
    e = _fft4(x[0], x[2], x[4], x[6])
    o = _fft4(x[1], x[3], x[5], x[7])
    t = [o[0], _mul_w8_1(o[1]), _mul_neg_i(o[2]), _mul_w8_3(o[3])]
    return [_cadd(e[k], t[k]) for k in range(4)] + [_csub(e[k], t[k]) for k in range(4)]


def _fourier_den_kernel(xw_ref, sga_ref, f1_ref, twc_ref, tws_ref, cs2_ref, o_ref, g_ref, x_ref):
    n2 = xw_ref.shape[0]
    g_ref[...] = jnp.dot(f1_ref[...], xw_ref[...], preferred_element_type=F32)

    def chunk(i, carry):
        r = pl.multiple_of(i * FFT_ROWS, FFT_ROWS)
        re_rows = pl.ds(r, FFT_ROWS)
        im_rows = pl.ds(n2 + r, FFT_ROWS)
        xs = []
        for n1 in range(FFT_RADIX):
            lanes = slice(n1 * FOURIER_W, (n1 + 1) * FOURIER_W)
            gr = g_ref[re_rows, lanes]
            gi = g_ref[im_rows, lanes]
            if n1 > 0:
                c = twc_ref[re_rows, lanes]
                s = tws_ref[re_rows, lanes]
                gr, gi = gr * c + gi * s, gi * c - gr * s
            xs.append((gr, gi))
        for k1, (xr, xi) in enumerate(_fft8(xs)):
            out_rows = pl.ds(k1 * n2 + r, FFT_ROWS)
            x_ref[out_rows, :FOURIER_W] = xr.astype(BF16)
            x_ref[out_rows, FOURIER_W:] = xi.astype(BF16)
        return carry

    lax.fori_loop(0, n2 // FFT_ROWS, chunk, 0)
    f = jnp.dot(x_ref[...], cs2_ref[...], preferred_element_type=F32)
    o_ref[...] = (f * sga_ref[...].astype(F32)).astype(BF16)


def _fourier_den(xw, sga, f1, twc, tws, cs2, *, seq):
    n_tok = sga.shape[0]
    n2 = seq // FFT_RADIX
    wide = FFT_RADIX * FOURIER_W
    const = lambda shape: pl.BlockSpec(shape, lambda b: (0, 0))
    return pl.pallas_call(
        _fourier_den_kernel,
        grid=(n_tok // seq,),
        in_specs=[
            pl.BlockSpec((n2, wide), lambda b: (b, 0)),
            pl.BlockSpec((seq, FOURIER_W), lambda b: (b, 0)),
            const((2 * n2, n2)), const((n2, wide)), const((n2, wide)),
            const((2 * FOURIER_W, FOURIER_W)),
        ],
        out_specs=pl.BlockSpec((seq, FOURIER_W), lambda b: (b, 0)),
        out_shape=jax.ShapeDtypeStruct((n_tok, FOURIER_W), BF16),
        scratch_shapes=[pltpu.VMEM((2 * n2, wide), F32),
                        pltpu.VMEM((seq, 2 * FOURIER_W), BF16)],
        compiler_params=_cparams(("arbitrary",)),
        name="fourier_den",
    )(xw, sga, f1, twc, tws, cs2)


def _outproj_tile(x_ref, mod_ref, ya_ref, zc_ref, zp_ref, zn_ref, bgs_ref, yc_ref,
                  cw_ref, cb_ref, w_ref, fg_ref, o_ref, tile, *, seq, final_norm):
    tm = x_ref.shape[0]
    z = zc_ref[...].astype(F32)
    row = lax.broadcasted_iota(jnp.int32, z.shape, 0)
    pos = (tile * tm + row) & (seq - 1)
    prev_row = zp_ref[HALO_ROWS - 1:HALO_ROWS, :].astype(F32)
    next_row = zn_ref[0:1, :].astype(F32)
    z_prev = jnp.where(row == 0, prev_row, pltpu.roll(z, 1, axis=0))
    z_prev = jnp.where(pos == 0, 0.0, z_prev)
    z_next = jnp.where(row == tm - 1, next_row, pltpu.roll(z, tm - 1, axis=0))
    z_next = jnp.where(pos == seq - 1, 0.0, z_next)
    cw = cw_ref[...]
    conv = z_prev * cw[0:1] + z * cw[1:2] + z_next * cw[2:3] + cb_ref[...]
    yb = (bgs_ref[...].astype(F32) * conv).astype(BF16)

    mixed = jnp.concatenate([ya_ref[...], yb, yc_ref[...]], axis=-1)
    out = jnp.dot(mixed, w_ref[...], preferred_element_type=F32)
    gate = mod_ref[0, 0][:, 2 * D_MODEL:]
    xn = x_ref[...] + gate * out
    if final_norm:
        ms = jnp.mean(xn * xn, axis=-1, keepdims=True)
        xn = xn * lax.rsqrt(ms + EPS) * fg_ref[...]
    o_ref[...] = xn


N_GROUP_INS = 8


def _outproj_kernel(*refs, n_ctx_tiles, seq, dec_seq, final_norm):
    ctx_ins = refs[:N_GROUP_INS]
    den_ins = refs[N_GROUP_INS:2 * N_GROUP_INS]
    cw_ref, cb_ref, w_ref, fg_ref, oc_ref, od_ref, wb_ref = refs[2 * N_GROUP_INS:]
    i = pl.program_id(0)

    @pl.when(i == 0)
    def _():
        wb_ref[...] = w_ref[0].astype(BF16)

    @pl.when(i < n_ctx_tiles)
    def _():
        _outproj_tile(*ctx_ins, cw_ref, cb_ref, wb_ref, fg_ref, oc_ref, i, seq=seq,
                      final_norm=final_norm)

    @pl.when(i >= n_ctx_tiles)
    def _():
        _outproj_tile(*den_ins, cw_ref, cb_ref, wb_ref, fg_ref, od_ref, i - n_ctx_tiles,
                      seq=dec_seq, final_norm=final_norm)


def _out_projection(ctx, den, mod, conv_w, conv_b, w_out, layer, final_g, *, seq, dec_seq,
                    final_norm):
    tm = OUT_TILE
    nct = ctx[0].shape[0] // tm
    ndt = den[0].shape[0] // tm
    halo_per_tile = tm // HALO_ROWS
    seq_tiles = dec_seq // tm

    def group_specs(tile_of, n_tok, mod_of):
        n_halo = n_tok // HALO_ROWS
        row = lambda w: pl.BlockSpec((tm, w), lambda i: (tile_of(i), 0))
        return [
            row(D_MODEL),
            pl.BlockSpec((1, 1, 1, 3 * D_MODEL), lambda i: (layer, mod_of(tile_of(i)), 0, 0)),
            row(FOURIER_W),
            row(CONV_W),
            pl.BlockSpec((HALO_ROWS, CONV_W),
                         lambda i: (jnp.maximum(tile_of(i) * halo_per_tile - 1, 0), 0)),
            pl.BlockSpec((HALO_ROWS, CONV_W),
                         lambda i: (jnp.minimum((tile_of(i) + 1) * halo_per_tile, n_halo - 1), 0)),
            row(CONV_W),
            row(ATTN_W),
        ], row(D_MODEL)

    ctx_specs, ctx_out = group_specs(lambda i: jnp.minimum(i, nct - 1), ctx[0].shape[0],
                                     lambda t: 0)
    den_specs, den_out = group_specs(lambda i: jnp.maximum(i - nct, 0), den[0].shape[0],
                                     lambda t: 1 + t // seq_tiles)

    def group_args(g):
        x, ya, zc, bgs, yc = g
        return [x, mod, ya, zc, zc, zc, bgs, yc]

    return pl.pallas_call(
        functools.partial(_outproj_kernel, n_ctx_tiles=nct, seq=seq, dec_seq=dec_seq,
                          final_norm=final_norm),
        grid=(nct + ndt,),
        in_specs=ctx_specs + den_specs + [
            pl.BlockSpec((3, CONV_W), lambda i: (0, 0)),
            pl.BlockSpec((1, CONV_W), lambda i: (0, 0)),
            pl.BlockSpec((1, D_MODEL, D_MODEL), lambda i: (layer, 0, 0)),
            pl.BlockSpec((1, D_MODEL), lambda i: (0, 0)),
        ],
        out_specs=[ctx_out, den_out],
        out_shape=[jax.ShapeDtypeStruct(ctx[0].shape, F32), jax.ShapeDtypeStruct(den[0].shape, F32)],
        scratch_shapes=[pltpu.VMEM((D_MODEL, D_MODEL), BF16)],
        compiler_params=_cparams(("arbitrary",)),
        name="out_projection",
    )(*group_args(ctx), *group_args(den), conv_w, conv_b.reshape(1, CONV_W), w_out,
      final_g.reshape(1, D_MODEL))


def kernel(x_prompt, x_sample, cache_k, cache_v, c, c_ctx, norm_g, w_mod, b_mod, w_in, conv_w,
           conv_b, lam_vec, subln_g, w_out, final_g):
    batch, seq, _ = x_prompt.shape
    dec_batch, dec_seq, _ = x_sample.shape

    mod = _modulation(c_ctx, c, w_mod, b_mod)

    rope_tabs = tuple(jnp.asarray(t) for t in _rope_tables(dec_seq))
    cs = jnp.asarray(_chan_tables()).astype(BF16)
    dn_ctx = jnp.asarray(_dft_tables(seq)).astype(BF16)
    f1, twc, tws, cs2 = (jnp.asarray(t) for t in _ct_tables(dec_seq))
    f1 = f1.astype(BF16)
    cs2 = cs2.astype(BF16)

    xc = x_prompt.reshape(batch * seq, D_MODEL)
    xl = x_sample.reshape(dec_batch * dec_seq, D_MODEL)
    new_k = new_v = None
    for l in range(DEPTH):
        lam_init = 0.8 - 0.6 * math.exp(-0.3 * l)
        ctx, new_k, new_v, den = _in_projection(
            xc, xl, mod, norm_g[l], w_in, l, rope_tabs, seq=seq, dec_seq=dec_seq,
            new_k=new_k, new_v=new_v)

        fa, sga, zc_c, bgs_c, q, k, v, sgc = ctx
        yc_c = _attention_ctx(q, k, v, sgc, lam_vec[l], subln_g[l], seq=seq, lam_init=lam_init)
        ya_c = _fourier_ctx(fa, sga, cs, dn_ctx, seq=seq)

        fa, sga, zc_d, bgs_d, q, k, v, sgc = den
        yc_d = _attention_den(q, k, v, sgc, cache_k, cache_v, lam_vec[l], subln_g[l], layer=l,
                              seq=dec_seq, lam_init=lam_init)
        ya_d = _fourier_den(fa, sga, f1, twc, tws, cs2, seq=dec_seq)

        xc, xl = _out_projection(
            (xc, ya_c, zc_c, bgs_c, yc_c), (xl, ya_d, zc_d, bgs_d, yc_d), mod,
            conv_w[l], conv_b[l], w_out, l, final_g, seq=seq, dec_seq=dec_seq,
            final_norm=l == DEPTH - 1)

    y_prompt = xc.reshape(batch, seq, D_MODEL)
    y_sample = xl.reshape(dec_batch, dec_seq, D_MODEL)
    return (y_prompt, y_sample, new_k, new_v)
```

```python
import functools
import math

import numpy as np
import jax
import jax.numpy as jnp
from jax import lax
from jax.experimental import pallas as pl
from jax.experimental.pallas import tpu as pltpu

D_MODEL = 1024
DEPTH = 2
GRID_W = 64
FOURIER_W = 256
CONV_W = 256
ATTN_W = 512
N_HEADS = 4
HEAD_DIM = 64
QK_W = 128
ROPE_BASE = 10000.0
ROPE_W = HEAD_DIM // 2
ROPE_ROT = ROPE_W // 2
EPS = 1e-6
IN_DIM = 3584

F32 = jnp.float32
BF16 = jnp.bfloat16

VMEM_LIMIT_BYTES = 60 * 1024 * 1024
TOKEN_TILE = 512
OUT_TILE = 1024
Q_TILE = 256
KEY_CHUNK = 256
ONES_ROWS = 16
LANES = 128
FFT_RADIX = 8
FFT_ROWS = 16
SQRT_HALF = 0.7071067811865476
CTX_ATTN_SEQS = 2
CTX_FOURIER_SEQS = 4
HALO_ROWS = 16
MOD_ROWS = 8
MOD_CHUNK = 3 * D_MODEL
LOG2E = 1.4426950408889634

_COLS = {}
_off = 0
for _name, _w in (("fa", 256), ("ga", 256), ("bg", 256), ("cg", 256), ("hc", 256), ("gb", 256),
                  ("q", 512), ("k", 512), ("v", 512), ("gc", 512)):
    _COLS[_name] = (_off, _off + _w)
    _off += _w


def _silu(x):
    return x * (1.0 / (1.0 + jnp.exp(-x)))


def _cparams(sem):
    return pltpu.CompilerParams(dimension_semantics=sem, vmem_limit_bytes=VMEM_LIMIT_BYTES)


def _rope_tables(n_tokens):
    n = np.arange(n_tokens)
    row = (n // GRID_W).astype(np.float64)
    col = (n % GRID_W).astype(np.float64)
    j = np.arange(QK_W)
    jj = j % HEAD_DIM
    idx = jj % ROPE_W
    inv = 1.0 / (ROPE_BASE ** (2.0 * (idx % ROPE_ROT) / ROPE_W))
    pos = np.where((jj < ROPE_W)[None, :], row[:, None], col[:, None])
    ang = pos * inv[None, :]
    cos = np.cos(ang)
    sin = np.sin(ang)
    first = (idx < ROPE_ROT)[None, :]
    sin_a = np.where(first, -sin, 0.0)
    sin_b = np.where(first, 0.0, sin)
    return (np.asarray(cos, np.float32), np.asarray(sin_a, np.float32), np.asarray(sin_b, np.float32))


def _dft_tables(n):
    k = np.arange(n)
    kn = (k[:, None] * k[None, :]) % n
    ang = 2.0 * np.pi * kn / n
    return np.asarray(np.concatenate([np.cos(ang), -np.sin(ang)], axis=1) / math.sqrt(n), np.float32)


def _chan_tables():
    k = np.arange(FOURIER_W)
    kn = (k[:, None] * k[None, :]) % FOURIER_W
    ang = 2.0 * np.pi * kn / FOURIER_W
    return np.asarray(np.concatenate([np.cos(ang), np.sin(ang)], axis=1) / math.sqrt(FOURIER_W), np.float32)


def _ct_tables(n):
    n2 = n // FFT_RADIX
    k = np.arange(n2)
    ang = 2.0 * np.pi * ((k[:, None] * k[None, :]) % n2) / n2
    f1 = np.concatenate([np.cos(ang), -np.sin(ang)], axis=0) / math.sqrt(n * FOURIER_W)
    tw = 2.0 * np.pi * k[:, None] * np.arange(FFT_RADIX)[None, :] / n
    twc = np.repeat(np.cos(tw), FOURIER_W, axis=1)
    tws = np.repeat(np.sin(tw), FOURIER_W, axis=1)
    c = np.arange(FOURIER_W)
    angc = 2.0 * np.pi * ((c[:, None] * c[None, :]) % FOURIER_W) / FOURIER_W
    cs2 = np.concatenate([np.cos(angc), np.sin(angc)], axis=0)
    return tuple(np.asarray(t, np.float32) for t in (f1, twc, tws, cs2))


def _mod_kernel(cctx_ref, c_ref, w_ref, b_ref, o_ref, s_ref):
    n_c = c_ref.shape[0]
    s_ref[...] = jnp.zeros_like(s_ref)
    s_ref[0:1, :] = _silu(cctx_ref[...])
    s_ref[1:1 + n_c, :] = _silu(c_ref[...])
    w = w_ref[0].astype(BF16)
    res = jnp.dot(s_ref[...].astype(BF16), w, preferred_element_type=F32) + b_ref[0]
    for r in range(MOD_ROWS):
        o_ref[0, r] = res[r:r + 1]


def _modulation(c_ctx, c, w_mod, b_mod):
    chunk = MOD_CHUNK
    n_chunks = 3 * D_MODEL // chunk
    return pl.pallas_call(
        _mod_kernel,
        grid=(DEPTH, n_chunks),
        in_specs=[
            pl.BlockSpec((1, D_MODEL), lambda l, j: (0, 0)),
            pl.BlockSpec(c.shape, lambda l, j: (0, 0)),
            pl.BlockSpec((1, D_MODEL, chunk), lambda l, j: (l, 0, j)),
            pl.BlockSpec((1, 1, chunk), lambda l, j: (l, 0, j)),
        ],
        out_specs=pl.BlockSpec((1, MOD_ROWS, 1, chunk), lambda l, j: (l, 0, 0, j)),
        out_shape=jax.ShapeDtypeStruct((DEPTH, MOD_ROWS, 1, 3 * D_MODEL), F32),
        scratch_shapes=[pltpu.VMEM((MOD_ROWS, D_MODEL), F32)],
        compiler_params=_cparams(("arbitrary", "arbitrary")),
        name="modulation",
    )(c_ctx.reshape(1, D_MODEL), c, w_mod, b_mod.reshape(DEPTH, 1, 3 * D_MODEL))


def _store_heads(ref, layer, t):
    seqs, _, seq, _, _ = ref.shape
    for s in range(seqs):
        for hd in range(N_HEADS):
            ref[s, layer, :, hd, :] = t[s * seq:(s + 1) * seq, hd * QK_W:(hd + 1) * QK_W]


def _inproj_tile(x_ref, mod_ref, g_ref, w_ref, rope_refs, outs, kv_refs, stage_ref):
    fa_ref, sga_ref, zc_ref, bgs_ref, q_ref, k_ref, v_ref, sgc_ref = outs

    x = x_ref[...]
    ms = jnp.mean(x * x, axis=-1, keepdims=True)
    y = x * lax.rsqrt(ms + EPS) * g_ref[...]
    m = mod_ref[0, 0]
    h = (y * (1.0 + m[:, D_MODEL:2 * D_MODEL]) + m[:, :D_MODEL]).astype(BF16)

    def proj(name):
        lo, hi = _COLS[name]
        return jnp.dot(h, w_ref[:, lo:hi], preferred_element_type=F32)

    if stage_ref is not None:
        fa = proj("fa")
        halves = stage_ref.shape[0]
        for hf in range(halves):
            stage_ref[hf] = fa[:, hf * LANES:(hf + 1) * LANES]
        rows = stage_ref.shape[1] // FFT_RADIX
        for n1 in range(FFT_RADIX):
            for hf in range(halves):
                lo = n1 * FOURIER_W + hf * LANES
                fa_ref[:, lo:lo + LANES] = (
                    stage_ref[hf, pl.ds(n1, rows, stride=FFT_RADIX), :].astype(BF16))
    else:
        fa_ref[...] = proj("fa").astype(BF16)
    sga_ref[...] = _silu(proj("ga")).astype(BF16)
    zc_ref[...] = (proj("cg") * proj("hc")).astype(BF16)
    bgs_ref[...] = (proj("bg") * _silu(proj("gb"))).astype(BF16)
    sgc_ref[...] = _silu(proj("gc")).astype(BF16)

    q = proj("q")
    k = proj("k")
    v = proj("v")
    if kv_refs is not None and len(kv_refs) == 2:
        kv_refs[0][...] = k
        kv_refs[1][...] = v
    elif kv_refs is not None:
        new_k_ref, new_v_ref, prev_k_ref, prev_v_ref = kv_refs
        _store_heads(new_k_ref, 0, prev_k_ref[...])
        _store_heads(new_v_ref, 0, prev_v_ref[...])
        _store_heads(new_k_ref, 1, k)
        _store_heads(new_v_ref, 1, v)
    v_ref[...] = v.astype(BF16)

    q_scale = HEAD_DIM ** -0.5 * LOG2E
    if rope_refs is not None:
        cos = rope_refs[0][...]
        sa = rope_refs[1][...]
        sb = rope_refs[2][...]
        for hd in range(N_HEADS):
            sl = slice(hd * QK_W, (hd + 1) * QK_W)
            for t, ref, scale in ((q, q_ref, q_scale), (k, k_ref, None)):
                th = t[:, sl]
                r = (th * cos + pltpu.roll(th, QK_W - ROPE_ROT, axis=1) * sa
                     + pltpu.roll(th, ROPE_ROT, axis=1) * sb)
                if scale is not None:
                    r = r * scale
                ref[:, sl] = r.astype(BF16)
    else:
        q_ref[...] = (q * q_scale).astype(BF16)
        k_ref[...] = k.astype(BF16)


N_PROJ_OUTS = 8


def _inproj_kernel(xc_ref, xd_ref, modc_ref, modd_ref, g_ref, w_ref, cos_ref, sa_ref, sb_ref,
                   *rest, n_ctx_tiles, n_prev):
    prev_kv = rest[:n_prev]
    rest = rest[n_prev:]
    ctx_outs = rest[:N_PROJ_OUTS]
    kv_refs = rest[N_PROJ_OUTS:N_PROJ_OUTS + 2] + prev_kv
    den_outs = rest[N_PROJ_OUTS + 2:2 * N_PROJ_OUTS + 2]
    wb_ref, stage_ref = rest[2 * N_PROJ_OUTS + 2:]
    i = pl.program_id(0)

    @pl.when(i == 0)
    def _():
        for lo, hi in _COLS.values():
            wb_ref[:, lo:hi] = w_ref[0, :, lo:hi].astype(BF16)

    @pl.when(i < n_ctx_tiles)
    def _():
        _inproj_tile(xc_ref, modc_ref, g_ref, wb_ref, None, ctx_outs, kv_refs, None)

    @pl.when(i >= n_ctx_tiles)
    def _():
        _inproj_tile(xd_ref, modd_ref, g_ref, wb_ref, (cos_ref, sa_ref, sb_ref), den_outs, None,
                     stage_ref)


def _in_projection(xc, xd, mod, norm_g, w_in, layer, rope_tabs, *, seq, dec_seq,
                   prev_kv):
    assert DEPTH == 2, "the last layer's call assembles the k/v of exactly two layers"
    tm = TOKEN_TILE
    n_c, n_d = xc.shape[0], xd.shape[0]
    nct, ndt = n_c // tm, n_d // tm
    seq_tiles = dec_seq // tm
    ctx_i = lambda i: jnp.minimum(i, nct - 1)
    den_i = lambda i: jnp.maximum(i - nct, 0)
    row_c = lambda w: pl.BlockSpec((tm, w), lambda i: (ctx_i(i), 0))
    row_d = lambda w: pl.BlockSpec((tm, w), lambda i: (den_i(i), 0))
    tab = pl.BlockSpec((tm, QK_W), lambda i: (den_i(i) % seq_tiles, 0))
    in_specs = [
        row_c(D_MODEL), row_d(D_MODEL),
        pl.BlockSpec((1, 1, 1, 3 * D_MODEL), lambda i: (layer, 0, 0, 0)),
        pl.BlockSpec((1, 1, 1, 3 * D_MODEL), lambda i: (layer, 1 + den_i(i) // seq_tiles, 0, 0)),
        pl.BlockSpec((1, D_MODEL), lambda i: (0, 0)),
        pl.BlockSpec((1, D_MODEL, IN_DIM), lambda i: (layer, 0, 0)),
        tab, tab, tab,
    ]
    args = [xc, xd, mod, mod, norm_g.reshape(1, D_MODEL), w_in, *rope_tabs]
    widths = (256, 256, 256, 256, 512, 512, 512, 512)
    seqs = tm // seq
    if prev_kv is None:
        kv_blk = row_c(ATTN_W)
        kv_shape = jax.ShapeDtypeStruct((n_c, ATTN_W), F32)
    else:
        kv_blk = pl.BlockSpec((seqs, DEPTH, seq, N_HEADS, QK_W), lambda i: (ctx_i(i), 0, 0, 0, 0))
        kv_shape = jax.ShapeDtypeStruct((n_c // seq, DEPTH, seq, N_HEADS, QK_W), F32)
        in_specs += [row_c(ATTN_W)] * 2
        args += list(prev_kv)
    out_specs = [row_c(w) for w in widths] + [kv_blk, kv_blk] + [row_d(w) for w in widths]
    out_shape = ([jax.ShapeDtypeStruct((n_c, w), BF16) for w in widths] + [kv_shape, kv_shape]
                 + [jax.ShapeDtypeStruct((n_d, w), BF16) for w in widths])
    fa_d = N_PROJ_OUTS + 2
    out_specs[fa_d] = pl.BlockSpec((tm // FFT_RADIX, FFT_RADIX * FOURIER_W), lambda i: (den_i(i), 0))
    out_shape[fa_d] = jax.ShapeDtypeStruct((n_d // FFT_RADIX, FFT_RADIX * FOURIER_W), BF16)
    outs = pl.pallas_call(
        functools.partial(_inproj_kernel, n_ctx_tiles=nct, n_prev=0 if prev_kv is None else 2),
        grid=(nct + ndt,),
        in_specs=in_specs,
        out_specs=out_specs,
        out_shape=out_shape,
        scratch_shapes=[pltpu.VMEM((D_MODEL, IN_DIM), BF16),
                        pltpu.VMEM((FOURIER_W // LANES, tm, LANES), F32)],
        compiler_params=_cparams(("arbitrary",)),
        name="in_projection",
    )(*args)
    return outs[:N_PROJ_OUTS], outs[N_PROJ_OUTS], outs[N_PROJ_OUTS + 1], outs[fa_d:]


def _lambda(lam_ref, lam_init):
    lv = lam_ref[...]
    a = jnp.sum(lv[0:1] * lv[1:2], axis=-1, keepdims=True)
    b = jnp.sum(lv[2:3] * lv[3:4], axis=-1, keepdims=True)
    return jnp.exp(a) - jnp.exp(b) + lam_init


def _stack_masked(q_t):
    first = lax.broadcasted_iota(jnp.int32, q_t.shape, 0) < HEAD_DIM
    return jnp.concatenate([jnp.where(first, q_t, 0.0), jnp.where(first, 0.0, q_t)],
                           axis=1).astype(BF16)


def _scores_chunk(k, qq_t, m):
    s = jnp.dot(k, qq_t, preferred_element_type=F32)
    mc = jnp.max(s, axis=0, keepdims=True)
    return s, (mc if m is None else jnp.maximum(m, mc))


def _with_ones_rows(v_t):
    return jnp.concatenate([v_t, jnp.ones((ONES_ROWS, v_t.shape[1]), F32)], axis=0).astype(BF16)


def _values_chunk(s, m, v1_t, acc):
    e = jnp.exp2(s - m).astype(BF16)
    oc = jnp.dot(v1_t, e, preferred_element_type=F32)
    return oc if acc is None else acc + oc


def _combine_t(acc, lam):
    tq = acc.shape[1] // 2
    r = 1.0 / acc[QK_W:QK_W + 1, :]
    o_t = acc[:QK_W, :]
    return o_t[:, :tq] * r[:, :tq] - o_t[:, tq:] * (r[:, tq:] * lam)


def _subln_gate(o, sg, sgc, lam_init):
    ms = jnp.mean(o * o, axis=-1, keepdims=True)
    y = o * lax.rsqrt(ms + EPS) * sg * (1.0 - lam_init)
    return (y * sgc.astype(F32)).astype(BF16)


def _attn_ctx_kernel(lam_ref, sg_ref, q_ref, k_ref, v_ref, sgc_ref, o_ref, *, lam_init, seq):
    lam = _lambda(lam_ref, lam_init)
    sg = sg_ref[...]
    n_seq = q_ref.shape[0] // seq
    q_t = q_ref[...].astype(F32).T
    v_t = v_ref[...].astype(F32).T
    pairs = [(slice(s * seq, (s + 1) * seq), slice(hd * QK_W, (hd + 1) * QK_W))
             for s in range(n_seq) for hd in range(N_HEADS)]
    qqs = [_stack_masked(q_t[hl, rows]) for rows, hl in pairs]
    v1s = [_with_ones_rows(v_t[hl, rows]) for rows, hl in pairs]
    sm = [_scores_chunk(k_ref[rows, hl], qq, None) for (rows, hl), qq in zip(pairs, qqs)]
    accs = [_values_chunk(s, m, v1, None) for (s, m), v1 in zip(sm, v1s)]
    y_t = [_combine_t(a, lam) for a in accs]
    for s in range(n_seq):
        rows = slice(s * seq, (s + 1) * seq)
        y = jnp.concatenate(y_t[s * N_HEADS:(s + 1) * N_HEADS], axis=0).T
        for hd in range(N_HEADS):
            hl = slice(hd * QK_W, (hd + 1) * QK_W)
            o_ref[rows, hl] = _subln_gate(y[:, hl], sg, sgc_ref[rows, hl], lam_init)


def _attention_ctx(q, k, v, sgc, lam_vec, subln_g, *, seq, lam_init):
    n_tok = q.shape[0]
    blk = pl.BlockSpec((CTX_ATTN_SEQS * seq, ATTN_W), lambda b: (b, 0))
    return pl.pallas_call(
        functools.partial(_attn_ctx_kernel, lam_init=lam_init, seq=seq),
        grid=(n_tok // (CTX_ATTN_SEQS * seq),),
        in_specs=[
            pl.BlockSpec((4, HEAD_DIM), lambda b: (0, 0)),
            pl.BlockSpec((1, QK_W), lambda b: (0, 0)),
            blk, blk, blk, blk,
        ],
        out_specs=blk,
        out_shape=jax.ShapeDtypeStruct((n_tok, ATTN_W), BF16),
        compiler_params=_cparams(("arbitrary",)),
        name="attention_ctx",
    )(lam_vec, subln_g.reshape(1, QK_W), q, k, v, sgc)


def _attn_den_kernel(lam_ref, sg_ref, q_ref, kn_ref, vn_ref, kc_ref, vc_ref, sgc_ref, o_ref,
                     kcs_ref, vt_ref, qq0_ref, qq1_ref, s0_ref, s1_ref, acc0_ref, acc1_ref,
                     *, lam_init):
    seq = q_ref.shape[0]
    past = kcs_ref.shape[1]
    past_chunks = past // KEY_CHUNK
    n_chunks = (past + seq) // KEY_CHUNK
    n_tiles = seq // Q_TILE
    n_elems = n_tiles * N_HEADS
    qq_refs = (qq0_ref, qq1_ref)
    s_refs = (s0_ref, s1_ref)
    acc_refs = (acc0_ref, acc1_ref)
    lam = _lambda(lam_ref, lam_init)
    sg = sg_ref[...]

    def lanes(hd):
        return slice(hd * QK_W, (hd + 1) * QK_W)

    def chunk(c):
        return slice(c * KEY_CHUNK, (c + 1) * KEY_CHUNK)

    def tile_rows(t):
        if isinstance(t, int):
            return pl.ds(t * Q_TILE, Q_TILE)
        return pl.ds(pl.multiple_of(t * Q_TILE, Q_TILE), Q_TILE)

    for hd in range(N_HEADS):
        kcs_ref[hd] = kc_ref[0, 0, :, hd, :].astype(BF16)
        vt_ref[hd, :, :past] = _with_ones_rows(vc_ref[0, 0, :, hd, :].T)
        vt_ref[hd, :, past:] = _with_ones_rows(vn_ref[:, lanes(hd)].astype(F32).T)

    def keys(hd, c):
        if c < past_chunks:
            return kcs_ref[hd, chunk(c), :]
        return kn_ref[chunk(c - past_chunks), lanes(hd)]

    def prep(t, hd, par):
        qq_refs[par][...] = _stack_masked(q_ref[tile_rows(t), lanes(hd)].astype(F32).T)

    def finish(t, hd, par):
        rows = tile_rows(t)
        y = _combine_t(acc_refs[par][...], lam).T
        o_ref[rows, lanes(hd)] = _subln_gate(y, sg, sgc_ref[rows, lanes(hd)], lam_init)

    def step(t, j, m_cur, first=False, last=False):
        def elem(off):
            return t + (j + off) // N_HEADS, (j + off) % N_HEADS

        par = j % 2
        e_static = N_HEADS * t + j if isinstance(t, int) else None
        do_finish = not (first and j == 0)
        do_scores = not (last and e_static + 1 >= n_elems)
        do_prep = not (last and e_static + 2 >= n_elems)
        if do_prep:
            prep(*elem(2), par)
        hd_nxt = elem(1)[1]
        qq = qq_refs[1 - par][...] if do_scores else None
        m_nxt = None
        acc = None
        for c in range(n_chunks):
            if do_scores:
                s, m_nxt = _scores_chunk(keys(hd_nxt, c), qq, m_nxt)
                s_refs[1 - par][chunk(c), :] = s
            acc = _values_chunk(s_refs[par][chunk(c), :], m_cur, vt_ref[j, :, chunk(c)], acc)
        acc_refs[par][...] = acc
        if do_finish:
            finish(*elem(-1), 1 - par)
        return m_nxt

    def tile_steps(t, m, **edge):
        for j in range(N_HEADS):
            m = step(t, j, m, **edge)
        return m

    prep(0, 0, 0)
    prep(0, 1, 1)
    qq = qq0_ref[...]
    m = None
    for c in range(n_chunks):
        s, m = _scores_chunk(keys(0, c), qq, m)
        s0_ref[chunk(c), :] = s
    m = tile_steps(0, m, first=True)
    m = lax.fori_loop(1, n_tiles - 1, tile_steps, m)
    tile_steps(n_tiles - 1, m, last=True)
    finish(n_tiles - 1, N_HEADS - 1, (n_elems - 1) % 2)


def _attention_den(q, k, v, sgc, cache_k, cache_v, lam_vec, subln_g, *, layer, seq, lam_init):
    n_tok = q.shape[0]
    past = cache_k.shape[2]
    blk = pl.BlockSpec((seq, ATTN_W), lambda b: (b, 0))
    cblk = pl.BlockSpec((1, 1, past, N_HEADS, QK_W), lambda b: (b, layer, 0, 0, 0))
    return pl.pallas_call(
        functools.partial(_attn_den_kernel, lam_init=lam_init),
        grid=(n_tok // seq,),
        in_specs=[
            pl.BlockSpec((4, HEAD_DIM), lambda b: (0, 0)),
            pl.BlockSpec((1, QK_W), lambda b: (0, 0)),
            blk, blk, blk, cblk, cblk, blk,
        ],
        out_specs=blk,
        out_shape=jax.ShapeDtypeStruct((n_tok, ATTN_W), BF16),
        scratch_shapes=[
            pltpu.VMEM((N_HEADS, past, QK_W), BF16),
            pltpu.VMEM((N_HEADS, QK_W + ONES_ROWS, past + seq), BF16),
            pltpu.VMEM((QK_W, 2 * Q_TILE), BF16),
            pltpu.VMEM((QK_W, 2 * Q_TILE), BF16),
            pltpu.VMEM((past + seq, 2 * Q_TILE), F32),
            pltpu.VMEM((past + seq, 2 * Q_TILE), F32),
            pltpu.VMEM((QK_W + ONES_ROWS, 2 * Q_TILE), F32),
            pltpu.VMEM((QK_W + ONES_ROWS, 2 * Q_TILE), F32),
        ],
        compiler_params=_cparams(("arbitrary",)),
        name="attention_den",
    )(lam_vec, subln_g.reshape(1, QK_W), q, k, v, cache_k, cache_v, sgc)


def _fourier_ctx_kernel(fa_ref, sga_ref, cs_ref, dn_ref, o_ref, *, seq):
    ab = jnp.dot(fa_ref[...], cs_ref[...], preferred_element_type=F32).astype(BF16)
    for s in range(fa_ref.shape[0] // seq):
        rows = slice(s * seq, (s + 1) * seq)
        ab2 = jnp.concatenate([ab[rows, :FOURIER_W], ab[rows, FOURIER_W:]], axis=0)
        f = jnp.dot(dn_ref[...], ab2, preferred_element_type=F32)
        o_ref[rows, :] = (f * sga_ref[rows, :].astype(F32)).astype(BF16)


def _fourier_ctx(fa, sga, cs, dn, *, seq):
    n_tok = fa.shape[0]
    blk = pl.BlockSpec((CTX_FOURIER_SEQS * seq, FOURIER_W), lambda b: (b, 0))
    return pl.pallas_call(
        functools.partial(_fourier_ctx_kernel, seq=seq),
        grid=(n_tok // (CTX_FOURIER_SEQS * seq),),
        in_specs=[blk, blk,
                  pl.BlockSpec((FOURIER_W, 2 * FOURIER_W), lambda b: (0, 0)),
                  pl.BlockSpec((seq, 2 * seq), lambda b: (0, 0))],
        out_specs=blk,
        out_shape=jax.ShapeDtypeStruct((n_tok, FOURIER_W), BF16),
        compiler_params=_cparams(("arbitrary",)),
        name="fourier_ctx",
    )(fa, sga, cs, dn)


def _cadd(a, b):
    return (a[0] + b[0], a[1] + b[1])


def _csub(a, b):
    return (a[0] - b[0], a[1] - b[1])


def _mul_neg_i(a):
    return (a[1], -a[0])


def _mul_w8_1(a):
    return ((a[0] + a[1]) * SQRT_HALF, (a[1] - a[0]) * SQRT_HALF)


def _mul_w8_3(a):
    return ((a[1] - a[0]) * SQRT_HALF, (-a[0] - a[1]) * SQRT_HALF)


def _fft4(a0, a1, a2, a3):
    e0, e1 = _cadd(a0, a2), _csub(a0, a2)
    o0, o1 = _cadd(a1, a3), _mul_neg_i(_csub(a1, a3))
    return [_cadd(e0, o0), _cadd(e1, o1), _csub(e0, o0), _csub(e1, o1)]


def _fft8(x):
    e = _fft4(x[0], x[2], x[4], x[6])
    o = _fft4(x[1], x[3], x[5], x[7])
    t = [o[0], _mul_w8_1(o[1]), _mul_neg_i(o[2]), _mul_w8_3(o[3])]
    return [_cadd(e[k], t[k]) for k in range(4)] + [_csub(e[k], t[k]) for k in range(4)]


def _fourier_den_kernel(xw_ref, sga_ref, f1_ref, twc_ref, tws_ref, cs2_ref, o_ref, g_ref, x_ref):
    n2 = xw_ref.shape[0]
    g_ref[...] = jnp.dot(f1_ref[...], xw_ref[...], preferred_element_type=F32)

    def chunk(i, carry):
        r = pl.multiple_of(i * FFT_ROWS, FFT_ROWS)
        re_rows = pl.ds(r, FFT_ROWS)
        im_rows = pl.ds(n2 + r, FFT_ROWS)
        xs = []
        for n1 in range(FFT_RADIX):
            lanes = slice(n1 * FOURIER_W, (n1 + 1) * FOURIER_W)
            gr = g_ref[re_rows, lanes]
            gi = g_ref[im_rows, lanes]
            if n1 > 0:
                c = twc_ref[re_rows, lanes]
                s = tws_ref[re_rows, lanes]
                gr, gi = gr * c + gi * s, gi * c - gr * s
            xs.append((gr, gi))
        for k1, (xr, xi) in enumerate(_fft8(xs)):
            out_rows = pl.ds(k1 * n2 + r, FFT_ROWS)
            x_ref[out_rows, :FOURIER_W] = xr.astype(BF16)
            x_ref[out_rows, FOURIER_W:] = xi.astype(BF16)
        return carry

    lax.fori_loop(0, n2 // FFT_ROWS, chunk, 0)
    f = jnp.dot(x_ref[...], cs2_ref[...], preferred_element_type=F32)
    o_ref[...] = (f * sga_ref[...].astype(F32)).astype(BF16)


def _fourier_den(xw, sga, f1, twc, tws, cs2, *, seq):
    n_tok = sga.shape[0]
    n2 = seq // FFT_RADIX
    wide = FFT_RADIX * FOURIER_W
    const = lambda shape: pl.BlockSpec(shape, lambda b: (0, 0))
    return pl.pallas_call(
        _fourier_den_kernel,
        grid=(n_tok // seq,),
        in_specs=[
            pl.BlockSpec((n2, wide), lambda b: (b, 0)),
            pl.BlockSpec((seq, FOURIER_W), lambda b: (b, 0)),
            const((2 * n2, n2)), const((n2, wide)), const((n2, wide)),
            const((2 * FOURIER_W, FOURIER_W)),
        ],
        out_specs=pl.BlockSpec((seq, FOURIER_W), lambda b: (b, 0)),
        out_shape=jax.ShapeDtypeStruct((n_tok, FOURIER_W), BF16),
        scratch_shapes=[pltpu.VMEM((2 * n2, wide), F32),
                        pltpu.VMEM((seq, 2 * FOURIER_W), BF16)],
        compiler_params=_cparams(("arbitrary",)),
        name="fourier_den",
    )(xw, sga, f1, twc, tws, cs2)


def _outproj_tile(x_ref, mod_ref, ya_ref, zc_ref, zp_ref, zn_ref, bgs_ref, yc_ref,
                  cw_ref, cb_ref, w_ref, fg_ref, o_ref, tile, *, seq, final_norm):
    tm = x_ref.shape[0]
    z = zc_ref[...].astype(F32)
    row = lax.broadcasted_iota(jnp.int32, z.shape, 0)
    pos = (tile * tm + row) & (seq - 1)
    prev_row = zp_ref[HALO_ROWS - 1:HALO_ROWS, :].astype(F32)
    next_row = zn_ref[0:1, :].astype(F32)
    z_prev = jnp.where(row == 0, prev_row, pltpu.roll(z, 1, axis=0))
    z_prev = jnp.where(pos == 0, 0.0, z_prev)
    z_next = jnp.where(row == tm - 1, next_row, pltpu.roll(z, tm - 1, axis=0))
    z_next = jnp.where(pos == seq - 1, 0.0, z_next)
    cw = cw_ref[...]
    conv = z_prev * cw[0:1] + z * cw[1:2] + z_next * cw[2:3] + cb_ref[...]
    yb = (bgs_ref[...].astype(F32) * conv).astype(BF16)

    mixed = jnp.concatenate([ya_ref[...], yb, yc_ref[...]], axis=-1)
    out = jnp.dot(mixed, w_ref[...], preferred_element_type=F32)
    gate = mod_ref[0, 0][:, 2 * D_MODEL:]
    xn = x_ref[...] + gate * out
    if final_norm:
        ms = jnp.mean(xn * xn, axis=-1, keepdims=True)
        xn = xn * lax.rsqrt(ms + EPS) * fg_ref[...]
    o_ref[...] = xn


N_GROUP_INS = 8


def _outproj_kernel(*refs, n_ctx_tiles, seq, dec_seq, final_norm):
    ctx_ins = refs[:N_GROUP_INS]
    den_ins = refs[N_GROUP_INS:2 * N_GROUP_INS]
    cw_ref, cb_ref, w_ref, fg_ref, oc_ref, od_ref, wb_ref = refs[2 * N_GROUP_INS:]
    i = pl.program_id(0)

    @pl.when(i == 0)
    def _():
        wb_ref[...] = w_ref[0].astype(BF16)

    @pl.when(i < n_ctx_tiles)
    def _():
        _outproj_tile(*ctx_ins, cw_ref, cb_ref, wb_ref, fg_ref, oc_ref, i, seq=seq,
                      final_norm=final_norm)

    @pl.when(i >= n_ctx_tiles)
    def _():
        _outproj_tile(*den_ins, cw_ref, cb_ref, wb_ref, fg_ref, od_ref, i - n_ctx_tiles,
                      seq=dec_seq, final_norm=final_norm)


def _out_projection(ctx, den, mod, conv_w, conv_b, w_out, layer, final_g, *, seq, dec_seq,
                    final_norm):
    tm = OUT_TILE
    nct = ctx[0].shape[0] // tm
    ndt = den[0].shape[0] // tm
    halo_per_tile = tm // HALO_ROWS
    seq_tiles = dec_seq // tm

    def group_specs(tile_of, n_tok, mod_of):
        n_halo = n_tok // HALO_ROWS
        row = lambda w: pl.BlockSpec((tm, w), lambda i: (tile_of(i), 0))
        return [
            row(D_MODEL),
            pl.BlockSpec((1, 1, 1, 3 * D_MODEL), lambda i: (layer, mod_of(tile_of(i)), 0, 0)),
            row(FOURIER_W),
            row(CONV_W),
            pl.BlockSpec((HALO_ROWS, CONV_W),
                         lambda i: (jnp.maximum(tile_of(i) * halo_per_tile - 1, 0), 0)),
            pl.BlockSpec((HALO_ROWS, CONV_W),
                         lambda i: (jnp.minimum((tile_of(i) + 1) * halo_per_tile, n_halo - 1), 0)),
            row(CONV_W),
            row(ATTN_W),
        ], row(D_MODEL)

    ctx_specs, ctx_out = group_specs(lambda i: jnp.minimum(i, nct - 1), ctx[0].shape[0],
                                     lambda t: 0)
    den_specs, den_out = group_specs(lambda i: jnp.maximum(i - nct, 0), den[0].shape[0],
                                     lambda t: 1 + t // seq_tiles)

    def group_args(g):
        x, ya, zc, bgs, yc = g
        return [x, mod, ya, zc, zc, zc, bgs, yc]

    return pl.pallas_call(
        functools.partial(_outproj_kernel, n_ctx_tiles=nct, seq=seq, dec_seq=dec_seq,
                          final_norm=final_norm),
        grid=(nct + ndt,),
        in_specs=ctx_specs + den_specs + [
            pl.BlockSpec((3, CONV_W), lambda i: (0, 0)),
            pl.BlockSpec((1, CONV_W), lambda i: (0, 0)),
            pl.BlockSpec((1, D_MODEL, D_MODEL), lambda i: (layer, 0, 0)),
            pl.BlockSpec((1, D_MODEL), lambda i: (0, 0)),
        ],
        out_specs=[ctx_out, den_out],
        out_shape=[jax.ShapeDtypeStruct(ctx[0].shape, F32), jax.ShapeDtypeStruct(den[0].shape, F32)],
        scratch_shapes=[pltpu.VMEM((D_MODEL, D_MODEL), BF16)],
        compiler_params=_cparams(("arbitrary",)),
        name="out_projection",
    )(*group_args(ctx), *group_args(den), conv_w, conv_b.reshape(1, CONV_W), w_out,
      final_g.reshape(1, D_MODEL))


def kernel(x_prompt, x_sample, cache_k, cache_v, c, c_ctx, norm_g, w_mod, b_mod, w_in, conv_w,
           conv_b, lam_vec, subln_g, w_out, final_g):
    batch, seq, _ = x_prompt.shape
    dec_batch, dec_seq, _ = x_sample.shape

    mod = _modulation(c_ctx, c, w_mod, b_mod)

    rope_tabs = tuple(jnp.asarray(t) for t in _rope_tables(dec_seq))
    cs = jnp.asarray(_chan_tables()).astype(BF16)
    dn_ctx = jnp.asarray(_dft_tables(seq)).astype(BF16)
    f1, twc, tws, cs2 = (jnp.asarray(t) for t in _ct_tables(dec_seq))
    f1 = f1.astype(BF16)
    cs2 = cs2.astype(BF16)

    xc = x_prompt.reshape(batch * seq, D_MODEL)
    xl = x_sample.reshape(dec_batch * dec_seq, D_MODEL)
    kv = None
    for l in range(DEPTH):
        lam_init = 0.8 - 0.6 * math.exp(-0.3 * l)
        ctx, k32, v32, den = _in_projection(
            xc, xl, mod, norm_g[l], w_in, l, rope_tabs, seq=seq, dec_seq=dec_seq,
            prev_kv=kv)
        kv = (k32, v32)

        fa, sga, zc_c, bgs_c, q, k, v, sgc = ctx
        yc_c = _attention_ctx(q, k, v, sgc, lam_vec[l], subln_g[l], seq=seq, lam_init=lam_init)
        ya_c = _fourier_ctx(fa, sga, cs, dn_ctx, seq=seq)

        fa, sga, zc_d, bgs_d, q, k, v, sgc = den
        yc_d = _attention_den(q, k, v, sgc, cache_k, cache_v, lam_vec[l], subln_g[l], layer=l,
                              seq=dec_seq, lam_init=lam_init)
        ya_d = _fourier_den(fa, sga, f1, twc, tws, cs2, seq=dec_seq)

        xc, xl = _out_projection(
            (xc, ya_c, zc_c, bgs_c, yc_c), (xl, ya_d, zc_d, bgs_d, yc_d), mod,
            conv_w[l], conv_b[l], w_out, l, final_g, seq=seq, dec_seq=dec_seq,
            final_norm=l == DEPTH - 1)

    y_prompt = xc.reshape(batch, seq, D_MODEL)
    y_sample = xl.reshape(dec_batch, dec_seq, D_MODEL)
    return (y_prompt, y_sample, *kv)
```

```python
import functools
import math

import numpy as np
import jax
import jax.numpy as jnp
from jax import lax
from jax.experimental import pallas as pl
from jax.experimental.pallas import tpu as pltpu

D_MODEL = 1024
DEPTH = 2
GRID_W = 64
FOURIER_W = 256
CONV_W = 256
ATTN_W = 512
N_HEADS = 4
HEAD_DIM = 64
QK_W = 128
ROPE_BASE = 10000.0
ROPE_W = HEAD_DIM // 2
ROPE_ROT = ROPE_W // 2
EPS = 1e-6
IN_DIM = 3584

F32 = jnp.float32
BF16 = jnp.bfloat16

VMEM_LIMIT_BYTES = 60 * 1024 * 1024
TOKEN_TILE = 512
OUT_TILE = 1024
Q_TILE = 256
KEY_CHUNK = 256
ONES_ROWS = 16
LANES = 128
FFT_RADIX = 8
FFT_ROWS = 16
SQRT_HALF = 0.7071067811865476
CTX_ATTN_SEQS = 2
CTX_FOURIER_SEQS = 4
HALO_ROWS = 16
MOD_ROWS = 8
MOD_CHUNK = 3 * D_MODEL
LOG2E = 1.4426950408889634

_COLS = {}
_off = 0
for _name, _w in (("fa", 256), ("ga", 256), ("bg", 256), ("cg", 256), ("hc", 256), ("gb", 256),
                  ("q", 512), ("k", 512), ("v", 512), ("gc", 512)):
    _COLS[_name] = (_off, _off + _w)
    _off += _w


def _silu(x):
    return x * (1.0 / (1.0 + jnp.exp(-x)))


def _cparams(sem):
    return pltpu.CompilerParams(dimension_semantics=sem, vmem_limit_bytes=VMEM_LIMIT_BYTES)


def _rope_tables(n_tokens):
    n = np.arange(n_tokens)
    row = (n // GRID_W).astype(np.float64)
    col = (n % GRID_W).astype(np.float64)
    j = np.arange(QK_W)
    jj = j % HEAD_DIM
    idx = jj % ROPE_W
    inv = 1.0 / (ROPE_BASE ** (2.0 * (idx % ROPE_ROT) / ROPE_W))
    pos = np.where((jj < ROPE_W)[None, :], row[:, None], col[:, None])
    ang = pos * inv[None, :]
    cos = np.cos(ang)
    sin = np.sin(ang)
    first = (idx < ROPE_ROT)[None, :]
    sin_a = np.where(first, -sin, 0.0)
    sin_b = np.where(first, 0.0, sin)
    return (np.asarray(cos, np.float32), np.asarray(sin_a, np.float32), np.asarray(sin_b, np.float32))


def _dft_tables(n):
    k = np.arange(n)
    kn = (k[:, None] * k[None, :]) % n
    ang = 2.0 * np.pi * kn / n
    return np.asarray(np.concatenate([np.cos(ang), -np.sin(ang)], axis=1) / math.sqrt(n), np.float32)


def _chan_tables():
    k = np.arange(FOURIER_W)
    kn = (k[:, None] * k[None, :]) % FOURIER_W
    ang = 2.0 * np.pi * kn / FOURIER_W
    return np.asarray(np.concatenate([np.cos(ang), np.sin(ang)], axis=1) / math.sqrt(FOURIER_W), np.float32)


def _ct_tables(n):
    n2 = n // FFT_RADIX
    k = np.arange(n2)
    ang = 2.0 * np.pi * ((k[:, None] * k[None, :]) % n2) / n2
    f1 = np.concatenate([np.cos(ang), -np.sin(ang)], axis=0) / math.sqrt(n * FOURIER_W)
    tw = 2.0 * np.pi * k[:, None] * np.arange(FFT_RADIX)[None, :] / n
    twc = np.repeat(np.cos(tw), FOURIER_W, axis=1)
    tws = np.repeat(np.sin(tw), FOURIER_W, axis=1)
    c = np.arange(FOURIER_W)
    angc = 2.0 * np.pi * ((c[:, None] * c[None, :]) % FOURIER_W) / FOURIER_W
    cs2 = np.concatenate([np.cos(angc), np.sin(angc)], axis=0)
    return tuple(np.asarray(t, np.float32) for t in (f1, twc, tws, cs2))


def _mod_kernel(cctx_ref, c_ref, w_ref, b_ref, o_ref, s_ref):
    n_c = c_ref.shape[0]
    s_ref[...] = jnp.zeros_like(s_ref)
    s_ref[0:1, :] = _silu(cctx_ref[...])
    s_ref[1:1 + n_c, :] = _silu(c_ref[...])
    w = w_ref[0].astype(BF16)
    res = jnp.dot(s_ref[...].astype(BF16), w, preferred_element_type=F32) + b_ref[0]
    for r in range(MOD_ROWS):
        o_ref[0, r] = res[r:r + 1]


def _modulation(c_ctx, c, w_mod, b_mod):
    chunk = MOD_CHUNK
    n_chunks = 3 * D_MODEL // chunk
    return pl.pallas_call(
        _mod_kernel,
        grid=(DEPTH, n_chunks),
        in_specs=[
            pl.BlockSpec((1, D_MODEL), lambda l, j: (0, 0)),
            pl.BlockSpec(c.shape, lambda l, j: (0, 0)),
            pl.BlockSpec((1, D_MODEL, chunk), lambda l, j: (l, 0, j)),
            pl.BlockSpec((1, 1, chunk), lambda l, j: (l, 0, j)),
        ],
        out_specs=pl.BlockSpec((1, MOD_ROWS, 1, chunk), lambda l, j: (l, 0, 0, j)),
        out_shape=jax.ShapeDtypeStruct((DEPTH, MOD_ROWS, 1, 3 * D_MODEL), F32),
        scratch_shapes=[pltpu.VMEM((MOD_ROWS, D_MODEL), F32)],
        compiler_params=_cparams(("arbitrary", "arbitrary")),
        name="modulation",
    )(c_ctx.reshape(1, D_MODEL), c, w_mod, b_mod.reshape(DEPTH, 1, 3 * D_MODEL))


def _store_heads(ref, layer, t):
    seqs, _, rows, _ = ref.shape
    seq = rows // N_HEADS
    for s in range(seqs):
        for hd in range(N_HEADS):
            ref[s, layer, pl.ds(hd, seq, stride=N_HEADS), :] = (
                t[s * seq:(s + 1) * seq, hd * QK_W:(hd + 1) * QK_W])


def _inproj_tile(x_ref, mod_ref, g_ref, w_ref, rope_refs, outs, kv_refs, stage_ref):
    fa_ref, sga_ref, zc_ref, bgs_ref, q_ref, k_ref, v_ref, sgc_ref = outs

    x = x_ref[...]
    ms = jnp.mean(x * x, axis=-1, keepdims=True)
    y = x * lax.rsqrt(ms + EPS) * g_ref[...]
    m = mod_ref[0, 0]
    h = (y * (1.0 + m[:, D_MODEL:2 * D_MODEL]) + m[:, :D_MODEL]).astype(BF16)

    def proj(name):
        lo, hi = _COLS[name]
        return jnp.dot(h, w_ref[:, lo:hi], preferred_element_type=F32)

    if stage_ref is not None:
        fa = proj("fa")
        halves = stage_ref.shape[0]
        for hf in range(halves):
            stage_ref[hf] = fa[:, hf * LANES:(hf + 1) * LANES]
        rows = stage_ref.shape[1] // FFT_RADIX
        for n1 in range(FFT_RADIX):
            for hf in range(halves):
                lo = n1 * FOURIER_W + hf * LANES
                fa_ref[:, lo:lo + LANES] = (
                    stage_ref[hf, pl.ds(n1, rows, stride=FFT_RADIX), :].astype(BF16))
    else:
        fa_ref[...] = proj("fa").astype(BF16)
    sga_ref[...] = _silu(proj("ga")).astype(BF16)
    zc_ref[...] = (proj("cg") * proj("hc")).astype(BF16)
    bgs_ref[...] = (proj("bg") * _silu(proj("gb"))).astype(BF16)
    sgc_ref[...] = _silu(proj("gc")).astype(BF16)

    q = proj("q")
    k = proj("k")
    v = proj("v")
    if kv_refs is not None and len(kv_refs) == 2:
        kv_refs[0][...] = k
        kv_refs[1][...] = v
    elif kv_refs is not None:
        new_k_ref, new_v_ref, prev_k_ref, prev_v_ref = kv_refs
        _store_heads(new_k_ref, 0, prev_k_ref[...])
        _store_heads(new_v_ref, 0, prev_v_ref[...])
        _store_heads(new_k_ref, 1, k)
        _store_heads(new_v_ref, 1, v)
    v_ref[...] = v.astype(BF16)

    q_scale = HEAD_DIM ** -0.5 * LOG2E
    if rope_refs is not None:
        cos = rope_refs[0][...]
        sa = rope_refs[1][...]
        sb = rope_refs[2][...]
        for hd in range(N_HEADS):
            sl = slice(hd * QK_W, (hd + 1) * QK_W)
            for t, ref, scale in ((q, q_ref, q_scale), (k, k_ref, None)):
                th = t[:, sl]
                r = (th * cos + pltpu.roll(th, QK_W - ROPE_ROT, axis=1) * sa
                     + pltpu.roll(th, ROPE_ROT, axis=1) * sb)
                if scale is not None:
                    r = r * scale
                ref[:, sl] = r.astype(BF16)
    else:
        q_ref[...] = (q * q_scale).astype(BF16)
        k_ref[...] = k.astype(BF16)


N_PROJ_OUTS = 8


def _inproj_kernel(xc_ref, xd_ref, modc_ref, modd_ref, g_ref, w_ref, cos_ref, sa_ref, sb_ref,
                   *rest, n_ctx_tiles, n_prev):
    prev_kv = rest[:n_prev]
    rest = rest[n_prev:]
    ctx_outs = rest[:N_PROJ_OUTS]
    kv_refs = rest[N_PROJ_OUTS:N_PROJ_OUTS + 2] + prev_kv
    den_outs = rest[N_PROJ_OUTS + 2:2 * N_PROJ_OUTS + 2]
    wb_ref, stage_ref = rest[2 * N_PROJ_OUTS + 2:]
    i = pl.program_id(0)

    @pl.when(i == 0)
    def _():
        for lo, hi in _COLS.values():
            wb_ref[:, lo:hi] = w_ref[0, :, lo:hi].astype(BF16)

    @pl.when(i < n_ctx_tiles)
    def _():
        _inproj_tile(xc_ref, modc_ref, g_ref, wb_ref, None, ctx_outs, kv_refs, None)

    @pl.when(i >= n_ctx_tiles)
    def _():
        _inproj_tile(xd_ref, modd_ref, g_ref, wb_ref, (cos_ref, sa_ref, sb_ref), den_outs, None,
                     stage_ref)


def _in_projection(xc, xd, mod, norm_g, w_in, layer, rope_tabs, *, seq, dec_seq,
                   prev_kv):
    assert DEPTH == 2, "the last layer's call assembles the k/v of exactly two layers"
    tm = TOKEN_TILE
    n_c, n_d = xc.shape[0], xd.shape[0]
    nct, ndt = n_c // tm, n_d // tm
    seq_tiles = dec_seq // tm
    ctx_i = lambda i: jnp.minimum(i, nct - 1)
    den_i = lambda i: jnp.maximum(i - nct, 0)
    row_c = lambda w: pl.BlockSpec((tm, w), lambda i: (ctx_i(i), 0))
    row_d = lambda w: pl.BlockSpec((tm, w), lambda i: (den_i(i), 0))
    tab = pl.BlockSpec((tm, QK_W), lambda i: (den_i(i) % seq_tiles, 0))
    in_specs = [
        row_c(D_MODEL), row_d(D_MODEL),
        pl.BlockSpec((1, 1, 1, 3 * D_MODEL), lambda i: (layer, 0, 0, 0)),
        pl.BlockSpec((1, 1, 1, 3 * D_MODEL), lambda i: (layer, 1 + den_i(i) // seq_tiles, 0, 0)),
        pl.BlockSpec((1, D_MODEL), lambda i: (0, 0)),
        pl.BlockSpec((1, D_MODEL, IN_DIM), lambda i: (layer, 0, 0)),
        tab, tab, tab,
    ]
    args = [xc, xd, mod, mod, norm_g.reshape(1, D_MODEL), w_in, *rope_tabs]
    widths = (256, 256, 256, 256, 512, 512, 512, 512)
    seqs = tm // seq
    if prev_kv is None:
        kv_blk = row_c(ATTN_W)
        kv_shape = jax.ShapeDtypeStruct((n_c, ATTN_W), F32)
    else:
        kv_blk = pl.BlockSpec((seqs, DEPTH, seq * N_HEADS, QK_W), lambda i: (ctx_i(i), 0, 0, 0))
        kv_shape = jax.ShapeDtypeStruct((n_c // seq, DEPTH, seq * N_HEADS, QK_W), F32)
        in_specs += [row_c(ATTN_W)] * 2
        args += list(prev_kv)
    out_specs = [row_c(w) for w in widths] + [kv_blk, kv_blk] + [row_d(w) for w in widths]
    out_shape = ([jax.ShapeDtypeStruct((n_c, w), BF16) for w in widths] + [kv_shape, kv_shape]
                 + [jax.ShapeDtypeStruct((n_d, w), BF16) for w in widths])
    fa_d = N_PROJ_OUTS + 2
    out_specs[fa_d] = pl.BlockSpec((tm // FFT_RADIX, FFT_RADIX * FOURIER_W), lambda i: (den_i(i), 0))
    out_shape[fa_d] = jax.ShapeDtypeStruct((n_d // FFT_RADIX, FFT_RADIX * FOURIER_W), BF16)
    outs = pl.pallas_call(
        functools.partial(_inproj_kernel, n_ctx_tiles=nct, n_prev=0 if prev_kv is None else 2),
        grid=(nct + ndt,),
        in_specs=in_specs,
        out_specs=out_specs,
        out_shape=out_shape,
        scratch_shapes=[pltpu.VMEM((D_MODEL, IN_DIM), BF16),
                        pltpu.VMEM((FOURIER_W // LANES, tm, LANES), F32)],
        compiler_params=_cparams(("arbitrary",)),
        name="in_projection",
    )(*args)
    return outs[:N_PROJ_OUTS], outs[N_PROJ_OUTS], outs[N_PROJ_OUTS + 1], outs[fa_d:]


def _lambda(lam_ref, lam_init):
    lv = lam_ref[...]
    a = jnp.sum(lv[0:1] * lv[1:2], axis=-1, keepdims=True)
    b = jnp.sum(lv[2:3] * lv[3:4], axis=-1, keepdims=True)
    return jnp.exp(a) - jnp.exp(b) + lam_init


def _stack_masked(q_t):
    first = lax.broadcasted_iota(jnp.int32, q_t.shape, 0) < HEAD_DIM
    return jnp.concatenate([jnp.where(first, q_t, 0.0), jnp.where(first, 0.0, q_t)],
                           axis=1).astype(BF16)


def _scores_chunk(k, qq_t, m):
    s = jnp.dot(k, qq_t, preferred_element_type=F32)
    mc = jnp.max(s, axis=0, keepdims=True)
    return s, (mc if m is None else jnp.maximum(m, mc))


def _with_ones_rows(v_t):
    return jnp.concatenate([v_t, jnp.ones((ONES_ROWS, v_t.shape[1]), F32)], axis=0).astype(BF16)


def _values_chunk(s, m, v1_t, acc):
    e = jnp.exp2(s - m).astype(BF16)
    oc = jnp.dot(v1_t, e, preferred_element_type=F32)
    return oc if acc is None else acc + oc


def _combine_t(acc, lam):
    tq = acc.shape[1] // 2
    r = 1.0 / acc[QK_W:QK_W + 1, :]
    o_t = acc[:QK_W, :]
    return o_t[:, :tq] * r[:, :tq] - o_t[:, tq:] * (r[:, tq:] * lam)


def _subln_gate(o, sg, sgc, lam_init):
    ms = jnp.mean(o * o, axis=-1, keepdims=True)
    y = o * lax.rsqrt(ms + EPS) * sg * (1.0 - lam_init)
    return (y * sgc.astype(F32)).astype(BF16)


def _attn_ctx_kernel(lam_ref, sg_ref, q_ref, k_ref, v_ref, sgc_ref, o_ref, *, lam_init, seq):
    lam = _lambda(lam_ref, lam_init)
    sg = sg_ref[...]
    n_seq = q_ref.shape[0] // seq
    q_t = q_ref[...].astype(F32).T
    v_t = v_ref[...].astype(F32).T
    pairs = [(slice(s * seq, (s + 1) * seq), slice(hd * QK_W, (hd + 1) * QK_W))
             for s in range(n_seq) for hd in range(N_HEADS)]
    qqs = [_stack_masked(q_t[hl, rows]) for rows, hl in pairs]
    v1s = [_with_ones_rows(v_t[hl, rows]) for rows, hl in pairs]
    sm = [_scores_chunk(k_ref[rows, hl], qq, None) for (rows, hl), qq in zip(pairs, qqs)]
    accs = [_values_chunk(s, m, v1, None) for (s, m), v1 in zip(sm, v1s)]
    y_t = [_combine_t(a, lam) for a in accs]
    for s in range(n_seq):
        rows = slice(s * seq, (s + 1) * seq)
        y = jnp.concatenate(y_t[s * N_HEADS:(s + 1) * N_HEADS], axis=0).T
        for hd in range(N_HEADS):
            hl = slice(hd * QK_W, (hd + 1) * QK_W)
            o_ref[rows, hl] = _subln_gate(y[:, hl], sg, sgc_ref[rows, hl], lam_init)


def _attention_ctx(q, k, v, sgc, lam_vec, subln_g, *, seq, lam_init):
    n_tok = q.shape[0]
    blk = pl.BlockSpec((CTX_ATTN_SEQS * seq, ATTN_W), lambda b: (b, 0))
    return pl.pallas_call(
        functools.partial(_attn_ctx_kernel, lam_init=lam_init, seq=seq),
        grid=(n_tok // (CTX_ATTN_SEQS * seq),),
        in_specs=[
            pl.BlockSpec((4, HEAD_DIM), lambda b: (0, 0)),
            pl.BlockSpec((1, QK_W), lambda b: (0, 0)),
            blk, blk, blk, blk,
        ],
        out_specs=blk,
        out_shape=jax.ShapeDtypeStruct((n_tok, ATTN_W), BF16),
        compiler_params=_cparams(("arbitrary",)),
        name="attention_ctx",
    )(lam_vec, subln_g.reshape(1, QK_W), q, k, v, sgc)


def _attn_den_kernel(lam_ref, sg_ref, q_ref, kn_ref, vn_ref, kc_ref, vc_ref, sgc_ref, o_ref,
                     kcs_ref, vt_ref, qq0_ref, qq1_ref, s0_ref, s1_ref, acc0_ref, acc1_ref,
                     *, lam_init):
    seq = q_ref.shape[0]
    past = kcs_ref.shape[1]
    past_chunks = past // KEY_CHUNK
    n_chunks = (past + seq) // KEY_CHUNK
    n_tiles = seq // Q_TILE
    n_elems = n_tiles * N_HEADS
    qq_refs = (qq0_ref, qq1_ref)
    s_refs = (s0_ref, s1_ref)
    acc_refs = (acc0_ref, acc1_ref)
    lam = _lambda(lam_ref, lam_init)
    sg = sg_ref[...]

    def lanes(hd):
        return slice(hd * QK_W, (hd + 1) * QK_W)

    def chunk(c):
        return slice(c * KEY_CHUNK, (c + 1) * KEY_CHUNK)

    def tile_rows(t):
        if isinstance(t, int):
            return pl.ds(t * Q_TILE, Q_TILE)
        return pl.ds(pl.multiple_of(t * Q_TILE, Q_TILE), Q_TILE)

    for hd in range(N_HEADS):
        kcs_ref[hd] = kc_ref[0, 0, :, hd, :].astype(BF16)
        vt_ref[hd, :, :past] = _with_ones_rows(vc_ref[0, 0, :, hd, :].T)
        vt_ref[hd, :, past:] = _with_ones_rows(vn_ref[:, lanes(hd)].astype(F32).T)

    def keys(hd, c):
        if c < past_chunks:
            return kcs_ref[hd, chunk(c), :]
        return kn_ref[chunk(c - past_chunks), lanes(hd)]

    def prep(t, hd, par):
        qq_refs[par][...] = _stack_masked(q_ref[tile_rows(t), lanes(hd)].astype(F32).T)

    def finish(t, hd, par):
        rows = tile_rows(t)
        y = _combine_t(acc_refs[par][...], lam).T
        o_ref[rows, lanes(hd)] = _subln_gate(y, sg, sgc_ref[rows, lanes(hd)], lam_init)

    def step(t, j, m_cur, first=False, last=False):
        def elem(off):
            return t + (j + off) // N_HEADS, (j + off) % N_HEADS

        par = j % 2
        e_static = N_HEADS * t + j if isinstance(t, int) else None
        do_finish = not (first and j == 0)
        do_scores = not (last and e_static + 1 >= n_elems)
        do_prep = not (last and e_static + 2 >= n_elems)
        if do_prep:
            prep(*elem(2), par)
        hd_nxt = elem(1)[1]
        qq = qq_refs[1 - par][...] if do_scores else None
        m_nxt = None
        acc = None
        for c in range(n_chunks):
            if do_scores:
                s, m_nxt = _scores_chunk(keys(hd_nxt, c), qq, m_nxt)
                s_refs[1 - par][chunk(c), :] = s
            acc = _values_chunk(s_refs[par][chunk(c), :], m_cur, vt_ref[j, :, chunk(c)], acc)
        acc_refs[par][...] = acc
        if do_finish:
            finish(*elem(-1), 1 - par)
        return m_nxt

    def tile_steps(t, m, **edge):
        for j in range(N_HEADS):
            m = step(t, j, m, **edge)
        return m

    prep(0, 0, 0)
    prep(0, 1, 1)
    qq = qq0_ref[...]
    m = None
    for c in range(n_chunks):
        s, m = _scores_chunk(keys(0, c), qq, m)
        s0_ref[chunk(c), :] = s
    m = tile_steps(0, m, first=True)
    m = lax.fori_loop(1, n_tiles - 1, tile_steps, m)
    tile_steps(n_tiles - 1, m, last=True)
    finish(n_tiles - 1, N_HEADS - 1, (n_elems - 1) % 2)


def _attention_den(q, k, v, sgc, cache_k, cache_v, lam_vec, subln_g, *, layer, seq, lam_init):
    n_tok = q.shape[0]
    past = cache_k.shape[2]
    blk = pl.BlockSpec((seq, ATTN_W), lambda b: (b, 0))
    cblk = pl.BlockSpec((1, 1, past, N_HEADS, QK_W), lambda b: (b, layer, 0, 0, 0))
    return pl.pallas_call(
        functools.partial(_attn_den_kernel, lam_init=lam_init),
        grid=(n_tok // seq,),
        in_specs=[
            pl.BlockSpec((4, HEAD_DIM), lambda b: (0, 0)),
            pl.BlockSpec((1, QK_W), lambda b: (0, 0)),
            blk, blk, blk, cblk, cblk, blk,
        ],
        out_specs=blk,
        out_shape=jax.ShapeDtypeStruct((n_tok, ATTN_W), BF16),
        scratch_shapes=[
            pltpu.VMEM((N_HEADS, past, QK_W), BF16),
            pltpu.VMEM((N_HEADS, QK_W + ONES_ROWS, past + seq), BF16),
            pltpu.VMEM((QK_W, 2 * Q_TILE), BF16),
            pltpu.VMEM((QK_W, 2 * Q_TILE), BF16),
            pltpu.VMEM((past + seq, 2 * Q_TILE), F32),
            pltpu.VMEM((past + seq, 2 * Q_TILE), F32),
            pltpu.VMEM((QK_W + ONES_ROWS, 2 * Q_TILE), F32),
            pltpu.VMEM((QK_W + ONES_ROWS, 2 * Q_TILE), F32),
        ],
        compiler_params=_cparams(("arbitrary",)),
        name="attention_den",
    )(lam_vec, subln_g.reshape(1, QK_W), q, k, v, cache_k, cache_v, sgc)


def _fourier_ctx_kernel(fa_ref, sga_ref, cs_ref, dn_ref, o_ref, *, seq):
    ab = jnp.dot(fa_ref[...], cs_ref[...], preferred_element_type=F32).astype(BF16)
    for s in range(fa_ref.shape[0] // seq):
        rows = slice(s * seq, (s + 1) * seq)
        ab2 = jnp.concatenate([ab[rows, :FOURIER_W], ab[rows, FOURIER_W:]], axis=0)
        f = jnp.dot(dn_ref[...], ab2, preferred_element_type=F32)
        o_ref[rows, :] = (f * sga_ref[rows, :].astype(F32)).astype(BF16)


def _fourier_ctx(fa, sga, cs, dn, *, seq):
    n_tok = fa.shape[0]
    blk = pl.BlockSpec((CTX_FOURIER_SEQS * seq, FOURIER_W), lambda b: (b, 0))
    return pl.pallas_call(
        functools.partial(_fourier_ctx_kernel, seq=seq),
        grid=(n_tok // (CTX_FOURIER_SEQS * seq),),
        in_specs=[blk, blk,
                  pl.BlockSpec((FOURIER_W, 2 * FOURIER_W), lambda b: (0, 0)),
                  pl.BlockSpec((seq, 2 * seq), lambda b: (0, 0))],
        out_specs=blk,
        out_shape=jax.ShapeDtypeStruct((n_tok, FOURIER_W), BF16),
        compiler_params=_cparams(("arbitrary",)),
        name="fourier_ctx",
    )(fa, sga, cs, dn)


def _cadd(a, b):
    return (a[0] + b[0], a[1] + b[1])


def _csub(a, b):
    return (a[0] - b[0], a[1] - b[1])


def _mul_neg_i(a):
    return (a[1], -a[0])


def _mul_w8_1(a):
    return ((a[0] + a[1]) * SQRT_HALF, (a[1] - a[0]) * SQRT_HALF)


def _mul_w8_3(a):
    return ((a[1] - a[0]) * SQRT_HALF, (-a[0] - a[1]) * SQRT_HALF)


def _fft4(a0, a1, a2, a3):
    e0, e1 = _cadd(a0, a2), _csub(a0, a2)
    o0, o1 = _cadd(a1, a3), _mul_neg_i(_csub(a1, a3))
    return [_cadd(e0, o0), _cadd(e1, o1), _csub(e0, o0), _csub(e1, o1)]


def _fft8(x):
    e = _fft4(x[0], x[2], x[4], x[6])
    o = _fft4(x[1], x[3], x[5], x[7])
    t = [o[0], _mul_w8_1(o[1]), _mul_neg_i(o[2]), _mul_w8_3(o[3])]
    return [_cadd(e[k], t[k]) for k in range(4)] + [_csub(e[k], t[k]) for k in range(4)]


def _fourier_den_kernel(xw_ref, sga_ref, f1_ref, twc_ref, tws_ref, cs2_ref, o_ref, g_ref, x_ref):
    n2 = xw_ref.shape[0]
    g_ref[...] = jnp.dot(f1_ref[...], xw_ref[...], preferred_element_type=F32)

    def chunk(i, carry):
        r = pl.multiple_of(i * FFT_ROWS, FFT_ROWS)
        re_rows = pl.ds(r, FFT_ROWS)
        im_rows = pl.ds(n2 + r, FFT_ROWS)
        xs = []
        for n1 in range(FFT_RADIX):
            lanes = slice(n1 * FOURIER_W, (n1 + 1) * FOURIER_W)
            gr = g_ref[re_rows, lanes]
            gi = g_ref[im_rows, lanes]
            if n1 > 0:
                c = twc_ref[re_rows, lanes]
                s = tws_ref[re_rows, lanes]
                gr, gi = gr * c + gi * s, gi * c - gr * s
            xs.append((gr, gi))
        for k1, (xr, xi) in enumerate(_fft8(xs)):
            out_rows = pl.ds(k1 * n2 + r, FFT_ROWS)
            x_ref[out_rows, :FOURIER_W] = xr.astype(BF16)
            x_ref[out_rows, FOURIER_W:] = xi.astype(BF16)
        return carry

    lax.fori_loop(0, n2 // FFT_ROWS, chunk, 0)
    f = jnp.dot(x_ref[...], cs2_ref[...], preferred_element_type=F32)
    o_ref[...] = (f * sga_ref[...].astype(F32)).astype(BF16)


def _fourier_den(xw, sga, f1, twc, tws, cs2, *, seq):
    n_tok = sga.shape[0]
    n2 = seq // FFT_RADIX
    wide = FFT_RADIX * FOURIER_W
    const = lambda shape: pl.BlockSpec(shape, lambda b: (0, 0))
    return pl.pallas_call(
        _fourier_den_kernel,
        grid=(n_tok // seq,),
        in_specs=[
            pl.BlockSpec((n2, wide), lambda b: (b, 0)),
            pl.BlockSpec((seq, FOURIER_W), lambda b: (b, 0)),
            const((2 * n2, n2)), const((n2, wide)), const((n2, wide)),
            const((2 * FOURIER_W, FOURIER_W)),
        ],
        out_specs=pl.BlockSpec((seq, FOURIER_W), lambda b: (b, 0)),
        out_shape=jax.ShapeDtypeStruct((n_tok, FOURIER_W), BF16),
        scratch_shapes=[pltpu.VMEM((2 * n2, wide), F32),
                        pltpu.VMEM((seq, 2 * FOURIER_W), BF16)],
        compiler_params=_cparams(("arbitrary",)),
        name="fourier_den",
    )(xw, sga, f1, twc, tws, cs2)


def _outproj_tile(x_ref, mod_ref, ya_ref, zc_ref, zp_ref, zn_ref, bgs_ref, yc_ref,
                  cw_ref, cb_ref, w_ref, fg_ref, o_ref, tile, *, seq, final_norm):
    tm = x_ref.shape[0]
    z = zc_ref[...].astype(F32)
    row = lax.broadcasted_iota(jnp.int32, z.shape, 0)
    pos = (tile * tm + row) & (seq - 1)
    prev_row = zp_ref[HALO_ROWS - 1:HALO_ROWS, :].astype(F32)
    next_row = zn_ref[0:1, :].astype(F32)
    z_prev = jnp.where(row == 0, prev_row, pltpu.roll(z, 1, axis=0))
    z_prev = jnp.where(pos == 0, 0.0, z_prev)
    z_next = jnp.where(row == tm - 1, next_row, pltpu.roll(z, tm - 1, axis=0))
    z_next = jnp.where(pos == seq - 1, 0.0, z_next)
    cw = cw_ref[...]
    conv = z_prev * cw[0:1] + z * cw[1:2] + z_next * cw[2:3] + cb_ref[...]
    yb = (bgs_ref[...].astype(F32) * conv).astype(BF16)

    mixed = jnp.concatenate([ya_ref[...], yb, yc_ref[...]], axis=-1)
    out = jnp.dot(mixed, w_ref[...], preferred_element_type=F32)
    gate = mod_ref[0, 0][:, 2 * D_MODEL:]
    xn = x_ref[...] + gate * out
    if final_norm:
        ms = jnp.mean(xn * xn, axis=-1, keepdims=True)
        xn = xn * lax.rsqrt(ms + EPS) * fg_ref[...]
    o_ref[...] = xn


N_GROUP_INS = 8


def _outproj_kernel(*refs, n_ctx_tiles, seq, dec_seq, final_norm):
    ctx_ins = refs[:N_GROUP_INS]
    den_ins = refs[N_GROUP_INS:2 * N_GROUP_INS]
    cw_ref, cb_ref, w_ref, fg_ref, oc_ref, od_ref, wb_ref = refs[2 * N_GROUP_INS:]
    i = pl.program_id(0)

    @pl.when(i == 0)
    def _():
        wb_ref[...] = w_ref[0].astype(BF16)

    @pl.when(i < n_ctx_tiles)
    def _():
        _outproj_tile(*ctx_ins, cw_ref, cb_ref, wb_ref, fg_ref, oc_ref, i, seq=seq,
                      final_norm=final_norm)

    @pl.when(i >= n_ctx_tiles)
    def _():
        _outproj_tile(*den_ins, cw_ref, cb_ref, wb_ref, fg_ref, od_ref, i - n_ctx_tiles,
                      seq=dec_seq, final_norm=final_norm)


def _out_projection(ctx, den, mod, conv_w, conv_b, w_out, layer, final_g, *, seq, dec_seq,
                    final_norm):
    tm = OUT_TILE
    nct = ctx[0].shape[0] // tm
    ndt = den[0].shape[0] // tm
    halo_per_tile = tm // HALO_ROWS
    seq_tiles = dec_seq // tm

    def group_specs(tile_of, n_tok, mod_of):
        n_halo = n_tok // HALO_ROWS
        row = lambda w: pl.BlockSpec((tm, w), lambda i: (tile_of(i), 0))
        return [
            row(D_MODEL),
            pl.BlockSpec((1, 1, 1, 3 * D_MODEL), lambda i: (layer, mod_of(tile_of(i)), 0, 0)),
            row(FOURIER_W),
            row(CONV_W),
            pl.BlockSpec((HALO_ROWS, CONV_W),
                         lambda i: (jnp.maximum(tile_of(i) * halo_per_tile - 1, 0), 0)),
            pl.BlockSpec((HALO_ROWS, CONV_W),
                         lambda i: (jnp.minimum((tile_of(i) + 1) * halo_per_tile, n_halo - 1), 0)),
            row(CONV_W),
            row(ATTN_W),
        ], row(D_MODEL)

    ctx_specs, ctx_out = group_specs(lambda i: jnp.minimum(i, nct - 1), ctx[0].shape[0],
                                     lambda t: 0)
    den_specs, den_out = group_specs(lambda i: jnp.maximum(i - nct, 0), den[0].shape[0],
                                     lambda t: 1 + t // seq_tiles)

    def group_args(g):
        x, ya, zc, bgs, yc = g
        return [x, mod, ya, zc, zc, zc, bgs, yc]

    return pl.pallas_call(
        functools.partial(_outproj_kernel, n_ctx_tiles=nct, seq=seq, dec_seq=dec_seq,
                          final_norm=final_norm),
        grid=(nct + ndt,),
        in_specs=ctx_specs + den_specs + [
            pl.BlockSpec((3, CONV_W), lambda i: (0, 0)),
            pl.BlockSpec((1, CONV_W), lambda i: (0, 0)),
            pl.BlockSpec((1, D_MODEL, D_MODEL), lambda i: (layer, 0, 0)),
            pl.BlockSpec((1, D_MODEL), lambda i: (0, 0)),
        ],
        out_specs=[ctx_out, den_out],
        out_shape=[jax.ShapeDtypeStruct(ctx[0].shape, F32), jax.ShapeDtypeStruct(den[0].shape, F32)],
        scratch_shapes=[pltpu.VMEM((D_MODEL, D_MODEL), BF16)],
        compiler_params=_cparams(("arbitrary",)),
        name="out_projection",
    )(*group_args(ctx), *group_args(den), conv_w, conv_b.reshape(1, CONV_W), w_out,
      final_g.reshape(1, D_MODEL))


def kernel(x_prompt, x_sample, cache_k, cache_v, c, c_ctx, norm_g, w_mod, b_mod, w_in, conv_w,
           conv_b, lam_vec, subln_g, w_out, final_g):
    batch, seq, _ = x_prompt.shape
    dec_batch, dec_seq, _ = x_sample.shape

    mod = _modulation(c_ctx, c, w_mod, b_mod)

    rope_tabs = tuple(jnp.asarray(t) for t in _rope_tables(dec_seq))
    cs = jnp.asarray(_chan_tables()).astype(BF16)
    dn_ctx = jnp.asarray(_dft_tables(seq)).astype(BF16)
    f1, twc, tws, cs2 = (jnp.asarray(t) for t in _ct_tables(dec_seq))
    f1 = f1.astype(BF16)
    cs2 = cs2.astype(BF16)

    xc = x_prompt.reshape(batch * seq, D_MODEL)
    xl = x_sample.reshape(dec_batch * dec_seq, D_MODEL)
    kv = None
    for l in range(DEPTH):
        lam_init = 0.8 - 0.6 * math.exp(-0.3 * l)
        ctx, k32, v32, den = _in_projection(
            xc, xl, mod, norm_g[l], w_in, l, rope_tabs, seq=seq, dec_seq=dec_seq,
            prev_kv=kv)
        kv = (k32, v32)

        fa, sga, zc_c, bgs_c, q, k, v, sgc = ctx
        yc_c = _attention_ctx(q, k, v, sgc, lam_vec[l], subln_g[l], seq=seq, lam_init=lam_init)
        ya_c = _fourier_ctx(fa, sga, cs, dn_ctx, seq=seq)

        fa, sga, zc_d, bgs_d, q, k, v, sgc = den
        yc_d = _attention_den(q, k, v, sgc, cache_k, cache_v, lam_vec[l], subln_g[l], layer=l,
                              seq=dec_seq, lam_init=lam_init)
        ya_d = _fourier_den(fa, sga, f1, twc, tws, cs2, seq=dec_seq)

        xc, xl = _out_projection(
            (xc, ya_c, zc_c, bgs_c, yc_c), (xl, ya_d, zc_d, bgs_d, yc_d), mod,
            conv_w[l], conv_b[l], w_out, l, final_g, seq=seq, dec_seq=dec_seq,
            final_norm=l == DEPTH - 1)

    y_prompt = xc.reshape(batch, seq, D_MODEL)
    y_sample = xl.reshape(dec_batch, dec_seq, D_MODEL)
    return (y_prompt, y_sample, *(t.reshape(batch, DEPTH, seq, N_HEADS, QK_W) for t in kv))
```

```python
import functools
import math

import numpy as np
import jax
import jax.numpy as jnp
from jax import lax
from jax.experimental import pallas as pl
from jax.experimental.pallas import tpu as pltpu

D_MODEL = 1024
DEPTH = 2
GRID_W = 64
FOURIER_W = 256
CONV_W = 256
ATTN_W = 512
N_HEADS = 4
HEAD_DIM = 64
QK_W = 128
ROPE_BASE = 10000.0
ROPE_W = HEAD_DIM // 2
ROPE_ROT = ROPE_W // 2
EPS = 1e-6
IN_DIM = 3584

F32 = jnp.float32
BF16 = jnp.bfloat16

VMEM_LIMIT_BYTES = 60 * 1024 * 1024
TOKEN_TILE = 512
OUT_TILE = 1024
Q_TILE = 256
KEY_CHUNK = 256
ONES_ROWS = 16
LANES = 128
FFT_RADIX = 8
FFT_ROWS = 16
SQRT_HALF = 0.7071067811865476
CTX_ATTN_SEQS = 2
CTX_FOURIER_SEQS = 4
HALO_ROWS = 16
MOD_ROWS = 8
MOD_CHUNK = 3 * D_MODEL
LOG2E = 1.4426950408889634

_COLS = {}
_off = 0
for _name, _w in (("fa", 256), ("ga", 256), ("bg", 256), ("cg", 256), ("hc", 256), ("gb", 256),
                  ("q", 512), ("k", 512), ("v", 512), ("gc", 512)):
    _COLS[_name] = (_off, _off + _w)
    _off += _w


def _silu(x):
    return x * (1.0 / (1.0 + jnp.exp(-x)))


def _cparams(sem):
    return pltpu.CompilerParams(dimension_semantics=sem, vmem_limit_bytes=VMEM_LIMIT_BYTES)


def _rope_tables(n_tokens):
    n = np.arange(n_tokens)
    row = (n // GRID_W).astype(np.float64)
    col = (n % GRID_W).astype(np.float64)
    j = np.arange(QK_W)
    jj = j % HEAD_DIM
    idx = jj % ROPE_W
    inv = 1.0 / (ROPE_BASE ** (2.0 * (idx % ROPE_ROT) / ROPE_W))
    pos = np.where((jj < ROPE_W)[None, :], row[:, None], col[:, None])
    ang = pos * inv[None, :]
    cos = np.cos(ang)
    sin = np.sin(ang)
    first = (idx < ROPE_ROT)[None, :]
    sin_a = np.where(first, -sin, 0.0)
    sin_b = np.where(first, 0.0, sin)
    return (np.asarray(cos, np.float32), np.asarray(sin_a, np.float32), np.asarray(sin_b, np.float32))


def _dft_tables(n):
    k = np.arange(n)
    kn = (k[:, None] * k[None, :]) % n
    ang = 2.0 * np.pi * kn / n
    return np.asarray(np.concatenate([np.cos(ang), -np.sin(ang)], axis=1) / math.sqrt(n), np.float32)


def _chan_tables():
    k = np.arange(FOURIER_W)
    kn = (k[:, None] * k[None, :]) % FOURIER_W
    ang = 2.0 * np.pi * kn / FOURIER_W
    return np.asarray(np.concatenate([np.cos(ang), np.sin(ang)], axis=1) / math.sqrt(FOURIER_W), np.float32)


def _ct_tables(n):
    n2 = n // FFT_RADIX
    k = np.arange(n2)
    ang = 2.0 * np.pi * ((k[:, None] * k[None, :]) % n2) / n2
    f1 = np.concatenate([np.cos(ang), -np.sin(ang)], axis=0) / math.sqrt(n * FOURIER_W)
    tw = 2.0 * np.pi * k[:, None] * np.arange(FFT_RADIX)[None, :] / n
    twc = np.repeat(np.cos(tw), FOURIER_W, axis=1)
    tws = np.repeat(np.sin(tw), FOURIER_W, axis=1)
    c = np.arange(FOURIER_W)
    angc = 2.0 * np.pi * ((c[:, None] * c[None, :]) % FOURIER_W) / FOURIER_W
    cs2 = np.concatenate([np.cos(angc), np.sin(angc)], axis=0)
    return tuple(np.asarray(t, np.float32) for t in (f1, twc, tws, cs2))


def _mod_kernel(cctx_ref, c_ref, w_ref, b_ref, o_ref, s_ref):
    n_c = c_ref.shape[0]
    s_ref[...] = jnp.zeros_like(s_ref)
    s_ref[0:1, :] = _silu(cctx_ref[...])
    s_ref[1:1 + n_c, :] = _silu(c_ref[...])
    w = w_ref[0].astype(BF16)
    res = jnp.dot(s_ref[...].astype(BF16), w, preferred_element_type=F32) + b_ref[0]
    for r in range(MOD_ROWS):
        o_ref[0, r] = res[r:r + 1]


def _modulation(c_ctx, c, w_mod, b_mod):
    chunk = MOD_CHUNK
    n_chunks = 3 * D_MODEL // chunk
    return pl.pallas_call(
        _mod_kernel,
        grid=(DEPTH, n_chunks),
        in_specs=[
            pl.BlockSpec((1, D_MODEL), lambda l, j: (0, 0)),
            pl.BlockSpec(c.shape, lambda l, j: (0, 0)),
            pl.BlockSpec((1, D_MODEL, chunk), lambda l, j: (l, 0, j)),
            pl.BlockSpec((1, 1, chunk), lambda l, j: (l, 0, j)),
        ],
        out_specs=pl.BlockSpec((1, MOD_ROWS, 1, chunk), lambda l, j: (l, 0, 0, j)),
        out_shape=jax.ShapeDtypeStruct((DEPTH, MOD_ROWS, 1, 3 * D_MODEL), F32),
        scratch_shapes=[pltpu.VMEM((MOD_ROWS, D_MODEL), F32)],
        compiler_params=_cparams(("arbitrary", "arbitrary")),
        name="modulation",
    )(c_ctx.reshape(1, D_MODEL), c, w_mod, b_mod.reshape(DEPTH, 1, 3 * D_MODEL))


def _store_heads(ref, layer, t):
    seqs, _, rows, _ = ref.shape
    seq = rows // N_HEADS
    for s in range(seqs):
        for hd in range(N_HEADS):
            ref[s, layer, pl.ds(hd, seq, stride=N_HEADS), :] = (
                t[s * seq:(s + 1) * seq, hd * QK_W:(hd + 1) * QK_W])


def _inproj_tile(x_ref, mod_ref, g_ref, w_ref, rope_refs, outs, kv_refs, stage_ref):
    fa_ref, sga_ref, zc_ref, bgs_ref, q_ref, k_ref, v_ref, sgc_ref = outs

    x = x_ref[...]
    ms = jnp.mean(x * x, axis=-1, keepdims=True)
    y = x * lax.rsqrt(ms + EPS) * g_ref[...]
    m = mod_ref[0, 0]
    h = (y * (1.0 + m[:, D_MODEL:2 * D_MODEL]) + m[:, :D_MODEL]).astype(BF16)

    def proj(name):
        lo, hi = _COLS[name]
        return jnp.dot(h, w_ref[:, lo:hi], preferred_element_type=F32)

    if stage_ref is not None:
        fa = proj("fa")
        halves = stage_ref.shape[0]
        for hf in range(halves):
            stage_ref[hf] = fa[:, hf * LANES:(hf + 1) * LANES]
        rows = stage_ref.shape[1] // FFT_RADIX
        for n1 in range(FFT_RADIX):
            for hf in range(halves):
                lo = n1 * FOURIER_W + hf * LANES
                fa_ref[:, lo:lo + LANES] = (
                    stage_ref[hf, pl.ds(n1, rows, stride=FFT_RADIX), :].astype(BF16))
    else:
        fa_ref[...] = proj("fa").astype(BF16)
    sga_ref[...] = _silu(proj("ga")).astype(BF16)
    zc_ref[...] = (proj("cg") * proj("hc")).astype(BF16)
    bgs_ref[...] = (proj("bg") * _silu(proj("gb"))).astype(BF16)
    sgc_ref[...] = _silu(proj("gc")).astype(BF16)

    q = proj("q")
    k = proj("k")
    v = proj("v")
    if kv_refs is not None and len(kv_refs) == 2:
        kv_refs[0][...] = k
        kv_refs[1][...] = v
    elif kv_refs is not None:
        new_k_ref, new_v_ref, prev_k_ref, prev_v_ref = kv_refs
        _store_heads(new_k_ref, 0, prev_k_ref[...])
        _store_heads(new_v_ref, 0, prev_v_ref[...])
        _store_heads(new_k_ref, 1, k)
        _store_heads(new_v_ref, 1, v)
    v_ref[...] = v.astype(BF16)

    q_scale = HEAD_DIM ** -0.5 * LOG2E
    if rope_refs is not None:
        cos = rope_refs[0][...]
        sa = rope_refs[1][...]
        sb = rope_refs[2][...]
        for hd in range(N_HEADS):
            sl = slice(hd * QK_W, (hd + 1) * QK_W)
            for t, ref, scale in ((q, q_ref, q_scale), (k, k_ref, None)):
                th = t[:, sl]
                r = (th * cos + pltpu.roll(th, QK_W - ROPE_ROT, axis=1) * sa
                     + pltpu.roll(th, ROPE_ROT, axis=1) * sb)
                if scale is not None:
                    r = r * scale
                ref[:, sl] = r.astype(BF16)
    else:
        q_ref[...] = (q * q_scale).astype(BF16)
        k_ref[...] = k.astype(BF16)


N_PROJ_OUTS = 8


def _inproj_kernel(xc_ref, xd_ref, modc_ref, modd_ref, g_ref, w_ref, cos_ref, sa_ref, sb_ref,
                   *rest, n_ctx_tiles, n_prev):
    prev_kv = rest[:n_prev]
    rest = rest[n_prev:]
    ctx_outs = rest[:N_PROJ_OUTS]
    kv_refs = rest[N_PROJ_OUTS:N_PROJ_OUTS + 2] + prev_kv
    den_outs = rest[N_PROJ_OUTS + 2:2 * N_PROJ_OUTS + 2]
    wb_ref, stage_ref = rest[2 * N_PROJ_OUTS + 2:]
    i = pl.program_id(0)

    @pl.when(i == 0)
    def _():
        for lo, hi in _COLS.values():
            wb_ref[:, lo:hi] = w_ref[0, :, lo:hi].astype(BF16)

    @pl.when(i < n_ctx_tiles)
    def _():
        _inproj_tile(xc_ref, modc_ref, g_ref, wb_ref, None, ctx_outs, kv_refs, None)

    @pl.when(i >= n_ctx_tiles)
    def _():
        _inproj_tile(xd_ref, modd_ref, g_ref, wb_ref, (cos_ref, sa_ref, sb_ref), den_outs, None,
                     stage_ref)


def _in_projection(xc, xd, mod, norm_g, w_in, layer, rope_tabs, *, seq, dec_seq,
                   prev_kv):
    assert DEPTH == 2, "the last layer's call assembles the k/v of exactly two layers"
    tm = TOKEN_TILE
    n_c, n_d = xc.shape[0], xd.shape[0]
    nct, ndt = n_c // tm, n_d // tm
    seq_tiles = dec_seq // tm
    ctx_i = lambda i: jnp.minimum(i, nct - 1)
    den_i = lambda i: jnp.maximum(i - nct, 0)
    row_c = lambda w: pl.BlockSpec((tm, w), lambda i: (ctx_i(i), 0))
    row_d = lambda w: pl.BlockSpec((tm, w), lambda i: (den_i(i), 0))
    tab = pl.BlockSpec((tm, QK_W), lambda i: (den_i(i) % seq_tiles, 0))
    in_specs = [
        row_c(D_MODEL), row_d(D_MODEL),
        pl.BlockSpec((1, 1, 1, 3 * D_MODEL), lambda i: (layer, 0, 0, 0)),
        pl.BlockSpec((1, 1, 1, 3 * D_MODEL), lambda i: (layer, 1 + den_i(i) // seq_tiles, 0, 0)),
        pl.BlockSpec((1, D_MODEL), lambda i: (0, 0)),
        pl.BlockSpec((1, D_MODEL, IN_DIM), lambda i: (layer, 0, 0)),
        tab, tab, tab,
    ]
    args = [xc, xd, mod, mod, norm_g.reshape(1, D_MODEL), w_in, *rope_tabs]
    widths = (256, 256, 256, 256, 512, 512, 512, 512)
    seqs = tm // seq
    if prev_kv is None:
        kv_blk = row_c(ATTN_W)
        kv_shape = jax.ShapeDtypeStruct((n_c, ATTN_W), F32)
    else:
        kv_blk = pl.BlockSpec((seqs, DEPTH, seq * N_HEADS, QK_W), lambda i: (ctx_i(i), 0, 0, 0))
        kv_shape = jax.ShapeDtypeStruct((n_c // seq, DEPTH, seq * N_HEADS, QK_W), F32)
        in_specs += [row_c(ATTN_W)] * 2
        args += list(prev_kv)
    out_specs = [row_c(w) for w in widths] + [kv_blk, kv_blk] + [row_d(w) for w in widths]
    out_shape = ([jax.ShapeDtypeStruct((n_c, w), BF16) for w in widths] + [kv_shape, kv_shape]
                 + [jax.ShapeDtypeStruct((n_d, w), BF16) for w in widths])
    fa_d = N_PROJ_OUTS + 2
    out_specs[fa_d] = pl.BlockSpec((tm // FFT_RADIX, FFT_RADIX * FOURIER_W), lambda i: (den_i(i), 0))
    out_shape[fa_d] = jax.ShapeDtypeStruct((n_d // FFT_RADIX, FFT_RADIX * FOURIER_W), BF16)
    outs = pl.pallas_call(
        functools.partial(_inproj_kernel, n_ctx_tiles=nct, n_prev=0 if prev_kv is None else 2),
        grid=(nct + ndt,),
        in_specs=in_specs,
        out_specs=out_specs,
        out_shape=out_shape,
        scratch_shapes=[pltpu.VMEM((D_MODEL, IN_DIM), BF16),
                        pltpu.VMEM((FOURIER_W // LANES, tm, LANES), F32)],
        compiler_params=_cparams(("arbitrary",)),
        name="in_projection",
    )(*args)
    return outs[:N_PROJ_OUTS], outs[N_PROJ_OUTS], outs[N_PROJ_OUTS + 1], outs[fa_d:]


def _lambda(lam_ref, lam_init):
    lv = lam_ref[...]
    a = jnp.sum(lv[0:1] * lv[1:2], axis=-1, keepdims=True)
    b = jnp.sum(lv[2:3] * lv[3:4], axis=-1, keepdims=True)
    return jnp.exp(a) - jnp.exp(b) + lam_init


def _stack_masked(q_t):
    first = lax.broadcasted_iota(jnp.int32, q_t.shape, 0) < HEAD_DIM
    return jnp.concatenate([jnp.where(first, q_t, 0.0), jnp.where(first, 0.0, q_t)],
                           axis=1).astype(BF16)


def _scores_chunk(k, qq_t, m):
    s = jnp.dot(k, qq_t, preferred_element_type=F32)
    mc = jnp.max(s, axis=0, keepdims=True)
    return s, (mc if m is None else jnp.maximum(m, mc))


def _with_ones_rows(v_t):
    return jnp.concatenate([v_t, jnp.ones((ONES_ROWS, v_t.shape[1]), F32)], axis=0).astype(BF16)


def _values_chunk(s, m, v1_t, acc):
    e = jnp.exp2(s - m).astype(BF16)
    oc = jnp.dot(v1_t, e, preferred_element_type=F32)
    return oc if acc is None else acc + oc


def _combine_t(acc, lam):
    tq = acc.shape[1] // 2
    r = 1.0 / acc[QK_W:QK_W + 1, :]
    o_t = acc[:QK_W, :]
    return o_t[:, :tq] * r[:, :tq] - o_t[:, tq:] * (r[:, tq:] * lam)


def _subln_gate(o, sg, sgc, lam_init):
    ms = jnp.mean(o * o, axis=-1, keepdims=True)
    y = o * lax.rsqrt(ms + EPS) * sg * (1.0 - lam_init)
    return (y * sgc.astype(F32)).astype(BF16)


def _attn_ctx_kernel(lam_ref, sg_ref, q_ref, k_ref, v_ref, sgc_ref, o_ref, *, lam_init, seq):
    lam = _lambda(lam_ref, lam_init)
    sg = sg_ref[...]
    n_seq = q_ref.shape[0] // seq
    q_t = q_ref[...].astype(F32).T
    v_t = v_ref[...].astype(F32).T
    pairs = [(slice(s * seq, (s + 1) * seq), slice(hd * QK_W, (hd + 1) * QK_W))
             for s in range(n_seq) for hd in range(N_HEADS)]
    qqs = [_stack_masked(q_t[hl, rows]) for rows, hl in pairs]
    v1s = [_with_ones_rows(v_t[hl, rows]) for rows, hl in pairs]
    sm = [_scores_chunk(k_ref[rows, hl], qq, None) for (rows, hl), qq in zip(pairs, qqs)]
    accs = [_values_chunk(s, m, v1, None) for (s, m), v1 in zip(sm, v1s)]
    y_t = [_combine_t(a, lam) for a in accs]
    for s in range(n_seq):
        rows = slice(s * seq, (s + 1) * seq)
        y = jnp.concatenate(y_t[s * N_HEADS:(s + 1) * N_HEADS], axis=0).T
        for hd in range(N_HEADS):
            hl = slice(hd * QK_W, (hd + 1) * QK_W)
            o_ref[rows, hl] = _subln_gate(y[:, hl], sg, sgc_ref[rows, hl], lam_init)


def _attention_ctx(q, k, v, sgc, lam_vec, subln_g, *, seq, lam_init):
    n_tok = q.shape[0]
    blk = pl.BlockSpec((CTX_ATTN_SEQS * seq, ATTN_W), lambda b: (b, 0))
    return pl.pallas_call(
        functools.partial(_attn_ctx_kernel, lam_init=lam_init, seq=seq),
        grid=(n_tok // (CTX_ATTN_SEQS * seq),),
        in_specs=[
            pl.BlockSpec((4, HEAD_DIM), lambda b: (0, 0)),
            pl.BlockSpec((1, QK_W), lambda b: (0, 0)),
            blk, blk, blk, blk,
        ],
        out_specs=blk,
        out_shape=jax.ShapeDtypeStruct((n_tok, ATTN_W), BF16),
        compiler_params=_cparams(("arbitrary",)),
        name="attention_ctx",
    )(lam_vec, subln_g.reshape(1, QK_W), q, k, v, sgc)


def _attn_den_kernel(lam_ref, sg_ref, q_ref, kn_ref, vn_ref, kc_ref, vc_ref, sgc_ref, o_ref,
                     kcs_ref, vt_ref, qq0_ref, qq1_ref, s0_ref, s1_ref, acc0_ref, acc1_ref,
                     *, lam_init):
    seq = q_ref.shape[0]
    past = kcs_ref.shape[1]
    past_chunks = past // KEY_CHUNK
    n_chunks = (past + seq) // KEY_CHUNK
    n_tiles = seq // Q_TILE
    n_elems = n_tiles * N_HEADS
    qq_refs = (qq0_ref, qq1_ref)
    s_refs = (s0_ref, s1_ref)
    acc_refs = (acc0_ref, acc1_ref)
    lam = _lambda(lam_ref, lam_init)
    sg = sg_ref[...]

    def lanes(hd):
        return slice(hd * QK_W, (hd + 1) * QK_W)

    def chunk(c):
        return slice(c * KEY_CHUNK, (c + 1) * KEY_CHUNK)

    def tile_rows(t):
        if isinstance(t, int):
            return pl.ds(t * Q_TILE, Q_TILE)
        return pl.ds(pl.multiple_of(t * Q_TILE, Q_TILE), Q_TILE)

    for hd in range(N_HEADS):
        cache_rows = pl.ds(hd, past, stride=N_HEADS)
        kcs_ref[hd] = kc_ref[0, 0, cache_rows, :].astype(BF16)
        vt_ref[hd, :, :past] = _with_ones_rows(vc_ref[0, 0, cache_rows, :].T)
        vt_ref[hd, :, past:] = _with_ones_rows(vn_ref[:, lanes(hd)].astype(F32).T)

    def keys(hd, c):
        if c < past_chunks:
            return kcs_ref[hd, chunk(c), :]
        return kn_ref[chunk(c - past_chunks), lanes(hd)]

    def prep(t, hd, par):
        qq_refs[par][...] = _stack_masked(q_ref[tile_rows(t), lanes(hd)].astype(F32).T)

    def finish(t, hd, par):
        rows = tile_rows(t)
        y = _combine_t(acc_refs[par][...], lam).T
        o_ref[rows, lanes(hd)] = _subln_gate(y, sg, sgc_ref[rows, lanes(hd)], lam_init)

    def step(t, j, m_cur, first=False, last=False):
        def elem(off):
            return t + (j + off) // N_HEADS, (j + off) % N_HEADS

        par = j % 2
        e_static = N_HEADS * t + j if isinstance(t, int) else None
        do_finish = not (first and j == 0)
        do_scores = not (last and e_static + 1 >= n_elems)
        do_prep = not (last and e_static + 2 >= n_elems)
        if do_prep:
            prep(*elem(2), par)
        hd_nxt = elem(1)[1]
        qq = qq_refs[1 - par][...] if do_scores else None
        m_nxt = None
        acc = None
        for c in range(n_chunks):
            if do_scores:
                s, m_nxt = _scores_chunk(keys(hd_nxt, c), qq, m_nxt)
                s_refs[1 - par][chunk(c), :] = s
            acc = _values_chunk(s_refs[par][chunk(c), :], m_cur, vt_ref[j, :, chunk(c)], acc)
        acc_refs[par][...] = acc
        if do_finish:
            finish(*elem(-1), 1 - par)
        return m_nxt

    def tile_steps(t, m, **edge):
        for j in range(N_HEADS):
            m = step(t, j, m, **edge)
        return m

    prep(0, 0, 0)
    prep(0, 1, 1)
    qq = qq0_ref[...]
    m = None
    for c in range(n_chunks):
        s, m = _scores_chunk(keys(0, c), qq, m)
        s0_ref[chunk(c), :] = s
    m = tile_steps(0, m, first=True)
    m = lax.fori_loop(1, n_tiles - 1, tile_steps, m)
    tile_steps(n_tiles - 1, m, last=True)
    finish(n_tiles - 1, N_HEADS - 1, (n_elems - 1) % 2)


def _attention_den(q, k, v, sgc, cache_k, cache_v, lam_vec, subln_g, *, layer, seq, lam_init):
    n_tok = q.shape[0]
    past = cache_k.shape[2] // N_HEADS
    blk = pl.BlockSpec((seq, ATTN_W), lambda b: (b, 0))
    cblk = pl.BlockSpec((1, 1, past * N_HEADS, QK_W), lambda b: (b, layer, 0, 0))
    return pl.pallas_call(
        functools.partial(_attn_den_kernel, lam_init=lam_init),
        grid=(n_tok // seq,),
        in_specs=[
            pl.BlockSpec((4, HEAD_DIM), lambda b: (0, 0)),
            pl.BlockSpec((1, QK_W), lambda b: (0, 0)),
            blk, blk, blk, cblk, cblk, blk,
        ],
        out_specs=blk,
        out_shape=jax.ShapeDtypeStruct((n_tok, ATTN_W), BF16),
        scratch_shapes=[
            pltpu.VMEM((N_HEADS, past, QK_W), BF16),
            pltpu.VMEM((N_HEADS, QK_W + ONES_ROWS, past + seq), BF16),
            pltpu.VMEM((QK_W, 2 * Q_TILE), BF16),
            pltpu.VMEM((QK_W, 2 * Q_TILE), BF16),
            pltpu.VMEM((past + seq, 2 * Q_TILE), F32),
            pltpu.VMEM((past + seq, 2 * Q_TILE), F32),
            pltpu.VMEM((QK_W + ONES_ROWS, 2 * Q_TILE), F32),
            pltpu.VMEM((QK_W + ONES_ROWS, 2 * Q_TILE), F32),
        ],
        compiler_params=_cparams(("arbitrary",)),
        name="attention_den",
    )(lam_vec, subln_g.reshape(1, QK_W), q, k, v, cache_k, cache_v, sgc)


def _fourier_ctx_kernel(fa_ref, sga_ref, cs_ref, dn_ref, o_ref, *, seq):
    ab = jnp.dot(fa_ref[...], cs_ref[...], preferred_element_type=F32).astype(BF16)
    for s in range(fa_ref.shape[0] // seq):
        rows = slice(s * seq, (s + 1) * seq)
        ab2 = jnp.concatenate([ab[rows, :FOURIER_W], ab[rows, FOURIER_W:]], axis=0)
        f = jnp.dot(dn_ref[...], ab2, preferred_element_type=F32)
        o_ref[rows, :] = (f * sga_ref[rows, :].astype(F32)).astype(BF16)


def _fourier_ctx(fa, sga, cs, dn, *, seq):
    n_tok = fa.shape[0]
    blk = pl.BlockSpec((CTX_FOURIER_SEQS * seq, FOURIER_W), lambda b: (b, 0))
    return pl.pallas_call(
        functools.partial(_fourier_ctx_kernel, seq=seq),
        grid=(n_tok // (CTX_FOURIER_SEQS * seq),),
        in_specs=[blk, blk,
                  pl.BlockSpec((FOURIER_W, 2 * FOURIER_W), lambda b: (0, 0)),
                  pl.BlockSpec((seq, 2 * seq), lambda b: (0, 0))],
        out_specs=blk,
        out_shape=jax.ShapeDtypeStruct((n_tok, FOURIER_W), BF16),
        compiler_params=_cparams(("arbitrary",)),
        name="fourier_ctx",
    )(fa, sga, cs, dn)


def _cadd(a, b):
    return (a[0] + b[0], a[1] + b[1])


def _csub(a, b):
    return (a[0] - b[0], a[1] - b[1])


def _mul_neg_i(a):
    return (a[1], -a[0])


def _mul_w8_1(a):
    return ((a[0] + a[1]) * SQRT_HALF, (a[1] - a[0]) * SQRT_HALF)


def _mul_w8_3(a):
    return ((a[1] - a[0]) * SQRT_HALF, (-a[0] - a[1]) * SQRT_HALF)


def _fft4(a0, a1, a2, a3):
    e0, e1 = _cadd(a0, a2), _csub(a0, a2)
    o0, o1 = _cadd(a1, a3), _mul_neg_i(_csub(a1, a3))
    return [_cadd(e0, o0), _cadd(e1, o1), _csub(e0, o0), _csub(e1, o1)]


def _fft8(x):
    e = _fft4(x[0], x[2], x[4], x[6])
    o = _fft4(x[1], x[3], x[5], x[7])
    t = [o[0], _mul_w8_1(o[1]), _mul_neg_i(o[2]), _mul_w8_3(o[3])]
    return [_cadd(e[k], t[k]) for k in range(4)] + [_csub(e[k], t[k]) for k in range(4)]


def _fourier_den_kernel(xw_ref, sga_ref, f1_ref, twc_ref, tws_ref, cs2_ref, o_ref, g_ref, x_ref):
    n2 = xw_ref.shape[0]
    g_ref[...] = jnp.dot(f1_ref[...], xw_ref[...], preferred_element_type=F32)

    def chunk(i, carry):
        r = pl.multiple_of(i * FFT_ROWS, FFT_ROWS)
        re_rows = pl.ds(r, FFT_ROWS)
        im_rows = pl.ds(n2 + r, FFT_ROWS)
        xs = []
        for n1 in range(FFT_RADIX):
            lanes = slice(n1 * FOURIER_W, (n1 + 1) * FOURIER_W)
            gr = g_ref[re_rows, lanes]
            gi = g_ref[im_rows, lanes]
            if n1 > 0:
                c = twc_ref[re_rows, lanes]
                s = tws_ref[re_rows, lanes]
                gr, gi = gr * c + gi * s, gi * c - gr * s
            xs.append((gr, gi))
        for k1, (xr, xi) in enumerate(_fft8(xs)):
            out_rows = pl.ds(k1 * n2 + r, FFT_ROWS)
            x_ref[out_rows, :FOURIER_W] = xr.astype(BF16)
            x_ref[out_rows, FOURIER_W:] = xi.astype(BF16)
        return carry

    lax.fori_loop(0, n2 // FFT_ROWS, chunk, 0)
    f = jnp.dot(x_ref[...], cs2_ref[...], preferred_element_type=F32)
    o_ref[...] = (f * sga_ref[...].astype(F32)).astype(BF16)


def _fourier_den(xw, sga, f1, twc, tws, cs2, *, seq):
    n_tok = sga.shape[0]
    n2 = seq // FFT_RADIX
    wide = FFT_RADIX * FOURIER_W
    const = lambda shape: pl.BlockSpec(shape, lambda b: (0, 0))
    return pl.pallas_call(
        _fourier_den_kernel,
        grid=(n_tok // seq,),
        in_specs=[
            pl.BlockSpec((n2, wide), lambda b: (b, 0)),
            pl.BlockSpec((seq, FOURIER_W), lambda b: (b, 0)),
            const((2 * n2, n2)), const((n2, wide)), const((n2, wide)),
            const((2 * FOURIER_W, FOURIER_W)),
        ],
        out_specs=pl.BlockSpec((seq, FOURIER_W), lambda b: (b, 0)),
        out_shape=jax.ShapeDtypeStruct((n_tok, FOURIER_W), BF16),
        scratch_shapes=[pltpu.VMEM((2 * n2, wide), F32),
                        pltpu.VMEM((seq, 2 * FOURIER_W), BF16)],
        compiler_params=_cparams(("arbitrary",)),
        name="fourier_den",
    )(xw, sga, f1, twc, tws, cs2)


def _outproj_tile(x_ref, mod_ref, ya_ref, zc_ref, zp_ref, zn_ref, bgs_ref, yc_ref,
                  cw_ref, cb_ref, w_ref, fg_ref, o_ref, tile, *, seq, final_norm):
    tm = x_ref.shape[0]
    z = zc_ref[...].astype(F32)
    row = lax.broadcasted_iota(jnp.int32, z.shape, 0)
    pos = (tile * tm + row) & (seq - 1)
    prev_row = zp_ref[HALO_ROWS - 1:HALO_ROWS, :].astype(F32)
    next_row = zn_ref[0:1, :].astype(F32)
    z_prev = jnp.where(row == 0, prev_row, pltpu.roll(z, 1, axis=0))
    z_prev = jnp.where(pos == 0, 0.0, z_prev)
    z_next = jnp.where(row == tm - 1, next_row, pltpu.roll(z, tm - 1, axis=0))
    z_next = jnp.where(pos == seq - 1, 0.0, z_next)
    cw = cw_ref[...]
    conv = z_prev * cw[0:1] + z * cw[1:2] + z_next * cw[2:3] + cb_ref[...]
    yb = (bgs_ref[...].astype(F32) * conv).astype(BF16)

    mixed = jnp.concatenate([ya_ref[...], yb, yc_ref[...]], axis=-1)
    out = jnp.dot(mixed, w_ref[...], preferred_element_type=F32)
    gate = mod_ref[0, 0][:, 2 * D_MODEL:]
    xn = x_ref[...] + gate * out
    if final_norm:
        ms = jnp.mean(xn * xn, axis=-1, keepdims=True)
        xn = xn * lax.rsqrt(ms + EPS) * fg_ref[...]
    o_ref[...] = xn


N_GROUP_INS = 8


def _outproj_kernel(*refs, n_ctx_tiles, seq, dec_seq, final_norm):
    ctx_ins = refs[:N_GROUP_INS]
    den_ins = refs[N_GROUP_INS:2 * N_GROUP_INS]
    cw_ref, cb_ref, w_ref, fg_ref, oc_ref, od_ref, wb_ref = refs[2 * N_GROUP_INS:]
    i = pl.program_id(0)

    @pl.when(i == 0)
    def _():
        wb_ref[...] = w_ref[0].astype(BF16)

    @pl.when(i < n_ctx_tiles)
    def _():
        _outproj_tile(*ctx_ins, cw_ref, cb_ref, wb_ref, fg_ref, oc_ref, i, seq=seq,
                      final_norm=final_norm)

    @pl.when(i >= n_ctx_tiles)
    def _():
        _outproj_tile(*den_ins, cw_ref, cb_ref, wb_ref, fg_ref, od_ref, i - n_ctx_tiles,
                      seq=dec_seq, final_norm=final_norm)


def _out_projection(ctx, den, mod, conv_w, conv_b, w_out, layer, final_g, *, seq, dec_seq,
                    final_norm):
    tm = OUT_TILE
    nct = ctx[0].shape[0] // tm
    ndt = den[0].shape[0] // tm
    halo_per_tile = tm // HALO_ROWS
    seq_tiles = dec_seq // tm

    def group_specs(tile_of, n_tok, mod_of):
        n_halo = n_tok // HALO_ROWS
        row = lambda w: pl.BlockSpec((tm, w), lambda i: (tile_of(i), 0))
        return [
            row(D_MODEL),
            pl.BlockSpec((1, 1, 1, 3 * D_MODEL), lambda i: (layer, mod_of(tile_of(i)), 0, 0)),
            row(FOURIER_W),
            row(CONV_W),
            pl.BlockSpec((HALO_ROWS, CONV_W),
                         lambda i: (jnp.maximum(tile_of(i) * halo_per_tile - 1, 0), 0)),
            pl.BlockSpec((HALO_ROWS, CONV_W),
                         lambda i: (jnp.minimum((tile_of(i) + 1) * halo_per_tile, n_halo - 1), 0)),
            row(CONV_W),
            row(ATTN_W),
        ], row(D_MODEL)

    ctx_specs, ctx_out = group_specs(lambda i: jnp.minimum(i, nct - 1), ctx[0].shape[0],
                                     lambda t: 0)
    den_specs, den_out = group_specs(lambda i: jnp.maximum(i - nct, 0), den[0].shape[0],
                                     lambda t: 1 + t // seq_tiles)

    def group_args(g):
        x, ya, zc, bgs, yc = g
        return [x, mod, ya, zc, zc, zc, bgs, yc]

    return pl.pallas_call(
        functools.partial(_outproj_kernel, n_ctx_tiles=nct, seq=seq, dec_seq=dec_seq,
                          final_norm=final_norm),
        grid=(nct + ndt,),
        in_specs=ctx_specs + den_specs + [
            pl.BlockSpec((3, CONV_W), lambda i: (0, 0)),
            pl.BlockSpec((1, CONV_W), lambda i: (0, 0)),
            pl.BlockSpec((1, D_MODEL, D_MODEL), lambda i: (layer, 0, 0)),
            pl.BlockSpec((1, D_MODEL), lambda i: (0, 0)),
        ],
        out_specs=[ctx_out, den_out],
        out_shape=[jax.ShapeDtypeStruct(ctx[0].shape, F32), jax.ShapeDtypeStruct(den[0].shape, F32)],
        scratch_shapes=[pltpu.VMEM((D_MODEL, D_MODEL), BF16)],
        compiler_params=_cparams(("arbitrary",)),
        name="out_projection",
    )(*group_args(ctx), *group_args(den), conv_w, conv_b.reshape(1, CONV_W), w_out,
      final_g.reshape(1, D_MODEL))


def kernel(x_prompt, x_sample, cache_k, cache_v, c, c_ctx, norm_g, w_mod, b_mod, w_in, conv_w,
           conv_b, lam_vec, subln_g, w_out, final_g):
    batch, seq, _ = x_prompt.shape
    dec_batch, dec_seq, _ = x_sample.shape

    mod = _modulation(c_ctx, c, w_mod, b_mod)

    rope_tabs = tuple(jnp.asarray(t) for t in _rope_tables(dec_seq))
    cs = jnp.asarray(_chan_tables()).astype(BF16)
    dn_ctx = jnp.asarray(_dft_tables(seq)).astype(BF16)
    f1, twc, tws, cs2 = (jnp.asarray(t) for t in _ct_tables(dec_seq))
    f1 = f1.astype(BF16)
    cs2 = cs2.astype(BF16)

    cache_k2 = cache_k.reshape(dec_batch, DEPTH, -1, QK_W)
    cache_v2 = cache_v.reshape(dec_batch, DEPTH, -1, QK_W)
    xc = x_prompt.reshape(batch * seq, D_MODEL)
    xl = x_sample.reshape(dec_batch * dec_seq, D_MODEL)
    kv = None
    for l in range(DEPTH):
        lam_init = 0.8 - 0.6 * math.exp(-0.3 * l)
        ctx, k32, v32, den = _in_projection(
            xc, xl, mod, norm_g[l], w_in, l, rope_tabs, seq=seq, dec_seq=dec_seq,
            prev_kv=kv)
        kv = (k32, v32)

        fa, sga, zc_c, bgs_c, q, k, v, sgc = ctx
        yc_c = _attention_ctx(q, k, v, sgc, lam_vec[l], subln_g[l], seq=seq, lam_init=lam_init)
        ya_c = _fourier_ctx(fa, sga, cs, dn_ctx, seq=seq)

        fa, sga, zc_d, bgs_d, q, k, v, sgc = den
        yc_d = _attention_den(q, k, v, sgc, cache_k2, cache_v2, lam_vec[l], subln_g[l], layer=l,
                              seq=dec_seq, lam_init=lam_init)
        ya_d = _fourier_den(fa, sga, f1, twc, tws, cs2, seq=dec_seq)

        xc, xl = _out_projection(
            (xc, ya_c, zc_c, bgs_c, yc_c), (xl, ya_d, zc_d, bgs_d, yc_d), mod,
            conv_w[l], conv_b[l], w_out, l, final_g, seq=seq, dec_seq=dec_seq,
            final_norm=l == DEPTH - 1)

    y_prompt = xc.reshape(batch, seq, D_MODEL)
    y_sample = xl.reshape(dec_batch, dec_seq, D_MODEL)
    return (y_prompt, y_sample, *(t.reshape(batch, DEPTH, seq, N_HEADS, QK_W) for t in kv))
```

```python
import functools
import math

import numpy as np
import jax
import jax.numpy as jnp
from jax import lax
from jax.experimental import pallas as pl
from jax.experimental.pallas import tpu as pltpu

D_MODEL = 1024
DEPTH = 2
GRID_W = 64
FOURIER_W = 256
CONV_W = 256
ATTN_W = 512
N_HEADS = 4
HEAD_DIM = 64
QK_W = 128
ROPE_BASE = 10000.0
ROPE_W = HEAD_DIM // 2
ROPE_ROT = ROPE_W // 2
EPS = 1e-6
IN_DIM = 3584

F32 = jnp.float32
BF16 = jnp.bfloat16

VMEM_LIMIT_BYTES = 60 * 1024 * 1024
TOKEN_TILE = 512
OUT_TILE = 1024
Q_TILE = 256
KEY_CHUNK = 256
ONES_ROWS = 16
LANES = 128
FFT_RADIX = 8
FFT_ROWS = 16
SQRT_HALF = 0.7071067811865476
CTX_SEQS = 2
HALO_ROWS = 16
MOD_ROWS = 8
MOD_CHUNK = 3 * D_MODEL
LOG2E = 1.4426950408889634

_COLS = {}
_off = 0
for _name, _w in (("fa", 256), ("ga", 256), ("bg", 256), ("cg", 256), ("hc", 256), ("gb", 256),
                  ("q", 512), ("k", 512), ("v", 512), ("gc", 512)):
    _COLS[_name] = (_off, _off + _w)
    _off += _w


def _silu(x):
    return x * (1.0 / (1.0 + jnp.exp(-x)))


def _cparams(sem):
    return pltpu.CompilerParams(dimension_semantics=sem, vmem_limit_bytes=VMEM_LIMIT_BYTES)


def _rope_tables(n_tokens):
    n = np.arange(n_tokens)
    row = (n // GRID_W).astype(np.float64)
    col = (n % GRID_W).astype(np.float64)
    j = np.arange(QK_W)
    jj = j % HEAD_DIM
    idx = jj % ROPE_W
    inv = 1.0 / (ROPE_BASE ** (2.0 * (idx % ROPE_ROT) / ROPE_W))
    pos = np.where((jj < ROPE_W)[None, :], row[:, None], col[:, None])
    ang = pos * inv[None, :]
    cos = np.cos(ang)
    sin = np.sin(ang)
    first = (idx < ROPE_ROT)[None, :]
    sin_a = np.where(first, -sin, 0.0)
    sin_b = np.where(first, 0.0, sin)
    return (np.asarray(cos, np.float32), np.asarray(sin_a, np.float32), np.asarray(sin_b, np.float32))


def _dft_tables(n):
    k = np.arange(n)
    kn = (k[:, None] * k[None, :]) % n
    ang = 2.0 * np.pi * kn / n
    return np.asarray(np.concatenate([np.cos(ang), -np.sin(ang)], axis=1) / math.sqrt(n), np.float32)


def _chan_tables():
    k = np.arange(FOURIER_W)
    kn = (k[:, None] * k[None, :]) % FOURIER_W
    ang = 2.0 * np.pi * kn / FOURIER_W
    return np.asarray(np.concatenate([np.cos(ang), np.sin(ang)], axis=1) / math.sqrt(FOURIER_W), np.float32)


def _ct_tables(n):
    n2 = n // FFT_RADIX
    k = np.arange(n2)
    ang = 2.0 * np.pi * ((k[:, None] * k[None, :]) % n2) / n2
    f1 = np.concatenate([np.cos(ang), -np.sin(ang)], axis=0) / math.sqrt(n * FOURIER_W)
    tw = 2.0 * np.pi * k[:, None] * np.arange(FFT_RADIX)[None, :] / n
    twc = np.repeat(np.cos(tw), FOURIER_W, axis=1)
    tws = np.repeat(np.sin(tw), FOURIER_W, axis=1)
    c = np.arange(FOURIER_W)
    angc = 2.0 * np.pi * ((c[:, None] * c[None, :]) % FOURIER_W) / FOURIER_W
    cs2 = np.concatenate([np.cos(angc), np.sin(angc)], axis=0)
    return tuple(np.asarray(t, np.float32) for t in (f1, twc, tws, cs2))


def _mod_kernel(cctx_ref, c_ref, w_ref, b_ref, o_ref, s_ref):
    n_c = c_ref.shape[0]
    s_ref[...] = jnp.zeros_like(s_ref)
    s_ref[0:1, :] = _silu(cctx_ref[...])
    s_ref[1:1 + n_c, :] = _silu(c_ref[...])
    w = w_ref[0].astype(BF16)
    res = jnp.dot(s_ref[...].astype(BF16), w, preferred_element_type=F32) + b_ref[0]
    for r in range(MOD_ROWS):
        o_ref[0, r] = res[r:r + 1]


def _modulation(c_ctx, c, w_mod, b_mod):
    chunk = MOD_CHUNK
    n_chunks = 3 * D_MODEL // chunk
    return pl.pallas_call(
        _mod_kernel,
        grid=(DEPTH, n_chunks),
        in_specs=[
            pl.BlockSpec((1, D_MODEL), lambda l, j: (0, 0)),
            pl.BlockSpec(c.shape, lambda l, j: (0, 0)),
            pl.BlockSpec((1, D_MODEL, chunk), lambda l, j: (l, 0, j)),
            pl.BlockSpec((1, 1, chunk), lambda l, j: (l, 0, j)),
        ],
        out_specs=pl.BlockSpec((1, MOD_ROWS, 1, chunk), lambda l, j: (l, 0, 0, j)),
        out_shape=jax.ShapeDtypeStruct((DEPTH, MOD_ROWS, 1, 3 * D_MODEL), F32),
        scratch_shapes=[pltpu.VMEM((MOD_ROWS, D_MODEL), F32)],
        compiler_params=_cparams(("arbitrary", "arbitrary")),
        name="modulation",
    )(c_ctx.reshape(1, D_MODEL), c, w_mod, b_mod.reshape(DEPTH, 1, 3 * D_MODEL))


def _store_heads(ref, layer, t):
    seqs, _, rows, _ = ref.shape
    seq = rows // N_HEADS
    for s in range(seqs):
        for hd in range(N_HEADS):
            ref[s, layer, pl.ds(hd, seq, stride=N_HEADS), :] = (
                t[s * seq:(s + 1) * seq, hd * QK_W:(hd + 1) * QK_W])


def _inproj_tile(x_ref, mod_ref, g_ref, w_ref, rope_refs, outs, kv_refs, stage_ref):
    fa_ref, sga_ref, zc_ref, bgs_ref, q_ref, k_ref, v_ref, sgc_ref = outs

    x = x_ref[...]
    ms = jnp.mean(x * x, axis=-1, keepdims=True)
    y = x * lax.rsqrt(ms + EPS) * g_ref[...]
    m = mod_ref[0, 0]
    h = (y * (1.0 + m[:, D_MODEL:2 * D_MODEL]) + m[:, :D_MODEL]).astype(BF16)

    def proj(name):
        lo, hi = _COLS[name]
        return jnp.dot(h, w_ref[:, lo:hi], preferred_element_type=F32)

    if stage_ref is not None:
        fa = proj("fa")
        halves = stage_ref.shape[0]
        for hf in range(halves):
            stage_ref[hf] = fa[:, hf * LANES:(hf + 1) * LANES]
        rows = stage_ref.shape[1] // FFT_RADIX
        for n1 in range(FFT_RADIX):
            for hf in range(halves):
                lo = n1 * FOURIER_W + hf * LANES
                fa_ref[:, lo:lo + LANES] = (
                    stage_ref[hf, pl.ds(n1, rows, stride=FFT_RADIX), :].astype(BF16))
    else:
        fa_ref[...] = proj("fa").astype(BF16)
    sga_ref[...] = _silu(proj("ga")).astype(BF16)
    zc_ref[...] = (proj("cg") * proj("hc")).astype(BF16)
    bgs_ref[...] = (proj("bg") * _silu(proj("gb"))).astype(BF16)
    sgc_ref[...] = _silu(proj("gc")).astype(BF16)

    q = proj("q")
    k = proj("k")
    v = proj("v")
    if kv_refs is not None and len(kv_refs) == 2:
        kv_refs[0][...] = k
        kv_refs[1][...] = v
    elif kv_refs is not None:
        new_k_ref, new_v_ref, prev_k_ref, prev_v_ref = kv_refs
        _store_heads(new_k_ref, 0, prev_k_ref[...])
        _store_heads(new_v_ref, 0, prev_v_ref[...])
        _store_heads(new_k_ref, 1, k)
        _store_heads(new_v_ref, 1, v)
    v_ref[...] = v.astype(BF16)

    q_scale = HEAD_DIM ** -0.5 * LOG2E
    if rope_refs is not None:
        cos = rope_refs[0][...]
        sa = rope_refs[1][...]
        sb = rope_refs[2][...]
        for hd in range(N_HEADS):
            sl = slice(hd * QK_W, (hd + 1) * QK_W)
            for t, ref, scale in ((q, q_ref, q_scale), (k, k_ref, None)):
                th = t[:, sl]
                r = (th * cos + pltpu.roll(th, QK_W - ROPE_ROT, axis=1) * sa
                     + pltpu.roll(th, ROPE_ROT, axis=1) * sb)
                if scale is not None:
                    r = r * scale
                ref[:, sl] = r.astype(BF16)
    else:
        q_ref[...] = (q * q_scale).astype(BF16)
        k_ref[...] = k.astype(BF16)


N_PROJ_OUTS = 8


def _inproj_kernel(xc_ref, xd_ref, modc_ref, modd_ref, g_ref, w_ref, cos_ref, sa_ref, sb_ref,
                   *rest, n_ctx_tiles, n_prev):
    prev_kv = rest[:n_prev]
    rest = rest[n_prev:]
    ctx_outs = rest[:N_PROJ_OUTS]
    kv_refs = rest[N_PROJ_OUTS:N_PROJ_OUTS + 2] + prev_kv
    den_outs = rest[N_PROJ_OUTS + 2:2 * N_PROJ_OUTS + 2]
    wb_ref, stage_ref = rest[2 * N_PROJ_OUTS + 2:]
    i = pl.program_id(0)

    @pl.when(i == 0)
    def _():
        for lo, hi in _COLS.values():
            wb_ref[:, lo:hi] = w_ref[0, :, lo:hi].astype(BF16)

    @pl.when(i < n_ctx_tiles)
    def _():
        _inproj_tile(xc_ref, modc_ref, g_ref, wb_ref, None, ctx_outs, kv_refs, None)

    @pl.when(i >= n_ctx_tiles)
    def _():
        _inproj_tile(xd_ref, modd_ref, g_ref, wb_ref, (cos_ref, sa_ref, sb_ref), den_outs, None,
                     stage_ref)


def _in_projection(xc, xd, mod, norm_g, w_in, layer, rope_tabs, *, seq, dec_seq,
                   prev_kv):
    assert DEPTH == 2, "the last layer's call assembles the k/v of exactly two layers"
    tm = TOKEN_TILE
    n_c, n_d = xc.shape[0], xd.shape[0]
    nct, ndt = n_c // tm, n_d // tm
    seq_tiles = dec_seq // tm
    ctx_i = lambda i: jnp.minimum(i, nct - 1)
    den_i = lambda i: jnp.maximum(i - nct, 0)
    row_c = lambda w: pl.BlockSpec((tm, w), lambda i: (ctx_i(i), 0))
    row_d = lambda w: pl.BlockSpec((tm, w), lambda i: (den_i(i), 0))
    tab = pl.BlockSpec((tm, QK_W), lambda i: (den_i(i) % seq_tiles, 0))
    in_specs = [
        row_c(D_MODEL), row_d(D_MODEL),
        pl.BlockSpec((1, 1, 1, 3 * D_MODEL), lambda i: (layer, 0, 0, 0)),
        pl.BlockSpec((1, 1, 1, 3 * D_MODEL), lambda i: (layer, 1 + den_i(i) // seq_tiles, 0, 0)),
        pl.BlockSpec((1, D_MODEL), lambda i: (0, 0)),
        pl.BlockSpec((1, D_MODEL, IN_DIM), lambda i: (layer, 0, 0)),
        tab, tab, tab,
    ]
    args = [xc, xd, mod, mod, norm_g.reshape(1, D_MODEL), w_in, *rope_tabs]
    widths = (256, 256, 256, 256, 512, 512, 512, 512)
    seqs = tm // seq
    if prev_kv is None:
        kv_blk = row_c(ATTN_W)
        kv_shape = jax.ShapeDtypeStruct((n_c, ATTN_W), F32)
    else:
        kv_blk = pl.BlockSpec((seqs, DEPTH, seq * N_HEADS, QK_W), lambda i: (ctx_i(i), 0, 0, 0))
        kv_shape = jax.ShapeDtypeStruct((n_c // seq, DEPTH, seq * N_HEADS, QK_W), F32)
        in_specs += [row_c(ATTN_W)] * 2
        args += list(prev_kv)
    out_specs = [row_c(w) for w in widths] + [kv_blk, kv_blk] + [row_d(w) for w in widths]
    out_shape = ([jax.ShapeDtypeStruct((n_c, w), BF16) for w in widths] + [kv_shape, kv_shape]
                 + [jax.ShapeDtypeStruct((n_d, w), BF16) for w in widths])
    fa_d = N_PROJ_OUTS + 2
    out_specs[fa_d] = pl.BlockSpec((tm // FFT_RADIX, FFT_RADIX * FOURIER_W), lambda i: (den_i(i), 0))
    out_shape[fa_d] = jax.ShapeDtypeStruct((n_d // FFT_RADIX, FFT_RADIX * FOURIER_W), BF16)
    outs = pl.pallas_call(
        functools.partial(_inproj_kernel, n_ctx_tiles=nct, n_prev=0 if prev_kv is None else 2),
        grid=(nct + ndt,),
        in_specs=in_specs,
        out_specs=out_specs,
        out_shape=out_shape,
        scratch_shapes=[pltpu.VMEM((D_MODEL, IN_DIM), BF16),
                        pltpu.VMEM((FOURIER_W // LANES, tm, LANES), F32)],
        compiler_params=_cparams(("arbitrary",)),
        name="in_projection",
    )(*args)
    return outs[:N_PROJ_OUTS], outs[N_PROJ_OUTS], outs[N_PROJ_OUTS + 1], outs[fa_d:]


def _lambda(lam_ref, lam_init):
    lv = lam_ref[...]
    a = jnp.sum(lv[0:1] * lv[1:2], axis=-1, keepdims=True)
    b = jnp.sum(lv[2:3] * lv[3:4], axis=-1, keepdims=True)
    return jnp.exp(a) - jnp.exp(b) + lam_init


def _stack_masked(q_t):
    first = lax.broadcasted_iota(jnp.int32, q_t.shape, 0) < HEAD_DIM
    return jnp.concatenate([jnp.where(first, q_t, 0.0), jnp.where(first, 0.0, q_t)],
                           axis=1).astype(BF16)


def _scores_chunk(k, qq_t, m):
    s = jnp.dot(k, qq_t, preferred_element_type=F32)
    mc = jnp.max(s, axis=0, keepdims=True)
    return s, (mc if m is None else jnp.maximum(m, mc))


def _with_ones_rows(v_t):
    return jnp.concatenate([v_t, jnp.ones((ONES_ROWS, v_t.shape[1]), F32)], axis=0).astype(BF16)


def _values_chunk(s, m, v1_t, acc):
    e = jnp.exp2(s - m).astype(BF16)
    oc = jnp.dot(v1_t, e, preferred_element_type=F32)
    return oc if acc is None else acc + oc


def _combine_t(acc, lam):
    tq = acc.shape[1] // 2
    r = 1.0 / acc[QK_W:QK_W + 1, :]
    o_t = acc[:QK_W, :]
    return o_t[:, :tq] * r[:, :tq] - o_t[:, tq:] * (r[:, tq:] * lam)


def _subln_gate(o, sg, sgc, lam_init):
    ms = jnp.mean(o * o, axis=-1, keepdims=True)
    y = o * lax.rsqrt(ms + EPS) * sg * (1.0 - lam_init)
    return (y * sgc.astype(F32)).astype(BF16)


def _attn_ctx_kernel(lam_ref, sg_ref, q_ref, k_ref, v_ref, sgc_ref, o_ref, *, lam_init, seq):
    lam = _lambda(lam_ref, lam_init)
    sg = sg_ref[...]
    n_seq = q_ref.shape[0] // seq
    q_t = q_ref[...].astype(F32).T
    v_t = v_ref[...].astype(F32).T
    pairs = [(slice(s * seq, (s + 1) * seq), slice(hd * QK_W, (hd + 1) * QK_W))
             for s in range(n_seq) for hd in range(N_HEADS)]
    qqs = [_stack_masked(q_t[hl, rows]) for rows, hl in pairs]
    v1s = [_with_ones_rows(v_t[hl, rows]) for rows, hl in pairs]
    sm = [_scores_chunk(k_ref[rows, hl], qq, None) for (rows, hl), qq in zip(pairs, qqs)]
    accs = [_values_chunk(s, m, v1, None) for (s, m), v1 in zip(sm, v1s)]
    y_t = [_combine_t(a, lam) for a in accs]
    for s in range(n_seq):
        rows = slice(s * seq, (s + 1) * seq)
        y = jnp.concatenate(y_t[s * N_HEADS:(s + 1) * N_HEADS], axis=0).T
        for hd in range(N_HEADS):
            hl = slice(hd * QK_W, (hd + 1) * QK_W)
            o_ref[rows, hl] = _subln_gate(y[:, hl], sg, sgc_ref[rows, hl], lam_init)


def _attn_den_kernel(lam_ref, sg_ref, q_ref, kn_ref, vn_ref, kc_ref, vc_ref, sgc_ref, o_ref,
                     kcs_ref, vt_ref, qq0_ref, qq1_ref, s0_ref, s1_ref, acc0_ref, acc1_ref,
                     *, lam_init):
    seq = q_ref.shape[0]
    past = kcs_ref.shape[1]
    past_chunks = past // KEY_CHUNK
    n_chunks = (past + seq) // KEY_CHUNK
    n_tiles = seq // Q_TILE
    n_elems = n_tiles * N_HEADS
    qq_refs = (qq0_ref, qq1_ref)
    s_refs = (s0_ref, s1_ref)
    acc_refs = (acc0_ref, acc1_ref)
    lam = _lambda(lam_ref, lam_init)
    sg = sg_ref[...]

    def lanes(hd):
        return slice(hd * QK_W, (hd + 1) * QK_W)

    def chunk(c):
        return slice(c * KEY_CHUNK, (c + 1) * KEY_CHUNK)

    def tile_rows(t):
        if isinstance(t, int):
            return pl.ds(t * Q_TILE, Q_TILE)
        return pl.ds(pl.multiple_of(t * Q_TILE, Q_TILE), Q_TILE)

    for hd in range(N_HEADS):
        cache_rows = pl.ds(hd, past, stride=N_HEADS)
        kcs_ref[hd] = kc_ref[0, 0, cache_rows, :].astype(BF16)
        vt_ref[hd, :, :past] = _with_ones_rows(vc_ref[0, 0, cache_rows, :].T)
        vt_ref[hd, :, past:] = _with_ones_rows(vn_ref[:, lanes(hd)].astype(F32).T)

    def keys(hd, c):
        if c < past_chunks:
            return kcs_ref[hd, chunk(c), :]
        return kn_ref[chunk(c - past_chunks), lanes(hd)]

    def prep(t, hd, par):
        qq_refs[par][...] = _stack_masked(q_ref[tile_rows(t), lanes(hd)].astype(F32).T)

    def finish(t, hd, par):
        rows = tile_rows(t)
        y = _combine_t(acc_refs[par][...], lam).T
        o_ref[rows, lanes(hd)] = _subln_gate(y, sg, sgc_ref[rows, lanes(hd)], lam_init)

    def step(t, j, m_cur, first=False, last=False):
        def elem(off):
            return t + (j + off) // N_HEADS, (j + off) % N_HEADS

        par = j % 2
        e_static = N_HEADS * t + j if isinstance(t, int) else None
        do_finish = not (first and j == 0)
        do_scores = not (last and e_static + 1 >= n_elems)
        do_prep = not (last and e_static + 2 >= n_elems)
        if do_prep:
            prep(*elem(2), par)
        hd_nxt = elem(1)[1]
        qq = qq_refs[1 - par][...] if do_scores else None
        m_nxt = None
        acc = None
        for c in range(n_chunks):
            if do_scores:
                s, m_nxt = _scores_chunk(keys(hd_nxt, c), qq, m_nxt)
                s_refs[1 - par][chunk(c), :] = s
            acc = _values_chunk(s_refs[par][chunk(c), :], m_cur, vt_ref[j, :, chunk(c)], acc)
        acc_refs[par][...] = acc
        if do_finish:
            finish(*elem(-1), 1 - par)
        return m_nxt

    def tile_steps(t, m, **edge):
        for j in range(N_HEADS):
            m = step(t, j, m, **edge)
        return m

    prep(0, 0, 0)
    prep(0, 1, 1)
    qq = qq0_ref[...]
    m = None
    for c in range(n_chunks):
        s, m = _scores_chunk(keys(0, c), qq, m)
        s0_ref[chunk(c), :] = s
    m = tile_steps(0, m, first=True)
    m = lax.fori_loop(1, n_tiles - 1, tile_steps, m)
    tile_steps(n_tiles - 1, m, last=True)
    finish(n_tiles - 1, N_HEADS - 1, (n_elems - 1) % 2)


def _attention_den(q, k, v, sgc, cache_k, cache_v, lam_vec, subln_g, *, layer, seq, lam_init):
    n_tok = q.shape[0]
    past = cache_k.shape[2] // N_HEADS
    blk = pl.BlockSpec((seq, ATTN_W), lambda b: (b, 0))
    cblk = pl.BlockSpec((1, 1, past * N_HEADS, QK_W), lambda b: (b, layer, 0, 0))
    return pl.pallas_call(
        functools.partial(_attn_den_kernel, lam_init=lam_init),
        grid=(n_tok // seq,),
        in_specs=[
            pl.BlockSpec((4, HEAD_DIM), lambda b: (0, 0)),
            pl.BlockSpec((1, QK_W), lambda b: (0, 0)),
            blk, blk, blk, cblk, cblk, blk,
        ],
        out_specs=blk,
        out_shape=jax.ShapeDtypeStruct((n_tok, ATTN_W), BF16),
        scratch_shapes=[
            pltpu.VMEM((N_HEADS, past, QK_W), BF16),
            pltpu.VMEM((N_HEADS, QK_W + ONES_ROWS, past + seq), BF16),
            pltpu.VMEM((QK_W, 2 * Q_TILE), BF16),
            pltpu.VMEM((QK_W, 2 * Q_TILE), BF16),
            pltpu.VMEM((past + seq, 2 * Q_TILE), F32),
            pltpu.VMEM((past + seq, 2 * Q_TILE), F32),
            pltpu.VMEM((QK_W + ONES_ROWS, 2 * Q_TILE), F32),
            pltpu.VMEM((QK_W + ONES_ROWS, 2 * Q_TILE), F32),
        ],
        compiler_params=_cparams(("arbitrary",)),
        name="attention_den",
    )(lam_vec, subln_g.reshape(1, QK_W), q, k, v, cache_k, cache_v, sgc)


def _fourier_ctx_kernel(fa_ref, sga_ref, cs_ref, dn_ref, o_ref, *, seq):
    ab = jnp.dot(fa_ref[...], cs_ref[...], preferred_element_type=F32).astype(BF16)
    for s in range(fa_ref.shape[0] // seq):
        rows = slice(s * seq, (s + 1) * seq)
        ab2 = jnp.concatenate([ab[rows, :FOURIER_W], ab[rows, FOURIER_W:]], axis=0)
        f = jnp.dot(dn_ref[...], ab2, preferred_element_type=F32)
        o_ref[rows, :] = (f * sga_ref[rows, :].astype(F32)).astype(BF16)


def _ctx_mixers_kernel(lam_ref, sg_ref, q_ref, k_ref, v_ref, sgc_ref, fa_ref, sga_ref, cs_ref, dn_ref,
                       yc_ref, ya_ref, *, lam_init, seq):
    _attn_ctx_kernel(lam_ref, sg_ref, q_ref, k_ref, v_ref, sgc_ref, yc_ref, lam_init=lam_init, seq=seq)
    _fourier_ctx_kernel(fa_ref, sga_ref, cs_ref, dn_ref, ya_ref, seq=seq)


def _ctx_mixers(q, k, v, sgc, fa, sga, lam_vec, subln_g, cs, dn, *, seq, lam_init):
    n_tok = q.shape[0]
    rows = CTX_SEQS * seq
    blk = lambda w: pl.BlockSpec((rows, w), lambda b: (b, 0))
    const = lambda shape: pl.BlockSpec(shape, lambda b: (0, 0))
    return pl.pallas_call(
        functools.partial(_ctx_mixers_kernel, lam_init=lam_init, seq=seq),
        grid=(n_tok // rows,),
        in_specs=[
            const((4, HEAD_DIM)), const((1, QK_W)),
            blk(ATTN_W), blk(ATTN_W), blk(ATTN_W), blk(ATTN_W),
            blk(FOURIER_W), blk(FOURIER_W),
            const((FOURIER_W, 2 * FOURIER_W)), const((seq, 2 * seq)),
        ],
        out_specs=[blk(ATTN_W), blk(FOURIER_W)],
        out_shape=[jax.ShapeDtypeStruct((n_tok, ATTN_W), BF16),
                   jax.ShapeDtypeStruct((n_tok, FOURIER_W), BF16)],
        compiler_params=_cparams(("arbitrary",)),
        name="ctx_mixers",
    )(lam_vec, subln_g.reshape(1, QK_W), q, k, v, sgc, fa, sga, cs, dn)


def _cadd(a, b):
    return (a[0] + b[0], a[1] + b[1])


def _csub(a, b):
    return (a[0] - b[0], a[1] - b[1])


def _mul_neg_i(a):
    return (a[1], -a[0])


def _mul_w8_1(a):
    return ((a[0] + a[1]) * SQRT_HALF, (a[1] - a[0]) * SQRT_HALF)


def _mul_w8_3(a):
    return ((a[1] - a[0]) * SQRT_HALF, (-a[0] - a[1]) * SQRT_HALF)


def _fft4(a0, a1, a2, a3):
    e0, e1 = _cadd(a0, a2), _csub(a0, a2)
    o0, o1 = _cadd(a1, a3), _mul_neg_i(_csub(a1, a3))
    return [_cadd(e0, o0), _cadd(e1, o1), _csub(e0, o0), _csub(e1, o1)]


def _fft8(x):
    e = _fft4(x[0], x[2], x[4], x[6])
    o = _fft4(x[1], x[3], x[5], x[7])
    t = [o[0], _mul_w8_1(o[1]), _mul_neg_i(o[2]), _mul_w8_3(o[3])]
    return [_cadd(e[k], t[k]) for k in range(4)] + [_csub(e[k], t[k]) for k in range(4)]


def _fourier_den_kernel(xw_ref, sga_ref, f1_ref, twc_ref, tws_ref, cs2_ref, o_ref, g_ref, x_ref):
    n2 = xw_ref.shape[0]
    g_ref[...] = jnp.dot(f1_ref[...], xw_ref[...], preferred_element_type=F32)

    def chunk(i, carry):
        r = pl.multiple_of(i * FFT_ROWS, FFT_ROWS)
        re_rows = pl.ds(r, FFT_ROWS)
        im_rows = pl.ds(n2 + r, FFT_ROWS)
        xs = []
        for n1 in range(FFT_RADIX):
            lanes = slice(n1 * FOURIER_W, (n1 + 1) * FOURIER_W)
            gr = g_ref[re_rows, lanes]
            gi = g_ref[im_rows, lanes]
            if n1 > 0:
                c = twc_ref[re_rows, lanes]
                s = tws_ref[re_rows, lanes]
                gr, gi = gr * c + gi * s, gi * c - gr * s
            xs.append((gr, gi))
        for k1, (xr, xi) in enumerate(_fft8(xs)):
            out_rows = pl.ds(k1 * n2 + r, FFT_ROWS)
            x_ref[out_rows, :FOURIER_W] = xr.astype(BF16)
            x_ref[out_rows, FOURIER_W:] = xi.astype(BF16)
        return carry

    lax.fori_loop(0, n2 // FFT_ROWS, chunk, 0)
    f = jnp.dot(x_ref[...], cs2_ref[...], preferred_element_type=F32)
    o_ref[...] = (f * sga_ref[...].astype(F32)).astype(BF16)


def _fourier_den(xw, sga, f1, twc, tws, cs2, *, seq):
    n_tok = sga.shape[0]
    n2 = seq // FFT_RADIX
    wide = FFT_RADIX * FOURIER_W
    const = lambda shape: pl.BlockSpec(shape, lambda b: (0, 0))
    return pl.pallas_call(
        _fourier_den_kernel,
        grid=(n_tok // seq,),
        in_specs=[
            pl.BlockSpec((n2, wide), lambda b: (b, 0)),
            pl.BlockSpec((seq, FOURIER_W), lambda b: (b, 0)),
            const((2 * n2, n2)), const((n2, wide)), const((n2, wide)),
            const((2 * FOURIER_W, FOURIER_W)),
        ],
        out_specs=pl.BlockSpec((seq, FOURIER_W), lambda b: (b, 0)),
        out_shape=jax.ShapeDtypeStruct((n_tok, FOURIER_W), BF16),
        scratch_shapes=[pltpu.VMEM((2 * n2, wide), F32),
                        pltpu.VMEM((seq, 2 * FOURIER_W), BF16)],
        compiler_params=_cparams(("arbitrary",)),
        name="fourier_den",
    )(xw, sga, f1, twc, tws, cs2)


def _outproj_tile(x_ref, mod_ref, ya_ref, zc_ref, zp_ref, zn_ref, bgs_ref, yc_ref,
                  cw_ref, cb_ref, w_ref, fg_ref, o_ref, tile, *, seq, final_norm):
    tm = x_ref.shape[0]
    z = zc_ref[...].astype(F32)
    row = lax.broadcasted_iota(jnp.int32, z.shape, 0)
    pos = (tile * tm + row) & (seq - 1)
    prev_row = zp_ref[HALO_ROWS - 1:HALO_ROWS, :].astype(F32)
    next_row = zn_ref[0:1, :].astype(F32)
    z_prev = jnp.where(row == 0, prev_row, pltpu.roll(z, 1, axis=0))
    z_prev = jnp.where(pos == 0, 0.0, z_prev)
    z_next = jnp.where(row == tm - 1, next_row, pltpu.roll(z, tm - 1, axis=0))
    z_next = jnp.where(pos == seq - 1, 0.0, z_next)
    cw = cw_ref[...]
    conv = z_prev * cw[0:1] + z * cw[1:2] + z_next * cw[2:3] + cb_ref[...]
    yb = (bgs_ref[...].astype(F32) * conv).astype(BF16)

    mixed = jnp.concatenate([ya_ref[...], yb, yc_ref[...]], axis=-1)
    out = jnp.dot(mixed, w_ref[...], preferred_element_type=F32)
    gate = mod_ref[0, 0][:, 2 * D_MODEL:]
    xn = x_ref[...] + gate * out
    if final_norm:
        ms = jnp.mean(xn * xn, axis=-1, keepdims=True)
        xn = xn * lax.rsqrt(ms + EPS) * fg_ref[...]
    o_ref[...] = xn


N_GROUP_INS = 8


def _outproj_kernel(*refs, n_ctx_tiles, seq, dec_seq, final_norm):
    ctx_ins = refs[:N_GROUP_INS]
    den_ins = refs[N_GROUP_INS:2 * N_GROUP_INS]
    cw_ref, cb_ref, w_ref, fg_ref, oc_ref, od_ref, wb_ref = refs[2 * N_GROUP_INS:]
    i = pl.program_id(0)

    @pl.when(i == 0)
    def _():
        wb_ref[...] = w_ref[0].astype(BF16)

    @pl.when(i < n_ctx_tiles)
    def _():
        _outproj_tile(*ctx_ins, cw_ref, cb_ref, wb_ref, fg_ref, oc_ref, i, seq=seq,
                      final_norm=final_norm)

    @pl.when(i >= n_ctx_tiles)
    def _():
        _outproj_tile(*den_ins, cw_ref, cb_ref, wb_ref, fg_ref, od_ref, i - n_ctx_tiles,
                      seq=dec_seq, final_norm=final_norm)


def _out_projection(ctx, den, mod, conv_w, conv_b, w_out, layer, final_g, *, seq, dec_seq,
                    final_norm):
    tm = OUT_TILE
    nct = ctx[0].shape[0] // tm
    ndt = den[0].shape[0] // tm
    halo_per_tile = tm // HALO_ROWS
    seq_tiles = dec_seq // tm

    def group_specs(tile_of, n_tok, mod_of):
        n_halo = n_tok // HALO_ROWS
        row = lambda w: pl.BlockSpec((tm, w), lambda i: (tile_of(i), 0))
        return [
            row(D_MODEL),
            pl.BlockSpec((1, 1, 1, 3 * D_MODEL), lambda i: (layer, mod_of(tile_of(i)), 0, 0)),
            row(FOURIER_W),
            row(CONV_W),
            pl.BlockSpec((HALO_ROWS, CONV_W),
                         lambda i: (jnp.maximum(tile_of(i) * halo_per_tile - 1, 0), 0)),
            pl.BlockSpec((HALO_ROWS, CONV_W),
                         lambda i: (jnp.minimum((tile_of(i) + 1) * halo_per_tile, n_halo - 1), 0)),
            row(CONV_W),
            row(ATTN_W),
        ], row(D_MODEL)

    ctx_specs, ctx_out = group_specs(lambda i: jnp.minimum(i, nct - 1), ctx[0].shape[0],
                                     lambda t: 0)
    den_specs, den_out = group_specs(lambda i: jnp.maximum(i - nct, 0), den[0].shape[0],
                                     lambda t: 1 + t // seq_tiles)

    def group_args(g):
        x, ya, zc, bgs, yc = g
        return [x, mod, ya, zc, zc, zc, bgs, yc]

    return pl.pallas_call(
        functools.partial(_outproj_kernel, n_ctx_tiles=nct, seq=seq, dec_seq=dec_seq,
                          final_norm=final_norm),
        grid=(nct + ndt,),
        in_specs=ctx_specs + den_specs + [
            pl.BlockSpec((3, CONV_W), lambda i: (0, 0)),
            pl.BlockSpec((1, CONV_W), lambda i: (0, 0)),
            pl.BlockSpec((1, D_MODEL, D_MODEL), lambda i: (layer, 0, 0)),
            pl.BlockSpec((1, D_MODEL), lambda i: (0, 0)),
        ],
        out_specs=[ctx_out, den_out],
        out_shape=[jax.ShapeDtypeStruct(ctx[0].shape, F32), jax.ShapeDtypeStruct(den[0].shape, F32)],
        scratch_shapes=[pltpu.VMEM((D_MODEL, D_MODEL), BF16)],
        compiler_params=_cparams(("arbitrary",)),
        name="out_projection",
    )(*group_args(ctx), *group_args(den), conv_w, conv_b.reshape(1, CONV_W), w_out,
      final_g.reshape(1, D_MODEL))


def kernel(x_prompt, x_sample, cache_k, cache_v, c, c_ctx, norm_g, w_mod, b_mod, w_in, conv_w,
           conv_b, lam_vec, subln_g, w_out, final_g):
    batch, seq, _ = x_prompt.shape
    dec_batch, dec_seq, _ = x_sample.shape

    mod = _modulation(c_ctx, c, w_mod, b_mod)

    rope_tabs = tuple(jnp.asarray(t) for t in _rope_tables(dec_seq))
    cs = jnp.asarray(_chan_tables()).astype(BF16)
    dn_ctx = jnp.asarray(_dft_tables(seq)).astype(BF16)
    f1, twc, tws, cs2 = (jnp.asarray(t) for t in _ct_tables(dec_seq))
    f1 = f1.astype(BF16)
    cs2 = cs2.astype(BF16)

    cache_k2 = cache_k.reshape(dec_batch, DEPTH, -1, QK_W)
    cache_v2 = cache_v.reshape(dec_batch, DEPTH, -1, QK_W)
    xc = x_prompt.reshape(batch * seq, D_MODEL)
    xl = x_sample.reshape(dec_batch * dec_seq, D_MODEL)
    kv = None
    for l in range(DEPTH):
        lam_init = 0.8 - 0.6 * math.exp(-0.3 * l)
        ctx, k32, v32, den = _in_projection(
            xc, xl, mod, norm_g[l], w_in, l, rope_tabs, seq=seq, dec_seq=dec_seq,
            prev_kv=kv)
        kv = (k32, v32)

        fa, sga, zc_c, bgs_c, q, k, v, sgc = ctx
        yc_c, ya_c = _ctx_mixers(q, k, v, sgc, fa, sga, lam_vec[l], subln_g[l], cs, dn_ctx,
                                 seq=seq, lam_init=lam_init)

        fa, sga, zc_d, bgs_d, q, k, v, sgc = den
        yc_d = _attention_den(q, k, v, sgc, cache_k2, cache_v2, lam_vec[l], subln_g[l], layer=l,
                              seq=dec_seq, lam_init=lam_init)
        ya_d = _fourier_den(fa, sga, f1, twc, tws, cs2, seq=dec_seq)

        xc, xl = _out_projection(
            (xc, ya_c, zc_c, bgs_c, yc_c), (xl, ya_d, zc_d, bgs_d, yc_d), mod,
            conv_w[l], conv_b[l], w_out, l, final_g, seq=seq, dec_seq=dec_seq,
            final_norm=l == DEPTH - 1)

    y_prompt = xc.reshape(batch, seq, D_MODEL)
    y_sample = xl.reshape(dec_batch, dec_seq, D_MODEL)
    return (y_prompt, y_sample, *(t.reshape(batch, DEPTH, seq, N_HEADS, QK_W) for t in kv))
```

```python
import functools
import math

import numpy as np
import jax
import jax.numpy as jnp
from jax import lax
from jax.experimental import pallas as pl
from jax.experimental.pallas import tpu as pltpu

D_MODEL = 1024
DEPTH = 2
GRID_W = 64
FOURIER_W = 256
CONV_W = 256
ATTN_W = 512
N_HEADS = 4
HEAD_DIM = 64
QK_W = 128
ROPE_BASE = 10000.0
ROPE_W = HEAD_DIM // 2
ROPE_ROT = ROPE_W // 2
EPS = 1e-6
IN_DIM = 3584

F32 = jnp.float32
BF16 = jnp.bfloat16

VMEM_LIMIT_BYTES = 60 * 1024 * 1024
TOKEN_TILE = 512
OUT_TILE = 1024
Q_TILE = 256
KEY_CHUNK = 256
ONES_ROWS = 16
LANES = 128
FFT_RADIX = 8
FFT_ROWS = 16
SQRT_HALF = 0.7071067811865476
CTX_SEQS = 2
HALO_ROWS = 16
MOD_ROWS = 8
MOD_CHUNK = 3 * D_MODEL
LOG2E = 1.4426950408889634

_COLS = {}
_off = 0
for _name, _w in (("fa", 256), ("ga", 256), ("bg", 256), ("cg", 256), ("hc", 256), ("gb", 256),
                  ("q", 512), ("k", 512), ("v", 512), ("gc", 512)):
    _COLS[_name] = (_off, _off + _w)
    _off += _w


def _silu(x):
    return x * (1.0 / (1.0 + jnp.exp(-x)))


def _cparams(sem):
    return pltpu.CompilerParams(dimension_semantics=sem, vmem_limit_bytes=VMEM_LIMIT_BYTES)


def _rope_tables(n_tokens):
    n = np.arange(n_tokens)
    row = (n // GRID_W).astype(np.float64)
    col = (n % GRID_W).astype(np.float64)
    j = np.arange(QK_W)
    jj = j % HEAD_DIM
    idx = jj % ROPE_W
    inv = 1.0 / (ROPE_BASE ** (2.0 * (idx % ROPE_ROT) / ROPE_W))
    pos = np.where((jj < ROPE_W)[None, :], row[:, None], col[:, None])
    ang = pos * inv[None, :]
    cos = np.cos(ang)
    sin = np.sin(ang)
    first = (idx < ROPE_ROT)[None, :]
    sin_a = np.where(first, -sin, 0.0)
    sin_b = np.where(first, 0.0, sin)
    return (np.asarray(cos, np.float32), np.asarray(sin_a, np.float32), np.asarray(sin_b, np.float32))


def _dft_tables(n):
    k = np.arange(n)
    kn = (k[:, None] * k[None, :]) % n
    ang = 2.0 * np.pi * kn / n
    return np.asarray(np.concatenate([np.cos(ang), -np.sin(ang)], axis=1) / math.sqrt(n), np.float32)


def _chan_tables():
    k = np.arange(FOURIER_W)
    kn = (k[:, None] * k[None, :]) % FOURIER_W
    ang = 2.0 * np.pi * kn / FOURIER_W
    return np.asarray(np.concatenate([np.cos(ang), np.sin(ang)], axis=1) / math.sqrt(FOURIER_W), np.float32)


def _ct_tables(n):
    n2 = n // FFT_RADIX
    k = np.arange(n2)
    ang = 2.0 * np.pi * ((k[:, None] * k[None, :]) % n2) / n2
    f1 = np.concatenate([np.cos(ang), -np.sin(ang)], axis=0) / math.sqrt(n * FOURIER_W)
    tw = 2.0 * np.pi * k[:, None] * np.arange(FFT_RADIX)[None, :] / n
    twc = np.repeat(np.cos(tw), FOURIER_W, axis=1)
    tws = np.repeat(np.sin(tw), FOURIER_W, axis=1)
    c = np.arange(FOURIER_W)
    angc = 2.0 * np.pi * ((c[:, None] * c[None, :]) % FOURIER_W) / FOURIER_W
    cs2 = np.concatenate([np.cos(angc), np.sin(angc)], axis=0)
    return tuple(np.asarray(t, np.float32) for t in (f1, twc, tws, cs2))


def _mod_kernel(cctx_ref, c_ref, w_ref, b_ref, o_ref, s_ref):
    n_c = c_ref.shape[0]
    s_ref[...] = jnp.zeros_like(s_ref)
    s_ref[0:1, :] = _silu(cctx_ref[...])
    s_ref[1:1 + n_c, :] = _silu(c_ref[...])
    w = w_ref[0].astype(BF16)
    res = jnp.dot(s_ref[...].astype(BF16), w, preferred_element_type=F32) + b_ref[0]
    for r in range(MOD_ROWS):
        o_ref[0, r] = res[r:r + 1]


def _modulation(c_ctx, c, w_mod, b_mod):
    chunk = MOD_CHUNK
    n_chunks = 3 * D_MODEL // chunk
    return pl.pallas_call(
        _mod_kernel,
        grid=(DEPTH, n_chunks),
        in_specs=[
            pl.BlockSpec((1, D_MODEL), lambda l, j: (0, 0)),
            pl.BlockSpec(c.shape, lambda l, j: (0, 0)),
            pl.BlockSpec((1, D_MODEL, chunk), lambda l, j: (l, 0, j)),
            pl.BlockSpec((1, 1, chunk), lambda l, j: (l, 0, j)),
        ],
        out_specs=pl.BlockSpec((1, MOD_ROWS, 1, chunk), lambda l, j: (l, 0, 0, j)),
        out_shape=jax.ShapeDtypeStruct((DEPTH, MOD_ROWS, 1, 3 * D_MODEL), F32),
        scratch_shapes=[pltpu.VMEM((MOD_ROWS, D_MODEL), F32)],
        compiler_params=_cparams(("arbitrary", "arbitrary")),
        name="modulation",
    )(c_ctx.reshape(1, D_MODEL), c, w_mod, b_mod.reshape(DEPTH, 1, 3 * D_MODEL))


def _store_heads(ref, layer, t):
    seqs, _, rows, _ = ref.shape
    seq = rows // N_HEADS
    for s in range(seqs):
        for hd in range(N_HEADS):
            ref[s, layer, pl.ds(hd, seq, stride=N_HEADS), :] = (
                t[s * seq:(s + 1) * seq, hd * QK_W:(hd + 1) * QK_W])


def _inproj_tile(x_ref, mod_ref, g_ref, w_ref, rope_refs, outs, kv_refs, stage_ref):
    fa_ref, sga_ref, zc_ref, bgs_ref, q_ref, k_ref, v_ref, sgc_ref = outs

    x = x_ref[...]
    ms = jnp.mean(x * x, axis=-1, keepdims=True)
    y = x * lax.rsqrt(ms + EPS) * g_ref[...]
    m = mod_ref[0, 0]
    h = (y * (1.0 + m[:, D_MODEL:2 * D_MODEL]) + m[:, :D_MODEL]).astype(BF16)

    def proj(name):
        lo, hi = _COLS[name]
        return jnp.dot(h, w_ref[:, lo:hi], preferred_element_type=F32)

    if stage_ref is not None:
        fa = proj("fa")
        halves = stage_ref.shape[0]
        for hf in range(halves):
            stage_ref[hf] = fa[:, hf * LANES:(hf + 1) * LANES]
        rows = stage_ref.shape[1] // FFT_RADIX
        for n1 in range(FFT_RADIX):
            for hf in range(halves):
                lo = n1 * FOURIER_W + hf * LANES
                fa_ref[:, lo:lo + LANES] = (
                    stage_ref[hf, pl.ds(n1, rows, stride=FFT_RADIX), :].astype(BF16))
    else:
        fa_ref[...] = proj("fa").astype(BF16)
    sga_ref[...] = _silu(proj("ga")).astype(BF16)
    zc_ref[...] = (proj("cg") * proj("hc")).astype(BF16)
    bgs_ref[...] = (proj("bg") * _silu(proj("gb"))).astype(BF16)
    sgc_ref[...] = _silu(proj("gc")).astype(BF16)

    q = proj("q")
    k = proj("k")
    v = proj("v")
    if kv_refs is not None and len(kv_refs) == 2:
        kv_refs[0][...] = k
        kv_refs[1][...] = v
    elif kv_refs is not None:
        new_k_ref, new_v_ref, prev_k_ref, prev_v_ref = kv_refs
        _store_heads(new_k_ref, 0, prev_k_ref[...])
        _store_heads(new_v_ref, 0, prev_v_ref[...])
        _store_heads(new_k_ref, 1, k)
        _store_heads(new_v_ref, 1, v)
    v_ref[...] = v.astype(BF16)

    q_scale = HEAD_DIM ** -0.5 * LOG2E
    if rope_refs is not None:
        cos = rope_refs[0][...]
        sa = rope_refs[1][...]
        sb = rope_refs[2][...]
        for hd in range(N_HEADS):
            sl = slice(hd * QK_W, (hd + 1) * QK_W)
            for t, ref, scale in ((q, q_ref, q_scale), (k, k_ref, None)):
                th = t[:, sl]
                r = (th * cos + pltpu.roll(th, QK_W - ROPE_ROT, axis=1) * sa
                     + pltpu.roll(th, ROPE_ROT, axis=1) * sb)
                if scale is not None:
                    r = r * scale
                ref[:, sl] = r.astype(BF16)
    else:
        q_ref[...] = (q * q_scale).astype(BF16)
        k_ref[...] = k.astype(BF16)


N_PROJ_OUTS = 8


def _inproj_kernel(xc_ref, xd_ref, modc_ref, modd_ref, g_ref, w_ref, cos_ref, sa_ref, sb_ref,
                   *rest, n_ctx_tiles, n_prev):
    prev_kv = rest[:n_prev]
    rest = rest[n_prev:]
    ctx_outs = rest[:N_PROJ_OUTS]
    kv_refs = rest[N_PROJ_OUTS:N_PROJ_OUTS + 2] + prev_kv
    den_outs = rest[N_PROJ_OUTS + 2:2 * N_PROJ_OUTS + 2]
    wb_ref, stage_ref = rest[2 * N_PROJ_OUTS + 2:]
    i = pl.program_id(0)

    @pl.when(i == 0)
    def _():
        for lo, hi in _COLS.values():
            wb_ref[:, lo:hi] = w_ref[0, :, lo:hi].astype(BF16)

    @pl.when(i < n_ctx_tiles)
    def _():
        _inproj_tile(xc_ref, modc_ref, g_ref, wb_ref, None, ctx_outs, kv_refs, None)

    @pl.when(i >= n_ctx_tiles)
    def _():
        _inproj_tile(xd_ref, modd_ref, g_ref, wb_ref, (cos_ref, sa_ref, sb_ref), den_outs, None,
                     stage_ref)


def _in_projection(xc, xd, mod, norm_g, w_in, layer, rope_tabs, *, seq, dec_seq,
                   prev_kv):
    assert DEPTH == 2, "the last layer's call assembles the k/v of exactly two layers"
    tm = TOKEN_TILE
    n_c, n_d = xc.shape[0], xd.shape[0]
    nct, ndt = n_c // tm, n_d // tm
    seq_tiles = dec_seq // tm
    ctx_i = lambda i: jnp.minimum(i, nct - 1)
    den_i = lambda i: jnp.maximum(i - nct, 0)
    row_c = lambda w: pl.BlockSpec((tm, w), lambda i: (ctx_i(i), 0))
    row_d = lambda w: pl.BlockSpec((tm, w), lambda i: (den_i(i), 0))
    tab = pl.BlockSpec((tm, QK_W), lambda i: (den_i(i) % seq_tiles, 0))
    in_specs = [
        row_c(D_MODEL), row_d(D_MODEL),
        pl.BlockSpec((1, 1, 1, 3 * D_MODEL), lambda i: (layer, 0, 0, 0)),
        pl.BlockSpec((1, 1, 1, 3 * D_MODEL), lambda i: (layer, 1 + den_i(i) // seq_tiles, 0, 0)),
        pl.BlockSpec((1, D_MODEL), lambda i: (0, 0)),
        pl.BlockSpec((1, D_MODEL, IN_DIM), lambda i: (layer, 0, 0)),
        tab, tab, tab,
    ]
    args = [xc, xd, mod, mod, norm_g.reshape(1, D_MODEL), w_in, *rope_tabs]
    widths = (256, 256, 256, 256, 512, 512, 512, 512)
    seqs = tm // seq
    if prev_kv is None:
        kv_blk = row_c(ATTN_W)
        kv_shape = jax.ShapeDtypeStruct((n_c, ATTN_W), F32)
    else:
        kv_blk = pl.BlockSpec((seqs, DEPTH, seq * N_HEADS, QK_W), lambda i: (ctx_i(i), 0, 0, 0))
        kv_shape = jax.ShapeDtypeStruct((n_c // seq, DEPTH, seq * N_HEADS, QK_W), F32)
        in_specs += [row_c(ATTN_W)] * 2
        args += list(prev_kv)
    out_specs = [row_c(w) for w in widths] + [kv_blk, kv_blk] + [row_d(w) for w in widths]
    out_shape = ([jax.ShapeDtypeStruct((n_c, w), BF16) for w in widths] + [kv_shape, kv_shape]
                 + [jax.ShapeDtypeStruct((n_d, w), BF16) for w in widths])
    fa_d = N_PROJ_OUTS + 2
    out_specs[fa_d] = pl.BlockSpec((tm // FFT_RADIX, FFT_RADIX * FOURIER_W), lambda i: (den_i(i), 0))
    out_shape[fa_d] = jax.ShapeDtypeStruct((n_d // FFT_RADIX, FFT_RADIX * FOURIER_W), BF16)
    outs = pl.pallas_call(
        functools.partial(_inproj_kernel, n_ctx_tiles=nct, n_prev=0 if prev_kv is None else 2),
        grid=(nct + ndt,),
        in_specs=in_specs,
        out_specs=out_specs,
        out_shape=out_shape,
        scratch_shapes=[pltpu.VMEM((D_MODEL, IN_DIM), BF16),
                        pltpu.VMEM((FOURIER_W // LANES, tm, LANES), F32)],
        compiler_params=_cparams(("arbitrary",)),
        name="in_projection",
    )(*args)
    return outs[:N_PROJ_OUTS], outs[N_PROJ_OUTS], outs[N_PROJ_OUTS + 1], outs[fa_d:]


def _lambda(lam_ref, lam_init):
    lv = lam_ref[...]
    a = jnp.sum(lv[0:1] * lv[1:2], axis=-1, keepdims=True)
    b = jnp.sum(lv[2:3] * lv[3:4], axis=-1, keepdims=True)
    return jnp.exp(a) - jnp.exp(b) + lam_init


def _stack_masked(q_t):
    first = lax.broadcasted_iota(jnp.int32, q_t.shape, 0) < HEAD_DIM
    return jnp.concatenate([jnp.where(first, q_t, 0.0), jnp.where(first, 0.0, q_t)],
                           axis=1).astype(BF16)


def _scores_chunk(k, qq_t, m):
    s = jnp.dot(k, qq_t, preferred_element_type=F32)
    mc = jnp.max(s, axis=0, keepdims=True)
    return s, (mc if m is None else jnp.maximum(m, mc))


def _with_ones_rows(v_t):
    return jnp.concatenate([v_t, jnp.ones((ONES_ROWS, v_t.shape[1]), F32)], axis=0).astype(BF16)


def _values_chunk(s, m, v1_t, acc):
    e = jnp.exp2(s - m).astype(BF16)
    oc = jnp.dot(v1_t, e, preferred_element_type=F32)
    return oc if acc is None else acc + oc


def _combine_t(acc, lam):
    tq = acc.shape[1] // 2
    r = 1.0 / acc[QK_W:QK_W + 1, :]
    o_t = acc[:QK_W, :]
    return o_t[:, :tq] * r[:, :tq] - o_t[:, tq:] * (r[:, tq:] * lam)


def _subln_gate(o, sg, sgc, lam_init):
    ms = jnp.mean(o * o, axis=-1, keepdims=True)
    y = o * lax.rsqrt(ms + EPS) * sg * (1.0 - lam_init)
    return (y * sgc.astype(F32)).astype(BF16)


def _attn_ctx_kernel(lam_ref, sg_ref, q_ref, k_ref, v_ref, sgc_ref, o_ref, *, lam_init, seq):
    lam = _lambda(lam_ref, lam_init)
    sg = sg_ref[...]
    n_seq = q_ref.shape[0] // seq
    q_t = q_ref[...].astype(F32).T
    v_t = v_ref[...].astype(F32).T
    pairs = [(slice(s * seq, (s + 1) * seq), slice(hd * QK_W, (hd + 1) * QK_W))
             for s in range(n_seq) for hd in range(N_HEADS)]
    qqs = [_stack_masked(q_t[hl, rows]) for rows, hl in pairs]
    v1s = [_with_ones_rows(v_t[hl, rows]) for rows, hl in pairs]
    sm = [_scores_chunk(k_ref[rows, hl], qq, None) for (rows, hl), qq in zip(pairs, qqs)]
    accs = [_values_chunk(s, m, v1, None) for (s, m), v1 in zip(sm, v1s)]
    y_t = [_combine_t(a, lam) for a in accs]
    for s in range(n_seq):
        rows = slice(s * seq, (s + 1) * seq)
        y = jnp.concatenate(y_t[s * N_HEADS:(s + 1) * N_HEADS], axis=0).T
        for hd in range(N_HEADS):
            hl = slice(hd * QK_W, (hd + 1) * QK_W)
            o_ref[rows, hl] = _subln_gate(y[:, hl], sg, sgc_ref[rows, hl], lam_init)


def _attn_den_kernel(lam_ref, sg_ref, q_ref, kn_ref, vn_ref, kc_ref, vc_ref, sgc_ref, o_ref,
                     kcs_ref, vt_ref, qq0_ref, qq1_ref, s0_ref, s1_ref, acc0_ref, acc1_ref,
                     *, lam_init):
    seq = q_ref.shape[0]
    past = kcs_ref.shape[1]
    past_chunks = past // KEY_CHUNK
    n_chunks = (past + seq) // KEY_CHUNK
    n_tiles = seq // Q_TILE
    n_elems = n_tiles * N_HEADS
    qq_refs = (qq0_ref, qq1_ref)
    s_refs = (s0_ref, s1_ref)
    acc_refs = (acc0_ref, acc1_ref)
    lam = _lambda(lam_ref, lam_init)
    sg = sg_ref[...]

    def lanes(hd):
        return slice(hd * QK_W, (hd + 1) * QK_W)

    def chunk(c):
        return slice(c * KEY_CHUNK, (c + 1) * KEY_CHUNK)

    def tile_rows(t):
        if isinstance(t, int):
            return pl.ds(t * Q_TILE, Q_TILE)
        return pl.ds(pl.multiple_of(t * Q_TILE, Q_TILE), Q_TILE)

    for hd in range(N_HEADS):
        cache_rows = pl.ds(hd, past, stride=N_HEADS)
        kcs_ref[hd] = kc_ref[0, 0, cache_rows, :].astype(BF16)
        vt_ref[hd, :, :past] = _with_ones_rows(vc_ref[0, 0, cache_rows, :].T)
        vt_ref[hd, :, past:] = _with_ones_rows(vn_ref[:, lanes(hd)].astype(F32).T)

    def keys(hd, c):
        if c < past_chunks:
            return kcs_ref[hd, chunk(c), :]
        return kn_ref[chunk(c - past_chunks), lanes(hd)]

    def prep(t, hd, par):
        qq_refs[par][...] = _stack_masked(q_ref[tile_rows(t), lanes(hd)].astype(F32).T)

    def finish(t, hd, par):
        rows = tile_rows(t)
        y = _combine_t(acc_refs[par][...], lam).T
        o_ref[rows, lanes(hd)] = _subln_gate(y, sg, sgc_ref[rows, lanes(hd)], lam_init)

    def step(t, j, m_cur, first=False, last=False):
        def elem(off):
            return t + (j + off) // N_HEADS, (j + off) % N_HEADS

        par = j % 2
        e_static = N_HEADS * t + j if isinstance(t, int) else None
        do_finish = not (first and j == 0)
        do_scores = not (last and e_static + 1 >= n_elems)
        do_prep = not (last and e_static + 2 >= n_elems)
        if do_prep:
            prep(*elem(2), par)
        hd_nxt = elem(1)[1]
        qq = qq_refs[1 - par][...] if do_scores else None
        m_nxt = None
        acc = None
        for c in range(n_chunks):
            if do_scores:
                s, m_nxt = _scores_chunk(keys(hd_nxt, c), qq, m_nxt)
                s_refs[1 - par][chunk(c), :] = s
            acc = _values_chunk(s_refs[par][chunk(c), :], m_cur, vt_ref[j, :, chunk(c)], acc)
        acc_refs[par][...] = acc
        if do_finish:
            finish(*elem(-1), 1 - par)
        return m_nxt

    def tile_steps(t, m, **edge):
        for j in range(N_HEADS):
            m = step(t, j, m, **edge)
        return m

    prep(0, 0, 0)
    prep(0, 1, 1)
    qq = qq0_ref[...]
    m = None
    for c in range(n_chunks):
        s, m = _scores_chunk(keys(0, c), qq, m)
        s0_ref[chunk(c), :] = s
    m = tile_steps(0, m, first=True)
    m = lax.fori_loop(1, n_tiles - 1, tile_steps, m)
    tile_steps(n_tiles - 1, m, last=True)
    finish(n_tiles - 1, N_HEADS - 1, (n_elems - 1) % 2)


def _den_mixers_kernel(lam_ref, sg_ref, q_ref, kn_ref, vn_ref, kc_ref, vc_ref, sgc_ref,
                       xw_ref, sga_ref, f1_ref, twc_ref, tws_ref, cs2_ref, yc_ref, ya_ref,
                       *scratch, lam_init):
    g_ref, x_ref = scratch[-2:]
    _fourier_den_kernel(xw_ref, sga_ref, f1_ref, twc_ref, tws_ref, cs2_ref, ya_ref, g_ref, x_ref)
    _attn_den_kernel(lam_ref, sg_ref, q_ref, kn_ref, vn_ref, kc_ref, vc_ref, sgc_ref, yc_ref,
                     *scratch[:-2], lam_init=lam_init)


def _den_mixers(q, k, v, sgc, cache_k, cache_v, xw, sga, lam_vec, subln_g, f1, twc, tws, cs2,
                *, layer, seq, lam_init):
    n_tok = q.shape[0]
    past = cache_k.shape[2] // N_HEADS
    n2 = seq // FFT_RADIX
    wide = FFT_RADIX * FOURIER_W
    blk = lambda w: pl.BlockSpec((seq, w), lambda b: (b, 0))
    cblk = pl.BlockSpec((1, 1, past * N_HEADS, QK_W), lambda b: (b, layer, 0, 0))
    const = lambda shape: pl.BlockSpec(shape, lambda b: (0, 0))
    return pl.pallas_call(
        functools.partial(_den_mixers_kernel, lam_init=lam_init),
        grid=(n_tok // seq,),
        in_specs=[
            const((4, HEAD_DIM)), const((1, QK_W)),
            blk(ATTN_W), blk(ATTN_W), blk(ATTN_W), cblk, cblk, blk(ATTN_W),
            pl.BlockSpec((n2, wide), lambda b: (b, 0)), blk(FOURIER_W),
            const((2 * n2, n2)), const((n2, wide)), const((n2, wide)),
            const((2 * FOURIER_W, FOURIER_W)),
        ],
        out_specs=[blk(ATTN_W), blk(FOURIER_W)],
        out_shape=[jax.ShapeDtypeStruct((n_tok, ATTN_W), BF16),
                   jax.ShapeDtypeStruct((n_tok, FOURIER_W), BF16)],
        scratch_shapes=[
            pltpu.VMEM((N_HEADS, past, QK_W), BF16),
            pltpu.VMEM((N_HEADS, QK_W + ONES_ROWS, past + seq), BF16),
            pltpu.VMEM((QK_W, 2 * Q_TILE), BF16),
            pltpu.VMEM((QK_W, 2 * Q_TILE), BF16),
            pltpu.VMEM((past + seq, 2 * Q_TILE), F32),
            pltpu.VMEM((past + seq, 2 * Q_TILE), F32),
            pltpu.VMEM((QK_W + ONES_ROWS, 2 * Q_TILE), F32),
            pltpu.VMEM((QK_W + ONES_ROWS, 2 * Q_TILE), F32),
            pltpu.VMEM((2 * n2, wide), F32),
            pltpu.VMEM((seq, 2 * FOURIER_W), BF16),
        ],
        compiler_params=_cparams(("arbitrary",)),
        name="den_mixers",
    )(lam_vec, subln_g.reshape(1, QK_W), q, k, v, cache_k, cache_v, sgc, xw, sga, f1, twc, tws, cs2)


def _fourier_ctx_kernel(fa_ref, sga_ref, cs_ref, dn_ref, o_ref, *, seq):
    ab = jnp.dot(fa_ref[...], cs_ref[...], preferred_element_type=F32).astype(BF16)
    for s in range(fa_ref.shape[0] // seq):
        rows = slice(s * seq, (s + 1) * seq)
        ab2 = jnp.concatenate([ab[rows, :FOURIER_W], ab[rows, FOURIER_W:]], axis=0)
        f = jnp.dot(dn_ref[...], ab2, preferred_element_type=F32)
        o_ref[rows, :] = (f * sga_ref[rows, :].astype(F32)).astype(BF16)


def _ctx_mixers_kernel(lam_ref, sg_ref, q_ref, k_ref, v_ref, sgc_ref, fa_ref, sga_ref, cs_ref, dn_ref,
                       yc_ref, ya_ref, *, lam_init, seq):
    _attn_ctx_kernel(lam_ref, sg_ref, q_ref, k_ref, v_ref, sgc_ref, yc_ref, lam_init=lam_init, seq=seq)
    _fourier_ctx_kernel(fa_ref, sga_ref, cs_ref, dn_ref, ya_ref, seq=seq)


def _ctx_mixers(q, k, v, sgc, fa, sga, lam_vec, subln_g, cs, dn, *, seq, lam_init):
    n_tok = q.shape[0]
    rows = CTX_SEQS * seq
    blk = lambda w: pl.BlockSpec((rows, w), lambda b: (b, 0))
    const = lambda shape: pl.BlockSpec(shape, lambda b: (0, 0))
    return pl.pallas_call(
        functools.partial(_ctx_mixers_kernel, lam_init=lam_init, seq=seq),
        grid=(n_tok // rows,),
        in_specs=[
            const((4, HEAD_DIM)), const((1, QK_W)),
            blk(ATTN_W), blk(ATTN_W), blk(ATTN_W), blk(ATTN_W),
            blk(FOURIER_W), blk(FOURIER_W),
            const((FOURIER_W, 2 * FOURIER_W)), const((seq, 2 * seq)),
        ],
        out_specs=[blk(ATTN_W), blk(FOURIER_W)],
        out_shape=[jax.ShapeDtypeStruct((n_tok, ATTN_W), BF16),
                   jax.ShapeDtypeStruct((n_tok, FOURIER_W), BF16)],
        compiler_params=_cparams(("arbitrary",)),
        name="ctx_mixers",
    )(lam_vec, subln_g.reshape(1, QK_W), q, k, v, sgc, fa, sga, cs, dn)


def _cadd(a, b):
    return (a[0] + b[0], a[1] + b[1])


def _csub(a, b):
    return (a[0] - b[0], a[1] - b[1])


def _mul_neg_i(a):
    return (a[1], -a[0])


def _mul_w8_1(a):
    return ((a[0] + a[1]) * SQRT_HALF, (a[1] - a[0]) * SQRT_HALF)


def _mul_w8_3(a):
    return ((a[1] - a[0]) * SQRT_HALF, (-a[0] - a[1]) * SQRT_HALF)


def _fft4(a0, a1, a2, a3):
    e0, e1 = _cadd(a0, a2), _csub(a0, a2)
    o0, o1 = _cadd(a1, a3), _mul_neg_i(_csub(a1, a3))
    return [_cadd(e0, o0), _cadd(e1, o1), _csub(e0, o0), _csub(e1, o1)]


def _fft8(x):
    e = _fft4(x[0], x[2], x[4], x[6])
    o = _fft4(x[1], x[3], x[5], x[7])
    t = [o[0], _mul_w8_1(o[1]), _mul_neg_i(o[2]), _mul_w8_3(o[3])]
    return [_cadd(e[k], t[k]) for k in range(4)] + [_csub(e[k], t[k]) for k in range(4)]


def _fourier_den_kernel(xw_ref, sga_ref, f1_ref, twc_ref, tws_ref, cs2_ref, o_ref, g_ref, x_ref):
    n2 = xw_ref.shape[0]
    g_ref[...] = jnp.dot(f1_ref[...], xw_ref[...], preferred_element_type=F32)

    def chunk(i):
        r = i * FFT_ROWS
        re_rows = pl.ds(r, FFT_ROWS)
        im_rows = pl.ds(n2 + r, FFT_ROWS)
        xs = []
        for n1 in range(FFT_RADIX):
            lanes = slice(n1 * FOURIER_W, (n1 + 1) * FOURIER_W)
            gr = g_ref[re_rows, lanes]
            gi = g_ref[im_rows, lanes]
            if n1 > 0:
                c = twc_ref[re_rows, lanes]
                s = tws_ref[re_rows, lanes]
                gr, gi = gr * c + gi * s, gi * c - gr * s
            xs.append((gr, gi))
        for k1, (xr, xi) in enumerate(_fft8(xs)):
            out_rows = pl.ds(k1 * n2 + r, FFT_ROWS)
            x_ref[out_rows, :FOURIER_W] = xr.astype(BF16)
            x_ref[out_rows, FOURIER_W:] = xi.astype(BF16)

    for i in range(n2 // FFT_ROWS):
        chunk(i)
    f = jnp.dot(x_ref[...], cs2_ref[...], preferred_element_type=F32)
    o_ref[...] = (f * sga_ref[...].astype(F32)).astype(BF16)


def _outproj_tile(x_ref, mod_ref, ya_ref, zc_ref, zp_ref, zn_ref, bgs_ref, yc_ref,
                  cw_ref, cb_ref, w_ref, fg_ref, o_ref, tile, *, seq, final_norm):
    tm = x_ref.shape[0]
    z = zc_ref[...].astype(F32)
    row = lax.broadcasted_iota(jnp.int32, z.shape, 0)
    pos = (tile * tm + row) & (seq - 1)
    prev_row = zp_ref[HALO_ROWS - 1:HALO_ROWS, :].astype(F32)
    next_row = zn_ref[0:1, :].astype(F32)
    z_prev = jnp.where(row == 0, prev_row, pltpu.roll(z, 1, axis=0))
    z_prev = jnp.where(pos == 0, 0.0, z_prev)
    z_next = jnp.where(row == tm - 1, next_row, pltpu.roll(z, tm - 1, axis=0))
    z_next = jnp.where(pos == seq - 1, 0.0, z_next)
    cw = cw_ref[...]
    conv = z_prev * cw[0:1] + z * cw[1:2] + z_next * cw[2:3] + cb_ref[...]
    yb = (bgs_ref[...].astype(F32) * conv).astype(BF16)

    mixed = jnp.concatenate([ya_ref[...], yb, yc_ref[...]], axis=-1)
    out = jnp.dot(mixed, w_ref[...], preferred_element_type=F32)
    gate = mod_ref[0, 0][:, 2 * D_MODEL:]
    xn = x_ref[...] + gate * out
    if final_norm:
        ms = jnp.mean(xn * xn, axis=-1, keepdims=True)
        xn = xn * lax.rsqrt(ms + EPS) * fg_ref[...]
    o_ref[...] = xn


N_GROUP_INS = 8


def _outproj_kernel(*refs, n_ctx_tiles, seq, dec_seq, final_norm):
    ctx_ins = refs[:N_GROUP_INS]
    den_ins = refs[N_GROUP_INS:2 * N_GROUP_INS]
    cw_ref, cb_ref, w_ref, fg_ref, oc_ref, od_ref, wb_ref = refs[2 * N_GROUP_INS:]
    i = pl.program_id(0)

    @pl.when(i == 0)
    def _():
        wb_ref[...] = w_ref[0].astype(BF16)

    @pl.when(i < n_ctx_tiles)
    def _():
        _outproj_tile(*ctx_ins, cw_ref, cb_ref, wb_ref, fg_ref, oc_ref, i, seq=seq,
                      final_norm=final_norm)

    @pl.when(i >= n_ctx_tiles)
    def _():
        _outproj_tile(*den_ins, cw_ref, cb_ref, wb_ref, fg_ref, od_ref, i - n_ctx_tiles,
                      seq=dec_seq, final_norm=final_norm)


def _out_projection(ctx, den, mod, conv_w, conv_b, w_out, layer, final_g, *, seq, dec_seq,
                    final_norm):
    tm = OUT_TILE
    nct = ctx[0].shape[0] // tm
    ndt = den[0].shape[0] // tm
    halo_per_tile = tm // HALO_ROWS
    seq_tiles = dec_seq // tm

    def group_specs(tile_of, n_tok, mod_of):
        n_halo = n_tok // HALO_ROWS
        row = lambda w: pl.BlockSpec((tm, w), lambda i: (tile_of(i), 0))
        return [
            row(D_MODEL),
            pl.BlockSpec((1, 1, 1, 3 * D_MODEL), lambda i: (layer, mod_of(tile_of(i)), 0, 0)),
            row(FOURIER_W),
            row(CONV_W),
            pl.BlockSpec((HALO_ROWS, CONV_W),
                         lambda i: (jnp.maximum(tile_of(i) * halo_per_tile - 1, 0), 0)),
            pl.BlockSpec((HALO_ROWS, CONV_W),
                         lambda i: (jnp.minimum((tile_of(i) + 1) * halo_per_tile, n_halo - 1), 0)),
            row(CONV_W),
            row(ATTN_W),
        ], row(D_MODEL)

    ctx_specs, ctx_out = group_specs(lambda i: jnp.minimum(i, nct - 1), ctx[0].shape[0],
                                     lambda t: 0)
    den_specs, den_out = group_specs(lambda i: jnp.maximum(i - nct, 0), den[0].shape[0],
                                     lambda t: 1 + t // seq_tiles)

    def group_args(g):
        x, ya, zc, bgs, yc = g
        return [x, mod, ya, zc, zc, zc, bgs, yc]

    return pl.pallas_call(
        functools.partial(_outproj_kernel, n_ctx_tiles=nct, seq=seq, dec_seq=dec_seq,
                          final_norm=final_norm),
        grid=(nct + ndt,),
        in_specs=ctx_specs + den_specs + [
            pl.BlockSpec((3, CONV_W), lambda i: (0, 0)),
            pl.BlockSpec((1, CONV_W), lambda i: (0, 0)),
            pl.BlockSpec((1, D_MODEL, D_MODEL), lambda i: (layer, 0, 0)),
            pl.BlockSpec((1, D_MODEL), lambda i: (0, 0)),
        ],
        out_specs=[ctx_out, den_out],
        out_shape=[jax.ShapeDtypeStruct(ctx[0].shape, F32), jax.ShapeDtypeStruct(den[0].shape, F32)],
        scratch_shapes=[pltpu.VMEM((D_MODEL, D_MODEL), BF16)],
        compiler_params=_cparams(("arbitrary",)),
        name="out_projection",
    )(*group_args(ctx), *group_args(den), conv_w, conv_b.reshape(1, CONV_W), w_out,
      final_g.reshape(1, D_MODEL))


def kernel(x_prompt, x_sample, cache_k, cache_v, c, c_ctx, norm_g, w_mod, b_mod, w_in, conv_w,
           conv_b, lam_vec, subln_g, w_out, final_g):
    batch, seq, _ = x_prompt.shape
    dec_batch, dec_seq, _ = x_sample.shape

    mod = _modulation(c_ctx, c, w_mod, b_mod)

    rope_tabs = tuple(jnp.asarray(t) for t in _rope_tables(dec_seq))
    cs = jnp.asarray(_chan_tables()).astype(BF16)
    dn_ctx = jnp.asarray(_dft_tables(seq)).astype(BF16)
    f1, twc, tws, cs2 = (jnp.asarray(t) for t in _ct_tables(dec_seq))
    f1 = f1.astype(BF16)
    cs2 = cs2.astype(BF16)

    cache_k2 = cache_k.reshape(dec_batch, DEPTH, -1, QK_W)
    cache_v2 = cache_v.reshape(dec_batch, DEPTH, -1, QK_W)
    xc = x_prompt.reshape(batch * seq, D_MODEL)
    xl = x_sample.reshape(dec_batch * dec_seq, D_MODEL)
    kv = None
    for l in range(DEPTH):
        lam_init = 0.8 - 0.6 * math.exp(-0.3 * l)
        ctx, k32, v32, den = _in_projection(
            xc, xl, mod, norm_g[l], w_in, l, rope_tabs, seq=seq, dec_seq=dec_seq,
            prev_kv=kv)
        kv = (k32, v32)

        fa, sga, zc_c, bgs_c, q, k, v, sgc = ctx
        yc_c, ya_c = _ctx_mixers(q, k, v, sgc, fa, sga, lam_vec[l], subln_g[l], cs, dn_ctx,
                                 seq=seq, lam_init=lam_init)

        fa, sga, zc_d, bgs_d, q, k, v, sgc = den
        yc_d, ya_d = _den_mixers(q, k, v, sgc, cache_k2, cache_v2, fa, sga, lam_vec[l], subln_g[l],
                                 f1, twc, tws, cs2, layer=l, seq=dec_seq, lam_init=lam_init)

        xc, xl = _out_projection(
            (xc, ya_c, zc_c, bgs_c, yc_c), (xl, ya_d, zc_d, bgs_d, yc_d), mod,
            conv_w[l], conv_b[l], w_out, l, final_g, seq=seq, dec_seq=dec_seq,
            final_norm=l == DEPTH - 1)

    y_prompt = xc.reshape(batch, seq, D_MODEL)
    y_sample = xl.reshape(dec_batch, dec_seq, D_MODEL)
    return (y_prompt, y_sample, *(t.reshape(batch, DEPTH, seq, N_HEADS, QK_W) for t in kv))
```

```python
import functools
import math

import numpy as np
import jax
import jax.numpy as jnp
from jax import lax
from jax.experimental import pallas as pl
from jax.experimental.pallas import tpu as pltpu

D_MODEL = 1024
DEPTH = 2
GRID_W = 64
FOURIER_W = 256
CONV_W = 256
ATTN_W = 512
N_HEADS = 4
HEAD_DIM = 64
QK_W = 128
ROPE_BASE = 10000.0
ROPE_W = HEAD_DIM // 2
ROPE_ROT = ROPE_W // 2
EPS = 1e-6
IN_DIM = 3584

F32 = jnp.float32
BF16 = jnp.bfloat16

VMEM_LIMIT_BYTES = 60 * 1024 * 1024
TOKEN_TILE = 512
OUT_TILE = 1024
Q_TILE = 256
KEY_CHUNK = 256
ONES_ROWS = 16
LANES = 128
FFT_RADIX = 8
FFT_ROWS = 16
SQRT_HALF = 0.7071067811865476
CTX_SEQS = 2
HALO_ROWS = 16
MOD_ROWS = 8
MOD_CHUNK = 3 * D_MODEL
LOG2E = 1.4426950408889634

_COLS = {}
_off = 0
for _name, _w in (("fa", 256), ("ga", 256), ("bg", 256), ("cg", 256), ("hc", 256), ("gb", 256),
                  ("q", 512), ("k", 512), ("v", 512), ("gc", 512)):
    _COLS[_name] = (_off, _off + _w)
    _off += _w


def _silu(x):
    return x * (1.0 / (1.0 + jnp.exp(-x)))


def _cparams(sem):
    return pltpu.CompilerParams(dimension_semantics=sem, vmem_limit_bytes=VMEM_LIMIT_BYTES)


def _rope_tables(n_tokens):
    n = np.arange(n_tokens)
    row = (n // GRID_W).astype(np.float64)
    col = (n % GRID_W).astype(np.float64)
    j = np.arange(QK_W)
    jj = j % HEAD_DIM
    idx = jj % ROPE_W
    inv = 1.0 / (ROPE_BASE ** (2.0 * (idx % ROPE_ROT) / ROPE_W))
    pos = np.where((jj < ROPE_W)[None, :], row[:, None], col[:, None])
    ang = pos * inv[None, :]
    cos = np.cos(ang)
    sin = np.sin(ang)
    first = (idx < ROPE_ROT)[None, :]
    sin_a = np.where(first, -sin, 0.0)
    sin_b = np.where(first, 0.0, sin)
    return (np.asarray(cos, np.float32), np.asarray(sin_a, np.float32), np.asarray(sin_b, np.float32))


def _dft_tables(n):
    k = np.arange(n)
    kn = (k[:, None] * k[None, :]) % n
    ang = 2.0 * np.pi * kn / n
    return np.asarray(np.concatenate([np.cos(ang), -np.sin(ang)], axis=1) / math.sqrt(n), np.float32)


def _chan_tables():
    k = np.arange(FOURIER_W)
    kn = (k[:, None] * k[None, :]) % FOURIER_W
    ang = 2.0 * np.pi * kn / FOURIER_W
    return np.asarray(np.concatenate([np.cos(ang), np.sin(ang)], axis=1) / math.sqrt(FOURIER_W), np.float32)


def _ct_tables(n):
    n2 = n // FFT_RADIX
    k = np.arange(n2)
    ang = 2.0 * np.pi * ((k[:, None] * k[None, :]) % n2) / n2
    f1 = np.concatenate([np.cos(ang), -np.sin(ang)], axis=0) / math.sqrt(n * FOURIER_W)
    tw = 2.0 * np.pi * k[:, None] * np.arange(FFT_RADIX)[None, :] / n
    twc = np.repeat(np.cos(tw), FOURIER_W, axis=1)
    tws = np.repeat(np.sin(tw), FOURIER_W, axis=1)
    c = np.arange(FOURIER_W)
    angc = 2.0 * np.pi * ((c[:, None] * c[None, :]) % FOURIER_W) / FOURIER_W
    cs2 = np.concatenate([np.cos(angc), np.sin(angc)], axis=0)
    return tuple(np.asarray(t, np.float32) for t in (f1, twc, tws, cs2))


def _mod_kernel(cctx_ref, c_ref, w_ref, b_ref, o_ref, s_ref):
    n_c = c_ref.shape[0]
    s_ref[...] = jnp.zeros_like(s_ref)
    s_ref[0:1, :] = _silu(cctx_ref[...])
    s_ref[1:1 + n_c, :] = _silu(c_ref[...])
    w = w_ref[0].astype(BF16)
    res = jnp.dot(s_ref[...].astype(BF16), w, preferred_element_type=F32) + b_ref[0]
    for r in range(MOD_ROWS):
        o_ref[0, r] = res[r:r + 1]


def _modulation(c_ctx, c, w_mod, b_mod):
    chunk = MOD_CHUNK
    n_chunks = 3 * D_MODEL // chunk
    return pl.pallas_call(
        _mod_kernel,
        grid=(DEPTH, n_chunks),
        in_specs=[
            pl.BlockSpec((1, D_MODEL), lambda l, j: (0, 0)),
            pl.BlockSpec(c.shape, lambda l, j: (0, 0)),
            pl.BlockSpec((1, D_MODEL, chunk), lambda l, j: (l, 0, j)),
            pl.BlockSpec((1, 1, chunk), lambda l, j: (l, 0, j)),
        ],
        out_specs=pl.BlockSpec((1, MOD_ROWS, 1, chunk), lambda l, j: (l, 0, 0, j)),
        out_shape=jax.ShapeDtypeStruct((DEPTH, MOD_ROWS, 1, 3 * D_MODEL), F32),
        scratch_shapes=[pltpu.VMEM((MOD_ROWS, D_MODEL), F32)],
        compiler_params=_cparams(("arbitrary", "arbitrary")),
        name="modulation",
    )(c_ctx.reshape(1, D_MODEL), c, w_mod, b_mod.reshape(DEPTH, 1, 3 * D_MODEL))


def _store_heads(ref, layer, t):
    seqs, _, rows, _ = ref.shape
    seq = rows // N_HEADS
    for s in range(seqs):
        for hd in range(N_HEADS):
            ref[s, layer, pl.ds(hd, seq, stride=N_HEADS), :] = (
                t[s * seq:(s + 1) * seq, hd * QK_W:(hd + 1) * QK_W])


def _inproj_tile(x_ref, mod_ref, g_ref, w_ref, rope_refs, outs, kv_refs, stage_ref):
    fa_ref, sga_ref, zc_ref, bgs_ref, q_ref, k_ref, v_ref, sgc_ref = outs

    x = x_ref[...]
    ms = jnp.mean(x * x, axis=-1, keepdims=True)
    y = x * lax.rsqrt(ms + EPS) * g_ref[...]
    m = mod_ref[0, 0]
    h = (y * (1.0 + m[:, D_MODEL:2 * D_MODEL]) + m[:, :D_MODEL]).astype(BF16)

    def proj(name):
        lo, hi = _COLS[name]
        return jnp.dot(h, w_ref[:, lo:hi], preferred_element_type=F32)

    if stage_ref is not None:
        fa = proj("fa")
        halves = stage_ref.shape[0]
        for hf in range(halves):
            stage_ref[hf] = fa[:, hf * LANES:(hf + 1) * LANES]
        rows = stage_ref.shape[1] // FFT_RADIX
        for n1 in range(FFT_RADIX):
            for hf in range(halves):
                lo = n1 * FOURIER_W + hf * LANES
                fa_ref[:, lo:lo + LANES] = (
                    stage_ref[hf, pl.ds(n1, rows, stride=FFT_RADIX), :].astype(BF16))
    else:
        fa_ref[...] = proj("fa").astype(BF16)
    sga_ref[...] = _silu(proj("ga")).astype(BF16)
    zc_ref[...] = (proj("cg") * proj("hc")).astype(BF16)
    bgs_ref[...] = (proj("bg") * _silu(proj("gb"))).astype(BF16)
    sgc_ref[...] = _silu(proj("gc")).astype(BF16)

    q = proj("q")
    k = proj("k")
    v = proj("v")
    if kv_refs is not None and len(kv_refs) == 2:
        kv_refs[0][...] = k
        kv_refs[1][...] = v
    elif kv_refs is not None:
        new_k_ref, new_v_ref, prev_k_ref, prev_v_ref = kv_refs
        _store_heads(new_k_ref, 0, prev_k_ref[...])
        _store_heads(new_v_ref, 0, prev_v_ref[...])
        _store_heads(new_k_ref, 1, k)
        _store_heads(new_v_ref, 1, v)
    v_ref[...] = v.astype(BF16)

    q_scale = HEAD_DIM ** -0.5 * LOG2E
    if rope_refs is not None:
        cos = rope_refs[0][...]
        sa = rope_refs[1][...]
        sb = rope_refs[2][...]
        for hd in range(N_HEADS):
            sl = slice(hd * QK_W, (hd + 1) * QK_W)
            for t, ref, scale in ((q, q_ref, q_scale), (k, k_ref, None)):
                th = t[:, sl]
                r = (th * cos + pltpu.roll(th, QK_W - ROPE_ROT, axis=1) * sa
                     + pltpu.roll(th, ROPE_ROT, axis=1) * sb)
                if scale is not None:
                    r = r * scale
                ref[:, sl] = r.astype(BF16)
    else:
        q_ref[...] = (q * q_scale).astype(BF16)
        k_ref[...] = k.astype(BF16)


N_PROJ_OUTS = 8


def _inproj_kernel(xc_ref, xd_ref, modc_ref, modd_ref, g_ref, w_ref, cos_ref, sa_ref, sb_ref,
                   *rest, layer, n_ctx_tiles, n_prev):
    prev_kv = rest[:n_prev]
    rest = rest[n_prev:]
    ctx_outs = rest[:N_PROJ_OUTS]
    kv_refs = rest[N_PROJ_OUTS:N_PROJ_OUTS + 2] + prev_kv
    den_outs = rest[N_PROJ_OUTS + 2:2 * N_PROJ_OUTS + 2]
    wb_ref, stage_ref = rest[2 * N_PROJ_OUTS + 2:]
    g_ref = g_ref.at[layer:layer + 1]
    i = pl.program_id(0)

    @pl.when(i == 0)
    def _():
        for lo, hi in _COLS.values():
            wb_ref[:, lo:hi] = w_ref[0, :, lo:hi].astype(BF16)

    @pl.when(i < n_ctx_tiles)
    def _():
        _inproj_tile(xc_ref, modc_ref, g_ref, wb_ref, None, ctx_outs, kv_refs, None)

    @pl.when(i >= n_ctx_tiles)
    def _():
        _inproj_tile(xd_ref, modd_ref, g_ref, wb_ref, (cos_ref, sa_ref, sb_ref), den_outs, None,
                     stage_ref)


def _in_projection(xc, xd, mod, norm_g, w_in, layer, rope_tabs, *, seq, dec_seq,
                   prev_kv):
    assert DEPTH == 2, "the last layer's call assembles the k/v of exactly two layers"
    tm = TOKEN_TILE
    n_c, n_d = xc.shape[0], xd.shape[0]
    nct, ndt = n_c // tm, n_d // tm
    seq_tiles = dec_seq // tm
    ctx_i = lambda i: jnp.minimum(i, nct - 1)
    den_i = lambda i: jnp.maximum(i - nct, 0)
    row_c = lambda w: pl.BlockSpec((tm, w), lambda i: (ctx_i(i), 0))
    row_d = lambda w: pl.BlockSpec((tm, w), lambda i: (den_i(i), 0))
    tab = pl.BlockSpec((tm, QK_W), lambda i: (den_i(i) % seq_tiles, 0))
    in_specs = [
        row_c(D_MODEL), row_d(D_MODEL),
        pl.BlockSpec((1, 1, 1, 3 * D_MODEL), lambda i: (layer, 0, 0, 0)),
        pl.BlockSpec((1, 1, 1, 3 * D_MODEL), lambda i: (layer, 1 + den_i(i) // seq_tiles, 0, 0)),
        pl.BlockSpec((DEPTH, D_MODEL), lambda i: (0, 0)),
        pl.BlockSpec((1, D_MODEL, IN_DIM), lambda i: (layer, 0, 0)),
        tab, tab, tab,
    ]
    args = [xc, xd, mod, mod, norm_g, w_in, *rope_tabs]
    widths = (256, 256, 256, 256, 512, 512, 512, 512)
    seqs = tm // seq
    if prev_kv is None:
        kv_blk = row_c(ATTN_W)
        kv_shape = jax.ShapeDtypeStruct((n_c, ATTN_W), F32)
    else:
        kv_blk = pl.BlockSpec((seqs, DEPTH, seq * N_HEADS, QK_W), lambda i: (ctx_i(i), 0, 0, 0))
        kv_shape = jax.ShapeDtypeStruct((n_c // seq, DEPTH, seq * N_HEADS, QK_W), F32)
        in_specs += [row_c(ATTN_W)] * 2
        args += list(prev_kv)
    out_specs = [row_c(w) for w in widths] + [kv_blk, kv_blk] + [row_d(w) for w in widths]
    out_shape = ([jax.ShapeDtypeStruct((n_c, w), BF16) for w in widths] + [kv_shape, kv_shape]
                 + [jax.ShapeDtypeStruct((n_d, w), BF16) for w in widths])
    fa_d = N_PROJ_OUTS + 2
    out_specs[fa_d] = pl.BlockSpec((tm // FFT_RADIX, FFT_RADIX * FOURIER_W), lambda i: (den_i(i), 0))
    out_shape[fa_d] = jax.ShapeDtypeStruct((n_d // FFT_RADIX, FFT_RADIX * FOURIER_W), BF16)
    outs = pl.pallas_call(
        functools.partial(_inproj_kernel, layer=layer, n_ctx_tiles=nct,
                          n_prev=0 if prev_kv is None else 2),
        grid=(nct + ndt,),
        in_specs=in_specs,
        out_specs=out_specs,
        out_shape=out_shape,
        scratch_shapes=[pltpu.VMEM((D_MODEL, IN_DIM), BF16),
                        pltpu.VMEM((FOURIER_W // LANES, tm, LANES), F32)],
        compiler_params=_cparams(("arbitrary",)),
        name="in_projection",
    )(*args)
    return outs[:N_PROJ_OUTS], outs[N_PROJ_OUTS], outs[N_PROJ_OUTS + 1], outs[fa_d:]


def _lambda(lam_ref, lam_init):
    lv = lam_ref[...]
    a = jnp.sum(lv[0:1] * lv[1:2], axis=-1, keepdims=True)
    b = jnp.sum(lv[2:3] * lv[3:4], axis=-1, keepdims=True)
    return jnp.exp(a) - jnp.exp(b) + lam_init


def _stack_masked(q_t):
    first = lax.broadcasted_iota(jnp.int32, q_t.shape, 0) < HEAD_DIM
    return jnp.concatenate([jnp.where(first, q_t, 0.0), jnp.where(first, 0.0, q_t)],
                           axis=1).astype(BF16)


def _scores_chunk(k, qq_t, m):
    s = jnp.dot(k, qq_t, preferred_element_type=F32)
    mc = jnp.max(s, axis=0, keepdims=True)
    return s, (mc if m is None else jnp.maximum(m, mc))


def _with_ones_rows(v_t):
    return jnp.concatenate([v_t, jnp.ones((ONES_ROWS, v_t.shape[1]), F32)], axis=0).astype(BF16)


def _values_chunk(s, m, v1_t, acc):
    e = jnp.exp2(s - m).astype(BF16)
    oc = jnp.dot(v1_t, e, preferred_element_type=F32)
    return oc if acc is None else acc + oc


def _combine_t(acc, lam):
    tq = acc.shape[1] // 2
    r = 1.0 / acc[QK_W:QK_W + 1, :]
    o_t = acc[:QK_W, :]
    return o_t[:, :tq] * r[:, :tq] - o_t[:, tq:] * (r[:, tq:] * lam)


def _subln_gate(o, sg, sgc, lam_init):
    ms = jnp.mean(o * o, axis=-1, keepdims=True)
    y = o * lax.rsqrt(ms + EPS) * sg * (1.0 - lam_init)
    return (y * sgc.astype(F32)).astype(BF16)


def _attn_ctx_kernel(lam_ref, sg_ref, q_ref, k_ref, v_ref, sgc_ref, o_ref, *, lam_init, seq):
    lam = _lambda(lam_ref, lam_init)
    sg = sg_ref[...]
    n_seq = q_ref.shape[0] // seq
    q_t = q_ref[...].astype(F32).T
    v_t = v_ref[...].astype(F32).T
    pairs = [(slice(s * seq, (s + 1) * seq), slice(hd * QK_W, (hd + 1) * QK_W))
             for s in range(n_seq) for hd in range(N_HEADS)]
    qqs = [_stack_masked(q_t[hl, rows]) for rows, hl in pairs]
    v1s = [_with_ones_rows(v_t[hl, rows]) for rows, hl in pairs]
    sm = [_scores_chunk(k_ref[rows, hl], qq, None) for (rows, hl), qq in zip(pairs, qqs)]
    accs = [_values_chunk(s, m, v1, None) for (s, m), v1 in zip(sm, v1s)]
    y_t = [_combine_t(a, lam) for a in accs]
    for s in range(n_seq):
        rows = slice(s * seq, (s + 1) * seq)
        y = jnp.concatenate(y_t[s * N_HEADS:(s + 1) * N_HEADS], axis=0).T
        for hd in range(N_HEADS):
            hl = slice(hd * QK_W, (hd + 1) * QK_W)
            o_ref[rows, hl] = _subln_gate(y[:, hl], sg, sgc_ref[rows, hl], lam_init)


def _attn_den_kernel(lam_ref, sg_ref, q_ref, kn_ref, vn_ref, kc_ref, vc_ref, sgc_ref, o_ref,
                     kcs_ref, vt_ref, qq0_ref, qq1_ref, s0_ref, s1_ref, acc0_ref, acc1_ref,
                     *, lam_init):
    seq = q_ref.shape[0]
    past = kcs_ref.shape[1]
    past_chunks = past // KEY_CHUNK
    n_chunks = (past + seq) // KEY_CHUNK
    n_tiles = seq // Q_TILE
    n_elems = n_tiles * N_HEADS
    qq_refs = (qq0_ref, qq1_ref)
    s_refs = (s0_ref, s1_ref)
    acc_refs = (acc0_ref, acc1_ref)
    lam = _lambda(lam_ref, lam_init)
    sg = sg_ref[...]

    def lanes(hd):
        return slice(hd * QK_W, (hd + 1) * QK_W)

    def chunk(c):
        return slice(c * KEY_CHUNK, (c + 1) * KEY_CHUNK)

    def tile_rows(t):
        if isinstance(t, int):
            return pl.ds(t * Q_TILE, Q_TILE)
        return pl.ds(pl.multiple_of(t * Q_TILE, Q_TILE), Q_TILE)

    for hd in range(N_HEADS):
        cache_rows = pl.ds(hd, past, stride=N_HEADS)
        kcs_ref[hd] = kc_ref[0, 0, cache_rows, :].astype(BF16)
        vt_ref[hd, :, :past] = _with_ones_rows(vc_ref[0, 0, cache_rows, :].T)
        vt_ref[hd, :, past:] = _with_ones_rows(vn_ref[:, lanes(hd)].astype(F32).T)

    def keys(hd, c):
        if c < past_chunks:
            return kcs_ref[hd, chunk(c), :]
        return kn_ref[chunk(c - past_chunks), lanes(hd)]

    def prep(t, hd, par):
        qq_refs[par][...] = _stack_masked(q_ref[tile_rows(t), lanes(hd)].astype(F32).T)

    def finish(t, hd, par):
        rows = tile_rows(t)
        y = _combine_t(acc_refs[par][...], lam).T
        o_ref[rows, lanes(hd)] = _subln_gate(y, sg, sgc_ref[rows, lanes(hd)], lam_init)

    def step(t, j, m_cur, first=False, last=False):
        def elem(off):
            return t + (j + off) // N_HEADS, (j + off) % N_HEADS

        par = j % 2
        e_static = N_HEADS * t + j if isinstance(t, int) else None
        do_finish = not (first and j == 0)
        do_scores = not (last and e_static + 1 >= n_elems)
        do_prep = not (last and e_static + 2 >= n_elems)
        if do_prep:
            prep(*elem(2), par)
        hd_nxt = elem(1)[1]
        qq = qq_refs[1 - par][...] if do_scores else None
        m_nxt = None
        acc = None
        for c in range(n_chunks):
            if do_scores:
                s, m_nxt = _scores_chunk(keys(hd_nxt, c), qq, m_nxt)
                s_refs[1 - par][chunk(c), :] = s
            acc = _values_chunk(s_refs[par][chunk(c), :], m_cur, vt_ref[j, :, chunk(c)], acc)
        acc_refs[par][...] = acc
        if do_finish:
            finish(*elem(-1), 1 - par)
        return m_nxt

    def tile_steps(t, m, **edge):
        for j in range(N_HEADS):
            m = step(t, j, m, **edge)
        return m

    prep(0, 0, 0)
    prep(0, 1, 1)
    qq = qq0_ref[...]
    m = None
    for c in range(n_chunks):
        s, m = _scores_chunk(keys(0, c), qq, m)
        s0_ref[chunk(c), :] = s
    m = tile_steps(0, m, first=True)
    m = lax.fori_loop(1, n_tiles - 1, tile_steps, m)
    tile_steps(n_tiles - 1, m, last=True)
    finish(n_tiles - 1, N_HEADS - 1, (n_elems - 1) % 2)


def _den_mixers_kernel(lam_ref, sg_ref, q_ref, kn_ref, vn_ref, kc_ref, vc_ref, sgc_ref,
                       xw_ref, sga_ref, f1_ref, twc_ref, tws_ref, cs2_ref, yc_ref, ya_ref,
                       *scratch, layer, lam_init):
    lam_ref, sg_ref = lam_ref.at[layer], sg_ref.at[layer:layer + 1]
    g_ref, x_ref = scratch[-2:]
    _fourier_den_kernel(xw_ref, sga_ref, f1_ref, twc_ref, tws_ref, cs2_ref, ya_ref, g_ref, x_ref)
    _attn_den_kernel(lam_ref, sg_ref, q_ref, kn_ref, vn_ref, kc_ref, vc_ref, sgc_ref, yc_ref,
                     *scratch[:-2], lam_init=lam_init)


def _den_mixers(q, k, v, sgc, cache_k, cache_v, xw, sga, lam_vec, subln_g, f1, twc, tws, cs2,
                *, layer, seq, lam_init):
    n_tok = q.shape[0]
    past = cache_k.shape[2] // N_HEADS
    n2 = seq // FFT_RADIX
    wide = FFT_RADIX * FOURIER_W
    blk = lambda w: pl.BlockSpec((seq, w), lambda b: (b, 0))
    cblk = pl.BlockSpec((1, 1, past * N_HEADS, QK_W), lambda b: (b, layer, 0, 0))
    const = lambda shape: pl.BlockSpec(shape, lambda b: (0,) * len(shape))
    return pl.pallas_call(
        functools.partial(_den_mixers_kernel, layer=layer, lam_init=lam_init),
        grid=(n_tok // seq,),
        in_specs=[
            const(lam_vec.shape), const(subln_g.shape),
            blk(ATTN_W), blk(ATTN_W), blk(ATTN_W), cblk, cblk, blk(ATTN_W),
            pl.BlockSpec((n2, wide), lambda b: (b, 0)), blk(FOURIER_W),
            const((2 * n2, n2)), const((n2, wide)), const((n2, wide)),
            const((2 * FOURIER_W, FOURIER_W)),
        ],
        out_specs=[blk(ATTN_W), blk(FOURIER_W)],
        out_shape=[jax.ShapeDtypeStruct((n_tok, ATTN_W), BF16),
                   jax.ShapeDtypeStruct((n_tok, FOURIER_W), BF16)],
        scratch_shapes=[
            pltpu.VMEM((N_HEADS, past, QK_W), BF16),
            pltpu.VMEM((N_HEADS, QK_W + ONES_ROWS, past + seq), BF16),
            pltpu.VMEM((QK_W, 2 * Q_TILE), BF16),
            pltpu.VMEM((QK_W, 2 * Q_TILE), BF16),
            pltpu.VMEM((past + seq, 2 * Q_TILE), F32),
            pltpu.VMEM((past + seq, 2 * Q_TILE), F32),
            pltpu.VMEM((QK_W + ONES_ROWS, 2 * Q_TILE), F32),
            pltpu.VMEM((QK_W + ONES_ROWS, 2 * Q_TILE), F32),
            pltpu.VMEM((2 * n2, wide), F32),
            pltpu.VMEM((seq, 2 * FOURIER_W), BF16),
        ],
        compiler_params=_cparams(("arbitrary",)),
        name="den_mixers",
    )(lam_vec, subln_g, q, k, v, cache_k, cache_v, sgc, xw, sga, f1, twc, tws, cs2)


def _fourier_ctx_kernel(fa_ref, sga_ref, cs_ref, dn_ref, o_ref, *, seq):
    ab = jnp.dot(fa_ref[...], cs_ref[...], preferred_element_type=F32).astype(BF16)
    for s in range(fa_ref.shape[0] // seq):
        rows = slice(s * seq, (s + 1) * seq)
        ab2 = jnp.concatenate([ab[rows, :FOURIER_W], ab[rows, FOURIER_W:]], axis=0)
        f = jnp.dot(dn_ref[...], ab2, preferred_element_type=F32)
        o_ref[rows, :] = (f * sga_ref[rows, :].astype(F32)).astype(BF16)


def _ctx_mixers_kernel(lam_ref, sg_ref, q_ref, k_ref, v_ref, sgc_ref, fa_ref, sga_ref, cs_ref, dn_ref,
                       yc_ref, ya_ref, *, layer, lam_init, seq):
    lam_ref, sg_ref = lam_ref.at[layer], sg_ref.at[layer:layer + 1]
    _attn_ctx_kernel(lam_ref, sg_ref, q_ref, k_ref, v_ref, sgc_ref, yc_ref, lam_init=lam_init, seq=seq)
    _fourier_ctx_kernel(fa_ref, sga_ref, cs_ref, dn_ref, ya_ref, seq=seq)


def _ctx_mixers(q, k, v, sgc, fa, sga, lam_vec, subln_g, cs, dn, *, layer, seq, lam_init):
    n_tok = q.shape[0]
    rows = CTX_SEQS * seq
    blk = lambda w: pl.BlockSpec((rows, w), lambda b: (b, 0))
    const = lambda shape: pl.BlockSpec(shape, lambda b: (0,) * len(shape))
    return pl.pallas_call(
        functools.partial(_ctx_mixers_kernel, layer=layer, lam_init=lam_init, seq=seq),
        grid=(n_tok // rows,),
        in_specs=[
            const(lam_vec.shape), const(subln_g.shape),
            blk(ATTN_W), blk(ATTN_W), blk(ATTN_W), blk(ATTN_W),
            blk(FOURIER_W), blk(FOURIER_W),
            const((FOURIER_W, 2 * FOURIER_W)), const((seq, 2 * seq)),
        ],
        out_specs=[blk(ATTN_W), blk(FOURIER_W)],
        out_shape=[jax.ShapeDtypeStruct((n_tok, ATTN_W), BF16),
                   jax.ShapeDtypeStruct((n_tok, FOURIER_W), BF16)],
        compiler_params=_cparams(("arbitrary",)),
        name="ctx_mixers",
    )(lam_vec, subln_g, q, k, v, sgc, fa, sga, cs, dn)


def _cadd(a, b):
    return (a[0] + b[0], a[1] + b[1])


def _csub(a, b):
    return (a[0] - b[0], a[1] - b[1])


def _mul_neg_i(a):
    return (a[1], -a[0])


def _mul_w8_1(a):
    return ((a[0] + a[1]) * SQRT_HALF, (a[1] - a[0]) * SQRT_HALF)


def _mul_w8_3(a):
    return ((a[1] - a[0]) * SQRT_HALF, (-a[0] - a[1]) * SQRT_HALF)


def _fft4(a0, a1, a2, a3):
    e0, e1 = _cadd(a0, a2), _csub(a0, a2)
    o0, o1 = _cadd(a1, a3), _mul_neg_i(_csub(a1, a3))
    return [_cadd(e0, o0), _cadd(e1, o1), _csub(e0, o0), _csub(e1, o1)]


def _fft8(x):
    e = _fft4(x[0], x[2], x[4], x[6])
    o = _fft4(x[1], x[3], x[5], x[7])
    t = [o[0], _mul_w8_1(o[1]), _mul_neg_i(o[2]), _mul_w8_3(o[3])]
    return [_cadd(e[k], t[k]) for k in range(4)] + [_csub(e[k], t[k]) for k in range(4)]


def _fourier_den_kernel(xw_ref, sga_ref, f1_ref, twc_ref, tws_ref, cs2_ref, o_ref, g_ref, x_ref):
    n2 = xw_ref.shape[0]
    g_ref[...] = jnp.dot(f1_ref[...], xw_ref[...], preferred_element_type=F32)

    def chunk(i):
        r = i * FFT_ROWS
        re_rows = pl.ds(r, FFT_ROWS)
        im_rows = pl.ds(n2 + r, FFT_ROWS)
        xs = []
        for n1 in range(FFT_RADIX):
            lanes = slice(n1 * FOURIER_W, (n1 + 1) * FOURIER_W)
            gr = g_ref[re_rows, lanes]
            gi = g_ref[im_rows, lanes]
            if n1 > 0:
                c = twc_ref[re_rows, lanes]
                s = tws_ref[re_rows, lanes]
                gr, gi = gr * c + gi * s, gi * c - gr * s
            xs.append((gr, gi))
        for k1, (xr, xi) in enumerate(_fft8(xs)):
            out_rows = pl.ds(k1 * n2 + r, FFT_ROWS)
            x_ref[out_rows, :FOURIER_W] = xr.astype(BF16)
            x_ref[out_rows, FOURIER_W:] = xi.astype(BF16)

    for i in range(n2 // FFT_ROWS):
        chunk(i)
    f = jnp.dot(x_ref[...], cs2_ref[...], preferred_element_type=F32)
    o_ref[...] = (f * sga_ref[...].astype(F32)).astype(BF16)


def _outproj_tile(x_ref, mod_ref, ya_ref, zc_ref, zp_ref, zn_ref, bgs_ref, yc_ref,
                  cw_ref, cb_ref, w_ref, fg_ref, o_ref, tile, *, seq, final_norm):
    tm = x_ref.shape[0]
    z = zc_ref[...].astype(F32)
    row = lax.broadcasted_iota(jnp.int32, z.shape, 0)
    pos = (tile * tm + row) & (seq - 1)
    prev_row = zp_ref[HALO_ROWS - 1:HALO_ROWS, :].astype(F32)
    next_row = zn_ref[0:1, :].astype(F32)
    z_prev = jnp.where(row == 0, prev_row, pltpu.roll(z, 1, axis=0))
    z_prev = jnp.where(pos == 0, 0.0, z_prev)
    z_next = jnp.where(row == tm - 1, next_row, pltpu.roll(z, tm - 1, axis=0))
    z_next = jnp.where(pos == seq - 1, 0.0, z_next)
    cw = cw_ref[...]
    conv = z_prev * cw[0:1] + z * cw[1:2] + z_next * cw[2:3] + cb_ref[...]
    yb = (bgs_ref[...].astype(F32) * conv).astype(BF16)

    mixed = jnp.concatenate([ya_ref[...], yb, yc_ref[...]], axis=-1)
    out = jnp.dot(mixed, w_ref[...], preferred_element_type=F32)
    gate = mod_ref[0, 0][:, 2 * D_MODEL:]
    xn = x_ref[...] + gate * out
    if final_norm:
        ms = jnp.mean(xn * xn, axis=-1, keepdims=True)
        xn = xn * lax.rsqrt(ms + EPS) * fg_ref[...]
    o_ref[...] = xn


N_GROUP_INS = 8


def _outproj_kernel(*refs, layer, n_ctx_tiles, seq, dec_seq, final_norm):
    ctx_ins = refs[:N_GROUP_INS]
    den_ins = refs[N_GROUP_INS:2 * N_GROUP_INS]
    cw_ref, cb_ref, w_ref, fg_ref, oc_ref, od_ref, wb_ref = refs[2 * N_GROUP_INS:]
    cw_ref, cb_ref = cw_ref.at[layer], cb_ref.at[layer:layer + 1]
    i = pl.program_id(0)

    @pl.when(i == 0)
    def _():
        wb_ref[...] = w_ref[0].astype(BF16)

    @pl.when(i < n_ctx_tiles)
    def _():
        _outproj_tile(*ctx_ins, cw_ref, cb_ref, wb_ref, fg_ref, oc_ref, i, seq=seq,
                      final_norm=final_norm)

    @pl.when(i >= n_ctx_tiles)
    def _():
        _outproj_tile(*den_ins, cw_ref, cb_ref, wb_ref, fg_ref, od_ref, i - n_ctx_tiles,
                      seq=dec_seq, final_norm=final_norm)


def _out_projection(ctx, den, mod, conv_w, conv_b, w_out, layer, final_g, *, seq, dec_seq,
                    final_norm):
    tm = OUT_TILE
    nct = ctx[0].shape[0] // tm
    ndt = den[0].shape[0] // tm
    halo_per_tile = tm // HALO_ROWS
    seq_tiles = dec_seq // tm

    def group_specs(tile_of, n_tok, mod_of):
        n_halo = n_tok // HALO_ROWS
        row = lambda w: pl.BlockSpec((tm, w), lambda i: (tile_of(i), 0))
        return [
            row(D_MODEL),
            pl.BlockSpec((1, 1, 1, 3 * D_MODEL), lambda i: (layer, mod_of(tile_of(i)), 0, 0)),
            row(FOURIER_W),
            row(CONV_W),
            pl.BlockSpec((HALO_ROWS, CONV_W),
                         lambda i: (jnp.maximum(tile_of(i) * halo_per_tile - 1, 0), 0)),
            pl.BlockSpec((HALO_ROWS, CONV_W),
                         lambda i: (jnp.minimum((tile_of(i) + 1) * halo_per_tile, n_halo - 1), 0)),
            row(CONV_W),
            row(ATTN_W),
        ], row(D_MODEL)

    ctx_specs, ctx_out = group_specs(lambda i: jnp.minimum(i, nct - 1), ctx[0].shape[0],
                                     lambda t: 0)
    den_specs, den_out = group_specs(lambda i: jnp.maximum(i - nct, 0), den[0].shape[0],
                                     lambda t: 1 + t // seq_tiles)

    def group_args(g):
        x, ya, zc, bgs, yc = g
        return [x, mod, ya, zc, zc, zc, bgs, yc]

    return pl.pallas_call(
        functools.partial(_outproj_kernel, layer=layer, n_ctx_tiles=nct, seq=seq, dec_seq=dec_seq,
                          final_norm=final_norm),
        grid=(nct + ndt,),
        in_specs=ctx_specs + den_specs + [
            pl.BlockSpec(conv_w.shape, lambda i: (0, 0, 0)),
            pl.BlockSpec(conv_b.shape, lambda i: (0, 0)),
            pl.BlockSpec((1, D_MODEL, D_MODEL), lambda i: (layer, 0, 0)),
            pl.BlockSpec((1, D_MODEL), lambda i: (0, 0)),
        ],
        out_specs=[ctx_out, den_out],
        out_shape=[jax.ShapeDtypeStruct(ctx[0].shape, F32), jax.ShapeDtypeStruct(den[0].shape, F32)],
        scratch_shapes=[pltpu.VMEM((D_MODEL, D_MODEL), BF16)],
        compiler_params=_cparams(("arbitrary",)),
        name="out_projection",
    )(*group_args(ctx), *group_args(den), conv_w, conv_b, w_out,
      final_g.reshape(1, D_MODEL))


def kernel(x_prompt, x_sample, cache_k, cache_v, c, c_ctx, norm_g, w_mod, b_mod, w_in, conv_w,
           conv_b, lam_vec, subln_g, w_out, final_g):
    batch, seq, _ = x_prompt.shape
    dec_batch, dec_seq, _ = x_sample.shape

    mod = _modulation(c_ctx, c, w_mod, b_mod)

    rope_tabs = tuple(jnp.asarray(t) for t in _rope_tables(dec_seq))
    cs = jnp.asarray(_chan_tables()).astype(BF16)
    dn_ctx = jnp.asarray(_dft_tables(seq)).astype(BF16)
    f1, twc, tws, cs2 = (jnp.asarray(t) for t in _ct_tables(dec_seq))
    f1 = f1.astype(BF16)
    cs2 = cs2.astype(BF16)

    cache_k2 = cache_k.reshape(dec_batch, DEPTH, -1, QK_W)
    cache_v2 = cache_v.reshape(dec_batch, DEPTH, -1, QK_W)
    xc = x_prompt.reshape(batch * seq, D_MODEL)
    xl = x_sample.reshape(dec_batch * dec_seq, D_MODEL)
    kv = None
    for l in range(DEPTH):
        lam_init = 0.8 - 0.6 * math.exp(-0.3 * l)
        ctx, k32, v32, den = _in_projection(
            xc, xl, mod, norm_g, w_in, l, rope_tabs, seq=seq, dec_seq=dec_seq,
            prev_kv=kv)
        kv = (k32, v32)

        fa, sga, zc_c, bgs_c, q, k, v, sgc = ctx
        yc_c, ya_c = _ctx_mixers(q, k, v, sgc, fa, sga, lam_vec, subln_g, cs, dn_ctx,
                                 layer=l, seq=seq, lam_init=lam_init)

        fa, sga, zc_d, bgs_d, q, k, v, sgc = den
        yc_d, ya_d = _den_mixers(q, k, v, sgc, cache_k2, cache_v2, fa, sga, lam_vec, subln_g,
                                 f1, twc, tws, cs2, layer=l, seq=dec_seq, lam_init=lam_init)

        xc, xl = _out_projection(
            (xc, ya_c, zc_c, bgs_c, yc_c), (xl, ya_d, zc_d, bgs_d, yc_d), mod,
            conv_w, conv_b, w_out, l, final_g, seq=seq, dec_seq=dec_seq,
            final_norm=l == DEPTH - 1)

    y_prompt = xc.reshape(batch, seq, D_MODEL)
    y_sample = xl.reshape(dec_batch, dec_seq, D_MODEL)
    return (y_prompt, y_sample, *(t.reshape(batch, DEPTH, seq, N_HEADS, QK_W) for t in kv))
```

```python
import functools
import math

import numpy as np
import jax
import jax.numpy as jnp
from jax import lax
from jax.experimental import pallas as pl
from jax.experimental.pallas import tpu as pltpu

D_MODEL = 1024
DEPTH = 2
GRID_W = 64
FOURIER_W = 256
CONV_W = 256
ATTN_W = 512
N_HEADS = 4
HEAD_DIM = 64
QK_W = 128
ROPE_BASE = 10000.0
ROPE_W = HEAD_DIM // 2
ROPE_ROT = ROPE_W // 2
EPS = 1e-6
IN_DIM = 3584

F32 = jnp.float32
BF16 = jnp.bfloat16

VMEM_LIMIT_BYTES = 60 * 1024 * 1024
TOKEN_TILE = 512
OUT_TILE = 1024
Q_TILE = 256
KEY_CHUNK = 256
ONES_ROWS = 16
LANES = 128
FFT_RADIX = 8
FFT_ROWS = 16
SQRT_HALF = 0.7071067811865476
CTX_SEQS = 2
HALO_ROWS = 16
MOD_ROWS = 8
MOD_CHUNK = 3 * D_MODEL
LOG2E = 1.4426950408889634

_COLS = {}
_off = 0
for _name, _w in (("fa", 256), ("ga", 256), ("bg", 256), ("cg", 256), ("hc", 256), ("gb", 256),
                  ("q", 512), ("k", 512), ("v", 512), ("gc", 512)):
    _COLS[_name] = (_off, _off + _w)
    _off += _w


def _silu(x):
    return x * (1.0 / (1.0 + jnp.exp(-x)))


def _cparams(sem):
    return pltpu.CompilerParams(dimension_semantics=sem, vmem_limit_bytes=VMEM_LIMIT_BYTES)


def _rope_tables(n_tokens):
    n = np.arange(n_tokens)
    row = (n // GRID_W).astype(np.float64)
    col = (n % GRID_W).astype(np.float64)
    j = np.arange(QK_W)
    jj = j % HEAD_DIM
    idx = jj % ROPE_W
    inv = 1.0 / (ROPE_BASE ** (2.0 * (idx % ROPE_ROT) / ROPE_W))
    pos = np.where((jj < ROPE_W)[None, :], row[:, None], col[:, None])
    ang = pos * inv[None, :]
    cos = np.cos(ang)
    sin = np.sin(ang)
    first = (idx < ROPE_ROT)[None, :]
    sin_a = np.where(first, -sin, 0.0)
    sin_b = np.where(first, 0.0, sin)
    return (np.asarray(cos, np.float32), np.asarray(sin_a, np.float32), np.asarray(sin_b, np.float32))


def _dft_tables(n):
    k = np.arange(n)
    kn = (k[:, None] * k[None, :]) % n
    ang = 2.0 * np.pi * kn / n
    return np.asarray(np.concatenate([np.cos(ang), -np.sin(ang)], axis=1) / math.sqrt(n), np.float32)


def _chan_tables():
    k = np.arange(FOURIER_W)
    kn = (k[:, None] * k[None, :]) % FOURIER_W
    ang = 2.0 * np.pi * kn / FOURIER_W
    return np.asarray(np.concatenate([np.cos(ang), np.sin(ang)], axis=1) / math.sqrt(FOURIER_W), np.float32)


def _ct_tables(n):
    n2 = n // FFT_RADIX
    k = np.arange(n2)
    ang = 2.0 * np.pi * ((k[:, None] * k[None, :]) % n2) / n2
    f1 = np.concatenate([np.cos(ang), -np.sin(ang)], axis=0) / math.sqrt(n * FOURIER_W)
    tw = 2.0 * np.pi * k[:, None] * np.arange(FFT_RADIX)[None, :] / n
    twc = np.repeat(np.cos(tw), FOURIER_W, axis=1)
    tws = np.repeat(np.sin(tw), FOURIER_W, axis=1)
    c = np.arange(FOURIER_W)
    angc = 2.0 * np.pi * ((c[:, None] * c[None, :]) % FOURIER_W) / FOURIER_W
    cs2 = np.concatenate([np.cos(angc), np.sin(angc)], axis=0)
    return tuple(np.asarray(t, np.float32) for t in (f1, twc, tws, cs2))


def _mod_kernel(cctx_ref, c_ref, w_ref, b_ref, o_ref, s_ref):
    n_c = c_ref.shape[0]
    s_ref[...] = jnp.zeros_like(s_ref)
    s_ref[0:1, :] = _silu(cctx_ref[...])
    s_ref[1:1 + n_c, :] = _silu(c_ref[...])
    w = w_ref[0].astype(BF16)
    res = jnp.dot(s_ref[...].astype(BF16), w, preferred_element_type=F32) + b_ref[0]
    for r in range(MOD_ROWS):
        o_ref[0, r] = res[r:r + 1]


def _mod_specs(c, layer, chunk, col):
    return [
        pl.BlockSpec((1, D_MODEL), lambda j: (0, 0)),
        pl.BlockSpec(c.shape, lambda j: (0, 0)),
        pl.BlockSpec((1, D_MODEL, chunk), lambda j: (layer, 0, col(j))),
        pl.BlockSpec((1, 1, chunk), lambda j: (layer, 0, col(j))),
    ], pl.BlockSpec((1, MOD_ROWS, 1, chunk), lambda j: (0, 0, 0, col(j)))


def _modulation(c_ctx, c, w_mod, b_mod, layer):
    in_specs, out_spec = _mod_specs(c, layer, MOD_CHUNK, lambda j: j)
    return pl.pallas_call(
        _mod_kernel,
        grid=(3 * D_MODEL // MOD_CHUNK,),
        in_specs=in_specs,
        out_specs=out_spec,
        out_shape=jax.ShapeDtypeStruct((1, MOD_ROWS, 1, 3 * D_MODEL), F32),
        scratch_shapes=[pltpu.VMEM((MOD_ROWS, D_MODEL), F32)],
        compiler_params=_cparams(("arbitrary",)),
        name="modulation",
    )(c_ctx.reshape(1, D_MODEL), c, w_mod, b_mod.reshape(DEPTH, 1, 3 * D_MODEL))


def _store_heads(ref, layer, t):
    seqs, _, rows, _ = ref.shape
    seq = rows // N_HEADS
    for s in range(seqs):
        for hd in range(N_HEADS):
            ref[s, layer, pl.ds(hd, seq, stride=N_HEADS), :] = (
                t[s * seq:(s + 1) * seq, hd * QK_W:(hd + 1) * QK_W])


def _inproj_tile(x_ref, mod_ref, g_ref, w_ref, rope_refs, outs, kv_refs, stage_ref):
    fa_ref, sga_ref, zc_ref, bgs_ref, q_ref, k_ref, v_ref, sgc_ref = outs

    x = x_ref[...]
    ms = jnp.mean(x * x, axis=-1, keepdims=True)
    y = x * lax.rsqrt(ms + EPS) * g_ref[...]
    m = mod_ref[0, 0]
    h = (y * (1.0 + m[:, D_MODEL:2 * D_MODEL]) + m[:, :D_MODEL]).astype(BF16)

    def proj(name):
        lo, hi = _COLS[name]
        return jnp.dot(h, w_ref[:, lo:hi], preferred_element_type=F32)

    if stage_ref is not None:
        fa = proj("fa")
        halves = stage_ref.shape[0]
        for hf in range(halves):
            stage_ref[hf] = fa[:, hf * LANES:(hf + 1) * LANES]
        rows = stage_ref.shape[1] // FFT_RADIX
        for n1 in range(FFT_RADIX):
            for hf in range(halves):
                lo = n1 * FOURIER_W + hf * LANES
                fa_ref[:, lo:lo + LANES] = (
                    stage_ref[hf, pl.ds(n1, rows, stride=FFT_RADIX), :].astype(BF16))
    else:
        fa_ref[...] = proj("fa").astype(BF16)
    sga_ref[...] = _silu(proj("ga")).astype(BF16)
    zc_ref[...] = (proj("cg") * proj("hc")).astype(BF16)
    bgs_ref[...] = (proj("bg") * _silu(proj("gb"))).astype(BF16)
    sgc_ref[...] = _silu(proj("gc")).astype(BF16)

    q = proj("q")
    k = proj("k")
    v = proj("v")
    if kv_refs is not None and len(kv_refs) == 2:
        kv_refs[0][...] = k
        kv_refs[1][...] = v
    elif kv_refs is not None:
        new_k_ref, new_v_ref, prev_k_ref, prev_v_ref = kv_refs
        _store_heads(new_k_ref, 0, prev_k_ref[...])
        _store_heads(new_v_ref, 0, prev_v_ref[...])
        _store_heads(new_k_ref, 1, k)
        _store_heads(new_v_ref, 1, v)
    v_ref[...] = v.astype(BF16)

    q_scale = HEAD_DIM ** -0.5 * LOG2E
    if rope_refs is not None:
        cos = rope_refs[0][...]
        sa = rope_refs[1][...]
        sb = rope_refs[2][...]
        for hd in range(N_HEADS):
            sl = slice(hd * QK_W, (hd + 1) * QK_W)
            for t, ref, scale in ((q, q_ref, q_scale), (k, k_ref, None)):
                th = t[:, sl]
                r = (th * cos + pltpu.roll(th, QK_W - ROPE_ROT, axis=1) * sa
                     + pltpu.roll(th, ROPE_ROT, axis=1) * sb)
                if scale is not None:
                    r = r * scale
                ref[:, sl] = r.astype(BF16)
    else:
        q_ref[...] = (q * q_scale).astype(BF16)
        k_ref[...] = k.astype(BF16)


N_PROJ_OUTS = 8


def _inproj_kernel(xc_ref, xd_ref, modc_ref, modd_ref, g_ref, w_ref, cos_ref, sa_ref, sb_ref,
                   *rest, layer, n_ctx_tiles, n_prev):
    prev_kv = rest[:n_prev]
    rest = rest[n_prev:]
    ctx_outs = rest[:N_PROJ_OUTS]
    kv_refs = rest[N_PROJ_OUTS:N_PROJ_OUTS + 2] + prev_kv
    den_outs = rest[N_PROJ_OUTS + 2:2 * N_PROJ_OUTS + 2]
    wb_ref, stage_ref = rest[2 * N_PROJ_OUTS + 2:]
    g_ref = g_ref.at[layer:layer + 1]
    i = pl.program_id(0)

    @pl.when(i == 0)
    def _():
        for lo, hi in _COLS.values():
            wb_ref[:, lo:hi] = w_ref[0, :, lo:hi].astype(BF16)

    @pl.when(i < n_ctx_tiles)
    def _():
        _inproj_tile(xc_ref, modc_ref, g_ref, wb_ref, None, ctx_outs, kv_refs, None)

    @pl.when(i >= n_ctx_tiles)
    def _():
        _inproj_tile(xd_ref, modd_ref, g_ref, wb_ref, (cos_ref, sa_ref, sb_ref), den_outs, None,
                     stage_ref)


def _in_projection(xc, xd, mod, norm_g, w_in, layer, rope_tabs, *, seq, dec_seq,
                   prev_kv):
    assert DEPTH == 2, "the last layer's call assembles the k/v of exactly two layers"
    tm = TOKEN_TILE
    n_c, n_d = xc.shape[0], xd.shape[0]
    nct, ndt = n_c // tm, n_d // tm
    seq_tiles = dec_seq // tm
    ctx_i = lambda i: jnp.minimum(i, nct - 1)
    den_i = lambda i: jnp.maximum(i - nct, 0)
    row_c = lambda w: pl.BlockSpec((tm, w), lambda i: (ctx_i(i), 0))
    row_d = lambda w: pl.BlockSpec((tm, w), lambda i: (den_i(i), 0))
    tab = pl.BlockSpec((tm, QK_W), lambda i: (den_i(i) % seq_tiles, 0))
    in_specs = [
        row_c(D_MODEL), row_d(D_MODEL),
        pl.BlockSpec((1, 1, 1, 3 * D_MODEL), lambda i: (0, 0, 0, 0)),
        pl.BlockSpec((1, 1, 1, 3 * D_MODEL), lambda i: (0, 1 + den_i(i) // seq_tiles, 0, 0)),
        pl.BlockSpec((DEPTH, D_MODEL), lambda i: (0, 0)),
        pl.BlockSpec((1, D_MODEL, IN_DIM), lambda i: (layer, 0, 0)),
        tab, tab, tab,
    ]
    args = [xc, xd, mod, mod, norm_g, w_in, *rope_tabs]
    widths = (256, 256, 256, 256, 512, 512, 512, 512)
    seqs = tm // seq
    if prev_kv is None:
        kv_blk = row_c(ATTN_W)
        kv_shape = jax.ShapeDtypeStruct((n_c, ATTN_W), F32)
    else:
        kv_blk = pl.BlockSpec((seqs, DEPTH, seq * N_HEADS, QK_W), lambda i: (ctx_i(i), 0, 0, 0))
        kv_shape = jax.ShapeDtypeStruct((n_c // seq, DEPTH, seq * N_HEADS, QK_W), F32)
        in_specs += [row_c(ATTN_W)] * 2
        args += list(prev_kv)
    out_specs = [row_c(w) for w in widths] + [kv_blk, kv_blk] + [row_d(w) for w in widths]
    out_shape = ([jax.ShapeDtypeStruct((n_c, w), BF16) for w in widths] + [kv_shape, kv_shape]
                 + [jax.ShapeDtypeStruct((n_d, w), BF16) for w in widths])
    fa_d = N_PROJ_OUTS + 2
    out_specs[fa_d] = pl.BlockSpec((tm // FFT_RADIX, FFT_RADIX * FOURIER_W), lambda i: (den_i(i), 0))
    out_shape[fa_d] = jax.ShapeDtypeStruct((n_d // FFT_RADIX, FFT_RADIX * FOURIER_W), BF16)
    outs = pl.pallas_call(
        functools.partial(_inproj_kernel, layer=layer, n_ctx_tiles=nct,
                          n_prev=0 if prev_kv is None else 2),
        grid=(nct + ndt,),
        in_specs=in_specs,
        out_specs=out_specs,
        out_shape=out_shape,
        scratch_shapes=[pltpu.VMEM((D_MODEL, IN_DIM), BF16),
                        pltpu.VMEM((FOURIER_W // LANES, tm, LANES), F32)],
        compiler_params=_cparams(("arbitrary",)),
        name="in_projection",
    )(*args)
    return outs[:N_PROJ_OUTS], outs[N_PROJ_OUTS], outs[N_PROJ_OUTS + 1], outs[fa_d:]


def _lambda(lam_ref, lam_init):
    lv = lam_ref[...]
    a = jnp.sum(lv[0:1] * lv[1:2], axis=-1, keepdims=True)
    b = jnp.sum(lv[2:3] * lv[3:4], axis=-1, keepdims=True)
    return jnp.exp(a) - jnp.exp(b) + lam_init


def _stack_masked(q_t):
    first = lax.broadcasted_iota(jnp.int32, q_t.shape, 0) < HEAD_DIM
    return jnp.concatenate([jnp.where(first, q_t, 0.0), jnp.where(first, 0.0, q_t)],
                           axis=1).astype(BF16)


def _scores_chunk(k, qq_t, m):
    s = jnp.dot(k, qq_t, preferred_element_type=F32)
    mc = jnp.max(s, axis=0, keepdims=True)
    return s, (mc if m is None else jnp.maximum(m, mc))


def _with_ones_rows(v_t):
    return jnp.concatenate([v_t, jnp.ones((ONES_ROWS, v_t.shape[1]), F32)], axis=0).astype(BF16)


def _values_chunk(s, m, v1_t, acc):
    e = jnp.exp2(s - m).astype(BF16)
    oc = jnp.dot(v1_t, e, preferred_element_type=F32)
    return oc if acc is None else acc + oc


def _combine_t(acc, lam):
    tq = acc.shape[1] // 2
    r = 1.0 / acc[QK_W:QK_W + 1, :]
    o_t = acc[:QK_W, :]
    return o_t[:, :tq] * r[:, :tq] - o_t[:, tq:] * (r[:, tq:] * lam)


def _subln_gate(o, sg, sgc, lam_init):
    ms = jnp.mean(o * o, axis=-1, keepdims=True)
    y = o * lax.rsqrt(ms + EPS) * sg * (1.0 - lam_init)
    return (y * sgc.astype(F32)).astype(BF16)


def _attn_ctx_kernel(lam_ref, sg_ref, q_ref, k_ref, v_ref, sgc_ref, o_ref, *, lam_init, seq):
    lam = _lambda(lam_ref, lam_init)
    sg = sg_ref[...]
    n_seq = q_ref.shape[0] // seq
    q_t = q_ref[...].astype(F32).T
    v_t = v_ref[...].astype(F32).T
    pairs = [(slice(s * seq, (s + 1) * seq), slice(hd * QK_W, (hd + 1) * QK_W))
             for s in range(n_seq) for hd in range(N_HEADS)]
    qqs = [_stack_masked(q_t[hl, rows]) for rows, hl in pairs]
    v1s = [_with_ones_rows(v_t[hl, rows]) for rows, hl in pairs]
    sm = [_scores_chunk(k_ref[rows, hl], qq, None) for (rows, hl), qq in zip(pairs, qqs)]
    accs = [_values_chunk(s, m, v1, None) for (s, m), v1 in zip(sm, v1s)]
    y_t = [_combine_t(a, lam) for a in accs]
    for s in range(n_seq):
        rows = slice(s * seq, (s + 1) * seq)
        y = jnp.concatenate(y_t[s * N_HEADS:(s + 1) * N_HEADS], axis=0).T
        for hd in range(N_HEADS):
            hl = slice(hd * QK_W, (hd + 1) * QK_W)
            o_ref[rows, hl] = _subln_gate(y[:, hl], sg, sgc_ref[rows, hl], lam_init)


def _attn_den_kernel(lam_ref, sg_ref, q_ref, kn_ref, vn_ref, kc_ref, vc_ref, sgc_ref, o_ref,
                     kcs_ref, vt_ref, qq0_ref, qq1_ref, s0_ref, s1_ref, acc0_ref, acc1_ref,
                     *, lam_init):
    seq = q_ref.shape[0]
    past = kcs_ref.shape[1]
    past_chunks = past // KEY_CHUNK
    n_chunks = (past + seq) // KEY_CHUNK
    n_tiles = seq // Q_TILE
    n_elems = n_tiles * N_HEADS
    qq_refs = (qq0_ref, qq1_ref)
    s_refs = (s0_ref, s1_ref)
    acc_refs = (acc0_ref, acc1_ref)
    lam = _lambda(lam_ref, lam_init)
    sg = sg_ref[...]

    def lanes(hd):
        return slice(hd * QK_W, (hd + 1) * QK_W)

    def chunk(c):
        return slice(c * KEY_CHUNK, (c + 1) * KEY_CHUNK)

    def tile_rows(t):
        if isinstance(t, int):
            return pl.ds(t * Q_TILE, Q_TILE)
        return pl.ds(pl.multiple_of(t * Q_TILE, Q_TILE), Q_TILE)

    for hd in range(N_HEADS):
        cache_rows = pl.ds(hd, past, stride=N_HEADS)
        kcs_ref[hd] = kc_ref[0, 0, cache_rows, :].astype(BF16)
        vt_ref[hd, :, :past] = _with_ones_rows(vc_ref[0, 0, cache_rows, :].T)
        vt_ref[hd, :, past:] = _with_ones_rows(vn_ref[:, lanes(hd)].astype(F32).T)

    def keys(hd, c):
        if c < past_chunks:
            return kcs_ref[hd, chunk(c), :]
        return kn_ref[chunk(c - past_chunks), lanes(hd)]

    def prep(t, hd, par):
        qq_refs[par][...] = _stack_masked(q_ref[tile_rows(t), lanes(hd)].astype(F32).T)

    def finish(t, hd, par):
        rows = tile_rows(t)
        y = _combine_t(acc_refs[par][...], lam).T
        o_ref[rows, lanes(hd)] = _subln_gate(y, sg, sgc_ref[rows, lanes(hd)], lam_init)

    def step(t, j, m_cur, first=False, last=False):
        def elem(off):
            return t + (j + off) // N_HEADS, (j + off) % N_HEADS

        par = j % 2
        e_static = N_HEADS * t + j if isinstance(t, int) else None
        do_finish = not (first and j == 0)
        do_scores = not (last and e_static + 1 >= n_elems)
        do_prep = not (last and e_static + 2 >= n_elems)
        if do_prep:
            prep(*elem(2), par)
        hd_nxt = elem(1)[1]
        qq = qq_refs[1 - par][...] if do_scores else None
        m_nxt = None
        acc = None
        for c in range(n_chunks):
            if do_scores:
                s, m_nxt = _scores_chunk(keys(hd_nxt, c), qq, m_nxt)
                s_refs[1 - par][chunk(c), :] = s
            acc = _values_chunk(s_refs[par][chunk(c), :], m_cur, vt_ref[j, :, chunk(c)], acc)
        acc_refs[par][...] = acc
        if do_finish:
            finish(*elem(-1), 1 - par)
        return m_nxt

    def tile_steps(t, m, **edge):
        for j in range(N_HEADS):
            m = step(t, j, m, **edge)
        return m

    prep(0, 0, 0)
    prep(0, 1, 1)
    qq = qq0_ref[...]
    m = None
    for c in range(n_chunks):
        s, m = _scores_chunk(keys(0, c), qq, m)
        s0_ref[chunk(c), :] = s
    m = tile_steps(0, m, first=True)
    m = lax.fori_loop(1, n_tiles - 1, tile_steps, m)
    tile_steps(n_tiles - 1, m, last=True)
    finish(n_tiles - 1, N_HEADS - 1, (n_elems - 1) % 2)


def _den_mixers_kernel(lam_ref, sg_ref, q_ref, kn_ref, vn_ref, kc_ref, vc_ref, sgc_ref,
                       xw_ref, sga_ref, f1_ref, twc_ref, tws_ref, cs2_ref, yc_ref, ya_ref,
                       *scratch, layer, lam_init):
    lam_ref, sg_ref = lam_ref.at[layer], sg_ref.at[layer:layer + 1]
    g_ref, x_ref = scratch[-2:]
    _fourier_den_kernel(xw_ref, sga_ref, f1_ref, twc_ref, tws_ref, cs2_ref, ya_ref, g_ref, x_ref)
    _attn_den_kernel(lam_ref, sg_ref, q_ref, kn_ref, vn_ref, kc_ref, vc_ref, sgc_ref, yc_ref,
                     *scratch[:-2], lam_init=lam_init)


def _den_mixers(q, k, v, sgc, cache_k, cache_v, xw, sga, lam_vec, subln_g, f1, twc, tws, cs2,
                *, layer, seq, lam_init):
    n_tok = q.shape[0]
    past = cache_k.shape[2] // N_HEADS
    n2 = seq // FFT_RADIX
    wide = FFT_RADIX * FOURIER_W
    blk = lambda w: pl.BlockSpec((seq, w), lambda b: (b, 0))
    cblk = pl.BlockSpec((1, 1, past * N_HEADS, QK_W), lambda b: (b, layer, 0, 0))
    const = lambda shape: pl.BlockSpec(shape, lambda b: (0,) * len(shape))
    return pl.pallas_call(
        functools.partial(_den_mixers_kernel, layer=layer, lam_init=lam_init),
        grid=(n_tok // seq,),
        in_specs=[
            const(lam_vec.shape), const(subln_g.shape),
            blk(ATTN_W), blk(ATTN_W), blk(ATTN_W), cblk, cblk, blk(ATTN_W),
            pl.BlockSpec((n2, wide), lambda b: (b, 0)), blk(FOURIER_W),
            const((2 * n2, n2)), const((n2, wide)), const((n2, wide)),
            const((2 * FOURIER_W, FOURIER_W)),
        ],
        out_specs=[blk(ATTN_W), blk(FOURIER_W)],
        out_shape=[jax.ShapeDtypeStruct((n_tok, ATTN_W), BF16),
                   jax.ShapeDtypeStruct((n_tok, FOURIER_W), BF16)],
        scratch_shapes=[
            pltpu.VMEM((N_HEADS, past, QK_W), BF16),
            pltpu.VMEM((N_HEADS, QK_W + ONES_ROWS, past + seq), BF16),
            pltpu.VMEM((QK_W, 2 * Q_TILE), BF16),
            pltpu.VMEM((QK_W, 2 * Q_TILE), BF16),
            pltpu.VMEM((past + seq, 2 * Q_TILE), F32),
            pltpu.VMEM((past + seq, 2 * Q_TILE), F32),
            pltpu.VMEM((QK_W + ONES_ROWS, 2 * Q_TILE), F32),
            pltpu.VMEM((QK_W + ONES_ROWS, 2 * Q_TILE), F32),
            pltpu.VMEM((2 * n2, wide), F32),
            pltpu.VMEM((seq, 2 * FOURIER_W), BF16),
        ],
        compiler_params=_cparams(("arbitrary",)),
        name="den_mixers",
    )(lam_vec, subln_g, q, k, v, cache_k, cache_v, sgc, xw, sga, f1, twc, tws, cs2)


def _fourier_ctx_kernel(fa_ref, sga_ref, cs_ref, dn_ref, o_ref, *, seq):
    ab = jnp.dot(fa_ref[...], cs_ref[...], preferred_element_type=F32).astype(BF16)
    for s in range(fa_ref.shape[0] // seq):
        rows = slice(s * seq, (s + 1) * seq)
        ab2 = jnp.concatenate([ab[rows, :FOURIER_W], ab[rows, FOURIER_W:]], axis=0)
        f = jnp.dot(dn_ref[...], ab2, preferred_element_type=F32)
        o_ref[rows, :] = (f * sga_ref[rows, :].astype(F32)).astype(BF16)


def _ctx_mixers_kernel(lam_ref, sg_ref, q_ref, k_ref, v_ref, sgc_ref, fa_ref, sga_ref, cs_ref, dn_ref,
                       *rest, layer, lam_init, seq):
    lam_ref, sg_ref = lam_ref.at[layer], sg_ref.at[layer:layer + 1]
    if len(rest) > 2:
        cctx_ref, c_ref, wm_ref, bm_ref, yc_ref, ya_ref, mod_ref, s_ref = rest
        _mod_kernel(cctx_ref, c_ref, wm_ref, bm_ref, mod_ref, s_ref)
    else:
        yc_ref, ya_ref = rest
    _attn_ctx_kernel(lam_ref, sg_ref, q_ref, k_ref, v_ref, sgc_ref, yc_ref, lam_init=lam_init, seq=seq)
    _fourier_ctx_kernel(fa_ref, sga_ref, cs_ref, dn_ref, ya_ref, seq=seq)


def _ctx_mixers(q, k, v, sgc, fa, sga, lam_vec, subln_g, cs, dn, next_mod, *, layer, seq, lam_init):
    n_tok = q.shape[0]
    rows = CTX_SEQS * seq
    n_steps = n_tok // rows
    blk = lambda w: pl.BlockSpec((rows, w), lambda b: (b, 0))
    const = lambda shape: pl.BlockSpec(shape, lambda b: (0,) * len(shape))
    in_specs = [
        const(lam_vec.shape), const(subln_g.shape),
        blk(ATTN_W), blk(ATTN_W), blk(ATTN_W), blk(ATTN_W),
        blk(FOURIER_W), blk(FOURIER_W),
        const((FOURIER_W, 2 * FOURIER_W)), const((seq, 2 * seq)),
    ]
    args = [lam_vec, subln_g, q, k, v, sgc, fa, sga, cs, dn]
    out_specs = [blk(ATTN_W), blk(FOURIER_W)]
    out_shape = [jax.ShapeDtypeStruct((n_tok, ATTN_W), BF16),
                 jax.ShapeDtypeStruct((n_tok, FOURIER_W), BF16)]
    scratch = []
    if next_mod is not None:
        c_ctx, c, w_mod, b_mod = next_mod
        mod_in, mod_out = _mod_specs(c, layer + 1, 3 * D_MODEL // n_steps, lambda b: b)
        in_specs += mod_in
        args += [c_ctx.reshape(1, D_MODEL), c, w_mod, b_mod.reshape(DEPTH, 1, 3 * D_MODEL)]
        out_specs.append(mod_out)
        out_shape.append(jax.ShapeDtypeStruct((1, MOD_ROWS, 1, 3 * D_MODEL), F32))
        scratch = [pltpu.VMEM((MOD_ROWS, D_MODEL), F32)]
    return pl.pallas_call(
        functools.partial(_ctx_mixers_kernel, layer=layer, lam_init=lam_init, seq=seq),
        grid=(n_steps,),
        in_specs=in_specs,
        out_specs=out_specs,
        out_shape=out_shape,
        scratch_shapes=scratch,
        compiler_params=_cparams(("arbitrary",)),
        name="ctx_mixers",
    )(*args)


def _cadd(a, b):
    return (a[0] + b[0], a[1] + b[1])


def _csub(a, b):
    return (a[0] - b[0], a[1] - b[1])


def _mul_neg_i(a):
    return (a[1], -a[0])


def _mul_w8_1(a):
    return ((a[0] + a[1]) * SQRT_HALF, (a[1] - a[0]) * SQRT_HALF)


def _mul_w8_3(a):
    return ((a[1] - a[0]) * SQRT_HALF, (-a[0] - a[1]) * SQRT_HALF)


def _fft4(a0, a1, a2, a3):
    e0, e1 = _cadd(a0, a2), _csub(a0, a2)
    o0, o1 = _cadd(a1, a3), _mul_neg_i(_csub(a1, a3))
    return [_cadd(e0, o0), _cadd(e1, o1), _csub(e0, o0), _csub(e1, o1)]


def _fft8(x):
    e = _fft4(x[0], x[2], x[4], x[6])
    o = _fft4(x[1], x[3], x[5], x[7])
    t = [o[0], _mul_w8_1(o[1]), _mul_neg_i(o[2]), _mul_w8_3(o[3])]
    return [_cadd(e[k], t[k]) for k in range(4)] + [_csub(e[k], t[k]) for k in range(4)]


def _fourier_den_kernel(xw_ref, sga_ref, f1_ref, twc_ref, tws_ref, cs2_ref, o_ref, g_ref, x_ref):
    n2 = xw_ref.shape[0]
    g_ref[...] = jnp.dot(f1_ref[...], xw_ref[...], preferred_element_type=F32)

    def chunk(i):
        r = i * FFT_ROWS
        re_rows = pl.ds(r, FFT_ROWS)
        im_rows = pl.ds(n2 + r, FFT_ROWS)
        xs = []
        for n1 in range(FFT_RADIX):
            lanes = slice(n1 * FOURIER_W, (n1 + 1) * FOURIER_W)
            gr = g_ref[re_rows, lanes]
            gi = g_ref[im_rows, lanes]
            if n1 > 0:
                c = twc_ref[re_rows, lanes]
                s = tws_ref[re_rows, lanes]
                gr, gi = gr * c + gi * s, gi * c - gr * s
            xs.append((gr, gi))
        for k1, (xr, xi) in enumerate(_fft8(xs)):
            out_rows = pl.ds(k1 * n2 + r, FFT_ROWS)
            x_ref[out_rows, :FOURIER_W] = xr.astype(BF16)
            x_ref[out_rows, FOURIER_W:] = xi.astype(BF16)

    for i in range(n2 // FFT_ROWS):
        chunk(i)
    f = jnp.dot(x_ref[...], cs2_ref[...], preferred_element_type=F32)
    o_ref[...] = (f * sga_ref[...].astype(F32)).astype(BF16)


def _outproj_tile(x_ref, mod_ref, ya_ref, zc_ref, zp_ref, zn_ref, bgs_ref, yc_ref,
                  cw_ref, cb_ref, w_ref, fg_ref, o_ref, tile, *, seq, final_norm):
    tm = x_ref.shape[0]
    z = zc_ref[...].astype(F32)
    row = lax.broadcasted_iota(jnp.int32, z.shape, 0)
    pos = (tile * tm + row) & (seq - 1)
    prev_row = zp_ref[HALO_ROWS - 1:HALO_ROWS, :].astype(F32)
    next_row = zn_ref[0:1, :].astype(F32)
    z_prev = jnp.where(row == 0, prev_row, pltpu.roll(z, 1, axis=0))
    z_prev = jnp.where(pos == 0, 0.0, z_prev)
    z_next = jnp.where(row == tm - 1, next_row, pltpu.roll(z, tm - 1, axis=0))
    z_next = jnp.where(pos == seq - 1, 0.0, z_next)
    cw = cw_ref[...]
    conv = z_prev * cw[0:1] + z * cw[1:2] + z_next * cw[2:3] + cb_ref[...]
    yb = (bgs_ref[...].astype(F32) * conv).astype(BF16)

    mixed = jnp.concatenate([ya_ref[...], yb, yc_ref[...]], axis=-1)
    out = jnp.dot(mixed, w_ref[...], preferred_element_type=F32)
    gate = mod_ref[0, 0][:, 2 * D_MODEL:]
    xn = x_ref[...] + gate * out
    if final_norm:
        ms = jnp.mean(xn * xn, axis=-1, keepdims=True)
        xn = xn * lax.rsqrt(ms + EPS) * fg_ref[...]
    o_ref[...] = xn


N_GROUP_INS = 8


def _outproj_kernel(*refs, layer, n_ctx_tiles, seq, dec_seq, final_norm):
    ctx_ins = refs[:N_GROUP_INS]
    den_ins = refs[N_GROUP_INS:2 * N_GROUP_INS]
    cw_ref, cb_ref, w_ref, fg_ref, oc_ref, od_ref, wb_ref = refs[2 * N_GROUP_INS:]
    cw_ref, cb_ref = cw_ref.at[layer], cb_ref.at[layer:layer + 1]
    i = pl.program_id(0)

    @pl.when(i == 0)
    def _():
        wb_ref[...] = w_ref[0].astype(BF16)

    @pl.when(i < n_ctx_tiles)
    def _():
        _outproj_tile(*ctx_ins, cw_ref, cb_ref, wb_ref, fg_ref, oc_ref, i, seq=seq,
                      final_norm=final_norm)

    @pl.when(i >= n_ctx_tiles)
    def _():
        _outproj_tile(*den_ins, cw_ref, cb_ref, wb_ref, fg_ref, od_ref, i - n_ctx_tiles,
                      seq=dec_seq, final_norm=final_norm)


def _out_projection(ctx, den, mod, conv_w, conv_b, w_out, layer, final_g, *, seq, dec_seq,
                    final_norm):
    tm = OUT_TILE
    nct = ctx[0].shape[0] // tm
    ndt = den[0].shape[0] // tm
    halo_per_tile = tm // HALO_ROWS
    seq_tiles = dec_seq // tm

    def group_specs(tile_of, n_tok, mod_of):
        n_halo = n_tok // HALO_ROWS
        row = lambda w: pl.BlockSpec((tm, w), lambda i: (tile_of(i), 0))
        return [
            row(D_MODEL),
            pl.BlockSpec((1, 1, 1, 3 * D_MODEL), lambda i: (0, mod_of(tile_of(i)), 0, 0)),
            row(FOURIER_W),
            row(CONV_W),
            pl.BlockSpec((HALO_ROWS, CONV_W),
                         lambda i: (jnp.maximum(tile_of(i) * halo_per_tile - 1, 0), 0)),
            pl.BlockSpec((HALO_ROWS, CONV_W),
                         lambda i: (jnp.minimum((tile_of(i) + 1) * halo_per_tile, n_halo - 1), 0)),
            row(CONV_W),
            row(ATTN_W),
        ], row(D_MODEL)

    ctx_specs, ctx_out = group_specs(lambda i: jnp.minimum(i, nct - 1), ctx[0].shape[0],
                                     lambda t: 0)
    den_specs, den_out = group_specs(lambda i: jnp.maximum(i - nct, 0), den[0].shape[0],
                                     lambda t: 1 + t // seq_tiles)

    def group_args(g):
        x, ya, zc, bgs, yc = g
        return [x, mod, ya, zc, zc, zc, bgs, yc]

    return pl.pallas_call(
        functools.partial(_outproj_kernel, layer=layer, n_ctx_tiles=nct, seq=seq, dec_seq=dec_seq,
                          final_norm=final_norm),
        grid=(nct + ndt,),
        in_specs=ctx_specs + den_specs + [
            pl.BlockSpec(conv_w.shape, lambda i: (0, 0, 0)),
            pl.BlockSpec(conv_b.shape, lambda i: (0, 0)),
            pl.BlockSpec((1, D_MODEL, D_MODEL), lambda i: (layer, 0, 0)),
            pl.BlockSpec((1, D_MODEL), lambda i: (0, 0)),
        ],
        out_specs=[ctx_out, den_out],
        out_shape=[jax.ShapeDtypeStruct(ctx[0].shape, F32), jax.ShapeDtypeStruct(den[0].shape, F32)],
        scratch_shapes=[pltpu.VMEM((D_MODEL, D_MODEL), BF16)],
        compiler_params=_cparams(("arbitrary",)),
        name="out_projection",
    )(*group_args(ctx), *group_args(den), conv_w, conv_b, w_out,
      final_g.reshape(1, D_MODEL))


def kernel(x_prompt, x_sample, cache_k, cache_v, c, c_ctx, norm_g, w_mod, b_mod, w_in, conv_w,
           conv_b, lam_vec, subln_g, w_out, final_g):
    batch, seq, _ = x_prompt.shape
    dec_batch, dec_seq, _ = x_sample.shape

    mod = _modulation(c_ctx, c, w_mod, b_mod, 0)

    rope_tabs = tuple(jnp.asarray(t) for t in _rope_tables(dec_seq))
    cs = jnp.asarray(_chan_tables()).astype(BF16)
    dn_ctx = jnp.asarray(_dft_tables(seq)).astype(BF16)
    f1, twc, tws, cs2 = (jnp.asarray(t) for t in _ct_tables(dec_seq))
    f1 = f1.astype(BF16)
    cs2 = cs2.astype(BF16)

    cache_k2 = cache_k.reshape(dec_batch, DEPTH, -1, QK_W)
    cache_v2 = cache_v.reshape(dec_batch, DEPTH, -1, QK_W)
    xc = x_prompt.reshape(batch * seq, D_MODEL)
    xl = x_sample.reshape(dec_batch * dec_seq, D_MODEL)
    kv = None
    for l in range(DEPTH):
        lam_init = 0.8 - 0.6 * math.exp(-0.3 * l)
        ctx, k32, v32, den = _in_projection(
            xc, xl, mod, norm_g, w_in, l, rope_tabs, seq=seq, dec_seq=dec_seq,
            prev_kv=kv)
        kv = (k32, v32)

        fa, sga, zc_c, bgs_c, q, k, v, sgc = ctx
        next_mod = (c_ctx, c, w_mod, b_mod) if l + 1 < DEPTH else None
        yc_c, ya_c, *mod_next = _ctx_mixers(q, k, v, sgc, fa, sga, lam_vec, subln_g, cs, dn_ctx,
                                            next_mod, layer=l, seq=seq, lam_init=lam_init)

        fa, sga, zc_d, bgs_d, q, k, v, sgc = den
        yc_d, ya_d = _den_mixers(q, k, v, sgc, cache_k2, cache_v2, fa, sga, lam_vec, subln_g,
                                 f1, twc, tws, cs2, layer=l, seq=dec_seq, lam_init=lam_init)

        xc, xl = _out_projection(
            (xc, ya_c, zc_c, bgs_c, yc_c), (xl, ya_d, zc_d, bgs_d, yc_d), mod,
            conv_w, conv_b, w_out, l, final_g, seq=seq, dec_seq=dec_seq,
            final_norm=l == DEPTH - 1)
        mod = mod_next[0] if mod_next else None

    y_prompt = xc.reshape(batch, seq, D_MODEL)
    y_sample = xl.reshape(dec_batch, dec_seq, D_MODEL)
    return (y_prompt, y_sample, *(t.reshape(batch, DEPTH, seq, N_HEADS, QK_W) for t in kv))
```

```python
import functools
import math

import numpy as np
import jax
import jax.numpy as jnp
from jax import lax
from jax.experimental import pallas as pl
from jax.experimental.pallas import tpu as pltpu

D_MODEL = 1024
DEPTH = 2
GRID_W = 64
FOURIER_W = 256
CONV_W = 256
ATTN_W = 512
N_HEADS = 4
HEAD_DIM = 64
QK_W = 128
ROPE_BASE = 10000.0
ROPE_W = HEAD_DIM // 2
ROPE_ROT = ROPE_W // 2
EPS = 1e-6
IN_DIM = 3584

F32 = jnp.float32
BF16 = jnp.bfloat16

VMEM_LIMIT_BYTES = 60 * 1024 * 1024
TOKEN_TILE = 512
OUT_TILE = 1024
Q_TILE = 256
KEY_CHUNK = 256
ONES_ROWS = 16
LANES = 128
FFT_RADIX = 8
FFT_ROWS = 16
SQRT_HALF = 0.7071067811865476
CTX_SEQS = 4
HALO_ROWS = 16
MOD_ROWS = 8
MOD_CHUNK = 3 * D_MODEL
LOG2E = 1.4426950408889634

_COLS = {}
_off = 0
for _name, _w in (("fa", 256), ("ga", 256), ("bg", 256), ("cg", 256), ("hc", 256), ("gb", 256),
                  ("q", 512), ("k", 512), ("v", 512), ("gc", 512)):
    _COLS[_name] = (_off, _off + _w)
    _off += _w


def _silu(x):
    return x * (1.0 / (1.0 + jnp.exp(-x)))


def _cparams(sem):
    return pltpu.CompilerParams(dimension_semantics=sem, vmem_limit_bytes=VMEM_LIMIT_BYTES)


def _rope_tables(n_tokens):
    n = np.arange(n_tokens)
    row = (n // GRID_W).astype(np.float64)
    col = (n % GRID_W).astype(np.float64)
    j = np.arange(QK_W)
    jj = j % HEAD_DIM
    idx = jj % ROPE_W
    inv = 1.0 / (ROPE_BASE ** (2.0 * (idx % ROPE_ROT) / ROPE_W))
    pos = np.where((jj < ROPE_W)[None, :], row[:, None], col[:, None])
    ang = pos * inv[None, :]
    cos = np.cos(ang)
    sin = np.sin(ang)
    first = (idx < ROPE_ROT)[None, :]
    sin_a = np.where(first, -sin, 0.0)
    sin_b = np.where(first, 0.0, sin)
    return (np.asarray(cos, np.float32), np.asarray(sin_a, np.float32), np.asarray(sin_b, np.float32))


def _dft_tables(n):
    k = np.arange(n)
    kn = (k[:, None] * k[None, :]) % n
    ang = 2.0 * np.pi * kn / n
    return np.asarray(np.concatenate([np.cos(ang), -np.sin(ang)], axis=1) / math.sqrt(n), np.float32)


def _chan_tables():
    k = np.arange(FOURIER_W)
    kn = (k[:, None] * k[None, :]) % FOURIER_W
    ang = 2.0 * np.pi * kn / FOURIER_W
    return np.asarray(np.concatenate([np.cos(ang), np.sin(ang)], axis=1) / math.sqrt(FOURIER_W), np.float32)


def _ct_tables(n):
    n2 = n // FFT_RADIX
    k = np.arange(n2)
    ang = 2.0 * np.pi * ((k[:, None] * k[None, :]) % n2) / n2
    f1 = np.concatenate([np.cos(ang), -np.sin(ang)], axis=0) / math.sqrt(n * FOURIER_W)
    tw = 2.0 * np.pi * k[:, None] * np.arange(FFT_RADIX)[None, :] / n
    twc = np.repeat(np.cos(tw), FOURIER_W, axis=1)
    tws = np.repeat(np.sin(tw), FOURIER_W, axis=1)
    c = np.arange(FOURIER_W)
    angc = 2.0 * np.pi * ((c[:, None] * c[None, :]) % FOURIER_W) / FOURIER_W
    cs2 = np.concatenate([np.cos(angc), np.sin(angc)], axis=0)
    return tuple(np.asarray(t, np.float32) for t in (f1, twc, tws, cs2))


def _mod_kernel(cctx_ref, c_ref, w_ref, b_ref, o_ref, s_ref):
    n_c = c_ref.shape[0]
    s_ref[...] = jnp.zeros_like(s_ref)
    s_ref[0:1, :] = _silu(cctx_ref[...])
    s_ref[1:1 + n_c, :] = _silu(c_ref[...])
    w = w_ref[0].astype(BF16)
    res = jnp.dot(s_ref[...].astype(BF16), w, preferred_element_type=F32) + b_ref[0]
    for r in range(MOD_ROWS):
        o_ref[0, r] = res[r:r + 1]


def _modulation(c_ctx, c, w_mod, b_mod):
    chunk = MOD_CHUNK
    n_chunks = 3 * D_MODEL // chunk
    return pl.pallas_call(
        _mod_kernel,
        grid=(DEPTH, n_chunks),
        in_specs=[
            pl.BlockSpec((1, D_MODEL), lambda l, j: (0, 0)),
            pl.BlockSpec(c.shape, lambda l, j: (0, 0)),
            pl.BlockSpec((1, D_MODEL, chunk), lambda l, j: (l, 0, j)),
            pl.BlockSpec((1, 1, chunk), lambda l, j: (l, 0, j)),
        ],
        out_specs=pl.BlockSpec((1, MOD_ROWS, 1, chunk), lambda l, j: (l, 0, 0, j)),
        out_shape=jax.ShapeDtypeStruct((DEPTH, MOD_ROWS, 1, 3 * D_MODEL), F32),
        scratch_shapes=[pltpu.VMEM((MOD_ROWS, D_MODEL), F32)],
        compiler_params=_cparams(("arbitrary", "arbitrary")),
        name="modulation",
    )(c_ctx.reshape(1, D_MODEL), c, w_mod, b_mod.reshape(DEPTH, 1, 3 * D_MODEL))


def _store_heads(ref, layer, t):
    seqs, _, rows, _ = ref.shape
    seq = rows // N_HEADS
    for s in range(seqs):
        for hd in range(N_HEADS):
            ref[s, layer, pl.ds(hd, seq, stride=N_HEADS), :] = (
                t[s * seq:(s + 1) * seq, hd * QK_W:(hd + 1) * QK_W])


def _inproj_tile(x_ref, mod_ref, g_ref, w_ref, rope_refs, outs, kv_refs, stage_ref):
    fa_ref, sga_ref, zc_ref, bgs_ref, q_ref, k_ref, v_ref, sgc_ref = outs

    x = x_ref[...]
    ms = jnp.mean(x * x, axis=-1, keepdims=True)
    y = x * lax.rsqrt(ms + EPS) * g_ref[...]
    m = mod_ref[0, 0]
    h = (y * (1.0 + m[:, D_MODEL:2 * D_MODEL]) + m[:, :D_MODEL]).astype(BF16)

    def proj(name):
        lo, hi = _COLS[name]
        return jnp.dot(h, w_ref[:, lo:hi], preferred_element_type=F32)

    if stage_ref is not None:
        fa = proj("fa")
        halves = stage_ref.shape[0]
        for hf in range(halves):
            stage_ref[hf] = fa[:, hf * LANES:(hf + 1) * LANES]
        rows = stage_ref.shape[1] // FFT_RADIX
        for n1 in range(FFT_RADIX):
            for hf in range(halves):
                lo = n1 * FOURIER_W + hf * LANES
                fa_ref[:, lo:lo + LANES] = (
                    stage_ref[hf, pl.ds(n1, rows, stride=FFT_RADIX), :].astype(BF16))
    else:
        fa_ref[...] = proj("fa").astype(BF16)
    sga_ref[...] = _silu(proj("ga")).astype(BF16)
    zc_ref[...] = (proj("cg") * proj("hc")).astype(BF16)
    bgs_ref[...] = (proj("bg") * _silu(proj("gb"))).astype(BF16)
    sgc_ref[...] = _silu(proj("gc")).astype(BF16)

    q = proj("q")
    k = proj("k")
    v = proj("v")
    if kv_refs is not None and len(kv_refs) == 2:
        kv_refs[0][...] = k
        kv_refs[1][...] = v
    elif kv_refs is not None:
        new_k_ref, new_v_ref, prev_k_ref, prev_v_ref = kv_refs
        _store_heads(new_k_ref, 0, prev_k_ref[...])
        _store_heads(new_v_ref, 0, prev_v_ref[...])
        _store_heads(new_k_ref, 1, k)
        _store_heads(new_v_ref, 1, v)
    v_ref[...] = v.astype(BF16)

    q_scale = HEAD_DIM ** -0.5 * LOG2E
    if rope_refs is not None:
        cos = rope_refs[0][...]
        sa = rope_refs[1][...]
        sb = rope_refs[2][...]
        for hd in range(N_HEADS):
            sl = slice(hd * QK_W, (hd + 1) * QK_W)
            for t, ref, scale in ((q, q_ref, q_scale), (k, k_ref, None)):
                th = t[:, sl]
                r = (th * cos + pltpu.roll(th, QK_W - ROPE_ROT, axis=1) * sa
                     + pltpu.roll(th, ROPE_ROT, axis=1) * sb)
                if scale is not None:
                    r = r * scale
                ref[:, sl] = r.astype(BF16)
    else:
        q_ref[...] = (q * q_scale).astype(BF16)
        k_ref[...] = k.astype(BF16)


N_PROJ_OUTS = 8


def _inproj_kernel(xc_ref, xd_ref, modc_ref, modd_ref, g_ref, w_ref, cos_ref, sa_ref, sb_ref,
                   *rest, layer, n_ctx_tiles, n_prev):
    prev_kv = rest[:n_prev]
    rest = rest[n_prev:]
    ctx_outs = rest[:N_PROJ_OUTS]
    kv_refs = rest[N_PROJ_OUTS:N_PROJ_OUTS + 2] + prev_kv
    den_outs = rest[N_PROJ_OUTS + 2:2 * N_PROJ_OUTS + 2]
    wb_ref, stage_ref = rest[2 * N_PROJ_OUTS + 2:]
    g_ref = g_ref.at[layer:layer + 1]
    i = pl.program_id(0)

    @pl.when(i == 0)
    def _():
        for lo, hi in _COLS.values():
            wb_ref[:, lo:hi] = w_ref[0, :, lo:hi].astype(BF16)

    @pl.when(i < n_ctx_tiles)
    def _():
        _inproj_tile(xc_ref, modc_ref, g_ref, wb_ref, None, ctx_outs, kv_refs, None)

    @pl.when(i >= n_ctx_tiles)
    def _():
        _inproj_tile(xd_ref, modd_ref, g_ref, wb_ref, (cos_ref, sa_ref, sb_ref), den_outs, None,
                     stage_ref)


def _in_projection(xc, xd, mod, norm_g, w_in, layer, rope_tabs, *, seq, dec_seq,
                   prev_kv):
    assert DEPTH == 2, "the last layer's call assembles the k/v of exactly two layers"
    tm = TOKEN_TILE
    n_c, n_d = xc.shape[0], xd.shape[0]
    nct, ndt = n_c // tm, n_d // tm
    seq_tiles = dec_seq // tm
    ctx_i = lambda i: jnp.minimum(i, nct - 1)
    den_i = lambda i: jnp.maximum(i - nct, 0)
    row_c = lambda w: pl.BlockSpec((tm, w), lambda i: (ctx_i(i), 0))
    row_d = lambda w: pl.BlockSpec((tm, w), lambda i: (den_i(i), 0))
    tab = pl.BlockSpec((tm, QK_W), lambda i: (den_i(i) % seq_tiles, 0))
    in_specs = [
        row_c(D_MODEL), row_d(D_MODEL),
        pl.BlockSpec((1, 1, 1, 3 * D_MODEL), lambda i: (layer, 0, 0, 0)),
        pl.BlockSpec((1, 1, 1, 3 * D_MODEL), lambda i: (layer, 1 + den_i(i) // seq_tiles, 0, 0)),
        pl.BlockSpec((DEPTH, D_MODEL), lambda i: (0, 0)),
        pl.BlockSpec((1, D_MODEL, IN_DIM), lambda i: (layer, 0, 0)),
        tab, tab, tab,
    ]
    args = [xc, xd, mod, mod, norm_g, w_in, *rope_tabs]
    widths = (256, 256, 256, 256, 512, 512, 512, 512)
    seqs = tm // seq
    if prev_kv is None:
        kv_blk = row_c(ATTN_W)
        kv_shape = jax.ShapeDtypeStruct((n_c, ATTN_W), F32)
    else:
        kv_blk = pl.BlockSpec((seqs, DEPTH, seq * N_HEADS, QK_W), lambda i: (ctx_i(i), 0, 0, 0))
        kv_shape = jax.ShapeDtypeStruct((n_c // seq, DEPTH, seq * N_HEADS, QK_W), F32)
        in_specs += [row_c(ATTN_W)] * 2
        args += list(prev_kv)
    out_specs = [row_c(w) for w in widths] + [kv_blk, kv_blk] + [row_d(w) for w in widths]
    out_shape = ([jax.ShapeDtypeStruct((n_c, w), BF16) for w in widths] + [kv_shape, kv_shape]
                 + [jax.ShapeDtypeStruct((n_d, w), BF16) for w in widths])
    fa_d = N_PROJ_OUTS + 2
    out_specs[fa_d] = pl.BlockSpec((tm // FFT_RADIX, FFT_RADIX * FOURIER_W), lambda i: (den_i(i), 0))
    out_shape[fa_d] = jax.ShapeDtypeStruct((n_d // FFT_RADIX, FFT_RADIX * FOURIER_W), BF16)
    outs = pl.pallas_call(
        functools.partial(_inproj_kernel, layer=layer, n_ctx_tiles=nct,
                          n_prev=0 if prev_kv is None else 2),
        grid=(nct + ndt,),
        in_specs=in_specs,
        out_specs=out_specs,
        out_shape=out_shape,
        scratch_shapes=[pltpu.VMEM((D_MODEL, IN_DIM), BF16),
                        pltpu.VMEM((FOURIER_W // LANES, tm, LANES), F32)],
        compiler_params=_cparams(("arbitrary",)),
        name="in_projection",
    )(*args)
    return outs[:N_PROJ_OUTS], outs[N_PROJ_OUTS], outs[N_PROJ_OUTS + 1], outs[fa_d:]


def _lambda(lam_ref, lam_init):
    lv = lam_ref[...]
    a = jnp.sum(lv[0:1] * lv[1:2], axis=-1, keepdims=True)
    b = jnp.sum(lv[2:3] * lv[3:4], axis=-1, keepdims=True)
    return jnp.exp(a) - jnp.exp(b) + lam_init


def _stack_masked(q_t):
    first = lax.broadcasted_iota(jnp.int32, q_t.shape, 0) < HEAD_DIM
    return jnp.concatenate([jnp.where(first, q_t, 0.0), jnp.where(first, 0.0, q_t)],
                           axis=1).astype(BF16)


def _scores_chunk(k, qq_t, m):
    s = jnp.dot(k, qq_t, preferred_element_type=F32)
    mc = jnp.max(s, axis=0, keepdims=True)
    return s, (mc if m is None else jnp.maximum(m, mc))


def _with_ones_rows(v_t):
    return jnp.concatenate([v_t, jnp.ones((ONES_ROWS, v_t.shape[1]), F32)], axis=0).astype(BF16)


def _values_chunk(s, m, v1_t, acc):
    e = jnp.exp2(s - m).astype(BF16)
    oc = jnp.dot(v1_t, e, preferred_element_type=F32)
    return oc if acc is None else acc + oc


def _combine_t(acc, lam):
    tq = acc.shape[1] // 2
    r = 1.0 / acc[QK_W:QK_W + 1, :]
    o_t = acc[:QK_W, :]
    return o_t[:, :tq] * r[:, :tq] - o_t[:, tq:] * (r[:, tq:] * lam)


def _subln_gate(o, sg, sgc, lam_init):
    ms = jnp.mean(o * o, axis=-1, keepdims=True)
    y = o * lax.rsqrt(ms + EPS) * sg * (1.0 - lam_init)
    return (y * sgc.astype(F32)).astype(BF16)


def _attn_ctx_kernel(lam_ref, sg_ref, q_ref, k_ref, v_ref, sgc_ref, o_ref, *, lam_init, seq):
    lam = _lambda(lam_ref, lam_init)
    sg = sg_ref[...]
    n_seq = q_ref.shape[0] // seq
    q_t = q_ref[...].astype(F32).T
    v_t = v_ref[...].astype(F32).T
    pairs = [(slice(s * seq, (s + 1) * seq), slice(hd * QK_W, (hd + 1) * QK_W))
             for s in range(n_seq) for hd in range(N_HEADS)]
    qqs = [_stack_masked(q_t[hl, rows]) for rows, hl in pairs]
    v1s = [_with_ones_rows(v_t[hl, rows]) for rows, hl in pairs]
    sm = [_scores_chunk(k_ref[rows, hl], qq, None) for (rows, hl), qq in zip(pairs, qqs)]
    accs = [_values_chunk(s, m, v1, None) for (s, m), v1 in zip(sm, v1s)]
    y_t = [_combine_t(a, lam) for a in accs]
    for s in range(n_seq):
        rows = slice(s * seq, (s + 1) * seq)
        y = jnp.concatenate(y_t[s * N_HEADS:(s + 1) * N_HEADS], axis=0).T
        for hd in range(N_HEADS):
            hl = slice(hd * QK_W, (hd + 1) * QK_W)
            o_ref[rows, hl] = _subln_gate(y[:, hl], sg, sgc_ref[rows, hl], lam_init)


def _attn_den_kernel(lam_ref, sg_ref, q_ref, kn_ref, vn_ref, kc_ref, vc_ref, sgc_ref, o_ref,
                     kcs_ref, vt_ref, qq0_ref, qq1_ref, s0_ref, s1_ref, acc0_ref, acc1_ref,
                     *, lam_init):
    seq = q_ref.shape[0]
    past = kcs_ref.shape[1]
    past_chunks = past // KEY_CHUNK
    n_chunks = (past + seq) // KEY_CHUNK
    n_tiles = seq // Q_TILE
    n_elems = n_tiles * N_HEADS
    qq_refs = (qq0_ref, qq1_ref)
    s_refs = (s0_ref, s1_ref)
    acc_refs = (acc0_ref, acc1_ref)
    lam = _lambda(lam_ref, lam_init)
    sg = sg_ref[...]

    def lanes(hd):
        return slice(hd * QK_W, (hd + 1) * QK_W)

    def chunk(c):
        return slice(c * KEY_CHUNK, (c + 1) * KEY_CHUNK)

    def tile_rows(t):
        if isinstance(t, int):
            return pl.ds(t * Q_TILE, Q_TILE)
        return pl.ds(pl.multiple_of(t * Q_TILE, Q_TILE), Q_TILE)

    for hd in range(N_HEADS):
        cache_rows = pl.ds(hd, past, stride=N_HEADS)
        kcs_ref[hd] = kc_ref[0, 0, cache_rows, :].astype(BF16)
        vt_ref[hd, :, :past] = _with_ones_rows(vc_ref[0, 0, cache_rows, :].T)
        vt_ref[hd, :, past:] = _with_ones_rows(vn_ref[:, lanes(hd)].astype(F32).T)

    def keys(hd, c):
        if c < past_chunks:
            return kcs_ref[hd, chunk(c), :]
        return kn_ref[chunk(c - past_chunks), lanes(hd)]

    def prep(t, hd, par):
        qq_refs[par][...] = _stack_masked(q_ref[tile_rows(t), lanes(hd)].astype(F32).T)

    def finish(t, hd, par):
        rows = tile_rows(t)
        y = _combine_t(acc_refs[par][...], lam).T
        o_ref[rows, lanes(hd)] = _subln_gate(y, sg, sgc_ref[rows, lanes(hd)], lam_init)

    def step(t, j, m_cur, first=False, last=False):
        def elem(off):
            return t + (j + off) // N_HEADS, (j + off) % N_HEADS

        par = j % 2
        e_static = N_HEADS * t + j if isinstance(t, int) else None
        do_finish = not (first and j == 0)
        do_scores = not (last and e_static + 1 >= n_elems)
        do_prep = not (last and e_static + 2 >= n_elems)
        if do_prep:
            prep(*elem(2), par)
        hd_nxt = elem(1)[1]
        qq = qq_refs[1 - par][...] if do_scores else None
        m_nxt = None
        acc = None
        for c in range(n_chunks):
            if do_scores:
                s, m_nxt = _scores_chunk(keys(hd_nxt, c), qq, m_nxt)
                s_refs[1 - par][chunk(c), :] = s
            acc = _values_chunk(s_refs[par][chunk(c), :], m_cur, vt_ref[j, :, chunk(c)], acc)
        acc_refs[par][...] = acc
        if do_finish:
            finish(*elem(-1), 1 - par)
        return m_nxt

    def tile_steps(t, m, **edge):
        for j in range(N_HEADS):
            m = step(t, j, m, **edge)
        return m

    prep(0, 0, 0)
    prep(0, 1, 1)
    qq = qq0_ref[...]
    m = None
    for c in range(n_chunks):
        s, m = _scores_chunk(keys(0, c), qq, m)
        s0_ref[chunk(c), :] = s
    m = tile_steps(0, m, first=True)
    m = lax.fori_loop(1, n_tiles - 1, tile_steps, m)
    tile_steps(n_tiles - 1, m, last=True)
    finish(n_tiles - 1, N_HEADS - 1, (n_elems - 1) % 2)


def _den_mixers_kernel(lam_ref, sg_ref, q_ref, kn_ref, vn_ref, kc_ref, vc_ref, sgc_ref,
                       xw_ref, sga_ref, f1_ref, twc_ref, tws_ref, cs2_ref, yc_ref, ya_ref,
                       *scratch, layer, lam_init):
    lam_ref, sg_ref = lam_ref.at[layer], sg_ref.at[layer:layer + 1]
    g_ref, x_ref = scratch[-2:]
    _attn_den_kernel(lam_ref, sg_ref, q_ref, kn_ref, vn_ref, kc_ref, vc_ref, sgc_ref, yc_ref,
                     *scratch[:-2], lam_init=lam_init)
    _fourier_den_kernel(xw_ref, sga_ref, f1_ref, twc_ref, tws_ref, cs2_ref, ya_ref, g_ref, x_ref)


def _den_mixers(q, k, v, sgc, cache_k, cache_v, xw, sga, lam_vec, subln_g, f1, twc, tws, cs2,
                *, layer, seq, lam_init):
    n_tok = q.shape[0]
    past = cache_k.shape[2] // N_HEADS
    n2 = seq // FFT_RADIX
    wide = FFT_RADIX * FOURIER_W
    blk = lambda w: pl.BlockSpec((seq, w), lambda b: (b, 0))
    cblk = pl.BlockSpec((1, 1, past * N_HEADS, QK_W), lambda b: (b, layer, 0, 0))
    const = lambda shape: pl.BlockSpec(shape, lambda b: (0,) * len(shape))
    return pl.pallas_call(
        functools.partial(_den_mixers_kernel, layer=layer, lam_init=lam_init),
        grid=(n_tok // seq,),
        in_specs=[
            const(lam_vec.shape), const(subln_g.shape),
            blk(ATTN_W), blk(ATTN_W), blk(ATTN_W), cblk, cblk, blk(ATTN_W),
            pl.BlockSpec((n2, wide), lambda b: (b, 0)), blk(FOURIER_W),
            const((2 * n2, n2)), const((n2, wide)), const((n2, wide)),
            const((2 * FOURIER_W, FOURIER_W)),
        ],
        out_specs=[blk(ATTN_W), blk(FOURIER_W)],
        out_shape=[jax.ShapeDtypeStruct((n_tok, ATTN_W), BF16),
                   jax.ShapeDtypeStruct((n_tok, FOURIER_W), BF16)],
        scratch_shapes=[
            pltpu.VMEM((N_HEADS, past, QK_W), BF16),
            pltpu.VMEM((N_HEADS, QK_W + ONES_ROWS, past + seq), BF16),
            pltpu.VMEM((QK_W, 2 * Q_TILE), BF16),
            pltpu.VMEM((QK_W, 2 * Q_TILE), BF16),
            pltpu.VMEM((past + seq, 2 * Q_TILE), F32),
            pltpu.VMEM((past + seq, 2 * Q_TILE), F32),
            pltpu.VMEM((QK_W + ONES_ROWS, 2 * Q_TILE), F32),
            pltpu.VMEM((QK_W + ONES_ROWS, 2 * Q_TILE), F32),
            pltpu.VMEM((2 * n2, wide), F32),
            pltpu.VMEM((seq, 2 * FOURIER_W), BF16),
        ],
        compiler_params=_cparams(("arbitrary",)),
        name="den_mixers",
    )(lam_vec, subln_g, q, k, v, cache_k, cache_v, sgc, xw, sga, f1, twc, tws, cs2)


def _fourier_ctx_kernel(fa_ref, sga_ref, cs_ref, dn_ref, o_ref, *, seq):
    ab = jnp.dot(fa_ref[...], cs_ref[...], preferred_element_type=F32).astype(BF16)
    for s in range(fa_ref.shape[0] // seq):
        rows = slice(s * seq, (s + 1) * seq)
        ab2 = jnp.concatenate([ab[rows, :FOURIER_W], ab[rows, FOURIER_W:]], axis=0)
        f = jnp.dot(dn_ref[...], ab2, preferred_element_type=F32)
        o_ref[rows, :] = (f * sga_ref[rows, :].astype(F32)).astype(BF16)


def _ctx_mixers_kernel(lam_ref, sg_ref, q_ref, k_ref, v_ref, sgc_ref, fa_ref, sga_ref, cs_ref, dn_ref,
                       yc_ref, ya_ref, *, layer, lam_init, seq):
    lam_ref, sg_ref = lam_ref.at[layer], sg_ref.at[layer:layer + 1]
    _attn_ctx_kernel(lam_ref, sg_ref, q_ref, k_ref, v_ref, sgc_ref, yc_ref, lam_init=lam_init, seq=seq)
    _fourier_ctx_kernel(fa_ref, sga_ref, cs_ref, dn_ref, ya_ref, seq=seq)


def _ctx_mixers(q, k, v, sgc, fa, sga, lam_vec, subln_g, cs, dn, *, layer, seq, lam_init):
    n_tok = q.shape[0]
    rows = CTX_SEQS * seq
    blk = lambda w: pl.BlockSpec((rows, w), lambda b: (b, 0))
    const = lambda shape: pl.BlockSpec(shape, lambda b: (0,) * len(shape))
    return pl.pallas_call(
        functools.partial(_ctx_mixers_kernel, layer=layer, lam_init=lam_init, seq=seq),
        grid=(n_tok // rows,),
        in_specs=[
            const(lam_vec.shape), const(subln_g.shape),
            blk(ATTN_W), blk(ATTN_W), blk(ATTN_W), blk(ATTN_W),
            blk(FOURIER_W), blk(FOURIER_W),
            const((FOURIER_W, 2 * FOURIER_W)), const((seq, 2 * seq)),
        ],
        out_specs=[blk(ATTN_W), blk(FOURIER_W)],
        out_shape=[jax.ShapeDtypeStruct((n_tok, ATTN_W), BF16),
                   jax.ShapeDtypeStruct((n_tok, FOURIER_W), BF16)],
        compiler_params=_cparams(("arbitrary",)),
        name="ctx_mixers",
    )(lam_vec, subln_g, q, k, v, sgc, fa, sga, cs, dn)


def _cadd(a, b):
    return (a[0] + b[0], a[1] + b[1])


def _csub(a, b):
    return (a[0] - b[0], a[1] - b[1])


def _mul_neg_i(a):
    return (a[1], -a[0])


def _mul_w8_1(a):
    return ((a[0] + a[1]) * SQRT_HALF, (a[1] - a[0]) * SQRT_HALF)


def _mul_w8_3(a):
    return ((a[1] - a[0]) * SQRT_HALF, (-a[0] - a[1]) * SQRT_HALF)


def _fft4(a0, a1, a2, a3):
    e0, e1 = _cadd(a0, a2), _csub(a0, a2)
    o0, o1 = _cadd(a1, a3), _mul_neg_i(_csub(a1, a3))
    return [_cadd(e0, o0), _cadd(e1, o1), _csub(e0, o0), _csub(e1, o1)]


def _fft8(x):
    e = _fft4(x[0], x[2], x[4], x[6])
    o = _fft4(x[1], x[3], x[5], x[7])
    t = [o[0], _mul_w8_1(o[1]), _mul_neg_i(o[2]), _mul_w8_3(o[3])]
    return [_cadd(e[k], t[k]) for k in range(4)] + [_csub(e[k], t[k]) for k in range(4)]


def _fourier_den_kernel(xw_ref, sga_ref, f1_ref, twc_ref, tws_ref, cs2_ref, o_ref, g_ref, x_ref):
    n2 = xw_ref.shape[0]
    g_ref[...] = jnp.dot(f1_ref[...], xw_ref[...], preferred_element_type=F32)

    def chunk(i):
        r = i * FFT_ROWS
        re_rows = pl.ds(r, FFT_ROWS)
        im_rows = pl.ds(n2 + r, FFT_ROWS)
        xs = []
        for n1 in range(FFT_RADIX):
            lanes = slice(n1 * FOURIER_W, (n1 + 1) * FOURIER_W)
            gr = g_ref[re_rows, lanes]
            gi = g_ref[im_rows, lanes]
            if n1 > 0:
                c = twc_ref[re_rows, lanes]
                s = tws_ref[re_rows, lanes]
                gr, gi = gr * c + gi * s, gi * c - gr * s
            xs.append((gr, gi))
        for k1, (xr, xi) in enumerate(_fft8(xs)):
            out_rows = pl.ds(k1 * n2 + r, FFT_ROWS)
            x_ref[out_rows, :FOURIER_W] = xr.astype(BF16)
            x_ref[out_rows, FOURIER_W:] = xi.astype(BF16)

    for i in range(n2 // FFT_ROWS):
        chunk(i)
    f = jnp.dot(x_ref[...], cs2_ref[...], preferred_element_type=F32)
    o_ref[...] = (f * sga_ref[...].astype(F32)).astype(BF16)


def _outproj_tile(x_ref, mod_ref, ya_ref, zc_ref, zp_ref, zn_ref, bgs_ref, yc_ref,
                  cw_ref, cb_ref, w_ref, fg_ref, o_ref, tile, *, seq, final_norm):
    tm = x_ref.shape[0]
    z = zc_ref[...].astype(F32)
    row = lax.broadcasted_iota(jnp.int32, z.shape, 0)
    pos = (tile * tm + row) & (seq - 1)
    prev_row = zp_ref[HALO_ROWS - 1:HALO_ROWS, :].astype(F32)
    next_row = zn_ref[0:1, :].astype(F32)
    z_prev = jnp.where(row == 0, prev_row, pltpu.roll(z, 1, axis=0))
    z_prev = jnp.where(pos == 0, 0.0, z_prev)
    z_next = jnp.where(row == tm - 1, next_row, pltpu.roll(z, tm - 1, axis=0))
    z_next = jnp.where(pos == seq - 1, 0.0, z_next)
    cw = cw_ref[...]
    conv = z_prev * cw[0:1] + z * cw[1:2] + z_next * cw[2:3] + cb_ref[...]
    yb = (bgs_ref[...].astype(F32) * conv).astype(BF16)

    mixed = jnp.concatenate([ya_ref[...], yb, yc_ref[...]], axis=-1)
    out = jnp.dot(mixed, w_ref[...], preferred_element_type=F32)
    gate = mod_ref[0, 0][:, 2 * D_MODEL:]
    xn = x_ref[...] + gate * out
    if final_norm:
        ms = jnp.mean(xn * xn, axis=-1, keepdims=True)
        xn = xn * lax.rsqrt(ms + EPS) * fg_ref[...]
    o_ref[...] = xn


N_GROUP_INS = 8


def _outproj_kernel(*refs, layer, n_ctx_tiles, seq, dec_seq, final_norm):
    ctx_ins = refs[:N_GROUP_INS]
    den_ins = refs[N_GROUP_INS:2 * N_GROUP_INS]
    cw_ref, cb_ref, w_ref, fg_ref, oc_ref, od_ref, wb_ref = refs[2 * N_GROUP_INS:]
    cw_ref, cb_ref = cw_ref.at[layer], cb_ref.at[layer:layer + 1]
    i = pl.program_id(0)

    @pl.when(i == 0)
    def _():
        wb_ref[...] = w_ref[0].astype(BF16)

    @pl.when(i < n_ctx_tiles)
    def _():
        _outproj_tile(*ctx_ins, cw_ref, cb_ref, wb_ref, fg_ref, oc_ref, i, seq=seq,
                      final_norm=final_norm)

    @pl.when(i >= n_ctx_tiles)
    def _():
        _outproj_tile(*den_ins, cw_ref, cb_ref, wb_ref, fg_ref, od_ref, i - n_ctx_tiles,
                      seq=dec_seq, final_norm=final_norm)


def _out_projection(ctx, den, mod, conv_w, conv_b, w_out, layer, final_g, *, seq, dec_seq,
                    final_norm):
    tm = OUT_TILE
    nct = ctx[0].shape[0] // tm
    ndt = den[0].shape[0] // tm
    halo_per_tile = tm // HALO_ROWS
    seq_tiles = dec_seq // tm

    def group_specs(tile_of, n_tok, mod_of):
        n_halo = n_tok // HALO_ROWS
        row = lambda w: pl.BlockSpec((tm, w), lambda i: (tile_of(i), 0))
        return [
            row(D_MODEL),
            pl.BlockSpec((1, 1, 1, 3 * D_MODEL), lambda i: (layer, mod_of(tile_of(i)), 0, 0)),
            row(FOURIER_W),
            row(CONV_W),
            pl.BlockSpec((HALO_ROWS, CONV_W),
                         lambda i: (jnp.maximum(tile_of(i) * halo_per_tile - 1, 0), 0)),
            pl.BlockSpec((HALO_ROWS, CONV_W),
                         lambda i: (jnp.minimum((tile_of(i) + 1) * halo_per_tile, n_halo - 1), 0)),
            row(CONV_W),
            row(ATTN_W),
        ], row(D_MODEL)

    ctx_specs, ctx_out = group_specs(lambda i: jnp.minimum(i, nct - 1), ctx[0].shape[0],
                                     lambda t: 0)
    den_specs, den_out = group_specs(lambda i: jnp.maximum(i - nct, 0), den[0].shape[0],
                                     lambda t: 1 + t // seq_tiles)

    def group_args(g):
        x, ya, zc, bgs, yc = g
        return [x, mod, ya, zc, zc, zc, bgs, yc]

    return pl.pallas_call(
        functools.partial(_outproj_kernel, layer=layer, n_ctx_tiles=nct, seq=seq, dec_seq=dec_seq,
                          final_norm=final_norm),
        grid=(nct + ndt,),
        in_specs=ctx_specs + den_specs + [
            pl.BlockSpec(conv_w.shape, lambda i: (0, 0, 0)),
            pl.BlockSpec(conv_b.shape, lambda i: (0, 0)),
            pl.BlockSpec((1, D_MODEL, D_MODEL), lambda i: (layer, 0, 0)),
            pl.BlockSpec((1, D_MODEL), lambda i: (0, 0)),
        ],
        out_specs=[ctx_out, den_out],
        out_shape=[jax.ShapeDtypeStruct(ctx[0].shape, F32), jax.ShapeDtypeStruct(den[0].shape, F32)],
        scratch_shapes=[pltpu.VMEM((D_MODEL, D_MODEL), BF16)],
        compiler_params=_cparams(("arbitrary",)),
        name="out_projection",
    )(*group_args(ctx), *group_args(den), conv_w, conv_b, w_out,
      final_g.reshape(1, D_MODEL))


def kernel(x_prompt, x_sample, cache_k, cache_v, c, c_ctx, norm_g, w_mod, b_mod, w_in, conv_w,
           conv_b, lam_vec, subln_g, w_out, final_g):
    batch, seq, _ = x_prompt.shape
    dec_batch, dec_seq, _ = x_sample.shape

    mod = _modulation(c_ctx, c, w_mod, b_mod)

    rope_tabs = tuple(jnp.asarray(t) for t in _rope_tables(dec_seq))
    cs = jnp.asarray(_chan_tables()).astype(BF16)
    dn_ctx = jnp.asarray(_dft_tables(seq)).astype(BF16)
    f1, twc, tws, cs2 = (jnp.asarray(t) for t in _ct_tables(dec_seq))
    f1 = f1.astype(BF16)
    cs2 = cs2.astype(BF16)

    cache_k2 = cache_k.reshape(dec_batch, DEPTH, -1, QK_W)
    cache_v2 = cache_v.reshape(dec_batch, DEPTH, -1, QK_W)
    xc = x_prompt.reshape(batch * seq, D_MODEL)
    xl = x_sample.reshape(dec_batch * dec_seq, D_MODEL)
    kv = None
    for l in range(DEPTH):
        lam_init = 0.8 - 0.6 * math.exp(-0.3 * l)
        ctx, k32, v32, den = _in_projection(
            xc, xl, mod, norm_g, w_in, l, rope_tabs, seq=seq, dec_seq=dec_seq,
            prev_kv=kv)
        kv = (k32, v32)

        fa, sga, zc_c, bgs_c, q, k, v, sgc = ctx
        yc_c, ya_c = _ctx_mixers(q, k, v, sgc, fa, sga, lam_vec, subln_g, cs, dn_ctx,
                                 layer=l, seq=seq, lam_init=lam_init)

        fa, sga, zc_d, bgs_d, q, k, v, sgc = den
        yc_d, ya_d = _den_mixers(q, k, v, sgc, cache_k2, cache_v2, fa, sga, lam_vec, subln_g,
                                 f1, twc, tws, cs2, layer=l, seq=dec_seq, lam_init=lam_init)

        xc, xl = _out_projection(
            (xc, ya_c, zc_c, bgs_c, yc_c), (xl, ya_d, zc_d, bgs_d, yc_d), mod,
            conv_w, conv_b, w_out, l, final_g, seq=seq, dec_seq=dec_seq,
            final_norm=l == DEPTH - 1)

    y_prompt = xc.reshape(batch, seq, D_MODEL)
    y_sample = xl.reshape(dec_batch, dec_seq, D_MODEL)
    return (y_prompt, y_sample, *(t.reshape(batch, DEPTH, seq, N_HEADS, QK_W) for t in kv))
```

```python
import functools
import math

import numpy as np
import jax
import jax.numpy as jnp
from jax import lax
from jax.experimental import pallas as pl
from jax.experimental.pallas import tpu as pltpu

D_MODEL = 1024
DEPTH = 2
GRID_W = 64
FOURIER_W = 256
CONV_W = 256
ATTN_W = 512
N_HEADS = 4
HEAD_DIM = 64
QK_W = 128
ROPE_BASE = 10000.0
ROPE_W = HEAD_DIM // 2
ROPE_ROT = ROPE_W // 2
EPS = 1e-6
IN_DIM = 3584

F32 = jnp.float32
BF16 = jnp.bfloat16

VMEM_LIMIT_BYTES = 60 * 1024 * 1024
TOKEN_TILE = 512
OUT_TILE = 1024
Q_TILE = 256
KEY_CHUNK = 256
ONES_ROWS = 16
LANES = 128
FFT_RADIX = 8
FFT_ROWS = 16
SQRT_HALF = 0.7071067811865476
CTX_SEQS = 4
HALO_ROWS = 16
MOD_ROWS = 8
MOD_CHUNK = 3 * D_MODEL
LOG2E = 1.4426950408889634

_COLS = {}
_off = 0
for _name, _w in (("fa", 256), ("ga", 256), ("bg", 256), ("cg", 256), ("hc", 256), ("gb", 256),
                  ("q", 512), ("k", 512), ("v", 512), ("gc", 512)):
    _COLS[_name] = (_off, _off + _w)
    _off += _w


def _silu(x):
    return x * (1.0 / (1.0 + jnp.exp(-x)))


def _cparams(sem):
    return pltpu.CompilerParams(dimension_semantics=sem, vmem_limit_bytes=VMEM_LIMIT_BYTES)


def _rope_tables(n_tokens):
    n = np.arange(n_tokens)
    row = (n // GRID_W).astype(np.float64)
    col = (n % GRID_W).astype(np.float64)
    j = np.arange(QK_W)
    jj = j % HEAD_DIM
    idx = jj % ROPE_W
    inv = 1.0 / (ROPE_BASE ** (2.0 * (idx % ROPE_ROT) / ROPE_W))
    pos = np.where((jj < ROPE_W)[None, :], row[:, None], col[:, None])
    ang = pos * inv[None, :]
    cos = np.cos(ang)
    sin = np.sin(ang)
    first = (idx < ROPE_ROT)[None, :]
    sin_a = np.where(first, -sin, 0.0)
    sin_b = np.where(first, 0.0, sin)
    return (np.asarray(cos, np.float32), np.asarray(sin_a, np.float32), np.asarray(sin_b, np.float32))


def _dft_tables(n):
    k = np.arange(n)
    kn = (k[:, None] * k[None, :]) % n
    ang = 2.0 * np.pi * kn / n
    return np.asarray(np.concatenate([np.cos(ang), -np.sin(ang)], axis=1) / math.sqrt(n), np.float32)


def _chan_tables():
    k = np.arange(FOURIER_W)
    kn = (k[:, None] * k[None, :]) % FOURIER_W
    ang = 2.0 * np.pi * kn / FOURIER_W
    return np.asarray(np.concatenate([np.cos(ang), np.sin(ang)], axis=1) / math.sqrt(FOURIER_W), np.float32)


def _ct_tables(n):
    n2 = n // FFT_RADIX
    k = np.arange(n2)
    ang = 2.0 * np.pi * ((k[:, None] * k[None, :]) % n2) / n2
    f1 = np.concatenate([np.cos(ang), -np.sin(ang)], axis=0) / math.sqrt(n * FOURIER_W)
    tw = 2.0 * np.pi * k[:, None] * np.arange(FFT_RADIX)[None, :] / n
    twc = np.repeat(np.cos(tw), FOURIER_W, axis=1)
    tws = np.repeat(np.sin(tw), FOURIER_W, axis=1)
    c = np.arange(FOURIER_W)
    angc = 2.0 * np.pi * ((c[:, None] * c[None, :]) % FOURIER_W) / FOURIER_W
    cs2 = np.concatenate([np.cos(angc), np.sin(angc)], axis=0)
    return tuple(np.asarray(t, np.float32) for t in (f1, twc, tws, cs2))


def _mod_kernel(cctx_ref, c_ref, w_ref, b_ref, o_ref, s_ref):
    n_c = c_ref.shape[0]
    s_ref[...] = jnp.zeros_like(s_ref)
    s_ref[0:1, :] = _silu(cctx_ref[...])
    s_ref[1:1 + n_c, :] = _silu(c_ref[...])
    w = w_ref[0].astype(BF16)
    res = jnp.dot(s_ref[...].astype(BF16), w, preferred_element_type=F32) + b_ref[0]
    for r in range(MOD_ROWS):
        o_ref[0, r] = res[r:r + 1]


def _modulation(c_ctx, c, w_mod, b_mod):
    chunk = MOD_CHUNK
    n_chunks = 3 * D_MODEL // chunk
    return pl.pallas_call(
        _mod_kernel,
        grid=(DEPTH, n_chunks),
        in_specs=[
            pl.BlockSpec((1, D_MODEL), lambda l, j: (0, 0)),
            pl.BlockSpec(c.shape, lambda l, j: (0, 0)),
            pl.BlockSpec((1, D_MODEL, chunk), lambda l, j: (l, 0, j)),
            pl.BlockSpec((1, 1, chunk), lambda l, j: (l, 0, j)),
        ],
        out_specs=pl.BlockSpec((1, MOD_ROWS, 1, chunk), lambda l, j: (l, 0, 0, j)),
        out_shape=jax.ShapeDtypeStruct((DEPTH, MOD_ROWS, 1, 3 * D_MODEL), F32),
        scratch_shapes=[pltpu.VMEM((MOD_ROWS, D_MODEL), F32)],
        compiler_params=_cparams(("arbitrary", "arbitrary")),
        name="modulation",
    )(c_ctx.reshape(1, D_MODEL), c, w_mod, b_mod.reshape(DEPTH, 1, 3 * D_MODEL))


def _store_heads(ref, layer, t):
    seqs, _, rows, _ = ref.shape
    seq = rows // N_HEADS
    for s in range(seqs):
        for hd in range(N_HEADS):
            ref[s, layer, pl.ds(hd, seq, stride=N_HEADS), :] = (
                t[s * seq:(s + 1) * seq, hd * QK_W:(hd + 1) * QK_W])


def _inproj_tile(x_ref, mod_ref, g_ref, w_ref, rope_refs, outs, kv_refs, stage_ref):
    fa_ref, sga_ref, zc_ref, bgs_ref, q_ref, k_ref, v_ref, sgc_ref = outs

    x = x_ref[...]
    ms = jnp.mean(x * x, axis=-1, keepdims=True)
    y = x * lax.rsqrt(ms + EPS) * g_ref[...]
    m = mod_ref[0, 0]
    h = (y * (1.0 + m[:, D_MODEL:2 * D_MODEL]) + m[:, :D_MODEL]).astype(BF16)

    def proj(name):
        lo, hi = _COLS[name]
        return jnp.dot(h, w_ref[:, lo:hi], preferred_element_type=F32)

    if stage_ref is not None:
        fa = proj("fa")
        halves = stage_ref.shape[0]
        for hf in range(halves):
            stage_ref[hf] = fa[:, hf * LANES:(hf + 1) * LANES]
        rows = stage_ref.shape[1] // FFT_RADIX
        for n1 in range(FFT_RADIX):
            for hf in range(halves):
                lo = n1 * FOURIER_W + hf * LANES
                fa_ref[:, lo:lo + LANES] = (
                    stage_ref[hf, pl.ds(n1, rows, stride=FFT_RADIX), :].astype(BF16))
    else:
        fa_ref[...] = proj("fa").astype(BF16)
    sga_ref[...] = _silu(proj("ga")).astype(BF16)
    zc_ref[...] = (proj("cg") * proj("hc")).astype(BF16)
    bgs_ref[...] = (proj("bg") * _silu(proj("gb"))).astype(BF16)
    sgc_ref[...] = _silu(proj("gc")).astype(BF16)

    q = proj("q")
    k = proj("k")
    v = proj("v")
    if kv_refs is not None and len(kv_refs) == 2:
        kv_refs[0][...] = k
        kv_refs[1][...] = v
    elif kv_refs is not None:
        new_k_ref, new_v_ref, prev_k_ref, prev_v_ref = kv_refs
        _store_heads(new_k_ref, 0, prev_k_ref[...])
        _store_heads(new_v_ref, 0, prev_v_ref[...])
        _store_heads(new_k_ref, 1, k)
        _store_heads(new_v_ref, 1, v)
    v_ref[...] = v.astype(BF16)

    q_scale = HEAD_DIM ** -0.5 * LOG2E
    if rope_refs is not None:
        cos = rope_refs[0][...]
        sa = rope_refs[1][...]
        sb = rope_refs[2][...]
        for hd in range(N_HEADS):
            sl = slice(hd * QK_W, (hd + 1) * QK_W)
            for t, ref, scale in ((q, q_ref, q_scale), (k, k_ref, None)):
                th = t[:, sl]
                r = (th * cos + pltpu.roll(th, QK_W - ROPE_ROT, axis=1) * sa
                     + pltpu.roll(th, ROPE_ROT, axis=1) * sb)
                if scale is not None:
                    r = r * scale
                ref[:, sl] = r.astype(BF16)
    else:
        q_ref[...] = (q * q_scale).astype(BF16)
        k_ref[...] = k.astype(BF16)


N_PROJ_OUTS = 8


def _inproj_kernel(xc_ref, xd_ref, modc_ref, modd_ref, g_ref, w_ref, cos_ref, sa_ref, sb_ref,
                   *rest, layer, n_ctx_tiles, n_prev):
    prev_kv = rest[:n_prev]
    rest = rest[n_prev:]
    ctx_outs = rest[:N_PROJ_OUTS]
    kv_refs = rest[N_PROJ_OUTS:N_PROJ_OUTS + 2] + prev_kv
    den_outs = rest[N_PROJ_OUTS + 2:2 * N_PROJ_OUTS + 2]
    wb_ref, stage_ref = rest[2 * N_PROJ_OUTS + 2:]
    g_ref = g_ref.at[layer:layer + 1]
    i = pl.program_id(0)

    @pl.when(i == 0)
    def _():
        for lo, hi in _COLS.values():
            wb_ref[:, lo:hi] = w_ref[0, :, lo:hi].astype(BF16)

    @pl.when(i < n_ctx_tiles)
    def _():
        _inproj_tile(xc_ref, modc_ref, g_ref, wb_ref, None, ctx_outs, kv_refs, None)

    @pl.when(i >= n_ctx_tiles)
    def _():
        _inproj_tile(xd_ref, modd_ref, g_ref, wb_ref, (cos_ref, sa_ref, sb_ref), den_outs, None,
                     stage_ref)


def _in_projection(xc, xd, mod, norm_g, w_in, layer, rope_tabs, *, seq, dec_seq,
                   prev_kv):
    assert DEPTH == 2, "the last layer's call assembles the k/v of exactly two layers"
    tm = TOKEN_TILE
    n_c, n_d = xc.shape[0], xd.shape[0]
    nct, ndt = n_c // tm, n_d // tm
    seq_tiles = dec_seq // tm
    ctx_i = lambda i: jnp.minimum(i, nct - 1)
    den_i = lambda i: jnp.maximum(i - nct, 0)
    row_c = lambda w: pl.BlockSpec((tm, w), lambda i: (ctx_i(i), 0))
    row_d = lambda w: pl.BlockSpec((tm, w), lambda i: (den_i(i), 0))
    tab = pl.BlockSpec((tm, QK_W), lambda i: (den_i(i) % seq_tiles, 0))
    in_specs = [
        row_c(D_MODEL), row_d(D_MODEL),
        pl.BlockSpec((1, 1, 1, 3 * D_MODEL), lambda i: (layer, 0, 0, 0)),
        pl.BlockSpec((1, 1, 1, 3 * D_MODEL), lambda i: (layer, 1 + den_i(i) // seq_tiles, 0, 0)),
        pl.BlockSpec((DEPTH, D_MODEL), lambda i: (0, 0)),
        pl.BlockSpec((1, D_MODEL, IN_DIM), lambda i: (layer, 0, 0)),
        tab, tab, tab,
    ]
    args = [xc, xd, mod, mod, norm_g, w_in, *rope_tabs]
    widths = (256, 256, 256, 256, 512, 512, 512, 512)
    seqs = tm // seq
    if prev_kv is None:
        kv_blk = row_c(ATTN_W)
        kv_shape = jax.ShapeDtypeStruct((n_c, ATTN_W), F32)
    else:
        kv_blk = pl.BlockSpec((seqs, DEPTH, seq * N_HEADS, QK_W), lambda i: (ctx_i(i), 0, 0, 0))
        kv_shape = jax.ShapeDtypeStruct((n_c // seq, DEPTH, seq * N_HEADS, QK_W), F32)
        in_specs += [row_c(ATTN_W)] * 2
        args += list(prev_kv)
    out_specs = [row_c(w) for w in widths] + [kv_blk, kv_blk] + [row_d(w) for w in widths]
    out_shape = ([jax.ShapeDtypeStruct((n_c, w), BF16) for w in widths] + [kv_shape, kv_shape]
                 + [jax.ShapeDtypeStruct((n_d, w), BF16) for w in widths])
    fa_d = N_PROJ_OUTS + 2
    out_specs[fa_d] = pl.BlockSpec((tm // FFT_RADIX, FFT_RADIX * FOURIER_W), lambda i: (den_i(i), 0))
    out_shape[fa_d] = jax.ShapeDtypeStruct((n_d // FFT_RADIX, FFT_RADIX * FOURIER_W), BF16)
    outs = pl.pallas_call(
        functools.partial(_inproj_kernel, layer=layer, n_ctx_tiles=nct,
                          n_prev=0 if prev_kv is None else 2),
        grid=(nct + ndt,),
        in_specs=in_specs,
        out_specs=out_specs,
        out_shape=out_shape,
        scratch_shapes=[pltpu.VMEM((D_MODEL, IN_DIM), BF16),
                        pltpu.VMEM((FOURIER_W // LANES, tm, LANES), F32)],
        compiler_params=_cparams(("arbitrary",)),
        name="in_projection",
    )(*args)
    return outs[:N_PROJ_OUTS], outs[N_PROJ_OUTS], outs[N_PROJ_OUTS + 1], outs[fa_d:]


def _lambda(lam_ref, lam_init):
    lv = lam_ref[...]
    a = jnp.sum(lv[0:1] * lv[1:2], axis=-1, keepdims=True)
    b = jnp.sum(lv[2:3] * lv[3:4], axis=-1, keepdims=True)
    return jnp.exp(a) - jnp.exp(b) + lam_init


def _stack_masked(q_t):
    first = lax.broadcasted_iota(jnp.int32, q_t.shape, 0) < HEAD_DIM
    return jnp.concatenate([jnp.where(first, q_t, 0.0), jnp.where(first, 0.0, q_t)],
                           axis=1).astype(BF16)


def _scores_chunk(k, qq_t, m):
    s = jnp.dot(k, qq_t, preferred_element_type=F32)
    mc = jnp.max(s, axis=0, keepdims=True)
    return s, (mc if m is None else jnp.maximum(m, mc))


def _with_ones_rows(v_t):
    return jnp.concatenate([v_t, jnp.ones((ONES_ROWS, v_t.shape[1]), F32)], axis=0).astype(BF16)


def _values_chunk(s, m, v1_t, acc):
    e = jnp.exp2(s - m).astype(BF16)
    oc = jnp.dot(v1_t, e, preferred_element_type=F32)
    return oc if acc is None else acc + oc


def _combine_t(acc, lam):
    tq = acc.shape[1] // 2
    r = 1.0 / acc[QK_W:QK_W + 1, :]
    o_t = acc[:QK_W, :]
    return o_t[:, :tq] * r[:, :tq] - o_t[:, tq:] * (r[:, tq:] * lam)


def _subln_gate(o, sg, sgc, lam_init):
    ms = jnp.mean(o * o, axis=-1, keepdims=True)
    y = o * lax.rsqrt(ms + EPS) * sg * (1.0 - lam_init)
    return (y * sgc.astype(F32)).astype(BF16)


def _attn_ctx_kernel(lam_ref, sg_ref, q_ref, k_ref, v_ref, sgc_ref, o_ref, *, lam_init, seq):
    lam = _lambda(lam_ref, lam_init)
    sg = sg_ref[...]
    n_seq = q_ref.shape[0] // seq
    q_t = q_ref[...].astype(F32).T
    v_t = v_ref[...].astype(F32).T
    pairs = [(slice(s * seq, (s + 1) * seq), slice(hd * QK_W, (hd + 1) * QK_W))
             for s in range(n_seq) for hd in range(N_HEADS)]
    qqs = [_stack_masked(q_t[hl, rows]) for rows, hl in pairs]
    v1s = [_with_ones_rows(v_t[hl, rows]) for rows, hl in pairs]
    sm = [_scores_chunk(k_ref[rows, hl], qq, None) for (rows, hl), qq in zip(pairs, qqs)]
    accs = [_values_chunk(s, m, v1, None) for (s, m), v1 in zip(sm, v1s)]
    y_t = [_combine_t(a, lam) for a in accs]
    for s in range(n_seq):
        rows = slice(s * seq, (s + 1) * seq)
        y = jnp.concatenate(y_t[s * N_HEADS:(s + 1) * N_HEADS], axis=0).T
        for hd in range(N_HEADS):
            hl = slice(hd * QK_W, (hd + 1) * QK_W)
            o_ref[rows, hl] = _subln_gate(y[:, hl], sg, sgc_ref[rows, hl], lam_init)


def _attn_den_kernel(lam_ref, sg_ref, q_ref, kn_ref, vn_ref, kc_ref, vc_ref, sgc_ref, o_ref,
                     kcs_ref, vt_ref, qq0_ref, qq1_ref, s0_ref, s1_ref, acc0_ref, acc1_ref,
                     *, lam_init):
    seq = q_ref.shape[0]
    past = kcs_ref.shape[1]
    past_chunks = past // KEY_CHUNK
    n_chunks = (past + seq) // KEY_CHUNK
    n_tiles = seq // Q_TILE
    n_elems = n_tiles * N_HEADS
    qq_refs = (qq0_ref, qq1_ref)
    s_refs = (s0_ref, s1_ref)
    acc_refs = (acc0_ref, acc1_ref)
    lam = _lambda(lam_ref, lam_init)
    sg = sg_ref[...]

    def lanes(hd):
        return slice(hd * QK_W, (hd + 1) * QK_W)

    def chunk(c):
        return slice(c * KEY_CHUNK, (c + 1) * KEY_CHUNK)

    def tile_rows(t):
        if isinstance(t, int):
            return pl.ds(t * Q_TILE, Q_TILE)
        return pl.ds(pl.multiple_of(t * Q_TILE, Q_TILE), Q_TILE)

    for hd in range(N_HEADS):
        cache_rows = pl.ds(hd, past, stride=N_HEADS)
        kcs_ref[hd] = kc_ref[0, 0, cache_rows, :].astype(BF16)
        vt_ref[hd, :, :past] = _with_ones_rows(vc_ref[0, 0, cache_rows, :].T)
        vt_ref[hd, :, past:] = _with_ones_rows(vn_ref[:, lanes(hd)].astype(F32).T)

    def keys(hd, c):
        if c < past_chunks:
            return kcs_ref[hd, chunk(c), :]
        return kn_ref[chunk(c - past_chunks), lanes(hd)]

    def prep(t, hd, par):
        qq_refs[par][...] = _stack_masked(q_ref[tile_rows(t), lanes(hd)].astype(F32).T)

    def finish(t, hd, par):
        rows = tile_rows(t)
        y = _combine_t(acc_refs[par][...], lam).T
        o_ref[rows, lanes(hd)] = _subln_gate(y, sg, sgc_ref[rows, lanes(hd)], lam_init)

    def step(t, j, m_cur, first=False, last=False):
        def elem(off):
            return t + (j + off) // N_HEADS, (j + off) % N_HEADS

        par = j % 2
        e_static = N_HEADS * t + j if isinstance(t, int) else None
        do_finish = not (first and j == 0)
        do_scores = not (last and e_static + 1 >= n_elems)
        do_prep = not (last and e_static + 2 >= n_elems)
        if do_prep:
            prep(*elem(2), par)
        hd_nxt = elem(1)[1]
        qq = qq_refs[1 - par][...] if do_scores else None
        m_nxt = None
        acc = None
        for c in range(n_chunks):
            if do_scores:
                s, m_nxt = _scores_chunk(keys(hd_nxt, c), qq, m_nxt)
                s_refs[1 - par][chunk(c), :] = s
            acc = _values_chunk(s_refs[par][chunk(c), :], m_cur, vt_ref[j, :, chunk(c)], acc)
        acc_refs[par][...] = acc
        if do_finish:
            finish(*elem(-1), 1 - par)
        return m_nxt

    def tile_steps(t, m, **edge):
        for j in range(N_HEADS):
            m = step(t, j, m, **edge)
        return m

    prep(0, 0, 0)
    prep(0, 1, 1)
    qq = qq0_ref[...]
    m = None
    for c in range(n_chunks):
        s, m = _scores_chunk(keys(0, c), qq, m)
        s0_ref[chunk(c), :] = s
    m = tile_steps(0, m, first=True)
    m = lax.fori_loop(1, n_tiles - 1, tile_steps, m)
    tile_steps(n_tiles - 1, m, last=True)
    finish(n_tiles - 1, N_HEADS - 1, (n_elems - 1) % 2)


def _den_mixers_kernel(lam_ref, sg_ref, q_ref, kn_ref, vn_ref, kc_ref, vc_ref, sgc_ref,
                       xw_ref, sga_ref, f1_ref, twc_ref, tws_ref, cs2_ref, yc_ref, ya_ref,
                       *scratch, layer, lam_init):
    lam_ref, sg_ref = lam_ref.at[layer], sg_ref.at[layer:layer + 1]
    g_ref, x_ref = scratch[-2:]
    _attn_den_kernel(lam_ref, sg_ref, q_ref, kn_ref, vn_ref, kc_ref, vc_ref, sgc_ref, yc_ref,
                     *scratch[:-2], lam_init=lam_init)
    _fourier_den_kernel(xw_ref, sga_ref, f1_ref, twc_ref, tws_ref, cs2_ref, ya_ref, g_ref, x_ref)


def _den_mixers(q, k, v, sgc, cache_k, cache_v, xw, sga, lam_vec, subln_g, f1, twc, tws, cs2,
                *, layer, seq, lam_init):
    n_tok = q.shape[0]
    past = cache_k.shape[2] // N_HEADS
    n2 = seq // FFT_RADIX
    wide = FFT_RADIX * FOURIER_W
    blk = lambda w: pl.BlockSpec((seq, w), lambda b: (b, 0))
    cblk = pl.BlockSpec((1, 1, past * N_HEADS, QK_W), lambda b: (b, layer, 0, 0))
    const = lambda shape: pl.BlockSpec(shape, lambda b: (0,) * len(shape))
    return pl.pallas_call(
        functools.partial(_den_mixers_kernel, layer=layer, lam_init=lam_init),
        grid=(n_tok // seq,),
        in_specs=[
            const(lam_vec.shape), const(subln_g.shape),
            blk(ATTN_W), blk(ATTN_W), blk(ATTN_W), cblk, cblk, blk(ATTN_W),
            pl.BlockSpec((n2, wide), lambda b: (b, 0)), blk(FOURIER_W),
            const((2 * n2, n2)), const((n2, wide)), const((n2, wide)),
            const((2 * FOURIER_W, FOURIER_W)),
        ],
        out_specs=[blk(ATTN_W), blk(FOURIER_W)],
        out_shape=[jax.ShapeDtypeStruct((n_tok, ATTN_W), BF16),
                   jax.ShapeDtypeStruct((n_tok, FOURIER_W), BF16)],
        scratch_shapes=[
            pltpu.VMEM((N_HEADS, past, QK_W), BF16),
            pltpu.VMEM((N_HEADS, QK_W + ONES_ROWS, past + seq), BF16),
            pltpu.VMEM((QK_W, 2 * Q_TILE), BF16),
            pltpu.VMEM((QK_W, 2 * Q_TILE), BF16),
            pltpu.VMEM((past + seq, 2 * Q_TILE), F32),
            pltpu.VMEM((past + seq, 2 * Q_TILE), F32),
            pltpu.VMEM((QK_W + ONES_ROWS, 2 * Q_TILE), F32),
            pltpu.VMEM((QK_W + ONES_ROWS, 2 * Q_TILE), F32),
            pltpu.VMEM((2 * n2, wide), F32),
            pltpu.VMEM((seq, 2 * FOURIER_W), BF16),
        ],
        compiler_params=_cparams(("arbitrary",)),
        name="den_mixers",
    )(lam_vec, subln_g, q, k, v, cache_k, cache_v, sgc, xw, sga, f1, twc, tws, cs2)


def _fourier_ctx_kernel(fa_ref, sga_ref, cs_ref, dn_ref, o_ref, *, seq):
    ab = jnp.dot(fa_ref[...], cs_ref[...], preferred_element_type=F32).astype(BF16)
    for s in range(fa_ref.shape[0] // seq):
        rows = slice(s * seq, (s + 1) * seq)
        ab2 = jnp.concatenate([ab[rows, :FOURIER_W], ab[rows, FOURIER_W:]], axis=0)
        f = jnp.dot(dn_ref[...], ab2, preferred_element_type=F32)
        o_ref[rows, :] = (f * sga_ref[rows, :].astype(F32)).astype(BF16)


def _ctx_mixers_kernel(lam_ref, sg_ref, q_ref, k_ref, v_ref, sgc_ref, fa_ref, sga_ref, cs_ref, dn_ref,
                       yc_ref, ya_ref, *, layer, lam_init, seq):
    lam_ref, sg_ref = lam_ref.at[layer], sg_ref.at[layer:layer + 1]
    _fourier_ctx_kernel(fa_ref, sga_ref, cs_ref, dn_ref, ya_ref, seq=seq)
    _attn_ctx_kernel(lam_ref, sg_ref, q_ref, k_ref, v_ref, sgc_ref, yc_ref, lam_init=lam_init, seq=seq)


def _ctx_mixers(q, k, v, sgc, fa, sga, lam_vec, subln_g, cs, dn, *, layer, seq, lam_init):
    n_tok = q.shape[0]
    rows = CTX_SEQS * seq
    blk = lambda w: pl.BlockSpec((rows, w), lambda b: (b, 0))
    const = lambda shape: pl.BlockSpec(shape, lambda b: (0,) * len(shape))
    return pl.pallas_call(
        functools.partial(_ctx_mixers_kernel, layer=layer, lam_init=lam_init, seq=seq),
        grid=(n_tok // rows,),
        in_specs=[
            const(lam_vec.shape), const(subln_g.shape),
            blk(ATTN_W), blk(ATTN_W), blk(ATTN_W), blk(ATTN_W),
            blk(FOURIER_W), blk(FOURIER_W),
            const((FOURIER_W, 2 * FOURIER_W)), const((seq, 2 * seq)),
        ],
        out_specs=[blk(ATTN_W), blk(FOURIER_W)],
        out_shape=[jax.ShapeDtypeStruct((n_tok, ATTN_W), BF16),
                   jax.ShapeDtypeStruct((n_tok, FOURIER_W), BF16)],
        compiler_params=_cparams(("arbitrary",)),
        name="ctx_mixers",
    )(lam_vec, subln_g, q, k, v, sgc, fa, sga, cs, dn)


def _cadd(a, b):
    return (a[0] + b[0], a[1] + b[1])


def _csub(a, b):
    return (a[0] - b[0], a[1] - b[1])


def _mul_neg_i(a):
    return (a[1], -a[0])


def _mul_w8_1(a):
    return ((a[0] + a[1]) * SQRT_HALF, (a[1] - a[0]) * SQRT_HALF)


def _mul_w8_3(a):
    return ((a[1] - a[0]) * SQRT_HALF, (-a[0] - a[1]) * SQRT_HALF)


def _fft4(a0, a1, a2, a3):
    e0, e1 = _cadd(a0, a2), _csub(a0, a2)
    o0, o1 = _cadd(a1, a3), _mul_neg_i(_csub(a1, a3))
    return [_cadd(e0, o0), _cadd(e1, o1), _csub(e0, o0), _csub(e1, o1)]


def _fft8(x):
    e = _fft4(x[0], x[2], x[4], x[6])
    o = _fft4(x[1], x[3], x[5], x[7])
    t = [o[0], _mul_w8_1(o[1]), _mul_neg_i(o[2]), _mul_w8_3(o[3])]
    return [_cadd(e[k], t[k]) for k in range(4)] + [_csub(e[k], t[k]) for k in range(4)]


def _fourier_den_kernel(xw_ref, sga_ref, f1_ref, twc_ref, tws_ref, cs2_ref, o_ref, g_ref, x_ref):
    n2 = xw_ref.shape[0]
    g_ref[...] = jnp.dot(f1_ref[...], xw_ref[...], preferred_element_type=F32)

    def chunk(i):
        r = i * FFT_ROWS
        re_rows = pl.ds(r, FFT_ROWS)
        im_rows = pl.ds(n2 + r, FFT_ROWS)
        xs = []
        for n1 in range(FFT_RADIX):
            lanes = slice(n1 * FOURIER_W, (n1 + 1) * FOURIER_W)
            gr = g_ref[re_rows, lanes]
            gi = g_ref[im_rows, lanes]
            if n1 > 0:
                c = twc_ref[re_rows, lanes]
                s = tws_ref[re_rows, lanes]
                gr, gi = gr * c + gi * s, gi * c - gr * s
            xs.append((gr, gi))
        for k1, (xr, xi) in enumerate(_fft8(xs)):
            out_rows = pl.ds(k1 * n2 + r, FFT_ROWS)
            x_ref[out_rows, :FOURIER_W] = xr.astype(BF16)
            x_ref[out_rows, FOURIER_W:] = xi.astype(BF16)

    for i in range(n2 // FFT_ROWS):
        chunk(i)
    f = jnp.dot(x_ref[...], cs2_ref[...], preferred_element_type=F32)
    o_ref[...] = (f * sga_ref[...].astype(F32)).astype(BF16)


def _outproj_tile(x_ref, mod_ref, ya_ref, zc_ref, zp_ref, zn_ref, bgs_ref, yc_ref,
                  cw_ref, cb_ref, w_ref, fg_ref, o_ref, tile, *, seq, final_norm):
    tm = x_ref.shape[0]
    z = zc_ref[...].astype(F32)
    row = lax.broadcasted_iota(jnp.int32, z.shape, 0)
    pos = (tile * tm + row) & (seq - 1)
    prev_row = zp_ref[HALO_ROWS - 1:HALO_ROWS, :].astype(F32)
    next_row = zn_ref[0:1, :].astype(F32)
    z_prev = jnp.where(row == 0, prev_row, pltpu.roll(z, 1, axis=0))
    z_prev = jnp.where(pos == 0, 0.0, z_prev)
    z_next = jnp.where(row == tm - 1, next_row, pltpu.roll(z, tm - 1, axis=0))
    z_next = jnp.where(pos == seq - 1, 0.0, z_next)
    cw = cw_ref[...]
    conv = z_prev * cw[0:1] + z * cw[1:2] + z_next * cw[2:3] + cb_ref[...]
    yb = (bgs_ref[...].astype(F32) * conv).astype(BF16)

    mixed = jnp.concatenate([ya_ref[...], yb, yc_ref[...]], axis=-1)
    out = jnp.dot(mixed, w_ref[...], preferred_element_type=F32)
    gate = mod_ref[0, 0][:, 2 * D_MODEL:]
    xn = x_ref[...] + gate * out
    if final_norm:
        ms = jnp.mean(xn * xn, axis=-1, keepdims=True)
        xn = xn * lax.rsqrt(ms + EPS) * fg_ref[...]
    o_ref[...] = xn


N_GROUP_INS = 8


def _outproj_kernel(*refs, layer, n_ctx_tiles, seq, dec_seq, final_norm):
    ctx_ins = refs[:N_GROUP_INS]
    den_ins = refs[N_GROUP_INS:2 * N_GROUP_INS]
    cw_ref, cb_ref, w_ref, fg_ref, oc_ref, od_ref, wb_ref = refs[2 * N_GROUP_INS:]
    cw_ref, cb_ref = cw_ref.at[layer], cb_ref.at[layer:layer + 1]
    i = pl.program_id(0)

    @pl.when(i == 0)
    def _():
        wb_ref[...] = w_ref[0].astype(BF16)

    @pl.when(i < n_ctx_tiles)
    def _():
        _outproj_tile(*ctx_ins, cw_ref, cb_ref, wb_ref, fg_ref, oc_ref, i, seq=seq,
                      final_norm=final_norm)

    @pl.when(i >= n_ctx_tiles)
    def _():
        _outproj_tile(*den_ins, cw_ref, cb_ref, wb_ref, fg_ref, od_ref, i - n_ctx_tiles,
                      seq=dec_seq, final_norm=final_norm)


def _out_projection(ctx, den, mod, conv_w, conv_b, w_out, layer, final_g, *, seq, dec_seq,
                    final_norm):
    tm = OUT_TILE
    nct = ctx[0].shape[0] // tm
    ndt = den[0].shape[0] // tm
    halo_per_tile = tm // HALO_ROWS
    seq_tiles = dec_seq // tm

    def group_specs(tile_of, n_tok, mod_of):
        n_halo = n_tok // HALO_ROWS
        row = lambda w: pl.BlockSpec((tm, w), lambda i: (tile_of(i), 0))
        return [
            row(D_MODEL),
            pl.BlockSpec((1, 1, 1, 3 * D_MODEL), lambda i: (layer, mod_of(tile_of(i)), 0, 0)),
            row(FOURIER_W),
            row(CONV_W),
            pl.BlockSpec((HALO_ROWS, CONV_W),
                         lambda i: (jnp.maximum(tile_of(i) * halo_per_tile - 1, 0), 0)),
            pl.BlockSpec((HALO_ROWS, CONV_W),
                         lambda i: (jnp.minimum((tile_of(i) + 1) * halo_per_tile, n_halo - 1), 0)),
            row(CONV_W),
            row(ATTN_W),
        ], row(D_MODEL)

    ctx_specs, ctx_out = group_specs(lambda i: jnp.minimum(i, nct - 1), ctx[0].shape[0],
                                     lambda t: 0)
    den_specs, den_out = group_specs(lambda i: jnp.maximum(i - nct, 0), den[0].shape[0],
                                     lambda t: 1 + t // seq_tiles)

    def group_args(g):
        x, ya, zc, bgs, yc = g
        return [x, mod, ya, zc, zc, zc, bgs, yc]

    return pl.pallas_call(
        functools.partial(_outproj_kernel, layer=layer, n_ctx_tiles=nct, seq=seq, dec_seq=dec_seq,
                          final_norm=final_norm),
        grid=(nct + ndt,),
        in_specs=ctx_specs + den_specs + [
            pl.BlockSpec(conv_w.shape, lambda i: (0, 0, 0)),
            pl.BlockSpec(conv_b.shape, lambda i: (0, 0)),
            pl.BlockSpec((1, D_MODEL, D_MODEL), lambda i: (layer, 0, 0)),
            pl.BlockSpec((1, D_MODEL), lambda i: (0, 0)),
        ],
        out_specs=[ctx_out, den_out],
        out_shape=[jax.ShapeDtypeStruct(ctx[0].shape, F32), jax.ShapeDtypeStruct(den[0].shape, F32)],
        scratch_shapes=[pltpu.VMEM((D_MODEL, D_MODEL), BF16)],
        compiler_params=_cparams(("arbitrary",)),
        name="out_projection",
    )(*group_args(ctx), *group_args(den), conv_w, conv_b, w_out,
      final_g.reshape(1, D_MODEL))


def kernel(x_prompt, x_sample, cache_k, cache_v, c, c_ctx, norm_g, w_mod, b_mod, w_in, conv_w,
           conv_b, lam_vec, subln_g, w_out, final_g):
    batch, seq, _ = x_prompt.shape
    dec_batch, dec_seq, _ = x_sample.shape

    mod = _modulation(c_ctx, c, w_mod, b_mod)

    rope_tabs = tuple(jnp.asarray(t) for t in _rope_tables(dec_seq))
    cs = jnp.asarray(_chan_tables()).astype(BF16)
    dn_ctx = jnp.asarray(_dft_tables(seq)).astype(BF16)
    f1, twc, tws, cs2 = (jnp.asarray(t) for t in _ct_tables(dec_seq))
    f1 = f1.astype(BF16)
    cs2 = cs2.astype(BF16)

    cache_k2 = cache_k.reshape(dec_batch, DEPTH, -1, QK_W)
    cache_v2 = cache_v.reshape(dec_batch, DEPTH, -1, QK_W)
    xc = x_prompt.reshape(batch * seq, D_MODEL)
    xl = x_sample.reshape(dec_batch * dec_seq, D_MODEL)
    kv = None
    for l in range(DEPTH):
        lam_init = 0.8 - 0.6 * math.exp(-0.3 * l)
        ctx, k32, v32, den = _in_projection(
            xc, xl, mod, norm_g, w_in, l, rope_tabs, seq=seq, dec_seq=dec_seq,
            prev_kv=kv)
        kv = (k32, v32)

        fa, sga, zc_c, bgs_c, q, k, v, sgc = ctx
        yc_c, ya_c = _ctx_mixers(q, k, v, sgc, fa, sga, lam_vec, subln_g, cs, dn_ctx,
                                 layer=l, seq=seq, lam_init=lam_init)

        fa, sga, zc_d, bgs_d, q, k, v, sgc = den
        yc_d, ya_d = _den_mixers(q, k, v, sgc, cache_k2, cache_v2, fa, sga, lam_vec, subln_g,
                                 f1, twc, tws, cs2, layer=l, seq=dec_seq, lam_init=lam_init)

        xc, xl = _out_projection(
            (xc, ya_c, zc_c, bgs_c, yc_c), (xl, ya_d, zc_d, bgs_d, yc_d), mod,
            conv_w, conv_b, w_out, l, final_g, seq=seq, dec_seq=dec_seq,
            final_norm=l == DEPTH - 1)

    y_prompt = xc.reshape(batch, seq, D_MODEL)
    y_sample = xl.reshape(dec_batch, dec_seq, D_MODEL)
    return (y_prompt, y_sample, *(t.reshape(batch, DEPTH, seq, N_HEADS, QK_W) for t in kv))
```

```python
import functools
import math

import numpy as np
import jax
import jax.numpy as jnp
from jax import lax
from jax.experimental import pallas as pl
from jax.experimental.pallas import tpu as pltpu

D_MODEL = 1024
DEPTH = 2
GRID_W = 64
FOURIER_W = 256
CONV_W = 256
ATTN_W = 512
N_HEADS = 4
HEAD_DIM = 64
QK_W = 128
ROPE_BASE = 10000.0
ROPE_W = HEAD_DIM // 2
ROPE_ROT = ROPE_W // 2
EPS = 1e-6
IN_DIM = 3584

F32 = jnp.float32
BF16 = jnp.bfloat16

VMEM_LIMIT_BYTES = 60 * 1024 * 1024
TOKEN_TILE = 512
OUT_TILE = 1024
Q_TILE = 256
KEY_CHUNK = 256
ONES_ROWS = 16
LANES = 128
FFT_RADIX = 8
FFT_ROWS = 16
SQRT_HALF = 0.7071067811865476
CTX_SEQS = 4
HALO_ROWS = 16
MOD_ROWS = 8
MOD_CHUNK = 3 * D_MODEL
LOG2E = 1.4426950408889634

_COLS = {}
_off = 0
for _name, _w in (("fa", 256), ("ga", 256), ("bg", 256), ("cg", 256), ("hc", 256), ("gb", 256),
                  ("q", 512), ("k", 512), ("v", 512), ("gc", 512)):
    _COLS[_name] = (_off, _off + _w)
    _off += _w


def _silu(x):
    return x * (1.0 / (1.0 + jnp.exp(-x)))


def _cparams(sem):
    return pltpu.CompilerParams(dimension_semantics=sem, vmem_limit_bytes=VMEM_LIMIT_BYTES)


def _rope_tables(n_tokens):
    n = np.arange(n_tokens)
    row = (n // GRID_W).astype(np.float64)
    col = (n % GRID_W).astype(np.float64)
    j = np.arange(QK_W)
    jj = j % HEAD_DIM
    idx = jj % ROPE_W
    inv = 1.0 / (ROPE_BASE ** (2.0 * (idx % ROPE_ROT) / ROPE_W))
    pos = np.where((jj < ROPE_W)[None, :], row[:, None], col[:, None])
    ang = pos * inv[None, :]
    cos = np.cos(ang)
    sin = np.sin(ang)
    first = (idx < ROPE_ROT)[None, :]
    sin_a = np.where(first, -sin, 0.0)
    sin_b = np.where(first, 0.0, sin)
    return (np.asarray(cos, np.float32), np.asarray(sin_a, np.float32), np.asarray(sin_b, np.float32))


def _dft_tables(n):
    k = np.arange(n)
    kn = (k[:, None] * k[None, :]) % n
    ang = 2.0 * np.pi * kn / n
    return np.asarray(np.concatenate([np.cos(ang), -np.sin(ang)], axis=1) / math.sqrt(n), np.float32)


def _chan_tables():
    k = np.arange(FOURIER_W)
    kn = (k[:, None] * k[None, :]) % FOURIER_W
    ang = 2.0 * np.pi * kn / FOURIER_W
    return np.asarray(np.concatenate([np.cos(ang), np.sin(ang)], axis=1) / math.sqrt(FOURIER_W), np.float32)


def _ct_tables(n):
    n2 = n // FFT_RADIX
    k = np.arange(n2)
    ang = 2.0 * np.pi * ((k[:, None] * k[None, :]) % n2) / n2
    f1 = np.concatenate([np.cos(ang), -np.sin(ang)], axis=0) / math.sqrt(n * FOURIER_W)
    tw = 2.0 * np.pi * k[:, None] * np.arange(FFT_RADIX)[None, :] / n
    twc = np.repeat(np.cos(tw), FOURIER_W, axis=1)
    tws = np.repeat(np.sin(tw), FOURIER_W, axis=1)
    c = np.arange(FOURIER_W)
    angc = 2.0 * np.pi * ((c[:, None] * c[None, :]) % FOURIER_W) / FOURIER_W
    cs2 = np.concatenate([np.cos(angc), np.sin(angc)], axis=0)
    return tuple(np.asarray(t, np.float32) for t in (f1, twc, tws, cs2))


def _mod_kernel(cctx_ref, c_ref, w_ref, b_ref, o_ref, s_ref):
    n_c = c_ref.shape[0]
    s_ref[...] = jnp.zeros_like(s_ref)
    s_ref[0:1, :] = _silu(cctx_ref[...])
    s_ref[1:1 + n_c, :] = _silu(c_ref[...])
    w = w_ref[0].astype(BF16)
    res = jnp.dot(s_ref[...].astype(BF16), w, preferred_element_type=F32) + b_ref[0]
    for r in range(MOD_ROWS):
        o_ref[0, r] = res[r:r + 1]


def _modulation(c_ctx, c, w_mod, b_mod):
    chunk = MOD_CHUNK
    n_chunks = 3 * D_MODEL // chunk
    return pl.pallas_call(
        _mod_kernel,
        grid=(DEPTH, n_chunks),
        in_specs=[
            pl.BlockSpec((1, D_MODEL), lambda l, j: (0, 0)),
            pl.BlockSpec(c.shape, lambda l, j: (0, 0)),
            pl.BlockSpec((1, D_MODEL, chunk), lambda l, j: (l, 0, j)),
            pl.BlockSpec((1, 1, chunk), lambda l, j: (l, 0, j)),
        ],
        out_specs=pl.BlockSpec((1, MOD_ROWS, 1, chunk), lambda l, j: (l, 0, 0, j)),
        out_shape=jax.ShapeDtypeStruct((DEPTH, MOD_ROWS, 1, 3 * D_MODEL), F32),
        scratch_shapes=[pltpu.VMEM((MOD_ROWS, D_MODEL), F32)],
        compiler_params=_cparams(("arbitrary", "arbitrary")),
        name="modulation",
    )(c_ctx.reshape(1, D_MODEL), c, w_mod, b_mod.reshape(DEPTH, 1, 3 * D_MODEL))


def _store_heads(ref, layer, t):
    seqs, _, rows, _ = ref.shape
    seq = rows // N_HEADS
    for s in range(seqs):
        for hd in range(N_HEADS):
            ref[s, layer, pl.ds(hd, seq, stride=N_HEADS), :] = (
                t[s * seq:(s + 1) * seq, hd * QK_W:(hd + 1) * QK_W])


def _inproj_tile(x_ref, mod_ref, g_ref, w_ref, rope_refs, outs, kv_refs, stage_ref):
    fa_ref, sga_ref, zc_ref, bgs_ref, q_ref, k_ref, v_ref, sgc_ref = outs

    x = x_ref[...]
    ms = jnp.mean(x * x, axis=-1, keepdims=True)
    y = x * lax.rsqrt(ms + EPS) * g_ref[...]
    m = mod_ref[0, 0]
    h = (y * (1.0 + m[:, D_MODEL:2 * D_MODEL]) + m[:, :D_MODEL]).astype(BF16)

    def proj(name):
        lo, hi = _COLS[name]
        return jnp.dot(h, w_ref[:, lo:hi], preferred_element_type=F32)

    if stage_ref is not None:
        fa = proj("fa")
        halves = stage_ref.shape[0]
        for hf in range(halves):
            stage_ref[hf] = fa[:, hf * LANES:(hf + 1) * LANES]
        rows = stage_ref.shape[1] // FFT_RADIX
        for n1 in range(FFT_RADIX):
            for hf in range(halves):
                lo = n1 * FOURIER_W + hf * LANES
                fa_ref[:, lo:lo + LANES] = (
                    stage_ref[hf, pl.ds(n1, rows, stride=FFT_RADIX), :].astype(BF16))
    else:
        fa_ref[...] = proj("fa").astype(BF16)
    sga_ref[...] = _silu(proj("ga")).astype(BF16)
    zc_ref[...] = (proj("cg") * proj("hc")).astype(BF16)
    bgs_ref[...] = (proj("bg") * _silu(proj("gb"))).astype(BF16)
    sgc_ref[...] = _silu(proj("gc")).astype(BF16)

    q = proj("q")
    k = proj("k")
    v = proj("v")
    if kv_refs is not None and len(kv_refs) == 2:
        kv_refs[0][...] = k
        kv_refs[1][...] = v
    elif kv_refs is not None:
        new_k_ref, new_v_ref, prev_k_ref, prev_v_ref = kv_refs
        _store_heads(new_k_ref, 0, prev_k_ref[...])
        _store_heads(new_v_ref, 0, prev_v_ref[...])
        _store_heads(new_k_ref, 1, k)
        _store_heads(new_v_ref, 1, v)
    v_ref[...] = v.astype(BF16)

    q_scale = HEAD_DIM ** -0.5 * LOG2E
    if rope_refs is not None:
        cos = rope_refs[0][...]
        sa = rope_refs[1][...]
        sb = rope_refs[2][...]
        for hd in range(N_HEADS):
            sl = slice(hd * QK_W, (hd + 1) * QK_W)
            for t, ref, scale in ((q, q_ref, q_scale), (k, k_ref, None)):
                th = t[:, sl]
                r = (th * cos + pltpu.roll(th, QK_W - ROPE_ROT, axis=1) * sa
                     + pltpu.roll(th, ROPE_ROT, axis=1) * sb)
                if scale is not None:
                    r = r * scale
                ref[:, sl] = r.astype(BF16)
    else:
        q_ref[...] = (q * q_scale).astype(BF16)
        k_ref[...] = k.astype(BF16)


N_PROJ_OUTS = 8


def _inproj_kernel(xc_ref, xd_ref, modc_ref, modd_ref, g_ref, w_ref, cos_ref, sa_ref, sb_ref,
                   *rest, layer, n_ctx_tiles, n_prev):
    prev_kv = rest[:n_prev]
    rest = rest[n_prev:]
    ctx_outs = rest[:N_PROJ_OUTS]
    kv_refs = rest[N_PROJ_OUTS:N_PROJ_OUTS + 2] + prev_kv
    den_outs = rest[N_PROJ_OUTS + 2:2 * N_PROJ_OUTS + 2]
    wb_ref, stage_ref = rest[2 * N_PROJ_OUTS + 2:]
    g_ref = g_ref.at[layer:layer + 1]
    i = pl.program_id(0)

    @pl.when(i == 0)
    def _():
        for lo, hi in _COLS.values():
            wb_ref[:, lo:hi] = w_ref[0, :, lo:hi].astype(BF16)

    @pl.when(i < n_ctx_tiles)
    def _():
        _inproj_tile(xc_ref, modc_ref, g_ref, wb_ref, None, ctx_outs, kv_refs, None)

    @pl.when(i >= n_ctx_tiles)
    def _():
        _inproj_tile(xd_ref, modd_ref, g_ref, wb_ref, (cos_ref, sa_ref, sb_ref), den_outs, None,
                     stage_ref)


def _in_projection(xc, xd, mod, norm_g, w_in, layer, rope_tabs, *, seq, dec_seq,
                   prev_kv):
    assert DEPTH == 2, "the last layer's call assembles the k/v of exactly two layers"
    tm = TOKEN_TILE
    n_c, n_d = xc.shape[0], xd.shape[0]
    nct, ndt = n_c // tm, n_d // tm
    seq_tiles = dec_seq // tm
    ctx_i = lambda i: jnp.minimum(i, nct - 1)
    den_i = lambda i: jnp.maximum(i - nct, 0)
    row_c = lambda w: pl.BlockSpec((tm, w), lambda i: (ctx_i(i), 0))
    row_d = lambda w: pl.BlockSpec((tm, w), lambda i: (den_i(i), 0))
    tab = pl.BlockSpec((tm, QK_W), lambda i: (den_i(i) % seq_tiles, 0))
    in_specs = [
        row_c(D_MODEL), row_d(D_MODEL),
        pl.BlockSpec((1, 1, 1, 3 * D_MODEL), lambda i: (layer, 0, 0, 0)),
        pl.BlockSpec((1, 1, 1, 3 * D_MODEL), lambda i: (layer, 1 + den_i(i) // seq_tiles, 0, 0)),
        pl.BlockSpec((DEPTH, D_MODEL), lambda i: (0, 0)),
        pl.BlockSpec((1, D_MODEL, IN_DIM), lambda i: (layer, 0, 0)),
        tab, tab, tab,
    ]
    args = [xc, xd, mod, mod, norm_g, w_in, *rope_tabs]
    widths = (256, 256, 256, 256, 512, 512, 512, 512)
    seqs = tm // seq
    if prev_kv is None:
        kv_blk = row_c(ATTN_W)
        kv_shape = jax.ShapeDtypeStruct((n_c, ATTN_W), F32)
    else:
        kv_blk = pl.BlockSpec((seqs, DEPTH, seq * N_HEADS, QK_W), lambda i: (ctx_i(i), 0, 0, 0))
        kv_shape = jax.ShapeDtypeStruct((n_c // seq, DEPTH, seq * N_HEADS, QK_W), F32)
        in_specs += [row_c(ATTN_W)] * 2
        args += list(prev_kv)
    out_specs = [row_c(w) for w in widths] + [kv_blk, kv_blk] + [row_d(w) for w in widths]
    out_shape = ([jax.ShapeDtypeStruct((n_c, w), BF16) for w in widths] + [kv_shape, kv_shape]
                 + [jax.ShapeDtypeStruct((n_d, w), BF16) for w in widths])
    fa_d = N_PROJ_OUTS + 2
    out_specs[fa_d] = pl.BlockSpec((tm // FFT_RADIX, FFT_RADIX * FOURIER_W), lambda i: (den_i(i), 0))
    out_shape[fa_d] = jax.ShapeDtypeStruct((n_d // FFT_RADIX, FFT_RADIX * FOURIER_W), BF16)
    outs = pl.pallas_call(
        functools.partial(_inproj_kernel, layer=layer, n_ctx_tiles=nct,
                          n_prev=0 if prev_kv is None else 2),
        grid=(nct + ndt,),
        in_specs=in_specs,
        out_specs=out_specs,
        out_shape=out_shape,
        scratch_shapes=[pltpu.VMEM((D_MODEL, IN_DIM), BF16),
                        pltpu.VMEM((FOURIER_W // LANES, tm, LANES), F32)],
        compiler_params=_cparams(("arbitrary",)),
        name="in_projection",
    )(*args)
    return outs[:N_PROJ_OUTS], outs[N_PROJ_OUTS], outs[N_PROJ_OUTS + 1], outs[fa_d:]


def _lambda(lam_ref, lam_init):
    lv = lam_ref[...]
    a = jnp.sum(lv[0:1] * lv[1:2], axis=-1, keepdims=True)
    b = jnp.sum(lv[2:3] * lv[3:4], axis=-1, keepdims=True)
    return jnp.exp(a) - jnp.exp(b) + lam_init


def _stack_masked(q_t):
    first = lax.broadcasted_iota(jnp.int32, q_t.shape, 0) < HEAD_DIM
    return jnp.concatenate([jnp.where(first, q_t, 0.0), jnp.where(first, 0.0, q_t)],
                           axis=1).astype(BF16)


def _scores_chunk(k, qq_t, m):
    s = jnp.dot(k, qq_t, preferred_element_type=F32)
    mc = jnp.max(s, axis=0, keepdims=True)
    return s, (mc if m is None else jnp.maximum(m, mc))


def _with_ones_rows(v_t):
    return jnp.concatenate([v_t, jnp.ones((ONES_ROWS, v_t.shape[1]), F32)], axis=0).astype(BF16)


def _values_chunk(s, m, v1_t, acc):
    e = jnp.exp2(s - m).astype(BF16)
    oc = jnp.dot(v1_t, e, preferred_element_type=F32)
    return oc if acc is None else acc + oc


def _combine_t(acc, lam):
    tq = acc.shape[1] // 2
    r = 1.0 / acc[QK_W:QK_W + 1, :]
    o_t = acc[:QK_W, :]
    return o_t[:, :tq] * r[:, :tq] - o_t[:, tq:] * (r[:, tq:] * lam)


def _subln_gate(o, sg, sgc, lam_init):
    ms = jnp.mean(o * o, axis=-1, keepdims=True)
    y = o * lax.rsqrt(ms + EPS) * sg * (1.0 - lam_init)
    return (y * sgc.astype(F32)).astype(BF16)


def _attn_ctx_kernel(lam_ref, sg_ref, q_ref, k_ref, v_ref, sgc_ref, o_ref, *, lam_init, seq):
    lam = _lambda(lam_ref, lam_init)
    sg = sg_ref[...]
    n_seq = q_ref.shape[0] // seq
    q_t = q_ref[...].astype(F32).T
    v_t = v_ref[...].astype(F32).T
    pairs = [(slice(s * seq, (s + 1) * seq), slice(hd * QK_W, (hd + 1) * QK_W))
             for s in range(n_seq) for hd in range(N_HEADS)]
    qqs = [_stack_masked(q_t[hl, rows]) for rows, hl in pairs]
    v1s = [_with_ones_rows(v_t[hl, rows]) for rows, hl in pairs]
    sm = [_scores_chunk(k_ref[rows, hl], qq, None) for (rows, hl), qq in zip(pairs, qqs)]
    accs = [_values_chunk(s, m, v1, None) for (s, m), v1 in zip(sm, v1s)]
    y_t = [_combine_t(a, lam) for a in accs]
    for s in range(n_seq):
        rows = slice(s * seq, (s + 1) * seq)
        y = jnp.concatenate(y_t[s * N_HEADS:(s + 1) * N_HEADS], axis=0).T
        for hd in range(N_HEADS):
            hl = slice(hd * QK_W, (hd + 1) * QK_W)
            o_ref[rows, hl] = _subln_gate(y[:, hl], sg, sgc_ref[rows, hl], lam_init)


def _attn_den_kernel(lam_ref, sg_ref, q_ref, kn_ref, vn_ref, kc_ref, vc_ref, sgc_ref, o_ref,
                     kcs_ref, vt_ref, qq0_ref, qq1_ref, s0_ref, s1_ref, acc0_ref, acc1_ref,
                     *, lam_init):
    seq = q_ref.shape[0]
    past = kcs_ref.shape[1]
    past_chunks = past // KEY_CHUNK
    n_chunks = (past + seq) // KEY_CHUNK
    n_tiles = seq // Q_TILE
    n_elems = n_tiles * N_HEADS
    qq_refs = (qq0_ref, qq1_ref)
    s_refs = (s0_ref, s1_ref)
    acc_refs = (acc0_ref, acc1_ref)
    lam = _lambda(lam_ref, lam_init)
    sg = sg_ref[...]

    def lanes(hd):
        return slice(hd * QK_W, (hd + 1) * QK_W)

    def chunk(c):
        return slice(c * KEY_CHUNK, (c + 1) * KEY_CHUNK)

    def tile_rows(t):
        if isinstance(t, int):
            return pl.ds(t * Q_TILE, Q_TILE)
        return pl.ds(pl.multiple_of(t * Q_TILE, Q_TILE), Q_TILE)

    for hd in range(N_HEADS):
        cache_rows = pl.ds(hd, past, stride=N_HEADS)
        kcs_ref[hd] = kc_ref[0, 0, cache_rows, :].astype(BF16)
        vt_ref[hd, :, :past] = _with_ones_rows(vc_ref[0, 0, cache_rows, :].T)
        vt_ref[hd, :, past:] = _with_ones_rows(vn_ref[:, lanes(hd)].astype(F32).T)

    def keys(hd, c):
        if c < past_chunks:
            return kcs_ref[hd, chunk(c), :]
        return kn_ref[chunk(c - past_chunks), lanes(hd)]

    def prep(t, hd, par):
        qq_refs[par][...] = _stack_masked(q_ref[tile_rows(t), lanes(hd)].astype(F32).T)

    def finish(t, hd, par):
        rows = tile_rows(t)
        y = _combine_t(acc_refs[par][...], lam).T
        o_ref[rows, lanes(hd)] = _subln_gate(y, sg, sgc_ref[rows, lanes(hd)], lam_init)

    def step(t, j, m_cur, first=False, last=False):
        def elem(off):
            return t + (j + off) // N_HEADS, (j + off) % N_HEADS

        par = j % 2
        e_static = N_HEADS * t + j if isinstance(t, int) else None
        do_finish = not (first and j == 0)
        do_scores = not (last and e_static + 1 >= n_elems)
        do_prep = not (last and e_static + 2 >= n_elems)
        if do_prep:
            prep(*elem(2), par)
        hd_nxt = elem(1)[1]
        qq = qq_refs[1 - par][...] if do_scores else None
        m_nxt = None
        acc = None
        for c in range(n_chunks):
            if do_scores:
                s, m_nxt = _scores_chunk(keys(hd_nxt, c), qq, m_nxt)
                s_refs[1 - par][chunk(c), :] = s
            acc = _values_chunk(s_refs[par][chunk(c), :], m_cur, vt_ref[j, :, chunk(c)], acc)
        acc_refs[par][...] = acc
        if do_finish:
            finish(*elem(-1), 1 - par)
        return m_nxt

    def tile_steps(t, m, **edge):
        for j in range(N_HEADS):
            m = step(t, j, m, **edge)
        return m

    prep(0, 0, 0)
    prep(0, 1, 1)
    qq = qq0_ref[...]
    m = None
    for c in range(n_chunks):
        s, m = _scores_chunk(keys(0, c), qq, m)
        s0_ref[chunk(c), :] = s
    m = tile_steps(0, m, first=True)
    def tile_pair(p, m):
        return tile_steps(2 * p + 2, tile_steps(2 * p + 1, m))

    m = lax.fori_loop(0, (n_tiles - 2) // 2, tile_pair, m)
    tile_steps(n_tiles - 1, m, last=True)
    finish(n_tiles - 1, N_HEADS - 1, (n_elems - 1) % 2)


def _den_mixers_kernel(lam_ref, sg_ref, q_ref, kn_ref, vn_ref, kc_ref, vc_ref, sgc_ref,
                       xw_ref, sga_ref, f1_ref, twc_ref, tws_ref, cs2_ref, yc_ref, ya_ref,
                       *scratch, layer, lam_init):
    lam_ref, sg_ref = lam_ref.at[layer], sg_ref.at[layer:layer + 1]
    g_ref, x_ref = scratch[-2:]
    _attn_den_kernel(lam_ref, sg_ref, q_ref, kn_ref, vn_ref, kc_ref, vc_ref, sgc_ref, yc_ref,
                     *scratch[:-2], lam_init=lam_init)
    _fourier_den_kernel(xw_ref, sga_ref, f1_ref, twc_ref, tws_ref, cs2_ref, ya_ref, g_ref, x_ref)


def _den_mixers(q, k, v, sgc, cache_k, cache_v, xw, sga, lam_vec, subln_g, f1, twc, tws, cs2,
                *, layer, seq, lam_init):
    n_tok = q.shape[0]
    past = cache_k.shape[2] // N_HEADS
    n2 = seq // FFT_RADIX
    wide = FFT_RADIX * FOURIER_W
    blk = lambda w: pl.BlockSpec((seq, w), lambda b: (b, 0))
    cblk = pl.BlockSpec((1, 1, past * N_HEADS, QK_W), lambda b: (b, layer, 0, 0))
    const = lambda shape: pl.BlockSpec(shape, lambda b: (0,) * len(shape))
    return pl.pallas_call(
        functools.partial(_den_mixers_kernel, layer=layer, lam_init=lam_init),
        grid=(n_tok // seq,),
        in_specs=[
            const(lam_vec.shape), const(subln_g.shape),
            blk(ATTN_W), blk(ATTN_W), blk(ATTN_W), cblk, cblk, blk(ATTN_W),
            pl.BlockSpec((n2, wide), lambda b: (b, 0)), blk(FOURIER_W),
            const((2 * n2, n2)), const((n2, wide)), const((n2, wide)),
            const((2 * FOURIER_W, FOURIER_W)),
        ],
        out_specs=[blk(ATTN_W), blk(FOURIER_W)],
        out_shape=[jax.ShapeDtypeStruct((n_tok, ATTN_W), BF16),
                   jax.ShapeDtypeStruct((n_tok, FOURIER_W), BF16)],
        scratch_shapes=[
            pltpu.VMEM((N_HEADS, past, QK_W), BF16),
            pltpu.VMEM((N_HEADS, QK_W + ONES_ROWS, past + seq), BF16),
            pltpu.VMEM((QK_W, 2 * Q_TILE), BF16),
            pltpu.VMEM((QK_W, 2 * Q_TILE), BF16),
            pltpu.VMEM((past + seq, 2 * Q_TILE), F32),
            pltpu.VMEM((past + seq, 2 * Q_TILE), F32),
            pltpu.VMEM((QK_W + ONES_ROWS, 2 * Q_TILE), F32),
            pltpu.VMEM((QK_W + ONES_ROWS, 2 * Q_TILE), F32),
            pltpu.VMEM((2 * n2, wide), F32),
            pltpu.VMEM((seq, 2 * FOURIER_W), BF16),
        ],
        compiler_params=_cparams(("arbitrary",)),
        name="den_mixers",
    )(lam_vec, subln_g, q, k, v, cache_k, cache_v, sgc, xw, sga, f1, twc, tws, cs2)


def _fourier_ctx_kernel(fa_ref, sga_ref, cs_ref, dn_ref, o_ref, *, seq):
    ab = jnp.dot(fa_ref[...], cs_ref[...], preferred_element_type=F32).astype(BF16)
    for s in range(fa_ref.shape[0] // seq):
        rows = slice(s * seq, (s + 1) * seq)
        ab2 = jnp.concatenate([ab[rows, :FOURIER_W], ab[rows, FOURIER_W:]], axis=0)
        f = jnp.dot(dn_ref[...], ab2, preferred_element_type=F32)
        o_ref[rows, :] = (f * sga_ref[rows, :].astype(F32)).astype(BF16)


def _ctx_mixers_kernel(lam_ref, sg_ref, q_ref, k_ref, v_ref, sgc_ref, fa_ref, sga_ref, cs_ref, dn_ref,
                       yc_ref, ya_ref, *, layer, lam_init, seq):
    lam_ref, sg_ref = lam_ref.at[layer], sg_ref.at[layer:layer + 1]
    _attn_ctx_kernel(lam_ref, sg_ref, q_ref, k_ref, v_ref, sgc_ref, yc_ref, lam_init=lam_init, seq=seq)
    _fourier_ctx_kernel(fa_ref, sga_ref, cs_ref, dn_ref, ya_ref, seq=seq)


def _ctx_mixers(q, k, v, sgc, fa, sga, lam_vec, subln_g, cs, dn, *, layer, seq, lam_init):
    n_tok = q.shape[0]
    rows = CTX_SEQS * seq
    blk = lambda w: pl.BlockSpec((rows, w), lambda b: (b, 0))
    const = lambda shape: pl.BlockSpec(shape, lambda b: (0,) * len(shape))
    return pl.pallas_call(
        functools.partial(_ctx_mixers_kernel, layer=layer, lam_init=lam_init, seq=seq),
        grid=(n_tok // rows,),
        in_specs=[
            const(lam_vec.shape), const(subln_g.shape),
            blk(ATTN_W), blk(ATTN_W), blk(ATTN_W), blk(ATTN_W),
            blk(FOURIER_W), blk(FOURIER_W),
            const((FOURIER_W, 2 * FOURIER_W)), const((seq, 2 * seq)),
        ],
        out_specs=[blk(ATTN_W), blk(FOURIER_W)],
        out_shape=[jax.ShapeDtypeStruct((n_tok, ATTN_W), BF16),
                   jax.ShapeDtypeStruct((n_tok, FOURIER_W), BF16)],
        compiler_params=_cparams(("arbitrary",)),
        name="ctx_mixers",
    )(lam_vec, subln_g, q, k, v, sgc, fa, sga, cs, dn)


def _cadd(a, b):
    return (a[0] + b[0], a[1] + b[1])


def _csub(a, b):
    return (a[0] - b[0], a[1] - b[1])


def _mul_neg_i(a):
    return (a[1], -a[0])


def _mul_w8_1(a):
    return ((a[0] + a[1]) * SQRT_HALF, (a[1] - a[0]) * SQRT_HALF)


def _mul_w8_3(a):
    return ((a[1] - a[0]) * SQRT_HALF, (-a[0] - a[1]) * SQRT_HALF)


def _fft4(a0, a1, a2, a3):
    e0, e1 = _cadd(a0, a2), _csub(a0, a2)
    o0, o1 = _cadd(a1, a3), _mul_neg_i(_csub(a1, a3))
    return [_cadd(e0, o0), _cadd(e1, o1), _csub(e0, o0), _csub(e1, o1)]


def _fft8(x):
    e = _fft4(x[0], x[2], x[4], x[6])
    o = _fft4(x[1], x[3], x[5], x[7])
    t = [o[0], _mul_w8_1(o[1]), _mul_neg_i(o[2]), _mul_w8_3(o[3])]
    return [_cadd(e[k], t[k]) for k in range(4)] + [_csub(e[k], t[k]) for k in range(4)]


def _fourier_den_kernel(xw_ref, sga_ref, f1_ref, twc_ref, tws_ref, cs2_ref, o_ref, g_ref, x_ref):
    n2 = xw_ref.shape[0]
    g_ref[...] = jnp.dot(f1_ref[...], xw_ref[...], preferred_element_type=F32)

    def chunk(i):
        r = i * FFT_ROWS
        re_rows = pl.ds(r, FFT_ROWS)
        im_rows = pl.ds(n2 + r, FFT_ROWS)
        xs = []
        for n1 in range(FFT_RADIX):
            lanes = slice(n1 * FOURIER_W, (n1 + 1) * FOURIER_W)
            gr = g_ref[re_rows, lanes]
            gi = g_ref[im_rows, lanes]
            if n1 > 0:
                c = twc_ref[re_rows, lanes]
                s = tws_ref[re_rows, lanes]
                gr, gi = gr * c + gi * s, gi * c - gr * s
            xs.append((gr, gi))
        for k1, (xr, xi) in enumerate(_fft8(xs)):
            out_rows = pl.ds(k1 * n2 + r, FFT_ROWS)
            x_ref[out_rows, :FOURIER_W] = xr.astype(BF16)
            x_ref[out_rows, FOURIER_W:] = xi.astype(BF16)

    for i in range(n2 // FFT_ROWS):
        chunk(i)
    f = jnp.dot(x_ref[...], cs2_ref[...], preferred_element_type=F32)
    o_ref[...] = (f * sga_ref[...].astype(F32)).astype(BF16)


def _outproj_tile(x_ref, mod_ref, ya_ref, zc_ref, zp_ref, zn_ref, bgs_ref, yc_ref,
                  cw_ref, cb_ref, w_ref, fg_ref, o_ref, tile, *, seq, final_norm):
    tm = x_ref.shape[0]
    z = zc_ref[...].astype(F32)
    row = lax.broadcasted_iota(jnp.int32, z.shape, 0)
    pos = (tile * tm + row) & (seq - 1)
    prev_row = zp_ref[HALO_ROWS - 1:HALO_ROWS, :].astype(F32)
    next_row = zn_ref[0:1, :].astype(F32)
    z_prev = jnp.where(row == 0, prev_row, pltpu.roll(z, 1, axis=0))
    z_prev = jnp.where(pos == 0, 0.0, z_prev)
    z_next = jnp.where(row == tm - 1, next_row, pltpu.roll(z, tm - 1, axis=0))
    z_next = jnp.where(pos == seq - 1, 0.0, z_next)
    cw = cw_ref[...]
    conv = z_prev * cw[0:1] + z * cw[1:2] + z_next * cw[2:3] + cb_ref[...]
    yb = (bgs_ref[...].astype(F32) * conv).astype(BF16)

    mixed = jnp.concatenate([ya_ref[...], yb, yc_ref[...]], axis=-1)
    out = jnp.dot(mixed, w_ref[...], preferred_element_type=F32)
    gate = mod_ref[0, 0][:, 2 * D_MODEL:]
    xn = x_ref[...] + gate * out
    if final_norm:
        ms = jnp.mean(xn * xn, axis=-1, keepdims=True)
        xn = xn * lax.rsqrt(ms + EPS) * fg_ref[...]
    o_ref[...] = xn


N_GROUP_INS = 8


def _outproj_kernel(*refs, layer, n_ctx_tiles, seq, dec_seq, final_norm):
    ctx_ins = refs[:N_GROUP_INS]
    den_ins = refs[N_GROUP_INS:2 * N_GROUP_INS]
    cw_ref, cb_ref, w_ref, fg_ref, oc_ref, od_ref, wb_ref = refs[2 * N_GROUP_INS:]
    cw_ref, cb_ref = cw_ref.at[layer], cb_ref.at[layer:layer + 1]
    i = pl.program_id(0)

    @pl.when(i == 0)
    def _():
        wb_ref[...] = w_ref[0].astype(BF16)

    @pl.when(i < n_ctx_tiles)
    def _():
        _outproj_tile(*ctx_ins, cw_ref, cb_ref, wb_ref, fg_ref, oc_ref, i, seq=seq,
                      final_norm=final_norm)

    @pl.when(i >= n_ctx_tiles)
    def _():
        _outproj_tile(*den_ins, cw_ref, cb_ref, wb_ref, fg_ref, od_ref, i - n_ctx_tiles,
                      seq=dec_seq, final_norm=final_norm)


def _out_projection(ctx, den, mod, conv_w, conv_b, w_out, layer, final_g, *, seq, dec_seq,
                    final_norm):
    tm = OUT_TILE
    nct = ctx[0].shape[0] // tm
    ndt = den[0].shape[0] // tm
    halo_per_tile = tm // HALO_ROWS
    seq_tiles = dec_seq // tm

    def group_specs(tile_of, n_tok, mod_of):
        n_halo = n_tok // HALO_ROWS
        row = lambda w: pl.BlockSpec((tm, w), lambda i: (tile_of(i), 0))
        return [
            row(D_MODEL),
            pl.BlockSpec((1, 1, 1, 3 * D_MODEL), lambda i: (layer, mod_of(tile_of(i)), 0, 0)),
            row(FOURIER_W),
            row(CONV_W),
            pl.BlockSpec((HALO_ROWS, CONV_W),
                         lambda i: (jnp.maximum(tile_of(i) * halo_per_tile - 1, 0), 0)),
            pl.BlockSpec((HALO_ROWS, CONV_W),
                         lambda i: (jnp.minimum((tile_of(i) + 1) * halo_per_tile, n_halo - 1), 0)),
            row(CONV_W),
            row(ATTN_W),
        ], row(D_MODEL)

    ctx_specs, ctx_out = group_specs(lambda i: jnp.minimum(i, nct - 1), ctx[0].shape[0],
                                     lambda t: 0)
    den_specs, den_out = group_specs(lambda i: jnp.maximum(i - nct, 0), den[0].shape[0],
                                     lambda t: 1 + t // seq_tiles)

    def group_args(g):
        x, ya, zc, bgs, yc = g
        return [x, mod, ya, zc, zc, zc, bgs, yc]

    return pl.pallas_call(
        functools.partial(_outproj_kernel, layer=layer, n_ctx_tiles=nct, seq=seq, dec_seq=dec_seq,
                          final_norm=final_norm),
        grid=(nct + ndt,),
        in_specs=ctx_specs + den_specs + [
            pl.BlockSpec(conv_w.shape, lambda i: (0, 0, 0)),
            pl.BlockSpec(conv_b.shape, lambda i: (0, 0)),
            pl.BlockSpec((1, D_MODEL, D_MODEL), lambda i: (layer, 0, 0)),
            pl.BlockSpec((1, D_MODEL), lambda i: (0, 0)),
        ],
        out_specs=[ctx_out, den_out],
        out_shape=[jax.ShapeDtypeStruct(ctx[0].shape, F32), jax.ShapeDtypeStruct(den[0].shape, F32)],
        scratch_shapes=[pltpu.VMEM((D_MODEL, D_MODEL), BF16)],
        compiler_params=_cparams(("arbitrary",)),
        name="out_projection",
    )(*group_args(ctx), *group_args(den), conv_w, conv_b, w_out,
      final_g.reshape(1, D_MODEL))


def kernel(x_prompt, x_sample, cache_k, cache_v, c, c_ctx, norm_g, w_mod, b_mod, w_in, conv_w,
           conv_b, lam_vec, subln_g, w_out, final_g):
    batch, seq, _ = x_prompt.shape
    dec_batch, dec_seq, _ = x_sample.shape

    mod = _modulation(c_ctx, c, w_mod, b_mod)

    rope_tabs = tuple(jnp.asarray(t) for t in _rope_tables(dec_seq))
    cs = jnp.asarray(_chan_tables()).astype(BF16)
    dn_ctx = jnp.asarray(_dft_tables(seq)).astype(BF16)
    f1, twc, tws, cs2 = (jnp.asarray(t) for t in _ct_tables(dec_seq))
    f1 = f1.astype(BF16)
    cs2 = cs2.astype(BF16)

    cache_k2 = cache_k.reshape(dec_batch, DEPTH, -1, QK_W)
    cache_v2 = cache_v.reshape(dec_batch, DEPTH, -1, QK_W)
    xc = x_prompt.reshape(batch * seq, D_MODEL)
    xl = x_sample.reshape(dec_batch * dec_seq, D_MODEL)
    kv = None
    for l in range(DEPTH):
        lam_init = 0.8 - 0.6 * math.exp(-0.3 * l)
        ctx, k32, v32, den = _in_projection(
            xc, xl, mod, norm_g, w_in, l, rope_tabs, seq=seq, dec_seq=dec_seq,
            prev_kv=kv)
        kv = (k32, v32)

        fa, sga, zc_c, bgs_c, q, k, v, sgc = ctx
        yc_c, ya_c = _ctx_mixers(q, k, v, sgc, fa, sga, lam_vec, subln_g, cs, dn_ctx,
                                 layer=l, seq=seq, lam_init=lam_init)

        fa, sga, zc_d, bgs_d, q, k, v, sgc = den
        yc_d, ya_d = _den_mixers(q, k, v, sgc, cache_k2, cache_v2, fa, sga, lam_vec, subln_g,
                                 f1, twc, tws, cs2, layer=l, seq=dec_seq, lam_init=lam_init)

        xc, xl = _out_projection(
            (xc, ya_c, zc_c, bgs_c, yc_c), (xl, ya_d, zc_d, bgs_d, yc_d), mod,
            conv_w, conv_b, w_out, l, final_g, seq=seq, dec_seq=dec_seq,
            final_norm=l == DEPTH - 1)

    y_prompt = xc.reshape(batch, seq, D_MODEL)
    y_sample = xl.reshape(dec_batch, dec_seq, D_MODEL)
    return (y_prompt, y_sample, *(t.reshape(batch, DEPTH, seq, N_HEADS, QK_W) for t in kv))
```
